```python
import math
import jax
import jax.numpy as jnp
from jax import lax
import numpy as np

D_MODEL = 1024
BATCH = 16
SEQ = 4096
DEPTH = 1

ATTN_HEADS = 8
ATTN_HEAD_DIM = 64
ATTN_V_DIM = 2 * ATTN_HEAD_DIM
ATTN_QBLOCK = 128
LAMBDA_BASE = 0.8
LAMBDA_SCALE = 0.6
LAMBDA_DECAY = 0.3
NUM_BUCKETS = 32
MAX_DISTANCE = 128
SSM_EXPAND = 2
D_INNER = SSM_EXPAND * D_MODEL
SSM_HEAD_DIM = 64
SSM_HEADS = D_INNER // SSM_HEAD_DIM
SSM_GROUPS = 8
SSM_HEADS_PER_GROUP = SSM_HEADS // SSM_GROUPS
D_STATE = 128
CONV_WIDTH = 4
SSM_CHUNK = 128
CONV_DIM = D_INNER + 2 * SSM_GROUPS * D_STATE
N_BRANCHES = 2
N_EXPERTS = 32
TOP_K = 4
D_EXPERT = D_MODEL
SWIGLU_LIMIT = 7.0
SWIGLU_ALPHA = 1.702
MOE_BLOCK = 128
RMS_EPS = 1e-6
SSM_EPS = 1e-5
Q_COLS = ATTN_HEADS * 2 * ATTN_HEAD_DIM
K_COLS = ATTN_HEADS * 2 * ATTN_HEAD_DIM
V_COLS = ATTN_HEADS * ATTN_V_DIM
Z_COLS = D_INNER
XBC_COLS = CONV_DIM
DT_COLS = SSM_HEADS
GATE_COLS = N_BRANCHES * D_MODEL
IN_COLS = Q_COLS + K_COLS + V_COLS + Z_COLS + XBC_COLS + DT_COLS + GATE_COLS

kernel_name = 'hybrid_diffattn_mamba2_moe_block'


def _rms_norm(x, g, eps):
    xf = x.astype(jnp.float32)
    y = xf * lax.rsqrt(jnp.mean(xf * xf, axis=-1, keepdims=True) + eps)
    return (y * g.astype(jnp.float32)).astype(x.dtype)


def _t5_bucket(dist):
    n = jnp.maximum(dist, 0)
    max_exact = NUM_BUCKETS // 2
    scaled = jnp.log(jnp.maximum(n, 1).astype(jnp.float32) / max_exact) / math.log(MAX_DISTANCE / max_exact)
    large = max_exact + (scaled * (NUM_BUCKETS - max_exact)).astype(jnp.int32)
    large = jnp.minimum(large, NUM_BUCKETS - 1)
    return jnp.where(n < max_exact, n, large)


def _diff_attention(q, k, v, lam, lambda_init, q_g, k_g, sub_g, rel_bias):
    b, s = q.shape[:2]
    q = _rms_norm(q.reshape(b, s, ATTN_HEADS, 2, ATTN_HEAD_DIM), q_g, RMS_EPS) * (ATTN_HEAD_DIM ** -0.5)
    k = _rms_norm(k.reshape(b, s, ATTN_HEADS, 2, ATTN_HEAD_DIM), k_g, RMS_EPS)
    q = q.transpose(0, 2, 3, 1, 4)
    k = k.transpose(0, 2, 3, 1, 4)
    v = v.reshape(b, s, ATTN_HEADS, ATTN_V_DIM).transpose(0, 2, 1, 3)
    k_pos = jnp.arange(s, dtype=jnp.int32)

    def q_block(i):
        start = i * ATTN_QBLOCK
        q_blk = lax.dynamic_slice_in_dim(q, start, ATTN_QBLOCK, axis=3)
        dist = (start + jnp.arange(ATTN_QBLOCK, dtype=jnp.int32))[:, None] - k_pos[None, :]
        bias = jnp.moveaxis(rel_bias[_t5_bucket(dist)], -1, 0).astype(jnp.float32)
        logits = jnp.einsum('bhmqd,bhmkd->bhmqk', q_blk, k).astype(jnp.float32) + bias[None, :, None]
        logits = jnp.where(dist >= 0, logits, -jnp.inf)
        p = jax.nn.softmax(logits, axis=-1)
        diff = p[:, :, 0] - lam * p[:, :, 1]
        return jnp.einsum('bhqk,bhkv->bhqv', diff.astype(v.dtype), v)

    o = lax.map(q_block, jnp.arange(s // ATTN_QBLOCK))
    o = o.transpose(1, 0, 3, 2, 4).reshape(b, s, ATTN_HEADS, ATTN_V_DIM)
    o = _rms_norm(o, sub_g, RMS_EPS) * (1.0 - lambda_init)
    return o.reshape(b, s, ATTN_HEADS * ATTN_V_DIM)


def _ssd_scan(xs, dt, a_log, bm, cm):
    b, s = xs.shape[:2]
    nc = s // SSM_CHUNK
    a = dt * (-jnp.exp(a_log.astype(jnp.float32)))

    def chunks(t):
        t = t.reshape((b, nc, SSM_CHUNK) + t.shape[2:])
        return jnp.moveaxis(t, 1, 0)

    gr = (SSM_GROUPS, SSM_HEADS_PER_GROUP)
    xs_c = chunks(xs.reshape((b, s) + gr + (SSM_HEAD_DIM,)))
    dt_c = chunks(dt.reshape((b, s) + gr))
    a_c = chunks(a.reshape((b, s) + gr))
    b_c = chunks(bm)
    c_c = chunks(cm)
    causal = jnp.tril(jnp.ones((SSM_CHUNK, SSM_CHUNK), dtype=bool))[None, :, :, None, None]

    def step(state, inp):
        xk, dtk, ak, bk, ck = inp
        acum = jnp.cumsum(ak, axis=1)
        seg = acum[:, :, None] - acum[:, None, :]
        decay = jnp.exp(jnp.where(causal, seg, -jnp.inf))
        cb = jnp.einsum('blgn,bsgn->blsg', ck, bk)
        mix = cb[..., None] * decay * dtk[:, None]
        y_diag = jnp.einsum('blsgr,bsgrp->blgrp', mix, xk)
        y_off = jnp.einsum('blgn,bgrpn->blgrp', ck, state) * jnp.exp(acum)[..., None]
        a_end = acum[:, -1]
        w_end = jnp.exp(a_end[:, None] - acum) * dtk
        state = state * jnp.exp(a_end)[..., None, None] + jnp.einsum('bsgn,bsgrp->bgrpn', bk, xk * w_end[..., None])
        return state, y_diag + y_off

    init = jnp.zeros((b, SSM_GROUPS, SSM_HEADS_PER_GROUP, SSM_HEAD_DIM, D_STATE), jnp.float32)
    _, ys = lax.scan(step, init, (xs_c, dt_c, a_c, b_c, c_c))
    return jnp.moveaxis(ys, 0, 1).reshape(b, s, SSM_HEADS, SSM_HEAD_DIM)


def _mamba2_mixer(z, xbc, dt_raw, conv_w, conv_b, dt_bias, a_log, d_skip, norm_g):
    b, s = z.shape[:2]
    xbc = lax.conv_general_dilated(
        xbc, conv_w[:, None, :].astype(xbc.dtype), window_strides=(1,),
        padding=[(CONV_WIDTH - 1, 0)], dimension_numbers=('NWC', 'WIO', 'NWC'),
        feature_group_count=CONV_DIM)
    xbc = jax.nn.silu(xbc + conv_b)
    xs, bm, cm = jnp.split(xbc, [D_INNER, D_INNER + SSM_GROUPS * D_STATE], axis=-1)
    xs = xs.reshape(b, s, SSM_HEADS, SSM_HEAD_DIM).astype(jnp.float32)
    bm = bm.reshape(b, s, SSM_GROUPS, D_STATE).astype(jnp.float32)
    cm = cm.reshape(b, s, SSM_GROUPS, D_STATE).astype(jnp.float32)
    dt = jax.nn.softplus(dt_raw.astype(jnp.float32) + dt_bias.astype(jnp.float32))
    y = _ssd_scan(xs, dt, a_log, bm, cm) + d_skip.astype(jnp.float32)[:, None] * xs
    y = y.reshape(b, s, D_INNER) * jax.nn.silu(z.astype(jnp.float32))
    y = _rms_norm(y.reshape(b, s, SSM_GROUPS, D_INNER // SSM_GROUPS), norm_g.reshape(SSM_GROUPS, -1), SSM_EPS)
    return y.reshape(b, s, D_INNER).astype(z.dtype)


def _moe(h, w_router, b_router, w1, b1, w2, b2):
    b, s, d = h.shape
    t = b * s
    hf = h.reshape(t, d)
    logits = jnp.einsum('td,de->te', hf, w_router).astype(jnp.float32) + b_router.astype(jnp.float32)
    top_val, top_idx = lax.top_k(logits, TOP_K)
    top_w = jax.nn.softmax(top_val, axis=-1)
    na = t * TOP_K
    flat_e = top_idx.reshape(na).astype(jnp.int32)
    flat_tok = jnp.arange(na, dtype=jnp.int32) // TOP_K
    flat_w = top_w.reshape(na)
    order = jnp.argsort(flat_e)
    sorted_e = flat_e[order]
    counts = jnp.zeros((N_EXPERTS,), jnp.int32).at[flat_e].add(1)
    padded = (counts + MOE_BLOCK - 1) // MOE_BLOCK * MOE_BLOCK
    start_raw = jnp.cumsum(counts) - counts
    end_pad = jnp.cumsum(padded)
    start_pad = end_pad - padded
    rank = jnp.arange(na, dtype=jnp.int32) - start_raw[sorted_e]
    dest = start_pad[sorted_e] + rank
    n_rows = na + N_EXPERTS * MOE_BLOCK
    n_blocks = n_rows // MOE_BLOCK
    buf_tok = jnp.full((n_rows,), t, jnp.int32).at[dest].set(flat_tok[order])
    buf_w = jnp.zeros((n_rows,), jnp.float32).at[dest].set(flat_w[order])
    block_start = jnp.arange(n_blocks, dtype=jnp.int32) * MOE_BLOCK
    block_e = jnp.minimum(jnp.searchsorted(end_pad, block_start, side='right'), N_EXPERTS - 1).astype(jnp.int32)
    h_pad = jnp.concatenate([hf, jnp.zeros((1, d), hf.dtype)], axis=0)

    def expert_block(args):
        tok, wts, e = args
        xb = h_pad[tok]
        gu = xb @ w1[e] + b1[e]
        gate, up = gu[:, :D_EXPERT], gu[:, D_EXPERT:]
        gate = jnp.minimum(gate, SWIGLU_LIMIT)
        up = jnp.clip(up, -SWIGLU_LIMIT, SWIGLU_LIMIT)
        act = (up + 1.0) * (gate * jax.nn.sigmoid(SWIGLU_ALPHA * gate))
        yb = act @ w2[e] + b2[e]
        return yb * wts[:, None].astype(yb.dtype)

    yb = lax.map(expert_block, (buf_tok.reshape(n_blocks, MOE_BLOCK), buf_w.reshape(n_blocks, MOE_BLOCK), block_e))
    out = jnp.zeros((t + 1, d), hf.dtype).at[buf_tok].add(yb.reshape(n_rows, d).astype(hf.dtype))
    return out[:t].reshape(b, s, d)


def setup_inputs(seed: int = 0) -> dict:
    key = jax.random.key(seed)
    ks = jax.random.split(key, 28)
    f32 = jnp.float32

    def nrm(k, shape, scale):
        return jax.random.normal(k, shape, f32) * scale

    def gain(k, shape):
        return 1.0 + 0.02 * jax.random.normal(k, shape, f32)

    dt0 = jnp.exp(jax.random.uniform(ks[14], (DEPTH, SSM_HEADS), f32, math.log(1e-3), math.log(1e-1)))
    return {
        'x': nrm(ks[0], (BATCH, SEQ, D_MODEL), 1.0),
        'g_mix': gain(ks[1], (DEPTH, D_MODEL)),
        'w_in': nrm(ks[2], (DEPTH, D_MODEL, IN_COLS), D_MODEL ** -0.5),
        'q_norm_g': gain(ks[3], (DEPTH, ATTN_HEAD_DIM)),
        'k_norm_g': gain(ks[4], (DEPTH, ATTN_HEAD_DIM)),
        'lambda_q1': nrm(ks[5], (DEPTH, ATTN_HEAD_DIM), 0.1),
        'lambda_k1': nrm(ks[6], (DEPTH, ATTN_HEAD_DIM), 0.1),
        'lambda_q2': nrm(ks[7], (DEPTH, ATTN_HEAD_DIM), 0.1),
        'lambda_k2': nrm(ks[8], (DEPTH, ATTN_HEAD_DIM), 0.1),
        'attn_sub_g': gain(ks[9], (DEPTH, ATTN_V_DIM)),
        'rel_bias': nrm(ks[10], (NUM_BUCKETS, ATTN_HEADS), 0.3),
        'w_attn_o': nrm(ks[11], (DEPTH, ATTN_HEADS * ATTN_V_DIM, D_MODEL), (ATTN_HEADS * ATTN_V_DIM) ** -0.5),
        'conv_w': nrm(ks[12], (DEPTH, CONV_WIDTH, CONV_DIM), CONV_WIDTH ** -0.5),
        'conv_b': nrm(ks[13], (DEPTH, CONV_DIM), 0.02),
        'dt_bias': dt0 + jnp.log(-jnp.expm1(-dt0)),
        'a_log': jnp.log(jax.random.uniform(ks[15], (DEPTH, SSM_HEADS), f32, 1.0, 16.0)),
        'd_skip': 1.0 + 0.1 * jax.random.normal(ks[16], (DEPTH, SSM_HEADS), f32),
        'ssm_norm_g': gain(ks[17], (DEPTH, D_INNER)),
        'w_ssm_o': nrm(ks[18], (DEPTH, D_INNER, D_MODEL), D_INNER ** -0.5),
        'w_out': nrm(ks[19], (DEPTH, D_MODEL, D_MODEL), D_MODEL ** -0.5),
        'g_ffn': gain(ks[20], (DEPTH, D_MODEL)),
        'w_router': nrm(ks[21], (DEPTH, D_MODEL, N_EXPERTS), D_MODEL ** -0.5),
        'b_router': nrm(ks[22], (DEPTH, N_EXPERTS), 0.01),
        'w1': nrm(ks[23], (DEPTH, N_EXPERTS, D_MODEL, 2 * D_EXPERT), D_MODEL ** -0.5),
        'b1': nrm(ks[24], (DEPTH, N_EXPERTS, 2 * D_EXPERT), 0.02),
        'w2': nrm(ks[25], (DEPTH, N_EXPERTS, D_EXPERT, D_MODEL), D_EXPERT ** -0.5),
        'b2': nrm(ks[26], (DEPTH, N_EXPERTS, D_MODEL), 0.02),
    }


def reference(x, g_mix, w_in, q_norm_g, k_norm_g, lambda_q1, lambda_k1, lambda_q2, lambda_k2,
              attn_sub_g, rel_bias, w_attn_o, conv_w, conv_b, dt_bias, a_log, d_skip, ssm_norm_g,
              w_ssm_o, w_out, g_ffn, w_router, b_router, w1, b1, w2, b2):
    b, s, d = x.shape
    sizes = (Q_COLS, K_COLS, V_COLS, Z_COLS, XBC_COLS, DT_COLS)
    split_pts = [sum(sizes[:i + 1]) for i in range(len(sizes))]
    for l in range(DEPTH):
        lambda_init = LAMBDA_BASE - LAMBDA_SCALE * math.exp(-LAMBDA_DECAY * l)
        h = _rms_norm(x, g_mix[l], RMS_EPS)
        proj = jnp.einsum('bsd,dc->bsc', h, w_in[l])
        q, k, v, z, xbc, dt_raw, gate_logits = jnp.split(proj, split_pts, axis=-1)
        lam = (jnp.exp(jnp.sum(lambda_q1[l].astype(jnp.float32) * lambda_k1[l].astype(jnp.float32)))
               - jnp.exp(jnp.sum(lambda_q2[l].astype(jnp.float32) * lambda_k2[l].astype(jnp.float32)))
               + lambda_init)
        attn = _diff_attention(q, k, v, lam, lambda_init, q_norm_g[l], k_norm_g[l], attn_sub_g[l], rel_bias)
        attn_out = jnp.einsum('bsc,cd->bsd', attn, w_attn_o[l])
        ssm = _mamba2_mixer(z, xbc, dt_raw, conv_w[l], conv_b[l], dt_bias[l], a_log[l], d_skip[l], ssm_norm_g[l])
        ssm_out = jnp.einsum('bsc,cd->bsd', ssm, w_ssm_o[l])
        gates = jax.nn.sigmoid(gate_logits.astype(jnp.float32)).reshape(b, s, N_BRANCHES, d).astype(x.dtype)
        merged = gates[:, :, 0] * attn_out + gates[:, :, 1] * ssm_out
        x = x + jnp.einsum('bsd,de->bse', merged, w_out[l])
        x = x + _moe(_rms_norm(x, g_ffn[l], RMS_EPS), w_router[l], b_router[l], w1[l], b1[l], w2[l], b2[l])
    return x
```

```python
import functools
import math

import jax
import jax.numpy as jnp
from jax import lax
from jax.experimental import pallas as pl
from jax.experimental.pallas import tpu as pltpu

F32 = jnp.float32
BF16 = jnp.bfloat16

D_MODEL = 1024
ATTN_HEADS = 8
ATTN_HEAD_DIM = 64
ATTN_V_DIM = 2 * ATTN_HEAD_DIM
LAMBDA_INIT = 0.8 - 0.6 * math.exp(-0.3 * 0)
NUM_BUCKETS = 32
MAX_DISTANCE = 128
D_INNER = 2 * D_MODEL
SSM_HEAD_DIM = 64
SSM_HEADS = D_INNER // SSM_HEAD_DIM
SSM_GROUPS = 8
SSM_HEADS_PER_GROUP = SSM_HEADS // SSM_GROUPS
D_STATE = 128
CONV_WIDTH = 4
SSM_CHUNK = 128
CONV_DIM = D_INNER + 2 * SSM_GROUPS * D_STATE
N_EXPERTS = 32
TOP_K = 4
D_EXPERT = D_MODEL
SWIGLU_LIMIT = 7.0
SWIGLU_ALPHA = 1.702
RMS_EPS = 1e-6
SSM_EPS = 1e-5
Q_COLS = ATTN_HEADS * 2 * ATTN_HEAD_DIM
K_COLS = Q_COLS
V_COLS = ATTN_HEADS * ATTN_V_DIM

LANES = 128
MXU_DIM = 256
DT_PAD = LANES
NEG_BIG = -1e30
VMEM_LIMIT = 56 * 1024 * 1024

ROW_TILE = 512
ATTN_TILE = 512
MOE_TILE = 256
DISPATCH_TILE = 256
COMBINE_TILE = 128


def _rms(x, eps):
    return x * lax.rsqrt(jnp.mean(x * x, axis=-1, keepdims=True) + eps)


def _sigmoid(x):
    return 1.0 / (1.0 + jnp.exp(-x))


def _params(sem):
    return pltpu.CompilerParams(dimension_semantics=sem, vmem_limit_bytes=VMEM_LIMIT)


def _resident(shape):
    return pl.BlockSpec(shape, lambda *_: (0,) * len(shape), pipeline_mode=pl.Buffered(1))


def _qkv_kernel(x_ref, g_ref, w_ref, gq_ref, gk_ref, q_ref, k_ref, v_ref):
    h = (_rms(x_ref[...], RMS_EPS) * g_ref[...]).astype(BF16)
    qkv = jnp.dot(h, w_ref[...], preferred_element_type=F32)
    r = lax.broadcasted_iota(jnp.int32, (MXU_DIM, MXU_DIM), 0) // ATTN_HEAD_DIM
    c = lax.broadcasted_iota(jnp.int32, (MXU_DIM, MXU_DIM), 1) // ATTN_HEAD_DIM
    group_ones = jnp.where(r == c, 1.0, 0.0).astype(BF16)

    def head_norm(t, gain_ref, out_ref):
        for cc in range(Q_COLS // MXU_DIM):
            sl = slice(cc * MXU_DIM, (cc + 1) * MXU_DIM)
            tc = t[:, sl]
            ss = jnp.dot((tc * tc).astype(BF16), group_ones, preferred_element_type=F32)
            out_ref[:, sl] = (tc * lax.rsqrt(ss * (1.0 / ATTN_HEAD_DIM) + RMS_EPS) * gain_ref[:, sl]).astype(BF16)

    head_norm(qkv[:, :Q_COLS], gq_ref, q_ref)
    head_norm(qkv[:, Q_COLS:Q_COLS + K_COLS], gk_ref, k_ref)
    v_ref[...] = qkv[:, Q_COLS + K_COLS:].astype(BF16)


def _qkv_proj(x2, g_mix, w_qkv, gq, gk):
    t = x2.shape[0]
    tm = min(ROW_TILE, t)
    row = lambda i: (i, 0)
    fix = lambda i: (0, 0)
    out = jax.ShapeDtypeStruct((t, D_MODEL), BF16)
    return pl.pallas_call(
        _qkv_kernel,
        grid=(t // tm,),
        in_specs=[pl.BlockSpec((tm, D_MODEL), row), pl.BlockSpec((1, D_MODEL), fix),
                  _resident((D_MODEL, 3 * D_MODEL)), pl.BlockSpec((1, D_MODEL), fix),
                  pl.BlockSpec((1, D_MODEL), fix)],
        out_specs=[pl.BlockSpec((tm, D_MODEL), row)] * 3,
        out_shape=[out, out, out],
        compiler_params=_params(("arbitrary",)),
        name="qkv_proj",
    )(x2, g_mix, w_qkv, gq, gk)


def _ssm_proj_kernel(x_ref, g_ref, w_ref, z_ref, xbc_ref, dt_ref):
    h = (_rms(x_ref[...], RMS_EPS) * g_ref[...]).astype(BF16)
    p = jnp.dot(h, w_ref[...], preferred_element_type=F32)
    z_ref[...] = p[:, :D_INNER].astype(BF16)
    xbc_ref[...] = p[:, D_INNER:D_INNER + CONV_DIM].astype(BF16)
    dt_ref[...] = p[:, D_INNER + CONV_DIM:]


def _ssm_proj(x2, g_mix, w_ssm):
    t = x2.shape[0]
    tm = min(ROW_TILE, t)
    ncol = D_INNER + CONV_DIM + DT_PAD
    row = lambda i: (i, 0)
    fix = lambda i: (0, 0)
    return pl.pallas_call(
        _ssm_proj_kernel,
        grid=(t // tm,),
        in_specs=[pl.BlockSpec((tm, D_MODEL), row), pl.BlockSpec((1, D_MODEL), fix),
                  _resident((D_MODEL, ncol))],
        out_specs=[pl.BlockSpec((tm, D_INNER), row), pl.BlockSpec((tm, CONV_DIM), row),
                   pl.BlockSpec((tm, DT_PAD), row)],
        out_shape=[jax.ShapeDtypeStruct((t, D_INNER), BF16), jax.ShapeDtypeStruct((t, CONV_DIM), BF16),
                   jax.ShapeDtypeStruct((t, DT_PAD), F32)],
        compiler_params=_params(("arbitrary",)),
        name="ssm_proj",
    )(x2, g_mix, w_ssm)


def _attn_kernel(it_ref, jt_ref, q_ref, k_ref, v_ref, bias_ref, lam_ref, subg_ref, o_ref,
                 q1_s, q2_s, m1_s, m2_s, l1_s, l2_s, acc1_s, acc2_s):
    step = pl.program_id(2)
    i = it_ref[step]
    j = jt_ref[step]

    @pl.when(j == 0)
    def _init():
        q = q_ref[...]
        lane = lax.broadcasted_iota(jnp.int32, q.shape, 1)
        zero = jnp.zeros_like(q)
        q1_s[...] = jnp.where(lane < ATTN_HEAD_DIM, q, zero)
        q2_s[...] = jnp.where(lane >= ATTN_HEAD_DIM, q, zero)
        for m_s, l_s, acc_s in ((m1_s, l1_s, acc1_s), (m2_s, l2_s, acc2_s)):
            m_s[...] = jnp.full(m_s.shape, NEG_BIG, F32)
            l_s[...] = jnp.zeros(l_s.shape, F32)
            acc_s[...] = jnp.zeros(acc_s.shape, F32)

    def update(bias):
        k = k_ref[...]
        v = v_ref[...]
        contract_last = (((1,), (1,)), ((), ()))
        for q_s, m_s, l_s, acc_s in ((q1_s, m1_s, l1_s, acc1_s), (q2_s, m2_s, l2_s, acc2_s)):
            s = lax.dot_general(q_s[...], k, contract_last, preferred_element_type=F32)
            if bias is not None:
                s = s + bias_ref[bias]
            m_old = m_s[...]
            m_new = jnp.maximum(m_old, jnp.max(s, axis=-1, keepdims=True))
            alpha = jnp.exp(m_old - m_new)
            p = jnp.exp(s - m_new)
            l_s[...] = alpha * l_s[...] + jnp.sum(p, axis=-1, keepdims=True)
            acc_s[...] = alpha * acc_s[...] + jnp.dot(p.astype(BF16), v, preferred_element_type=F32)
            m_s[...] = m_new

    @pl.when(j == i)
    def _diag():
        update(0)

    @pl.when(j == i - 1)
    def _prev():
        update(1)

    @pl.when(j < i - 1)
    def _far():
        update(None)

    @pl.when(j == i)
    def _finalize():
        lam_v = lam_ref[...]
        lam = (jnp.exp(jnp.sum(lam_v[0:1] * lam_v[1:2], axis=-1, keepdims=True))
               - jnp.exp(jnp.sum(lam_v[2:3] * lam_v[3:4], axis=-1, keepdims=True)) + LAMBDA_INIT)
        o = acc1_s[...] / l1_s[...] - lam * (acc2_s[...] / l2_s[...])
        o_ref[...] = (_rms(o, RMS_EPS) * subg_ref[...] * (1.0 - LAMBDA_INIT)).astype(BF16)


def _t5_bucket(dist):
    n = jnp.maximum(dist, 0)
    max_exact = NUM_BUCKETS // 2
    scaled = jnp.log(jnp.maximum(n, 1).astype(F32) / max_exact) / math.log(MAX_DISTANCE / max_exact)
    large = max_exact + (scaled * (NUM_BUCKETS - max_exact)).astype(jnp.int32)
    large = jnp.minimum(large, NUM_BUCKETS - 1)
    return jnp.where(n < max_exact, n, large)


def _bias_tiles(rel_bias, tile):
    assert tile >= MAX_DISTANCE
    r = jnp.arange(tile, dtype=jnp.int32)
    tiles = []
    for off in (0, tile):
        dist = r[:, None] - r[None, :] + off
        b = jnp.moveaxis(rel_bias[_t5_bucket(dist)], -1, 0).astype(F32) - rel_bias[NUM_BUCKETS - 1][:, None, None]
        tiles.append(jnp.where(dist[None] >= 0, b, NEG_BIG))
    return jnp.stack(tiles, axis=1)


def _diff_attention(qn, kn, v, bias, lam_vecs, sub_g):
    b, s, _ = qn.shape
    tile = min(ATTN_TILE, s)
    nq = s // tile
    it = jnp.asarray([i for i in range(nq) for _ in range(i + 1)], jnp.int32)
    jt = jnp.asarray([j for i in range(nq) for j in range(i + 1)], jnp.int32)
    q_map = lambda bb, h, st, it_r, jt_r: (bb, it_r[st], h)
    kv_map = lambda bb, h, st, it_r, jt_r: (bb, jt_r[st], h)
    grid_spec = pltpu.PrefetchScalarGridSpec(
        num_scalar_prefetch=2,
        grid=(b, ATTN_HEADS, int(it.shape[0])),
        in_specs=[pl.BlockSpec((None, tile, ATTN_V_DIM), q_map),
                  pl.BlockSpec((None, tile, ATTN_V_DIM), kv_map),
                  pl.BlockSpec((None, tile, ATTN_V_DIM), kv_map),
                  pl.BlockSpec((None, 2, tile, tile), lambda bb, h, st, it_r, jt_r: (h, 0, 0, 0)),
                  pl.BlockSpec((4, ATTN_HEAD_DIM), lambda bb, h, st, it_r, jt_r: (0, 0)),
                  pl.BlockSpec((1, ATTN_V_DIM), lambda bb, h, st, it_r, jt_r: (0, 0))],
        out_specs=pl.BlockSpec((None, tile, ATTN_V_DIM), q_map),
        scratch_shapes=[pltpu.VMEM((tile, ATTN_V_DIM), BF16), pltpu.VMEM((tile, ATTN_V_DIM), BF16),
                        pltpu.VMEM((tile, 1), F32), pltpu.VMEM((tile, 1), F32),
                        pltpu.VMEM((tile, 1), F32), pltpu.VMEM((tile, 1), F32),
                        pltpu.VMEM((tile, ATTN_V_DIM), F32), pltpu.VMEM((tile, ATTN_V_DIM), F32)],
    )
    return pl.pallas_call(
        _attn_kernel,
        grid_spec=grid_spec,
        out_shape=jax.ShapeDtypeStruct((b, s, V_COLS), BF16),
        compiler_params=_params(("arbitrary", "arbitrary", "arbitrary")),
        name="diff_attn",
    )(it, jt, qn, kn, v, bias, lam_vecs, sub_g)


def _ssd_kernel(z_ref, xbc_ref, dt_ref, cw_ref, cb_ref, dtb_ref, alog_ref, dskip_ref, ng_ref, y_ref,
                state_s, hist_s):
    L = SSM_CHUNK
    hp = SSM_HEADS_PER_GROUP * SSM_HEAD_DIM
    tail = 8

    @pl.when(pl.program_id(1) == 0)
    def _reset():
        state_s[...] = jnp.zeros(state_s.shape, F32)
        hist_s[0:tail, :] = jnp.zeros((tail, CONV_DIM), F32)

    xin = xbc_ref[...].astype(F32)
    hist_s[tail:tail + L, :] = xin
    conv = cb_ref[...]
    for jj in range(CONV_WIDTH):
        off = tail - (CONV_WIDTH - 1) + jj
        conv = conv + cw_ref[jj:jj + 1, :] * hist_s[off:off + L, :]
    hist_s[0:tail, :] = xin[L - tail:, :]
    act = conv * _sigmoid(conv)
    xs = act[:, :D_INNER]
    bm = act[:, D_INNER:D_INNER + SSM_GROUPS * D_STATE]
    cm = act[:, D_INNER + SSM_GROUPS * D_STATE:]

    dtl = dt_ref[...] + dtb_ref[...]
    dt = jnp.maximum(dtl, 0.0) + jnp.log(1.0 + jnp.exp(-jnp.abs(dtl)))
    a = dt * (-jnp.exp(alog_ref[...]))
    row = lax.broadcasted_iota(jnp.int32, (L, L), 0)
    col = lax.broadcasted_iota(jnp.int32, (L, L), 1)
    causal = row >= col
    tril = jnp.where(causal, 1.0, 0.0).astype(F32)
    acum = jnp.dot(tril, a, preferred_element_type=F32, precision=lax.Precision.HIGHEST)
    acum_t = acum.T
    a_end = acum[L - 1:L, :]
    w_end = jnp.exp(a_end - acum) * dt
    e_acum = jnp.exp(acum)
    e_end = jnp.exp(a_end)
    lane = lax.broadcasted_iota(jnp.int32, (L, LANES), 1)
    low = lane < SSM_HEAD_DIM

    def pair_cols(arr, h0):
        return jnp.where(low, arr[:, h0:h0 + 1], arr[:, h0 + 1:h0 + 2])

    contract_last = (((1,), (1,)), ((), ()))
    contract_first = (((0,), (0,)), ((), ()))
    y_parts = []
    for g in range(SSM_GROUPS):
        bg = bm[:, g * D_STATE:(g + 1) * D_STATE].astype(BF16)
        cg = cm[:, g * D_STATE:(g + 1) * D_STATE].astype(BF16)
        cb = lax.dot_general(cg, bg, contract_last, preferred_element_type=F32)
        st = state_s[g]
        y_off = jnp.dot(cg, st.astype(BF16), preferred_element_type=F32)
        xw_parts = []
        dec_parts = []
        for pr in range(SSM_HEADS_PER_GROUP // 2):
            h0 = g * SSM_HEADS_PER_GROUP + 2 * pr
            ch = slice(h0 * SSM_HEAD_DIM, (h0 + 2) * SSM_HEAD_DIM)
            x_pair = xs[:, ch]
            xdt = (x_pair * pair_cols(dt, h0)).astype(BF16)
            yd = []
            for hh in (h0, h0 + 1):
                seg = acum[:, hh:hh + 1] - acum_t[hh:hh + 1, :]
                decay = jnp.exp(jnp.where(causal, seg, NEG_BIG))
                yd.append(jnp.dot((cb * decay).astype(BF16), xdt, preferred_element_type=F32))
            y_diag = jnp.where(low, yd[0], yd[1])
            off = y_off[:, 2 * pr * SSM_HEAD_DIM:(2 * pr + 2) * SSM_HEAD_DIM]
            y_parts.append(y_diag + off * pair_cols(e_acum, h0))
            xw_parts.append(x_pair * pair_cols(w_end, h0))
            dec_parts.append(jnp.where(low[0:1], e_end[:, h0:h0 + 1], e_end[:, h0 + 1:h0 + 2]))
        xw = jnp.concatenate(xw_parts, axis=-1).astype(BF16)
        dec = jnp.concatenate(dec_parts, axis=-1)
        state_s[g] = st * dec + lax.dot_general(bg, xw, contract_first, preferred_element_type=F32)
    y = jnp.concatenate(y_parts, axis=-1) + dskip_ref[...] * xs
    zf = z_ref[...].astype(F32)
    y = y * (zf * _sigmoid(zf))
    gsz = D_INNER // SSM_GROUPS
    for g in range(SSM_GROUPS):
        sl = slice(g * gsz, (g + 1) * gsz)
        y_ref[:, sl] = (_rms(y[:, sl], SSM_EPS) * ng_ref[:, sl]).astype(BF16)


def _ssd(z, xbc, dt_raw, conv_w, conv_b, dt_bias, a_log, d_skip_ch, norm_g):
    b, s, _ = z.shape
    nc = s // SSM_CHUNK
    blk = lambda bb, c: (bb, c, 0)
    fix = lambda bb, c: (0, 0)
    return pl.pallas_call(
        _ssd_kernel,
        grid=(b, nc),
        in_specs=[pl.BlockSpec((None, SSM_CHUNK, D_INNER), blk), pl.BlockSpec((None, SSM_CHUNK, CONV_DIM), blk),
                  pl.BlockSpec((None, SSM_CHUNK, DT_PAD), blk),
                  pl.BlockSpec((CONV_WIDTH, CONV_DIM), fix), pl.BlockSpec((1, CONV_DIM), fix),
                  pl.BlockSpec((1, DT_PAD), fix), pl.BlockSpec((1, DT_PAD), fix),
                  pl.BlockSpec((1, D_INNER), fix), pl.BlockSpec((1, D_INNER), fix)],
        out_specs=pl.BlockSpec((None, SSM_CHUNK, D_INNER), blk),
        out_shape=jax.ShapeDtypeStruct((b, s, D_INNER), BF16),
        scratch_shapes=[pltpu.VMEM((SSM_GROUPS, D_STATE, SSM_HEADS_PER_GROUP * SSM_HEAD_DIM), F32),
                        pltpu.VMEM((8 + SSM_CHUNK, CONV_DIM), F32)],
        compiler_params=_params(("arbitrary", "arbitrary")),
        name="ssd",
    )(z, xbc, dt_raw, conv_w, conv_b, dt_bias, a_log, d_skip_ch, norm_g)


def _merge_kernel(x_ref, attn_ref, y_ref, gmix_ref, wg_ref, wao_ref, wso_ref, wout_ref, gffn_ref, wr_ref, br_ref,
                  x1_ref, h2_ref, tope_ref, topw_ref, rank_ref, cnt_ref, carry_s):
    @pl.when(pl.program_id(0) == 0)
    def _reset():
        carry_s[...] = jnp.zeros(carry_s.shape, F32)

    x = x_ref[...]
    tm = x.shape[0]
    h = (_rms(x, RMS_EPS) * gmix_ref[...]).astype(BF16)
    gates = _sigmoid(jnp.dot(h, wg_ref[...], preferred_element_type=F32))
    attn_out = jnp.dot(attn_ref[...], wao_ref[...], preferred_element_type=F32)
    ssm_out = jnp.dot(y_ref[...], wso_ref[...], preferred_element_type=F32)
    merged = gates[:, :D_MODEL] * attn_out + gates[:, D_MODEL:] * ssm_out
    x1 = x + jnp.dot(merged.astype(BF16), wout_ref[...], preferred_element_type=F32)
    x1_ref[...] = x1
    h2 = _rms(x1, RMS_EPS) * gffn_ref[...]
    h2_ref[...] = h2

    logits = lax.dot_general(wr_ref[...], h2, (((1,), (1,)), ((), ())), preferred_element_type=F32,
                             precision=lax.Precision.HIGHEST) + br_ref[...]
    eid = lax.broadcasted_iota(jnp.int32, logits.shape, 0)
    vals, hits = [], []
    member = jnp.zeros(logits.shape, F32)
    work = logits
    for kk in range(TOP_K):
        m = jnp.max(work, axis=0, keepdims=True)
        idx = jnp.min(jnp.where(work == m, eid, N_EXPERTS), axis=0, keepdims=True)
        hit = eid == idx
        tope_ref[kk:kk + 1, :] = idx
        vals.append(m)
        hits.append(hit)
        member = jnp.where(hit, 1.0, member)
        work = jnp.where(hit, -jnp.inf, work)
    ex = [jnp.exp(v - vals[0]) for v in vals]
    denom = ex[0] + ex[1] + ex[2] + ex[3]
    for kk in range(TOP_K):
        topw_ref[kk:kk + 1, :] = ex[kk] / denom

    r = lax.broadcasted_iota(jnp.int32, (tm, tm), 0)
    c = lax.broadcasted_iota(jnp.int32, (tm, tm), 1)
    before = jnp.where(r < c, 1.0, 0.0).astype(BF16)
    prefix = jnp.dot(member.astype(BF16), before, preferred_element_type=F32) + carry_s[...]
    for kk in range(TOP_K):
        rank_ref[kk:kk + 1, :] = jnp.sum(jnp.where(hits[kk], prefix, 0.0), axis=0, keepdims=True).astype(jnp.int32)
    carry = carry_s[...] + jnp.sum(member, axis=1, keepdims=True)
    carry_s[...] = carry
    cnt_ref[...] = jnp.broadcast_to(carry, cnt_ref.shape).astype(jnp.int32)


def _merge(x2, attn, y, g_mix, w_gate, w_ao, w_so, w_out, g_ffn, w_r_t, b_r):
    t = x2.shape[0]
    tm = min(ROW_TILE, t)
    row = lambda i: (i, 0)
    colb = lambda i: (0, i)
    fix = lambda i: (0, 0)
    return pl.pallas_call(
        _merge_kernel,
        grid=(t // tm,),
        in_specs=[pl.BlockSpec((tm, D_MODEL), row), pl.BlockSpec((tm, V_COLS), row), pl.BlockSpec((tm, D_INNER), row),
                  pl.BlockSpec((1, D_MODEL), fix), _resident((D_MODEL, 2 * D_MODEL)),
                  _resident((V_COLS, D_MODEL)), _resident((D_INNER, D_MODEL)),
                  _resident((D_MODEL, D_MODEL)), pl.BlockSpec((1, D_MODEL), fix),
                  pl.BlockSpec((N_EXPERTS, D_MODEL), fix), pl.BlockSpec((N_EXPERTS, 1), fix)],
        out_specs=[pl.BlockSpec((tm, D_MODEL), row), pl.BlockSpec((tm, D_MODEL), row),
                   pl.BlockSpec((TOP_K, tm), colb), pl.BlockSpec((TOP_K, tm), colb), pl.BlockSpec((TOP_K, tm), colb),
                   pl.BlockSpec((N_EXPERTS, LANES), fix)],
        out_shape=[jax.ShapeDtypeStruct((t, D_MODEL), F32), jax.ShapeDtypeStruct((t, D_MODEL), F32),
                   jax.ShapeDtypeStruct((TOP_K, t), jnp.int32), jax.ShapeDtypeStruct((TOP_K, t), F32),
                   jax.ShapeDtypeStruct((TOP_K, t), jnp.int32), jax.ShapeDtypeStruct((N_EXPERTS, LANES), jnp.int32)],
        scratch_shapes=[pltpu.VMEM((N_EXPERTS, 1), F32)],
        compiler_params=_params(("arbitrary",)),
        name="merge_router",
    )(x2, attn, y, g_mix, w_gate, w_ao, w_so, w_out, g_ffn, w_r_t, b_r)


def _dispatch_kernel(dest_hbm, pad_ref, h2_ref, xs_hbm, dest_s, zero_s, sem, zsem, dsem):
    i = pl.program_id(0)
    tm = h2_ref.shape[0]
    n = TOP_K * tm

    @pl.when(i == 0)
    def _zero_pad_rows():
        zero_s[...] = jnp.zeros(zero_s.shape, F32)
        for e in range(N_EXPERTS):
            start = pl.multiple_of(pad_ref[e], MOE_TILE)
            pltpu.make_async_copy(zero_s, xs_hbm.at[pl.ds(start, MOE_TILE)], zsem).start()
        for e in range(N_EXPERTS):
            pltpu.make_async_copy(zero_s, xs_hbm.at[pl.ds(0, MOE_TILE)], zsem).wait()

    cp = pltpu.make_async_copy(dest_hbm.at[pl.ds(i * n, n)], dest_s, dsem)
    cp.start()
    cp.wait()

    def body(tt, carry):
        for kk in range(TOP_K):
            d = dest_s[tt * TOP_K + kk]
            pltpu.make_async_copy(h2_ref.at[pl.ds(tt, 1)], xs_hbm.at[pl.ds(d, 1)], sem).start()
        return carry

    lax.fori_loop(0, tm, body, 0)

    def drain(tt, carry):
        pltpu.make_async_copy(h2_ref.at[pl.ds(0, 1)], xs_hbm.at[pl.ds(0, 1)], sem).wait()
        return carry

    lax.fori_loop(0, n, drain, 0)


def _dispatch(dest_flat, pad_start, h2, n_rows):
    t = h2.shape[0]
    tm = min(DISPATCH_TILE, t)
    grid_spec = pltpu.PrefetchScalarGridSpec(
        num_scalar_prefetch=0,
        grid=(t // tm,),
        in_specs=[pl.BlockSpec(memory_space=pl.ANY),
                  pl.BlockSpec(memory_space=pltpu.SMEM),
                  pl.BlockSpec((tm, D_MODEL), lambda i: (i, 0))],
        out_specs=pl.BlockSpec(memory_space=pl.ANY),
        scratch_shapes=[pltpu.SMEM((TOP_K * tm,), jnp.int32), pltpu.VMEM((MOE_TILE, D_MODEL), F32),
                        pltpu.SemaphoreType.DMA(()), pltpu.SemaphoreType.DMA(()), pltpu.SemaphoreType.DMA(())],
    )
    return pl.pallas_call(
        _dispatch_kernel,
        grid_spec=grid_spec,
        out_shape=jax.ShapeDtypeStruct((n_rows, D_MODEL), F32),
        compiler_params=_params(("arbitrary",)),
        name="moe_dispatch",
    )(dest_flat, pad_start, h2)


def _expert_kernel(be_ref, nb_ref, x_ref, w1_ref, b1_ref, w2_ref, b2_ref, y_ref):
    @pl.when(pl.program_id(0) < nb_ref[0])
    def _run():
        xb = x_ref[...].astype(BF16)
        gu = jnp.dot(xb, w1_ref[...], preferred_element_type=F32) + b1_ref[...]
        gate = jnp.minimum(gu[:, :D_EXPERT], SWIGLU_LIMIT)
        up = jnp.clip(gu[:, D_EXPERT:], -SWIGLU_LIMIT, SWIGLU_LIMIT)
        act = (up + 1.0) * (gate * _sigmoid(SWIGLU_ALPHA * gate))
        y_ref[...] = jnp.dot(act.astype(BF16), w2_ref[...], preferred_element_type=F32) + b2_ref[...]


def _experts(block_e, n_used, xs, w1, b1, w2, b2):
    n_rows = xs.shape[0]
    nb = n_rows // MOE_TILE
    row = lambda i, be, nu: (jnp.minimum(i, nu[0] - 1), 0)
    wsel = lambda i, be, nu: (be[i], 0, 0)
    grid_spec = pltpu.PrefetchScalarGridSpec(
        num_scalar_prefetch=2,
        grid=(nb,),
        in_specs=[pl.BlockSpec((MOE_TILE, D_MODEL), row),
                  pl.BlockSpec((None, D_MODEL, 2 * D_EXPERT), wsel), pl.BlockSpec((None, 1, 2 * D_EXPERT), wsel),
                  pl.BlockSpec((None, D_EXPERT, D_MODEL), wsel), pl.BlockSpec((None, 1, D_MODEL), wsel)],
        out_specs=pl.BlockSpec((MOE_TILE, D_MODEL), row),
    )
    return pl.pallas_call(
        _expert_kernel,
        grid_spec=grid_spec,
        out_shape=jax.ShapeDtypeStruct((n_rows, D_MODEL), F32),
        compiler_params=_params(("arbitrary",)),
        name="moe_experts",
    )(block_e, n_used, xs, w1, b1, w2, b2)


def _combine_kernel(dest_hbm, y_hbm, x1_ref, w_ref, o_ref, dest_s, buf_s, sem, dsem):
    i = pl.program_id(0)
    tm = x1_ref.shape[0]
    n = TOP_K * tm
    cp = pltpu.make_async_copy(dest_hbm.at[pl.ds(i * n, n)], dest_s, dsem)
    cp.start()
    cp.wait()

    def body(tt, carry):
        for kk in range(TOP_K):
            d = dest_s[tt * TOP_K + kk]
            pltpu.make_async_copy(y_hbm.at[pl.ds(d, 1)], buf_s.at[kk, pl.ds(tt, 1)], sem).start()
        return carry

    lax.fori_loop(0, tm, body, 0)

    def drain(tt, carry):
        pltpu.make_async_copy(y_hbm.at[pl.ds(0, 1)], buf_s.at[0, pl.ds(0, 1)], sem).wait()
        return carry

    lax.fori_loop(0, n, drain, 0)
    acc = x1_ref[...]
    w = w_ref[...]
    for kk in range(TOP_K):
        acc = acc + buf_s[kk] * w[:, kk:kk + 1]
    o_ref[...] = acc


def _combine(dest_flat, y, x1, top_w_tok):
    t = x1.shape[0]
    tm = min(COMBINE_TILE, t)
    grid_spec = pltpu.PrefetchScalarGridSpec(
        num_scalar_prefetch=0,
        grid=(t // tm,),
        in_specs=[pl.BlockSpec(memory_space=pl.ANY), pl.BlockSpec(memory_space=pl.ANY),
                  pl.BlockSpec((tm, D_MODEL), lambda i: (i, 0)), pl.BlockSpec((tm, TOP_K), lambda i: (i, 0))],
        out_specs=pl.BlockSpec((tm, D_MODEL), lambda i: (i, 0)),
        scratch_shapes=[pltpu.SMEM((TOP_K * tm,), jnp.int32), pltpu.VMEM((TOP_K, tm, D_MODEL), F32),
                        pltpu.SemaphoreType.DMA(()), pltpu.SemaphoreType.DMA(())],
    )
    return pl.pallas_call(
        _combine_kernel,
        grid_spec=grid_spec,
        out_shape=jax.ShapeDtypeStruct((t, D_MODEL), F32),
        compiler_params=_params(("arbitrary",)),
        name="moe_combine",
    )(dest_flat, y, x1, top_w_tok)


def kernel(x, g_mix, w_in, q_norm_g, k_norm_g, lambda_q1, lambda_k1, lambda_q2, lambda_k2, attn_sub_g, rel_bias,
           w_attn_o, conv_w, conv_b, dt_bias, a_log, d_skip, ssm_norm_g, w_ssm_o, w_out, g_ffn, w_router, b_router,
           w1, b1, w2, b2):
    b, s, d = x.shape
    t = b * s
    l = 0
    x2 = x.reshape(t, d)

    w = w_in[l]
    c0 = Q_COLS + K_COLS + V_COLS
    c1 = c0 + D_INNER + CONV_DIM
    w_qkv = w[:, :c0].astype(BF16)
    w_dt = jnp.pad(w[:, c1:c1 + SSM_HEADS], ((0, 0), (0, DT_PAD - SSM_HEADS)))
    w_ssm = jnp.concatenate([w[:, c0:c1], w_dt], axis=1).astype(BF16)
    w_gate = w[:, c1 + SSM_HEADS:].astype(BF16)
    n_hd = Q_COLS // ATTN_HEAD_DIM
    gq = (jnp.tile(q_norm_g[l], n_hd) * (ATTN_HEAD_DIM ** -0.5)).reshape(1, Q_COLS)
    gk = jnp.tile(k_norm_g[l], n_hd).reshape(1, K_COLS)
    gm = g_mix[l].reshape(1, d)

    qn, kn, v = _qkv_proj(x2, gm, w_qkv, gq, gk)
    z, xbc, dt_raw = _ssm_proj(x2, gm, w_ssm)

    bias = _bias_tiles(rel_bias, min(ATTN_TILE, s))
    lam_vecs = jnp.stack([lambda_q1[l], lambda_k1[l], lambda_q2[l], lambda_k2[l]]).astype(F32)
    attn = _diff_attention(qn.reshape(b, s, -1), kn.reshape(b, s, -1), v.reshape(b, s, -1), bias, lam_vecs,
                           attn_sub_g[l].reshape(1, ATTN_V_DIM))

    pad_h = (0, DT_PAD - SSM_HEADS)
    y = _ssd(z.reshape(b, s, -1), xbc.reshape(b, s, -1), dt_raw.reshape(b, s, -1), conv_w[l],
             conv_b[l].reshape(1, -1), jnp.pad(dt_bias[l], pad_h).reshape(1, -1),
             jnp.pad(a_log[l], pad_h).reshape(1, -1), jnp.repeat(d_skip[l], SSM_HEAD_DIM).reshape(1, -1),
             ssm_norm_g[l].reshape(1, -1))

    x1, h2, top_e, top_w, rank, counts = _merge(
        x2, attn.reshape(t, -1), y.reshape(t, -1), gm, w_gate, w_attn_o[l].astype(BF16), w_ssm_o[l].astype(BF16),
        w_out[l].astype(BF16), g_ffn[l].reshape(1, d), w_router[l].T, b_router[l].reshape(-1, 1))

    counts = counts[:, 0]
    padded = (counts + MOE_TILE - 1) // MOE_TILE * MOE_TILE
    end_pad = jnp.cumsum(padded)
    start_pad = end_pad - padded
    n_rows = t * TOP_K + N_EXPERTS * MOE_TILE
    n_blocks = n_rows // MOE_TILE
    block_start = jnp.arange(n_blocks, dtype=jnp.int32) * MOE_TILE
    block_e = jnp.minimum(jnp.sum(block_start[:, None] >= end_pad[None, :], axis=1), N_EXPERTS - 1).astype(jnp.int32)
    n_used = (end_pad[-1:] // MOE_TILE).astype(jnp.int32)
    onehot = top_e[:, :, None] == jnp.arange(N_EXPERTS, dtype=jnp.int32)
    dest = rank + jnp.sum(jnp.where(onehot, start_pad.astype(jnp.int32), 0), axis=-1)
    dest_flat = dest.T.reshape(-1)
    pad_start = jnp.maximum(end_pad - MOE_TILE, 0).astype(jnp.int32)

    xs = _dispatch(dest_flat, pad_start, h2, n_rows)
    ys = _experts(block_e, n_used, xs, w1[l].astype(BF16), b1[l][:, None, :], w2[l].astype(BF16), b2[l][:, None, :])
    out = _combine(dest_flat, ys, x1, top_w.T)
    return out.reshape(b, s, d)
```

```python
import functools
import math

import jax
import jax.numpy as jnp
from jax import lax
from jax.experimental import pallas as pl
from jax.experimental.pallas import tpu as pltpu

F32 = jnp.float32
BF16 = jnp.bfloat16

D_MODEL = 1024
ATTN_HEADS = 8
ATTN_HEAD_DIM = 64
ATTN_V_DIM = 2 * ATTN_HEAD_DIM
LAMBDA_INIT = 0.8 - 0.6 * math.exp(-0.3 * 0)
NUM_BUCKETS = 32
MAX_DISTANCE = 128
D_INNER = 2 * D_MODEL
SSM_HEAD_DIM = 64
SSM_HEADS = D_INNER // SSM_HEAD_DIM
SSM_GROUPS = 8
SSM_HEADS_PER_GROUP = SSM_HEADS // SSM_GROUPS
D_STATE = 128
CONV_WIDTH = 4
SSM_CHUNK = 128
CONV_DIM = D_INNER + 2 * SSM_GROUPS * D_STATE
N_EXPERTS = 32
TOP_K = 4
D_EXPERT = D_MODEL
SWIGLU_LIMIT = 7.0
SWIGLU_ALPHA = 1.702
RMS_EPS = 1e-6
SSM_EPS = 1e-5
Q_COLS = ATTN_HEADS * 2 * ATTN_HEAD_DIM
K_COLS = Q_COLS
V_COLS = ATTN_HEADS * ATTN_V_DIM

LANES = 128
MXU_DIM = 256
DT_PAD = LANES
NEG_BIG = -1e30
LOG2E = math.log2(math.e)
EXP2_SAFE_BOUND = 80.0
VMEM_LIMIT = 56 * 1024 * 1024

ROW_TILE = 512
ATTN_TILE = 512
MOE_TILE = 256
DISPATCH_TILE = 256
COMBINE_TILE = 128


def _rms(x, eps):
    return x * lax.rsqrt(jnp.mean(x * x, axis=-1, keepdims=True) + eps)


def _sigmoid(x):
    return 1.0 / (1.0 + jnp.exp(-x))


def _params(sem):
    return pltpu.CompilerParams(dimension_semantics=sem, vmem_limit_bytes=VMEM_LIMIT)


def _resident(shape):
    return pl.BlockSpec(shape, lambda *_: (0,) * len(shape), pipeline_mode=pl.Buffered(1))


def _qkv_kernel(x_ref, g_ref, w_ref, gq_ref, gk_ref, q_ref, k_ref, v_ref):
    h = (_rms(x_ref[...], RMS_EPS) * g_ref[...]).astype(BF16)
    qkv = jnp.dot(h, w_ref[...], preferred_element_type=F32)
    r = lax.broadcasted_iota(jnp.int32, (MXU_DIM, MXU_DIM), 0) // ATTN_HEAD_DIM
    c = lax.broadcasted_iota(jnp.int32, (MXU_DIM, MXU_DIM), 1) // ATTN_HEAD_DIM
    group_ones = jnp.where(r == c, 1.0, 0.0).astype(BF16)

    def head_norm(t, gain_ref, out_ref):
        for cc in range(Q_COLS // MXU_DIM):
            sl = slice(cc * MXU_DIM, (cc + 1) * MXU_DIM)
            tc = t[:, sl]
            ss = jnp.dot((tc * tc).astype(BF16), group_ones, preferred_element_type=F32)
            out_ref[:, sl] = (tc * lax.rsqrt(ss * (1.0 / ATTN_HEAD_DIM) + RMS_EPS) * gain_ref[:, sl]).astype(BF16)

    head_norm(qkv[:, :Q_COLS], gq_ref, q_ref)
    head_norm(qkv[:, Q_COLS:Q_COLS + K_COLS], gk_ref, k_ref)
    v_ref[...] = qkv[:, Q_COLS + K_COLS:].astype(BF16)


def _qkv_proj(x2, g_mix, w_qkv, gq, gk):
    t = x2.shape[0]
    tm = min(ROW_TILE, t)
    row = lambda i: (i, 0)
    fix = lambda i: (0, 0)
    out = jax.ShapeDtypeStruct((t, D_MODEL), BF16)
    return pl.pallas_call(
        _qkv_kernel,
        grid=(t // tm,),
        in_specs=[pl.BlockSpec((tm, D_MODEL), row), pl.BlockSpec((1, D_MODEL), fix),
                  _resident((D_MODEL, 3 * D_MODEL)), pl.BlockSpec((1, D_MODEL), fix),
                  pl.BlockSpec((1, D_MODEL), fix)],
        out_specs=[pl.BlockSpec((tm, D_MODEL), row)] * 3,
        out_shape=[out, out, out],
        compiler_params=_params(("arbitrary",)),
        name="qkv_proj",
    )(x2, g_mix, w_qkv, gq, gk)


def _ssm_proj_kernel(x_ref, g_ref, w_ref, z_ref, xbc_ref, dt_ref):
    h = (_rms(x_ref[...], RMS_EPS) * g_ref[...]).astype(BF16)
    p = jnp.dot(h, w_ref[...], preferred_element_type=F32)
    z_ref[...] = p[:, :D_INNER].astype(BF16)
    xbc_ref[...] = p[:, D_INNER:D_INNER + CONV_DIM].astype(BF16)
    dt_ref[...] = p[:, D_INNER + CONV_DIM:]


def _ssm_proj(x2, g_mix, w_ssm):
    t = x2.shape[0]
    tm = min(ROW_TILE, t)
    ncol = D_INNER + CONV_DIM + DT_PAD
    row = lambda i: (i, 0)
    fix = lambda i: (0, 0)
    return pl.pallas_call(
        _ssm_proj_kernel,
        grid=(t // tm,),
        in_specs=[pl.BlockSpec((tm, D_MODEL), row), pl.BlockSpec((1, D_MODEL), fix),
                  _resident((D_MODEL, ncol))],
        out_specs=[pl.BlockSpec((tm, D_INNER), row), pl.BlockSpec((tm, CONV_DIM), row),
                   pl.BlockSpec((tm, DT_PAD), row)],
        out_shape=[jax.ShapeDtypeStruct((t, D_INNER), BF16), jax.ShapeDtypeStruct((t, CONV_DIM), BF16),
                   jax.ShapeDtypeStruct((t, DT_PAD), F32)],
        compiler_params=_params(("arbitrary",)),
        name="ssm_proj",
    )(x2, g_mix, w_ssm)


def _split_maps(q):
    lane = lax.broadcasted_iota(jnp.int32, q.shape, 1)
    zero = jnp.zeros_like(q)
    return jnp.where(lane < ATTN_HEAD_DIM, q, zero), jnp.where(lane >= ATTN_HEAD_DIM, q, zero)


def _attn_finalize(acc1, l1, acc2, l2, lam_ref, subg_ref, o_ref):
    lam_v = lam_ref[...]
    lam = (jnp.exp(jnp.sum(lam_v[0:1] * lam_v[1:2], axis=-1, keepdims=True))
           - jnp.exp(jnp.sum(lam_v[2:3] * lam_v[3:4], axis=-1, keepdims=True)) + LAMBDA_INIT)
    o = acc1 / l1 - lam * (acc2 / l2)
    o_ref[...] = (_rms(o, RMS_EPS) * subg_ref[...] * (1.0 - LAMBDA_INIT)).astype(BF16)


def _attn_bounded_kernel(q_ref, k_ref, v_ref, bias_ref, lam_ref, subg_ref, o_ref, qq_s, vv_s, acc_s):
    i = pl.program_id(2)
    tq = q_ref.shape[0]
    tk = tq

    @pl.when(i == 0)
    def _extend_v():
        vv_s[:, :ATTN_V_DIM] = v_ref[...]
        vv_s[:, ATTN_V_DIM:] = jnp.ones((vv_s.shape[0], ATTN_V_DIM), BF16)

    q1, q2 = _split_maps(q_ref[...])
    qq_s[0:tq, :] = q1
    qq_s[tq:2 * tq, :] = q2
    acc_s[...] = jnp.zeros(acc_s.shape, F32)
    contract_last = (((1,), (1,)), ((), ()))

    def block(j, bias):
        rows = pl.ds(pl.multiple_of(j * tk, tk), tk)
        k = k_ref[rows, :]
        vv = vv_s[rows, :]
        for half in range(2):
            hs = slice(half * tq, (half + 1) * tq)
            s = lax.dot_general(qq_s[hs, :], k, contract_last, preferred_element_type=F32)
            if bias is not None:
                s = s + bias_ref[bias]
            acc_s[hs, :] += jnp.dot(jnp.exp2(s).astype(BF16), vv, preferred_element_type=F32)

    def far(j, carry):
        block(j, None)
        return carry

    lax.fori_loop(0, i - 1, far, 0)

    @pl.when(i >= 1)
    def _prev():
        block(i - 1, 1)

    block(i, 0)
    _attn_finalize(acc_s[0:tq, :ATTN_V_DIM], acc_s[0:tq, ATTN_V_DIM:], acc_s[tq:2 * tq, :ATTN_V_DIM],
                   acc_s[tq:2 * tq, ATTN_V_DIM:], lam_ref, subg_ref, o_ref)


def _attn_online_kernel(it_ref, jt_ref, q_ref, k_ref, v_ref, bias_ref, lam_ref, subg_ref, o_ref,
                        q1_s, q2_s, m1_s, m2_s, l1_s, l2_s, acc1_s, acc2_s):
    step = pl.program_id(2)
    i = it_ref[step]
    j = jt_ref[step]

    @pl.when(j == 0)
    def _init():
        q1_s[...], q2_s[...] = _split_maps(q_ref[...])
        for m_s, l_s, acc_s in ((m1_s, l1_s, acc1_s), (m2_s, l2_s, acc2_s)):
            m_s[...] = jnp.full(m_s.shape, NEG_BIG, F32)
            l_s[...] = jnp.zeros(l_s.shape, F32)
            acc_s[...] = jnp.zeros(acc_s.shape, F32)

    def update(bias):
        k = k_ref[...]
        v = v_ref[...]
        contract_last = (((1,), (1,)), ((), ()))
        for q_s, m_s, l_s, acc_s in ((q1_s, m1_s, l1_s, acc1_s), (q2_s, m2_s, l2_s, acc2_s)):
            s = lax.dot_general(q_s[...], k, contract_last, preferred_element_type=F32)
            if bias is not None:
                s = s + bias_ref[bias]
            m_old = m_s[...]
            m_new = jnp.maximum(m_old, jnp.max(s, axis=-1, keepdims=True))
            alpha = jnp.exp2(m_old - m_new)
            p = jnp.exp2(s - m_new)
            l_s[...] = alpha * l_s[...] + jnp.sum(p, axis=-1, keepdims=True)
            acc_s[...] = alpha * acc_s[...] + jnp.dot(p.astype(BF16), v, preferred_element_type=F32)
            m_s[...] = m_new

    @pl.when(j == i)
    def _diag():
        update(0)

    @pl.when(j == i - 1)
    def _prev():
        update(1)

    @pl.when(j < i - 1)
    def _far():
        update(None)

    @pl.when(j == i)
    def _finalize():
        _attn_finalize(acc1_s[...], l1_s[...], acc2_s[...], l2_s[...], lam_ref, subg_ref, o_ref)


def _t5_bucket(dist):
    n = jnp.maximum(dist, 0)
    max_exact = NUM_BUCKETS // 2
    scaled = jnp.log(jnp.maximum(n, 1).astype(F32) / max_exact) / math.log(MAX_DISTANCE / max_exact)
    large = max_exact + (scaled * (NUM_BUCKETS - max_exact)).astype(jnp.int32)
    large = jnp.minimum(large, NUM_BUCKETS - 1)
    return jnp.where(n < max_exact, n, large)


def _bias_tiles(rel_bias, tile):
    blk = MAX_DISTANCE
    assert tile % blk == 0
    nb = tile // blk
    table = (rel_bias - rel_bias[NUM_BUCKETS - 1]).astype(F32) * LOG2E
    r = jnp.arange(blk, dtype=jnp.int32)
    d0 = r[:, None] - r[None, :]

    def lookup(dist):
        onehot = (_t5_bucket(dist)[..., None] == jnp.arange(NUM_BUCKETS, dtype=jnp.int32)).astype(F32)
        return jnp.einsum('qkn,nh->hqk', onehot, table, precision=lax.Precision.HIGHEST)

    on_diag = jnp.where(d0[None] >= 0, lookup(d0), NEG_BIG)
    sub_diag = lookup(d0 + blk)
    zeros = jnp.zeros_like(sub_diag)
    masked = jnp.full_like(sub_diag, NEG_BIG)

    def assemble(pick):
        return jnp.concatenate(
            [jnp.concatenate([pick(bi, bj) for bj in range(nb)], axis=-1) for bi in range(nb)], axis=-2)

    diag_tile = assemble(lambda bi, bj: on_diag if bi == bj else sub_diag if bi == bj + 1 else zeros if bi > bj else masked)
    prev_tile = assemble(lambda bi, bj: sub_diag if (bi == 0 and bj == nb - 1) else zeros)
    return jnp.stack([diag_tile, prev_tile], axis=1)


def _attn_bounded(qn, kn, v, bias, lam_vecs, sub_g):
    b, s, _ = qn.shape
    tile = min(ATTN_TILE, s)
    q_map = lambda bb, h, i: (bb, i, h)
    kv_map = lambda bb, h, i: (bb, 0, h)
    return pl.pallas_call(
        _attn_bounded_kernel,
        grid=(b, ATTN_HEADS, s // tile),
        in_specs=[pl.BlockSpec((None, tile, ATTN_V_DIM), q_map),
                  pl.BlockSpec((None, s, ATTN_V_DIM), kv_map),
                  pl.BlockSpec((None, s, ATTN_V_DIM), kv_map),
                  pl.BlockSpec((None, 2, tile, tile), lambda bb, h, i: (h, 0, 0, 0)),
                  pl.BlockSpec((4, ATTN_HEAD_DIM), lambda bb, h, i: (0, 0)),
                  pl.BlockSpec((1, ATTN_V_DIM), lambda bb, h, i: (0, 0))],
        out_specs=pl.BlockSpec((None, tile, ATTN_V_DIM), q_map),
        out_shape=jax.ShapeDtypeStruct((b, s, V_COLS), BF16),
        scratch_shapes=[pltpu.VMEM((2 * tile, ATTN_V_DIM), BF16), pltpu.VMEM((s, 2 * ATTN_V_DIM), BF16),
                        pltpu.VMEM((2 * tile, 2 * ATTN_V_DIM), F32)],
        compiler_params=_params(("arbitrary", "arbitrary", "arbitrary")),
        name="diff_attn_bounded",
    )(qn, kn, v, bias, lam_vecs, sub_g)


def _attn_online(qn, kn, v, bias, lam_vecs, sub_g):
    b, s, _ = qn.shape
    tile = min(ATTN_TILE, s)
    nq = s // tile
    it = jnp.asarray([i for i in range(nq) for _ in range(i + 1)], jnp.int32)
    jt = jnp.asarray([j for i in range(nq) for j in range(i + 1)], jnp.int32)
    q_map = lambda bb, h, st, it_r, jt_r: (bb, it_r[st], h)
    kv_map = lambda bb, h, st, it_r, jt_r: (bb, jt_r[st], h)
    grid_spec = pltpu.PrefetchScalarGridSpec(
        num_scalar_prefetch=2,
        grid=(b, ATTN_HEADS, int(it.shape[0])),
        in_specs=[pl.BlockSpec((None, tile, ATTN_V_DIM), q_map),
                  pl.BlockSpec((None, tile, ATTN_V_DIM), kv_map),
                  pl.BlockSpec((None, tile, ATTN_V_DIM), kv_map),
                  pl.BlockSpec((None, 2, tile, tile), lambda bb, h, st, it_r, jt_r: (h, 0, 0, 0)),
                  pl.BlockSpec((4, ATTN_HEAD_DIM), lambda bb, h, st, it_r, jt_r: (0, 0)),
                  pl.BlockSpec((1, ATTN_V_DIM), lambda bb, h, st, it_r, jt_r: (0, 0))],
        out_specs=pl.BlockSpec((None, tile, ATTN_V_DIM), q_map),
        scratch_shapes=[pltpu.VMEM((tile, ATTN_V_DIM), BF16), pltpu.VMEM((tile, ATTN_V_DIM), BF16),
                        pltpu.VMEM((tile, 1), F32), pltpu.VMEM((tile, 1), F32),
                        pltpu.VMEM((tile, 1), F32), pltpu.VMEM((tile, 1), F32),
                        pltpu.VMEM((tile, ATTN_V_DIM), F32), pltpu.VMEM((tile, ATTN_V_DIM), F32)],
    )
    return pl.pallas_call(
        _attn_online_kernel,
        grid_spec=grid_spec,
        out_shape=jax.ShapeDtypeStruct((b, s, V_COLS), BF16),
        compiler_params=_params(("arbitrary", "arbitrary", "arbitrary")),
        name="diff_attn_online",
    )(it, jt, qn, kn, v, bias, lam_vecs, sub_g)


def _diff_attention(qn, kn, v, rel_bias, q_gain, k_gain, lam_vecs, sub_g):
    tile = min(ATTN_TILE, qn.shape[1])
    bias = _bias_tiles(rel_bias, tile)
    spread = jnp.max(jnp.abs(rel_bias - rel_bias[NUM_BUCKETS - 1]))
    bound = LOG2E * (1.05 * math.sqrt(ATTN_HEAD_DIM) * jnp.max(jnp.abs(q_gain)) * jnp.max(jnp.abs(k_gain)) + spread)
    args = (qn, kn, v, bias, lam_vecs, sub_g)
    return lax.cond(bound < EXP2_SAFE_BOUND, lambda a: _attn_bounded(*a), lambda a: _attn_online(*a), args)


def _ssd_kernel(z_ref, xbc_ref, dt_ref, cw_ref, cb_ref, dtb_ref, alog_ref, dskip_ref, ng_ref, y_ref,
                state_s, hist_s):
    L = SSM_CHUNK
    hp = SSM_HEADS_PER_GROUP * SSM_HEAD_DIM
    tail = 8

    @pl.when(pl.program_id(1) == 0)
    def _reset():
        state_s[...] = jnp.zeros(state_s.shape, F32)
        hist_s[0:tail, :] = jnp.zeros((tail, CONV_DIM), F32)

    xin = xbc_ref[...].astype(F32)
    hist_s[tail:tail + L, :] = xin
    conv = cb_ref[...]
    for jj in range(CONV_WIDTH):
        off = tail - (CONV_WIDTH - 1) + jj
        conv = conv + cw_ref[jj:jj + 1, :] * hist_s[off:off + L, :]
    hist_s[0:tail, :] = xin[L - tail:, :]
    act = conv * _sigmoid(conv)
    xs = act[:, :D_INNER]
    bm = act[:, D_INNER:D_INNER + SSM_GROUPS * D_STATE]
    cm = act[:, D_INNER + SSM_GROUPS * D_STATE:]

    dtl = dt_ref[...] + dtb_ref[...]
    dt = jnp.maximum(dtl, 0.0) + jnp.log(1.0 + jnp.exp(-jnp.abs(dtl)))
    a = dt * (-jnp.exp(alog_ref[...]))
    row = lax.broadcasted_iota(jnp.int32, (L, L), 0)
    col = lax.broadcasted_iota(jnp.int32, (L, L), 1)
    causal = row >= col
    tril = jnp.where(causal, 1.0, 0.0).astype(F32)
    acum = jnp.dot(tril, a, preferred_element_type=F32, precision=lax.Precision.HIGHEST)
    acum_t = acum.T
    a_end = acum[L - 1:L, :]
    w_end = jnp.exp(a_end - acum) * dt
    e_acum = jnp.exp(acum)
    e_end = jnp.exp(a_end)
    lane = lax.broadcasted_iota(jnp.int32, (L, LANES), 1)
    low = lane < SSM_HEAD_DIM

    def pair_cols(arr, h0):
        return jnp.where(low, arr[:, h0:h0 + 1], arr[:, h0 + 1:h0 + 2])

    contract_last = (((1,), (1,)), ((), ()))
    contract_first = (((0,), (0,)), ((), ()))
    y_parts = []
    for g in range(SSM_GROUPS):
        bg = bm[:, g * D_STATE:(g + 1) * D_STATE].astype(BF16)
        cg = cm[:, g * D_STATE:(g + 1) * D_STATE].astype(BF16)
        cb = lax.dot_general(cg, bg, contract_last, preferred_element_type=F32)
        st = state_s[g]
        y_off = jnp.dot(cg, st.astype(BF16), preferred_element_type=F32)
        xw_parts = []
        dec_parts = []
        for pr in range(SSM_HEADS_PER_GROUP // 2):
            h0 = g * SSM_HEADS_PER_GROUP + 2 * pr
            ch = slice(h0 * SSM_HEAD_DIM, (h0 + 2) * SSM_HEAD_DIM)
            x_pair = xs[:, ch]
            xdt = (x_pair * pair_cols(dt, h0)).astype(BF16)
            yd = []
            for hh in (h0, h0 + 1):
                seg = acum[:, hh:hh + 1] - acum_t[hh:hh + 1, :]
                decay = jnp.exp(jnp.where(causal, seg, NEG_BIG))
                yd.append(jnp.dot((cb * decay).astype(BF16), xdt, preferred_element_type=F32))
            y_diag = jnp.where(low, yd[0], yd[1])
            off = y_off[:, 2 * pr * SSM_HEAD_DIM:(2 * pr + 2) * SSM_HEAD_DIM]
            y_parts.append(y_diag + off * pair_cols(e_acum, h0))
            xw_parts.append(x_pair * pair_cols(w_end, h0))
            dec_parts.append(jnp.where(low[0:1], e_end[:, h0:h0 + 1], e_end[:, h0 + 1:h0 + 2]))
        xw = jnp.concatenate(xw_parts, axis=-1).astype(BF16)
        dec = jnp.concatenate(dec_parts, axis=-1)
        state_s[g] = st * dec + lax.dot_general(bg, xw, contract_first, preferred_element_type=F32)
    y = jnp.concatenate(y_parts, axis=-1) + dskip_ref[...] * xs
    zf = z_ref[...].astype(F32)
    y = y * (zf * _sigmoid(zf))
    gsz = D_INNER // SSM_GROUPS
    for g in range(SSM_GROUPS):
        sl = slice(g * gsz, (g + 1) * gsz)
        y_ref[:, sl] = (_rms(y[:, sl], SSM_EPS) * ng_ref[:, sl]).astype(BF16)


def _ssd(z, xbc, dt_raw, conv_w, conv_b, dt_bias, a_log, d_skip_ch, norm_g):
    b, s, _ = z.shape
    nc = s // SSM_CHUNK
    blk = lambda bb, c: (bb, c, 0)
    fix = lambda bb, c: (0, 0)
    return pl.pallas_call(
        _ssd_kernel,
        grid=(b, nc),
        in_specs=[pl.BlockSpec((None, SSM_CHUNK, D_INNER), blk), pl.BlockSpec((None, SSM_CHUNK, CONV_DIM), blk),
                  pl.BlockSpec((None, SSM_CHUNK, DT_PAD), blk),
                  pl.BlockSpec((CONV_WIDTH, CONV_DIM), fix), pl.BlockSpec((1, CONV_DIM), fix),
                  pl.BlockSpec((1, DT_PAD), fix), pl.BlockSpec((1, DT_PAD), fix),
                  pl.BlockSpec((1, D_INNER), fix), pl.BlockSpec((1, D_INNER), fix)],
        out_specs=pl.BlockSpec((None, SSM_CHUNK, D_INNER), blk),
        out_shape=jax.ShapeDtypeStruct((b, s, D_INNER), BF16),
        scratch_shapes=[pltpu.VMEM((SSM_GROUPS, D_STATE, SSM_HEADS_PER_GROUP * SSM_HEAD_DIM), F32),
                        pltpu.VMEM((8 + SSM_CHUNK, CONV_DIM), F32)],
        compiler_params=_params(("arbitrary", "arbitrary")),
        name="ssd",
    )(z, xbc, dt_raw, conv_w, conv_b, dt_bias, a_log, d_skip_ch, norm_g)


def _merge_kernel(x_ref, attn_ref, y_ref, gmix_ref, wg_ref, wao_ref, wso_ref, wout_ref, gffn_ref, wr_ref, br_ref,
                  x1_ref, h2_ref, tope_ref, topw_ref, rank_ref, cnt_ref, carry_s):
    @pl.when(pl.program_id(0) == 0)
    def _reset():
        carry_s[...] = jnp.zeros(carry_s.shape, F32)

    x = x_ref[...]
    tm = x.shape[0]
    h = (_rms(x, RMS_EPS) * gmix_ref[...]).astype(BF16)
    gates = _sigmoid(jnp.dot(h, wg_ref[...], preferred_element_type=F32))
    attn_out = jnp.dot(attn_ref[...], wao_ref[...], preferred_element_type=F32)
    ssm_out = jnp.dot(y_ref[...], wso_ref[...], preferred_element_type=F32)
    merged = gates[:, :D_MODEL] * attn_out + gates[:, D_MODEL:] * ssm_out
    x1 = x + jnp.dot(merged.astype(BF16), wout_ref[...], preferred_element_type=F32)
    x1_ref[...] = x1
    h2 = _rms(x1, RMS_EPS) * gffn_ref[...]
    h2_ref[...] = h2

    logits = lax.dot_general(wr_ref[...], h2, (((1,), (1,)), ((), ())), preferred_element_type=F32,
                             precision=lax.Precision.HIGHEST) + br_ref[...]
    eid = lax.broadcasted_iota(jnp.int32, logits.shape, 0)
    vals, hits = [], []
    member = jnp.zeros(logits.shape, F32)
    work = logits
    for kk in range(TOP_K):
        m = jnp.max(work, axis=0, keepdims=True)
        idx = jnp.min(jnp.where(work == m, eid, N_EXPERTS), axis=0, keepdims=True)
        hit = eid == idx
        tope_ref[kk:kk + 1, :] = idx
        vals.append(m)
        hits.append(hit)
        member = jnp.where(hit, 1.0, member)
        work = jnp.where(hit, -jnp.inf, work)
    ex = [jnp.exp(v - vals[0]) for v in vals]
    denom = ex[0] + ex[1] + ex[2] + ex[3]
    for kk in range(TOP_K):
        topw_ref[kk:kk + 1, :] = ex[kk] / denom

    r = lax.broadcasted_iota(jnp.int32, (tm, tm), 0)
    c = lax.broadcasted_iota(jnp.int32, (tm, tm), 1)
    before = jnp.where(r < c, 1.0, 0.0).astype(BF16)
    prefix = jnp.dot(member.astype(BF16), before, preferred_element_type=F32) + carry_s[...]
    for kk in range(TOP_K):
        rank_ref[kk:kk + 1, :] = jnp.sum(jnp.where(hits[kk], prefix, 0.0), axis=0, keepdims=True).astype(jnp.int32)
    carry = carry_s[...] + jnp.sum(member, axis=1, keepdims=True)
    carry_s[...] = carry
    cnt_ref[...] = jnp.broadcast_to(carry, cnt_ref.shape).astype(jnp.int32)


def _merge(x2, attn, y, g_mix, w_gate, w_ao, w_so, w_out, g_ffn, w_r_t, b_r):
    t = x2.shape[0]
    tm = min(ROW_TILE, t)
    row = lambda i: (i, 0)
    colb = lambda i: (0, i)
    fix = lambda i: (0, 0)
    return pl.pallas_call(
        _merge_kernel,
        grid=(t // tm,),
        in_specs=[pl.BlockSpec((tm, D_MODEL), row), pl.BlockSpec((tm, V_COLS), row), pl.BlockSpec((tm, D_INNER), row),
                  pl.BlockSpec((1, D_MODEL), fix), _resident((D_MODEL, 2 * D_MODEL)),
                  _resident((V_COLS, D_MODEL)), _resident((D_INNER, D_MODEL)),
                  _resident((D_MODEL, D_MODEL)), pl.BlockSpec((1, D_MODEL), fix),
                  pl.BlockSpec((N_EXPERTS, D_MODEL), fix), pl.BlockSpec((N_EXPERTS, 1), fix)],
        out_specs=[pl.BlockSpec((tm, D_MODEL), row), pl.BlockSpec((tm, D_MODEL), row),
                   pl.BlockSpec((TOP_K, tm), colb), pl.BlockSpec((TOP_K, tm), colb), pl.BlockSpec((TOP_K, tm), colb),
                   pl.BlockSpec((N_EXPERTS, LANES), fix)],
        out_shape=[jax.ShapeDtypeStruct((t, D_MODEL), F32), jax.ShapeDtypeStruct((t, D_MODEL), F32),
                   jax.ShapeDtypeStruct((TOP_K, t), jnp.int32), jax.ShapeDtypeStruct((TOP_K, t), F32),
                   jax.ShapeDtypeStruct((TOP_K, t), jnp.int32), jax.ShapeDtypeStruct((N_EXPERTS, LANES), jnp.int32)],
        scratch_shapes=[pltpu.VMEM((N_EXPERTS, 1), F32)],
        compiler_params=_params(("arbitrary",)),
        name="merge_router",
    )(x2, attn, y, g_mix, w_gate, w_ao, w_so, w_out, g_ffn, w_r_t, b_r)


def _dispatch_kernel(dest_hbm, pad_ref, h2_ref, xs_hbm, dest_s, zero_s, sem, zsem, dsem):
    i = pl.program_id(0)
    tm = h2_ref.shape[0]
    n = TOP_K * tm

    @pl.when(i == 0)
    def _zero_pad_rows():
        zero_s[...] = jnp.zeros(zero_s.shape, F32)
        for e in range(N_EXPERTS):
            start = pl.multiple_of(pad_ref[e], MOE_TILE)
            pltpu.make_async_copy(zero_s, xs_hbm.at[pl.ds(start, MOE_TILE)], zsem).start()
        for e in range(N_EXPERTS):
            pltpu.make_async_copy(zero_s, xs_hbm.at[pl.ds(0, MOE_TILE)], zsem).wait()

    cp = pltpu.make_async_copy(dest_hbm.at[pl.ds(i * n, n)], dest_s, dsem)
    cp.start()
    cp.wait()

    def body(tt, carry):
        for kk in range(TOP_K):
            d = dest_s[tt * TOP_K + kk]
            pltpu.make_async_copy(h2_ref.at[pl.ds(tt, 1)], xs_hbm.at[pl.ds(d, 1)], sem).start()
        return carry

    lax.fori_loop(0, tm, body, 0)

    def drain(tt, carry):
        pltpu.make_async_copy(h2_ref.at[pl.ds(0, 1)], xs_hbm.at[pl.ds(0, 1)], sem).wait()
        return carry

    lax.fori_loop(0, n, drain, 0)


def _dispatch(dest_flat, pad_start, h2, n_rows):
    t = h2.shape[0]
    tm = min(DISPATCH_TILE, t)
    grid_spec = pltpu.PrefetchScalarGridSpec(
        num_scalar_prefetch=0,
        grid=(t // tm,),
        in_specs=[pl.BlockSpec(memory_space=pl.ANY),
                  pl.BlockSpec(memory_space=pltpu.SMEM),
                  pl.BlockSpec((tm, D_MODEL), lambda i: (i, 0))],
        out_specs=pl.BlockSpec(memory_space=pl.ANY),
        scratch_shapes=[pltpu.SMEM((TOP_K * tm,), jnp.int32), pltpu.VMEM((MOE_TILE, D_MODEL), F32),
                        pltpu.SemaphoreType.DMA(()), pltpu.SemaphoreType.DMA(()), pltpu.SemaphoreType.DMA(())],
    )
    return pl.pallas_call(
        _dispatch_kernel,
        grid_spec=grid_spec,
        out_shape=jax.ShapeDtypeStruct((n_rows, D_MODEL), F32),
        compiler_params=_params(("arbitrary",)),
        name="moe_dispatch",
    )(dest_flat, pad_start, h2)


def _expert_kernel(be_ref, nb_ref, x_ref, w1_ref, b1_ref, w2_ref, b2_ref, y_ref):
    @pl.when(pl.program_id(0) < nb_ref[0])
    def _run():
        xb = x_ref[...].astype(BF16)
        gu = jnp.dot(xb, w1_ref[...], preferred_element_type=F32) + b1_ref[...]
        gate = jnp.minimum(gu[:, :D_EXPERT], SWIGLU_LIMIT)
        up = jnp.clip(gu[:, D_EXPERT:], -SWIGLU_LIMIT, SWIGLU_LIMIT)
        act = (up + 1.0) * (gate * _sigmoid(SWIGLU_ALPHA * gate))
        y_ref[...] = jnp.dot(act.astype(BF16), w2_ref[...], preferred_element_type=F32) + b2_ref[...]


def _experts(block_e, n_used, xs, w1, b1, w2, b2):
    n_rows = xs.shape[0]
    nb = n_rows // MOE_TILE
    row = lambda i, be, nu: (jnp.minimum(i, nu[0] - 1), 0)
    wsel = lambda i, be, nu: (be[i], 0, 0)
    grid_spec = pltpu.PrefetchScalarGridSpec(
        num_scalar_prefetch=2,
        grid=(nb,),
        in_specs=[pl.BlockSpec((MOE_TILE, D_MODEL), row),
                  pl.BlockSpec((None, D_MODEL, 2 * D_EXPERT), wsel), pl.BlockSpec((None, 1, 2 * D_EXPERT), wsel),
                  pl.BlockSpec((None, D_EXPERT, D_MODEL), wsel), pl.BlockSpec((None, 1, D_MODEL), wsel)],
        out_specs=pl.BlockSpec((MOE_TILE, D_MODEL), row),
    )
    return pl.pallas_call(
        _expert_kernel,
        grid_spec=grid_spec,
        out_shape=jax.ShapeDtypeStruct((n_rows, D_MODEL), F32),
        compiler_params=_params(("arbitrary",)),
        name="moe_experts",
    )(block_e, n_used, xs, w1, b1, w2, b2)


def _combine_kernel(dest_hbm, y_hbm, x1_ref, w_ref, o_ref, dest_s, buf_s, sem, dsem):
    i = pl.program_id(0)
    tm = x1_ref.shape[0]
    n = TOP_K * tm
    cp = pltpu.make_async_copy(dest_hbm.at[pl.ds(i * n, n)], dest_s, dsem)
    cp.start()
    cp.wait()

    def body(tt, carry):
        for kk in range(TOP_K):
            d = dest_s[tt * TOP_K + kk]
            pltpu.make_async_copy(y_hbm.at[pl.ds(d, 1)], buf_s.at[kk, pl.ds(tt, 1)], sem).start()
        return carry

    lax.fori_loop(0, tm, body, 0)

    def drain(tt, carry):
        pltpu.make_async_copy(y_hbm.at[pl.ds(0, 1)], buf_s.at[0, pl.ds(0, 1)], sem).wait()
        return carry

    lax.fori_loop(0, n, drain, 0)
    acc = x1_ref[...]
    w = w_ref[...]
    for kk in range(TOP_K):
        acc = acc + buf_s[kk] * w[:, kk:kk + 1]
    o_ref[...] = acc


def _combine(dest_flat, y, x1, top_w_tok):
    t = x1.shape[0]
    tm = min(COMBINE_TILE, t)
    grid_spec = pltpu.PrefetchScalarGridSpec(
        num_scalar_prefetch=0,
        grid=(t // tm,),
        in_specs=[pl.BlockSpec(memory_space=pl.ANY), pl.BlockSpec(memory_space=pl.ANY),
                  pl.BlockSpec((tm, D_MODEL), lambda i: (i, 0)), pl.BlockSpec((tm, TOP_K), lambda i: (i, 0))],
        out_specs=pl.BlockSpec((tm, D_MODEL), lambda i: (i, 0)),
        scratch_shapes=[pltpu.SMEM((TOP_K * tm,), jnp.int32), pltpu.VMEM((TOP_K, tm, D_MODEL), F32),
                        pltpu.SemaphoreType.DMA(()), pltpu.SemaphoreType.DMA(())],
    )
    return pl.pallas_call(
        _combine_kernel,
        grid_spec=grid_spec,
        out_shape=jax.ShapeDtypeStruct((t, D_MODEL), F32),
        compiler_params=_params(("arbitrary",)),
        name="moe_combine",
    )(dest_flat, y, x1, top_w_tok)


def kernel(x, g_mix, w_in, q_norm_g, k_norm_g, lambda_q1, lambda_k1, lambda_q2, lambda_k2, attn_sub_g, rel_bias,
           w_attn_o, conv_w, conv_b, dt_bias, a_log, d_skip, ssm_norm_g, w_ssm_o, w_out, g_ffn, w_router, b_router,
           w1, b1, w2, b2):
    b, s, d = x.shape
    t = b * s
    l = 0
    x2 = x.reshape(t, d)

    w = w_in[l]
    c0 = Q_COLS + K_COLS + V_COLS
    c1 = c0 + D_INNER + CONV_DIM
    w_qkv = w[:, :c0].astype(BF16)
    w_dt = jnp.pad(w[:, c1:c1 + SSM_HEADS], ((0, 0), (0, DT_PAD - SSM_HEADS)))
    w_ssm = jnp.concatenate([w[:, c0:c1], w_dt], axis=1).astype(BF16)
    w_gate = w[:, c1 + SSM_HEADS:].astype(BF16)
    n_hd = Q_COLS // ATTN_HEAD_DIM
    gq = (jnp.tile(q_norm_g[l], n_hd) * (ATTN_HEAD_DIM ** -0.5 * LOG2E)).reshape(1, Q_COLS)
    gk = jnp.tile(k_norm_g[l], n_hd).reshape(1, K_COLS)
    gm = g_mix[l].reshape(1, d)

    qn, kn, v = _qkv_proj(x2, gm, w_qkv, gq, gk)
    z, xbc, dt_raw = _ssm_proj(x2, gm, w_ssm)

    lam_vecs = jnp.stack([lambda_q1[l], lambda_k1[l], lambda_q2[l], lambda_k2[l]]).astype(F32)
    attn = _diff_attention(qn.reshape(b, s, -1), kn.reshape(b, s, -1), v.reshape(b, s, -1), rel_bias, q_norm_g[l],
                           k_norm_g[l], lam_vecs, attn_sub_g[l].reshape(1, ATTN_V_DIM))

    pad_h = (0, DT_PAD - SSM_HEADS)
    y = _ssd(z.reshape(b, s, -1), xbc.reshape(b, s, -1), dt_raw.reshape(b, s, -1), conv_w[l],
             conv_b[l].reshape(1, -1), jnp.pad(dt_bias[l], pad_h).reshape(1, -1),
             jnp.pad(a_log[l], pad_h).reshape(1, -1), jnp.repeat(d_skip[l], SSM_HEAD_DIM).reshape(1, -1),
             ssm_norm_g[l].reshape(1, -1))

    x1, h2, top_e, top_w, rank, counts = _merge(
        x2, attn.reshape(t, -1), y.reshape(t, -1), gm, w_gate, w_attn_o[l].astype(BF16), w_ssm_o[l].astype(BF16),
        w_out[l].astype(BF16), g_ffn[l].reshape(1, d), w_router[l].T, b_router[l].reshape(-1, 1))

    counts = counts[:, 0]
    padded = (counts + MOE_TILE - 1) // MOE_TILE * MOE_TILE
    end_pad = jnp.cumsum(padded)
    start_pad = end_pad - padded
    n_rows = t * TOP_K + N_EXPERTS * MOE_TILE
    n_blocks = n_rows // MOE_TILE
    block_start = jnp.arange(n_blocks, dtype=jnp.int32) * MOE_TILE
    block_e = jnp.minimum(jnp.sum(block_start[:, None] >= end_pad[None, :], axis=1), N_EXPERTS - 1).astype(jnp.int32)
    n_used = (end_pad[-1:] // MOE_TILE).astype(jnp.int32)
    onehot = top_e[:, :, None] == jnp.arange(N_EXPERTS, dtype=jnp.int32)
    dest = rank + jnp.sum(jnp.where(onehot, start_pad.astype(jnp.int32), 0), axis=-1)
    dest_flat = dest.T.reshape(-1)
    pad_start = jnp.maximum(end_pad - MOE_TILE, 0).astype(jnp.int32)

    xs = _dispatch(dest_flat, pad_start, h2, n_rows)
    ys = _experts(block_e, n_used, xs, w1[l].astype(BF16), b1[l][:, None, :], w2[l].astype(BF16), b2[l][:, None, :])
    out = _combine(dest_flat, ys, x1, top_w.T)
    return out.reshape(b, s, d)
```

```python
import functools
import math

import jax
import jax.numpy as jnp
from jax import lax
from jax.experimental import pallas as pl
from jax.experimental.pallas import tpu as pltpu

F32 = jnp.float32
BF16 = jnp.bfloat16

D_MODEL = 1024
ATTN_HEADS = 8
ATTN_HEAD_DIM = 64
ATTN_V_DIM = 2 * ATTN_HEAD_DIM
LAMBDA_INIT = 0.8 - 0.6 * math.exp(-0.3 * 0)
NUM_BUCKETS = 32
MAX_DISTANCE = 128
D_INNER = 2 * D_MODEL
SSM_HEAD_DIM = 64
SSM_HEADS = D_INNER // SSM_HEAD_DIM
SSM_GROUPS = 8
SSM_HEADS_PER_GROUP = SSM_HEADS // SSM_GROUPS
D_STATE = 128
CONV_WIDTH = 4
SSM_CHUNK = 128
CONV_DIM = D_INNER + 2 * SSM_GROUPS * D_STATE
N_EXPERTS = 32
TOP_K = 4
D_EXPERT = D_MODEL
SWIGLU_LIMIT = 7.0
SWIGLU_ALPHA = 1.702
RMS_EPS = 1e-6
SSM_EPS = 1e-5
Q_COLS = ATTN_HEADS * 2 * ATTN_HEAD_DIM
K_COLS = Q_COLS
V_COLS = ATTN_HEADS * ATTN_V_DIM

LANES = 128
MXU_DIM = 256
DT_PAD = LANES
NEG_BIG = -1e30
LOG2E = math.log2(math.e)
EXP2_SAFE_BOUND = 80.0
VMEM_LIMIT = 56 * 1024 * 1024

ROW_TILE = 512
ATTN_TILE = 512
MOE_TILE = 256
ROW_ALIGN = 16


def _rms(x, eps):
    return x * lax.rsqrt(jnp.mean(x * x, axis=-1, keepdims=True) + eps)


def _sigmoid(x):
    return 1.0 / (1.0 + jnp.exp(-x))


def _params(sem):
    return pltpu.CompilerParams(dimension_semantics=sem, vmem_limit_bytes=VMEM_LIMIT)


def _resident(shape):
    return pl.BlockSpec(shape, lambda *_: (0,) * len(shape), pipeline_mode=pl.Buffered(1))


def _qkv_kernel(x_ref, g_ref, w_ref, gq_ref, gk_ref, q_ref, k_ref, v_ref):
    h = (_rms(x_ref[...], RMS_EPS) * g_ref[...]).astype(BF16)
    qkv = jnp.dot(h, w_ref[...], preferred_element_type=F32)
    r = lax.broadcasted_iota(jnp.int32, (MXU_DIM, MXU_DIM), 0) // ATTN_HEAD_DIM
    c = lax.broadcasted_iota(jnp.int32, (MXU_DIM, MXU_DIM), 1) // ATTN_HEAD_DIM
    group_ones = jnp.where(r == c, 1.0, 0.0).astype(BF16)

    def head_norm(t, gain_ref, out_ref):
        for cc in range(Q_COLS // MXU_DIM):
            sl = slice(cc * MXU_DIM, (cc + 1) * MXU_DIM)
            tc = t[:, sl]
            ss = jnp.dot((tc * tc).astype(BF16), group_ones, preferred_element_type=F32)
            out_ref[:, sl] = (tc * lax.rsqrt(ss * (1.0 / ATTN_HEAD_DIM) + RMS_EPS) * gain_ref[:, sl]).astype(BF16)

    head_norm(qkv[:, :Q_COLS], gq_ref, q_ref)
    head_norm(qkv[:, Q_COLS:Q_COLS + K_COLS], gk_ref, k_ref)
    v_ref[...] = qkv[:, Q_COLS + K_COLS:].astype(BF16)


def _qkv_proj(x2, g_mix, w_qkv, gq, gk):
    t = x2.shape[0]
    tm = min(ROW_TILE, t)
    row = lambda i: (i, 0)
    fix = lambda i: (0, 0)
    out = jax.ShapeDtypeStruct((t, D_MODEL), BF16)
    return pl.pallas_call(
        _qkv_kernel,
        grid=(t // tm,),
        in_specs=[pl.BlockSpec((tm, D_MODEL), row), pl.BlockSpec((1, D_MODEL), fix),
                  _resident((D_MODEL, 3 * D_MODEL)), pl.BlockSpec((1, D_MODEL), fix),
                  pl.BlockSpec((1, D_MODEL), fix)],
        out_specs=[pl.BlockSpec((tm, D_MODEL), row)] * 3,
        out_shape=[out, out, out],
        compiler_params=_params(("arbitrary",)),
        name="qkv_proj",
    )(x2, g_mix, w_qkv, gq, gk)


def _ssm_proj_kernel(x_ref, g_ref, w_ref, z_ref, xbc_ref, dt_ref):
    h = (_rms(x_ref[...], RMS_EPS) * g_ref[...]).astype(BF16)
    p = jnp.dot(h, w_ref[...], preferred_element_type=F32)
    z_ref[...] = p[:, :D_INNER].astype(BF16)
    xbc_ref[...] = p[:, D_INNER:D_INNER + CONV_DIM].astype(BF16)
    dt_ref[...] = p[:, D_INNER + CONV_DIM:]


def _ssm_proj(x2, g_mix, w_ssm):
    t = x2.shape[0]
    tm = min(ROW_TILE, t)
    ncol = D_INNER + CONV_DIM + DT_PAD
    row = lambda i: (i, 0)
    fix = lambda i: (0, 0)
    return pl.pallas_call(
        _ssm_proj_kernel,
        grid=(t // tm,),
        in_specs=[pl.BlockSpec((tm, D_MODEL), row), pl.BlockSpec((1, D_MODEL), fix),
                  _resident((D_MODEL, ncol))],
        out_specs=[pl.BlockSpec((tm, D_INNER), row), pl.BlockSpec((tm, CONV_DIM), row),
                   pl.BlockSpec((tm, DT_PAD), row)],
        out_shape=[jax.ShapeDtypeStruct((t, D_INNER), BF16), jax.ShapeDtypeStruct((t, CONV_DIM), BF16),
                   jax.ShapeDtypeStruct((t, DT_PAD), F32)],
        compiler_params=_params(("arbitrary",)),
        name="ssm_proj",
    )(x2, g_mix, w_ssm)


def _split_maps(q):
    lane = lax.broadcasted_iota(jnp.int32, q.shape, 1)
    zero = jnp.zeros_like(q)
    return jnp.where(lane < ATTN_HEAD_DIM, q, zero), jnp.where(lane >= ATTN_HEAD_DIM, q, zero)


def _attn_finalize(acc1, l1, acc2, l2, lam_ref, subg_ref, o_ref):
    lam_v = lam_ref[...]
    lam = (jnp.exp(jnp.sum(lam_v[0:1] * lam_v[1:2], axis=-1, keepdims=True))
           - jnp.exp(jnp.sum(lam_v[2:3] * lam_v[3:4], axis=-1, keepdims=True)) + LAMBDA_INIT)
    o = acc1 / l1 - lam * (acc2 / l2)
    o_ref[...] = (_rms(o, RMS_EPS) * subg_ref[...] * (1.0 - LAMBDA_INIT)).astype(BF16)


def _attn_bounded_kernel(q_ref, k_ref, v_ref, bias_ref, lam_ref, subg_ref, o_ref, qq_s, vv_s, acc_s):
    i = pl.program_id(2)
    tq = q_ref.shape[0]
    tk = tq

    @pl.when(i == 0)
    def _extend_v():
        vv_s[:, :ATTN_V_DIM] = v_ref[...]
        vv_s[:, ATTN_V_DIM:] = jnp.ones((vv_s.shape[0], ATTN_V_DIM), BF16)

    q1, q2 = _split_maps(q_ref[...])
    qq_s[0:tq, :] = q1
    qq_s[tq:2 * tq, :] = q2
    acc_s[...] = jnp.zeros(acc_s.shape, F32)
    contract_last = (((1,), (1,)), ((), ()))

    def block(j, bias):
        rows = pl.ds(pl.multiple_of(j * tk, tk), tk)
        k = k_ref[rows, :]
        vv = vv_s[rows, :]
        for half in range(2):
            hs = slice(half * tq, (half + 1) * tq)
            s = lax.dot_general(qq_s[hs, :], k, contract_last, preferred_element_type=F32)
            if bias is not None:
                s = s + bias_ref[bias]
            acc_s[hs, :] += jnp.dot(jnp.exp2(s).astype(BF16), vv, preferred_element_type=F32)

    def far(j, carry):
        block(j, None)
        return carry

    lax.fori_loop(0, i - 1, far, 0)

    @pl.when(i >= 1)
    def _prev():
        block(i - 1, 1)

    block(i, 0)
    _attn_finalize(acc_s[0:tq, :ATTN_V_DIM], acc_s[0:tq, ATTN_V_DIM:], acc_s[tq:2 * tq, :ATTN_V_DIM],
                   acc_s[tq:2 * tq, ATTN_V_DIM:], lam_ref, subg_ref, o_ref)


def _attn_online_kernel(it_ref, jt_ref, q_ref, k_ref, v_ref, bias_ref, lam_ref, subg_ref, o_ref,
                        q1_s, q2_s, m1_s, m2_s, l1_s, l2_s, acc1_s, acc2_s):
    step = pl.program_id(2)
    i = it_ref[step]
    j = jt_ref[step]

    @pl.when(j == 0)
    def _init():
        q1_s[...], q2_s[...] = _split_maps(q_ref[...])
        for m_s, l_s, acc_s in ((m1_s, l1_s, acc1_s), (m2_s, l2_s, acc2_s)):
            m_s[...] = jnp.full(m_s.shape, NEG_BIG, F32)
            l_s[...] = jnp.zeros(l_s.shape, F32)
            acc_s[...] = jnp.zeros(acc_s.shape, F32)

    def update(bias):
        k = k_ref[...]
        v = v_ref[...]
        contract_last = (((1,), (1,)), ((), ()))
        for q_s, m_s, l_s, acc_s in ((q1_s, m1_s, l1_s, acc1_s), (q2_s, m2_s, l2_s, acc2_s)):
            s = lax.dot_general(q_s[...], k, contract_last, preferred_element_type=F32)
            if bias is not None:
                s = s + bias_ref[bias]
            m_old = m_s[...]
            m_new = jnp.maximum(m_old, jnp.max(s, axis=-1, keepdims=True))
            alpha = jnp.exp2(m_old - m_new)
            p = jnp.exp2(s - m_new)
            l_s[...] = alpha * l_s[...] + jnp.sum(p, axis=-1, keepdims=True)
            acc_s[...] = alpha * acc_s[...] + jnp.dot(p.astype(BF16), v, preferred_element_type=F32)
            m_s[...] = m_new

    @pl.when(j == i)
    def _diag():
        update(0)

    @pl.when(j == i - 1)
    def _prev():
        update(1)

    @pl.when(j < i - 1)
    def _far():
        update(None)

    @pl.when(j == i)
    def _finalize():
        _attn_finalize(acc1_s[...], l1_s[...], acc2_s[...], l2_s[...], lam_ref, subg_ref, o_ref)


def _t5_bucket(dist):
    n = jnp.maximum(dist, 0)
    max_exact = NUM_BUCKETS // 2
    scaled = jnp.log(jnp.maximum(n, 1).astype(F32) / max_exact) / math.log(MAX_DISTANCE / max_exact)
    large = max_exact + (scaled * (NUM_BUCKETS - max_exact)).astype(jnp.int32)
    large = jnp.minimum(large, NUM_BUCKETS - 1)
    return jnp.where(n < max_exact, n, large)


def _bias_tiles(rel_bias, tile):
    blk = MAX_DISTANCE
    assert tile % blk == 0
    nb = tile // blk
    table = (rel_bias - rel_bias[NUM_BUCKETS - 1]).astype(F32) * LOG2E
    r = jnp.arange(blk, dtype=jnp.int32)
    d0 = r[:, None] - r[None, :]

    def lookup(dist):
        onehot = (_t5_bucket(dist)[..., None] == jnp.arange(NUM_BUCKETS, dtype=jnp.int32)).astype(F32)
        return jnp.einsum('qkn,nh->hqk', onehot, table, precision=lax.Precision.HIGHEST)

    on_diag = jnp.where(d0[None] >= 0, lookup(d0), NEG_BIG)
    sub_diag = lookup(d0 + blk)
    zeros = jnp.zeros_like(sub_diag)
    masked = jnp.full_like(sub_diag, NEG_BIG)

    def assemble(pick):
        return jnp.concatenate(
            [jnp.concatenate([pick(bi, bj) for bj in range(nb)], axis=-1) for bi in range(nb)], axis=-2)

    diag_tile = assemble(lambda bi, bj: on_diag if bi == bj else sub_diag if bi == bj + 1 else zeros if bi > bj else masked)
    prev_tile = assemble(lambda bi, bj: sub_diag if (bi == 0 and bj == nb - 1) else zeros)
    return jnp.stack([diag_tile, prev_tile], axis=1)


def _attn_bounded(qn, kn, v, bias, lam_vecs, sub_g):
    b, s, _ = qn.shape
    tile = min(ATTN_TILE, s)
    q_map = lambda bb, h, i: (bb, i, h)
    kv_map = lambda bb, h, i: (bb, 0, h)
    return pl.pallas_call(
        _attn_bounded_kernel,
        grid=(b, ATTN_HEADS, s // tile),
        in_specs=[pl.BlockSpec((None, tile, ATTN_V_DIM), q_map),
                  pl.BlockSpec((None, s, ATTN_V_DIM), kv_map),
                  pl.BlockSpec((None, s, ATTN_V_DIM), kv_map),
                  pl.BlockSpec((None, 2, tile, tile), lambda bb, h, i: (h, 0, 0, 0)),
                  pl.BlockSpec((4, ATTN_HEAD_DIM), lambda bb, h, i: (0, 0)),
                  pl.BlockSpec((1, ATTN_V_DIM), lambda bb, h, i: (0, 0))],
        out_specs=pl.BlockSpec((None, tile, ATTN_V_DIM), q_map),
        out_shape=jax.ShapeDtypeStruct((b, s, V_COLS), BF16),
        scratch_shapes=[pltpu.VMEM((2 * tile, ATTN_V_DIM), BF16), pltpu.VMEM((s, 2 * ATTN_V_DIM), BF16),
                        pltpu.VMEM((2 * tile, 2 * ATTN_V_DIM), F32)],
        compiler_params=_params(("arbitrary", "arbitrary", "arbitrary")),
        name="diff_attn_bounded",
    )(qn, kn, v, bias, lam_vecs, sub_g)


def _attn_online(qn, kn, v, bias, lam_vecs, sub_g):
    b, s, _ = qn.shape
    tile = min(ATTN_TILE, s)
    nq = s // tile
    it = jnp.asarray([i for i in range(nq) for _ in range(i + 1)], jnp.int32)
    jt = jnp.asarray([j for i in range(nq) for j in range(i + 1)], jnp.int32)
    q_map = lambda bb, h, st, it_r, jt_r: (bb, it_r[st], h)
    kv_map = lambda bb, h, st, it_r, jt_r: (bb, jt_r[st], h)
    grid_spec = pltpu.PrefetchScalarGridSpec(
        num_scalar_prefetch=2,
        grid=(b, ATTN_HEADS, int(it.shape[0])),
        in_specs=[pl.BlockSpec((None, tile, ATTN_V_DIM), q_map),
                  pl.BlockSpec((None, tile, ATTN_V_DIM), kv_map),
                  pl.BlockSpec((None, tile, ATTN_V_DIM), kv_map),
                  pl.BlockSpec((None, 2, tile, tile), lambda bb, h, st, it_r, jt_r: (h, 0, 0, 0)),
                  pl.BlockSpec((4, ATTN_HEAD_DIM), lambda bb, h, st, it_r, jt_r: (0, 0)),
                  pl.BlockSpec((1, ATTN_V_DIM), lambda bb, h, st, it_r, jt_r: (0, 0))],
        out_specs=pl.BlockSpec((None, tile, ATTN_V_DIM), q_map),
        scratch_shapes=[pltpu.VMEM((tile, ATTN_V_DIM), BF16), pltpu.VMEM((tile, ATTN_V_DIM), BF16),
                        pltpu.VMEM((tile, 1), F32), pltpu.VMEM((tile, 1), F32),
                        pltpu.VMEM((tile, 1), F32), pltpu.VMEM((tile, 1), F32),
                        pltpu.VMEM((tile, ATTN_V_DIM), F32), pltpu.VMEM((tile, ATTN_V_DIM), F32)],
    )
    return pl.pallas_call(
        _attn_online_kernel,
        grid_spec=grid_spec,
        out_shape=jax.ShapeDtypeStruct((b, s, V_COLS), BF16),
        compiler_params=_params(("arbitrary", "arbitrary", "arbitrary")),
        name="diff_attn_online",
    )(it, jt, qn, kn, v, bias, lam_vecs, sub_g)


def _diff_attention(qn, kn, v, rel_bias, q_gain, k_gain, lam_vecs, sub_g):
    tile = min(ATTN_TILE, qn.shape[1])
    bias = _bias_tiles(rel_bias, tile)
    spread = jnp.max(jnp.abs(rel_bias - rel_bias[NUM_BUCKETS - 1]))
    bound = LOG2E * (1.05 * math.sqrt(ATTN_HEAD_DIM) * jnp.max(jnp.abs(q_gain)) * jnp.max(jnp.abs(k_gain)) + spread)
    args = (qn, kn, v, bias, lam_vecs, sub_g)
    return lax.cond(bound < EXP2_SAFE_BOUND, lambda a: _attn_bounded(*a), lambda a: _attn_online(*a), args)


def _ssd_kernel(z_ref, xbc_ref, dt_ref, cw_ref, cb_ref, dtb_ref, alog_ref, dskip_ref, ng_ref, y_ref,
                state_s, hist_s):
    L = SSM_CHUNK
    hp = SSM_HEADS_PER_GROUP * SSM_HEAD_DIM
    tail = 8

    @pl.when(pl.program_id(1) == 0)
    def _reset():
        state_s[...] = jnp.zeros(state_s.shape, F32)
        hist_s[0:tail, :] = jnp.zeros((tail, CONV_DIM), F32)

    xin = xbc_ref[...].astype(F32)
    hist_s[tail:tail + L, :] = xin
    conv = cb_ref[...]
    for jj in range(CONV_WIDTH):
        off = tail - (CONV_WIDTH - 1) + jj
        conv = conv + cw_ref[jj:jj + 1, :] * hist_s[off:off + L, :]
    hist_s[0:tail, :] = xin[L - tail:, :]
    act = conv * _sigmoid(conv)
    xs = act[:, :D_INNER]
    bm = act[:, D_INNER:D_INNER + SSM_GROUPS * D_STATE]
    cm = act[:, D_INNER + SSM_GROUPS * D_STATE:]

    dtl = dt_ref[...] + dtb_ref[...]
    dt = jnp.maximum(dtl, 0.0) + jnp.log(1.0 + jnp.exp(-jnp.abs(dtl)))
    a = dt * (-jnp.exp(alog_ref[...]))
    row = lax.broadcasted_iota(jnp.int32, (L, L), 0)
    col = lax.broadcasted_iota(jnp.int32, (L, L), 1)
    causal = row >= col
    tril = jnp.where(causal, 1.0, 0.0).astype(F32)
    acum = jnp.dot(tril, a, preferred_element_type=F32, precision=lax.Precision.HIGHEST)
    acum_t = acum.T
    a_end = acum[L - 1:L, :]
    w_end = jnp.exp(a_end - acum) * dt
    e_acum = jnp.exp(acum)
    e_end = jnp.exp(a_end)
    lane = lax.broadcasted_iota(jnp.int32, (L, LANES), 1)
    low = lane < SSM_HEAD_DIM

    def pair_cols(arr, h0):
        return jnp.where(low, arr[:, h0:h0 + 1], arr[:, h0 + 1:h0 + 2])

    contract_last = (((1,), (1,)), ((), ()))
    contract_first = (((0,), (0,)), ((), ()))
    y_parts = []
    for g in range(SSM_GROUPS):
        bg = bm[:, g * D_STATE:(g + 1) * D_STATE].astype(BF16)
        cg = cm[:, g * D_STATE:(g + 1) * D_STATE].astype(BF16)
        cb = lax.dot_general(cg, bg, contract_last, preferred_element_type=F32)
        st = state_s[g]
        y_off = jnp.dot(cg, st.astype(BF16), preferred_element_type=F32)
        xw_parts = []
        dec_parts = []
        for pr in range(SSM_HEADS_PER_GROUP // 2):
            h0 = g * SSM_HEADS_PER_GROUP + 2 * pr
            ch = slice(h0 * SSM_HEAD_DIM, (h0 + 2) * SSM_HEAD_DIM)
            x_pair = xs[:, ch]
            xdt = (x_pair * pair_cols(dt, h0)).astype(BF16)
            yd = []
            for hh in (h0, h0 + 1):
                seg = acum[:, hh:hh + 1] - acum_t[hh:hh + 1, :]
                decay = jnp.exp(jnp.where(causal, seg, NEG_BIG))
                yd.append(jnp.dot((cb * decay).astype(BF16), xdt, preferred_element_type=F32))
            y_diag = jnp.where(low, yd[0], yd[1])
            off = y_off[:, 2 * pr * SSM_HEAD_DIM:(2 * pr + 2) * SSM_HEAD_DIM]
            y_parts.append(y_diag + off * pair_cols(e_acum, h0))
            xw_parts.append(x_pair * pair_cols(w_end, h0))
            dec_parts.append(jnp.where(low[0:1], e_end[:, h0:h0 + 1], e_end[:, h0 + 1:h0 + 2]))
        xw = jnp.concatenate(xw_parts, axis=-1).astype(BF16)
        dec = jnp.concatenate(dec_parts, axis=-1)
        state_s[g] = st * dec + lax.dot_general(bg, xw, contract_first, preferred_element_type=F32)
    y = jnp.concatenate(y_parts, axis=-1) + dskip_ref[...] * xs
    zf = z_ref[...].astype(F32)
    y = y * (zf * _sigmoid(zf))
    gsz = D_INNER // SSM_GROUPS
    for g in range(SSM_GROUPS):
        sl = slice(g * gsz, (g + 1) * gsz)
        y_ref[:, sl] = (_rms(y[:, sl], SSM_EPS) * ng_ref[:, sl]).astype(BF16)


def _ssd(z, xbc, dt_raw, conv_w, conv_b, dt_bias, a_log, d_skip_ch, norm_g):
    b, s, _ = z.shape
    nc = s // SSM_CHUNK
    blk = lambda bb, c: (bb, c, 0)
    fix = lambda bb, c: (0, 0)
    return pl.pallas_call(
        _ssd_kernel,
        grid=(b, nc),
        in_specs=[pl.BlockSpec((None, SSM_CHUNK, D_INNER), blk), pl.BlockSpec((None, SSM_CHUNK, CONV_DIM), blk),
                  pl.BlockSpec((None, SSM_CHUNK, DT_PAD), blk),
                  pl.BlockSpec((CONV_WIDTH, CONV_DIM), fix), pl.BlockSpec((1, CONV_DIM), fix),
                  pl.BlockSpec((1, DT_PAD), fix), pl.BlockSpec((1, DT_PAD), fix),
                  pl.BlockSpec((1, D_INNER), fix), pl.BlockSpec((1, D_INNER), fix)],
        out_specs=pl.BlockSpec((None, SSM_CHUNK, D_INNER), blk),
        out_shape=jax.ShapeDtypeStruct((b, s, D_INNER), BF16),
        scratch_shapes=[pltpu.VMEM((SSM_GROUPS, D_STATE, SSM_HEADS_PER_GROUP * SSM_HEAD_DIM), F32),
                        pltpu.VMEM((8 + SSM_CHUNK, CONV_DIM), F32)],
        compiler_params=_params(("arbitrary", "arbitrary")),
        name="ssd",
    )(z, xbc, dt_raw, conv_w, conv_b, dt_bias, a_log, d_skip_ch, norm_g)


def _round_up(n, m):
    return (n + m - 1) // m * m


def _merge_kernel(x_ref, attn_ref, y_ref, gmix_ref, wg_ref, wao_ref, wso_ref, wout_ref, gffn_ref, wr_ref, br_ref,
                  x1_ref, h2_ref, pos_ref, topw_ref, cnt_ref):
    x = x_ref[...]
    tm = x.shape[0]
    h = (_rms(x, RMS_EPS) * gmix_ref[...]).astype(BF16)
    gates = _sigmoid(jnp.dot(h, wg_ref[...], preferred_element_type=F32))
    attn_out = jnp.dot(attn_ref[...], wao_ref[...], preferred_element_type=F32)
    ssm_out = jnp.dot(y_ref[...], wso_ref[...], preferred_element_type=F32)
    merged = gates[:, :D_MODEL] * attn_out + gates[:, D_MODEL:] * ssm_out
    x1 = x + jnp.dot(merged.astype(BF16), wout_ref[...], preferred_element_type=F32)
    x1_ref[...] = x1
    h2 = _rms(x1, RMS_EPS) * gffn_ref[...]
    h2_ref[...] = h2.astype(BF16)

    logits = lax.dot_general(wr_ref[...], h2, (((1,), (1,)), ((), ())), preferred_element_type=F32,
                             precision=lax.Precision.HIGHEST) + br_ref[...]
    eid = lax.broadcasted_iota(jnp.int32, logits.shape, 0)
    vals, hits = [], []
    member = jnp.zeros(logits.shape, F32)
    work = logits
    for kk in range(TOP_K):
        m = jnp.max(work, axis=0, keepdims=True)
        idx = jnp.min(jnp.where(work == m, eid, N_EXPERTS), axis=0, keepdims=True)
        hit = eid == idx
        vals.append(m)
        hits.append(hit)
        member = jnp.where(hit, 1.0, member)
        work = jnp.where(hit, -jnp.inf, work)
    ex = [jnp.exp(v - vals[0]) for v in vals]
    denom = ex[0] + ex[1] + ex[2] + ex[3]
    for kk in range(TOP_K):
        topw_ref[kk:kk + 1, :] = ex[kk] / denom

    r = lax.broadcasted_iota(jnp.int32, (tm, tm), 0)
    c = lax.broadcasted_iota(jnp.int32, (tm, tm), 1)
    before = jnp.where(r < c, 1.0, 0.0).astype(BF16)
    prefix = jnp.dot(member.astype(BF16), before, preferred_element_type=F32)
    cnt = jnp.sum(member, axis=1, keepdims=True).astype(jnp.int32)
    cnt_al = jnp.bitwise_and(cnt + (ROW_ALIGN - 1), -ROW_ALIGN)
    cnt_al = jnp.broadcast_to(cnt_al, (N_EXPERTS, LANES)).astype(F32)
    er = lax.broadcasted_iota(jnp.int32, (N_EXPERTS, N_EXPERTS), 0)
    ec = lax.broadcasted_iota(jnp.int32, (N_EXPERTS, N_EXPERTS), 1)
    lower = jnp.where(ec < er, 1.0, 0.0).astype(F32)
    run_start = jnp.dot(lower, cnt_al, preferred_element_type=F32, precision=lax.Precision.HIGHEST)[:, 0:1]
    base = prefix + run_start
    for kk in range(TOP_K):
        pos_ref[kk:kk + 1, :] = jnp.sum(jnp.where(hits[kk], base, 0.0), axis=0, keepdims=True).astype(jnp.int32)
    cnt_ref[...] = jnp.broadcast_to(cnt, cnt_ref.shape)


def _merge(x2, attn, y, g_mix, w_gate, w_ao, w_so, w_out, g_ffn, w_r_t, b_r):
    t = x2.shape[0]
    tm = min(ROW_TILE, t)
    row = lambda i: (i, 0)
    colb = lambda i: (0, i)
    fix = lambda i: (0, 0)
    return pl.pallas_call(
        _merge_kernel,
        grid=(t // tm,),
        in_specs=[pl.BlockSpec((tm, D_MODEL), row), pl.BlockSpec((tm, V_COLS), row), pl.BlockSpec((tm, D_INNER), row),
                  pl.BlockSpec((1, D_MODEL), fix), _resident((D_MODEL, 2 * D_MODEL)),
                  _resident((V_COLS, D_MODEL)), _resident((D_INNER, D_MODEL)),
                  _resident((D_MODEL, D_MODEL)), pl.BlockSpec((1, D_MODEL), fix),
                  pl.BlockSpec((N_EXPERTS, D_MODEL), fix), pl.BlockSpec((N_EXPERTS, 1), fix)],
        out_specs=[pl.BlockSpec((tm, D_MODEL), row), pl.BlockSpec((tm, D_MODEL), row),
                   pl.BlockSpec((TOP_K, tm), colb), pl.BlockSpec((TOP_K, tm), colb),
                   pl.BlockSpec((N_EXPERTS, LANES), row)],
        out_shape=[jax.ShapeDtypeStruct((t, D_MODEL), F32), jax.ShapeDtypeStruct((t, D_MODEL), BF16),
                   jax.ShapeDtypeStruct((TOP_K, t), jnp.int32), jax.ShapeDtypeStruct((TOP_K, t), F32),
                   jax.ShapeDtypeStruct((t // tm * N_EXPERTS, LANES), jnp.int32)],
        compiler_params=_params(("arbitrary",)),
        name="merge_router",
    )(x2, attn, y, g_mix, w_gate, w_ao, w_so, w_out, g_ffn, w_r_t, b_r)


def _sorted_rows(tm):
    return _round_up(TOP_K * tm + N_EXPERTS * (ROW_ALIGN - 1), LANES)


def _run_copies(n, max_n, vmem_ref, vmem_off, hbm_ref, hbm_off, sem, to_hbm, wait):
    done = 0
    pieces = [ROW_ALIGN << p for p in range((max_n // ROW_ALIGN).bit_length())]
    for bit in reversed(pieces):
        take = (n & bit) != 0

        @pl.when(take)
        def _piece(bit=bit, done=done):
            v = vmem_ref.at[pl.ds(pl.multiple_of(vmem_off + done, ROW_ALIGN), bit)]
            h = hbm_ref.at[pl.ds(pl.multiple_of(hbm_off + done, ROW_ALIGN), bit)]
            cp = pltpu.make_async_copy(v, h, sem) if to_hbm else pltpu.make_async_copy(h, v, sem)
            cp.wait() if wait else cp.start()

        done = done + jnp.where(take, bit, 0)


def _dispatch_kernel(off_ref, cnt_ref, start_ref, last_ref, h2_ref, pos_ref, xs_hbm, buf_s, zero_s, sem, zsem):
    i = pl.program_id(0)
    tm = h2_ref.shape[0]
    rows = buf_s.shape[0]

    @pl.when(i == 0)
    def _zero_last_blocks():
        zero_s[...] = jnp.zeros(zero_s.shape, BF16)
        for e in range(N_EXPERTS):
            start = pl.multiple_of(last_ref[e], MOE_TILE)
            pltpu.make_async_copy(zero_s, xs_hbm.at[pl.ds(start, MOE_TILE)], zsem).start()
        for e in range(N_EXPERTS):
            pltpu.make_async_copy(zero_s, xs_hbm.at[pl.ds(0, MOE_TILE)], zsem).wait()

        def slack(wait):
            def body(blk, carry):
                cp = pltpu.make_async_copy(zero_s, xs_hbm.at[pl.ds(pl.multiple_of(blk * MOE_TILE, MOE_TILE), MOE_TILE)],
                                           zsem)
                cp.wait() if wait else cp.start()
                return carry
            lax.fori_loop(last_ref[N_EXPERTS], xs_hbm.shape[0] // MOE_TILE, body, 0)

        slack(False)
        slack(True)

    pos = pos_ref[...]
    rid = lax.broadcasted_iota(jnp.int32, (rows, tm), 0)
    sel = jnp.zeros((rows, tm), F32)
    for kk in range(TOP_K):
        sel = sel + jnp.where(rid == pos[kk:kk + 1, :], 1.0, 0.0)
    buf_s[...] = jnp.dot(sel.astype(BF16), h2_ref[...], preferred_element_type=F32).astype(BF16)

    for wait in (False, True):
        for e in range(N_EXPERTS):
            idx = i * N_EXPERTS + e
            _run_copies(cnt_ref[idx], tm, buf_s, start_ref[idx], xs_hbm, off_ref[idx], sem, True, wait)


def _dispatch(run_off, run_cnt, run_start, last_block, h2, pos, n_rows):
    t = h2.shape[0]
    tm = min(ROW_TILE, t)
    grid_spec = pltpu.PrefetchScalarGridSpec(
        num_scalar_prefetch=4,
        grid=(t // tm,),
        in_specs=[pl.BlockSpec((tm, D_MODEL), lambda i, *_: (i, 0)),
                  pl.BlockSpec((TOP_K, tm), lambda i, *_: (0, i))],
        out_specs=pl.BlockSpec(memory_space=pl.ANY),
        scratch_shapes=[pltpu.VMEM((_sorted_rows(tm), D_MODEL), BF16), pltpu.VMEM((MOE_TILE, D_MODEL), BF16),
                        pltpu.SemaphoreType.DMA(()), pltpu.SemaphoreType.DMA(())],
    )
    return pl.pallas_call(
        _dispatch_kernel,
        grid_spec=grid_spec,
        out_shape=jax.ShapeDtypeStruct((n_rows, D_MODEL), BF16),
        compiler_params=_params(("arbitrary",)),
        name="moe_dispatch",
    )(run_off, run_cnt, run_start, last_block, h2, pos)


def _expert_kernel(be_ref, nb_ref, x_ref, w1_ref, b1_ref, w2_ref, b2_ref, y_ref):
    @pl.when(pl.program_id(0) < nb_ref[0])
    def _run():
        gu = jnp.dot(x_ref[...], w1_ref[...], preferred_element_type=F32) + b1_ref[...]
        gate = jnp.minimum(gu[:, :D_EXPERT], SWIGLU_LIMIT)
        up = jnp.clip(gu[:, D_EXPERT:], -SWIGLU_LIMIT, SWIGLU_LIMIT)
        act = (up + 1.0) * (gate * _sigmoid(SWIGLU_ALPHA * gate))
        y = jnp.dot(act.astype(BF16), w2_ref[...], preferred_element_type=F32) + b2_ref[...]
        y_ref[...] = y.astype(BF16)

    @pl.when(pl.program_id(0) >= nb_ref[0])
    def _slack():
        y_ref[...] = jnp.zeros(y_ref.shape, BF16)


def _experts(block_e, n_used, xs, w1, b1, w2, b2):
    n_rows = xs.shape[0]
    nb = n_rows // MOE_TILE
    row_in = lambda i, be, nu: (jnp.minimum(i, nu[0] - 1), 0)
    row = lambda i, be, nu: (i, 0)
    wsel = lambda i, be, nu: (be[i], 0, 0)
    grid_spec = pltpu.PrefetchScalarGridSpec(
        num_scalar_prefetch=2,
        grid=(nb,),
        in_specs=[pl.BlockSpec((MOE_TILE, D_MODEL), row_in),
                  pl.BlockSpec((None, D_MODEL, 2 * D_EXPERT), wsel), pl.BlockSpec((None, 1, 2 * D_EXPERT), wsel),
                  pl.BlockSpec((None, D_EXPERT, D_MODEL), wsel), pl.BlockSpec((None, 1, D_MODEL), wsel)],
        out_specs=pl.BlockSpec((MOE_TILE, D_MODEL), row),
    )
    return pl.pallas_call(
        _expert_kernel,
        grid_spec=grid_spec,
        out_shape=jax.ShapeDtypeStruct((n_rows, D_MODEL), BF16),
        compiler_params=_params(("arbitrary",)),
        name="moe_experts",
    )(block_e, n_used, xs, w1, b1, w2, b2)


def _combine_kernel(off_ref, cnt_ref, start_ref, y_hbm, x1_ref, pos_ref, w_ref, o_ref, buf_s, sem):
    i = pl.program_id(0)
    tm = x1_ref.shape[0]
    rows = buf_s.shape[0]

    @pl.when(i == 0)
    def _clear():
        buf_s[...] = jnp.zeros(buf_s.shape, BF16)

    def copies(wait):
        for e in range(N_EXPERTS):
            idx = i * N_EXPERTS + e
            _run_copies(cnt_ref[idx], tm, buf_s, start_ref[idx], y_hbm, off_ref[idx], sem, False, wait)

    copies(False)
    pos = pos_ref[...]
    w = w_ref[...]
    cid = lax.broadcasted_iota(jnp.int32, (tm, rows), 1)
    wsel = jnp.zeros((tm, rows), F32)
    for kk in range(TOP_K):
        wsel = wsel + jnp.where(cid == pos[:, kk:kk + 1], w[:, kk:kk + 1], 0.0)
    w_hi = wsel.astype(BF16)
    w_lo = (wsel - w_hi.astype(F32)).astype(BF16)
    copies(True)
    ys = buf_s[...]
    o_ref[...] = (x1_ref[...] + jnp.dot(w_hi, ys, preferred_element_type=F32)
                  + jnp.dot(w_lo, ys, preferred_element_type=F32))


def _combine(run_off, run_cnt, run_start, y, x1, pos_tok, w_tok):
    t = x1.shape[0]
    tm = min(ROW_TILE, t)
    grid_spec = pltpu.PrefetchScalarGridSpec(
        num_scalar_prefetch=3,
        grid=(t // tm,),
        in_specs=[pl.BlockSpec(memory_space=pl.ANY),
                  pl.BlockSpec((tm, D_MODEL), lambda i, *_: (i, 0)),
                  pl.BlockSpec((tm, TOP_K), lambda i, *_: (i, 0)), pl.BlockSpec((tm, TOP_K), lambda i, *_: (i, 0))],
        out_specs=pl.BlockSpec((tm, D_MODEL), lambda i, *_: (i, 0)),
        scratch_shapes=[pltpu.VMEM((_sorted_rows(tm), D_MODEL), BF16), pltpu.SemaphoreType.DMA(())],
    )
    return pl.pallas_call(
        _combine_kernel,
        grid_spec=grid_spec,
        out_shape=jax.ShapeDtypeStruct((t, D_MODEL), F32),
        compiler_params=_params(("arbitrary",)),
        name="moe_combine",
    )(run_off, run_cnt, run_start, y, x1, pos_tok, w_tok)


def kernel(x, g_mix, w_in, q_norm_g, k_norm_g, lambda_q1, lambda_k1, lambda_q2, lambda_k2, attn_sub_g, rel_bias,
           w_attn_o, conv_w, conv_b, dt_bias, a_log, d_skip, ssm_norm_g, w_ssm_o, w_out, g_ffn, w_router, b_router,
           w1, b1, w2, b2):
    b, s, d = x.shape
    t = b * s
    l = 0
    x2 = x.reshape(t, d)

    w = w_in[l]
    c0 = Q_COLS + K_COLS + V_COLS
    c1 = c0 + D_INNER + CONV_DIM
    w_qkv = w[:, :c0].astype(BF16)
    w_dt = jnp.pad(w[:, c1:c1 + SSM_HEADS], ((0, 0), (0, DT_PAD - SSM_HEADS)))
    w_ssm = jnp.concatenate([w[:, c0:c1], w_dt], axis=1).astype(BF16)
    w_gate = w[:, c1 + SSM_HEADS:].astype(BF16)
    n_hd = Q_COLS // ATTN_HEAD_DIM
    gq = (jnp.tile(q_norm_g[l], n_hd) * (ATTN_HEAD_DIM ** -0.5 * LOG2E)).reshape(1, Q_COLS)
    gk = jnp.tile(k_norm_g[l], n_hd).reshape(1, K_COLS)
    gm = g_mix[l].reshape(1, d)

    qn, kn, v = _qkv_proj(x2, gm, w_qkv, gq, gk)
    z, xbc, dt_raw = _ssm_proj(x2, gm, w_ssm)

    lam_vecs = jnp.stack([lambda_q1[l], lambda_k1[l], lambda_q2[l], lambda_k2[l]]).astype(F32)
    attn = _diff_attention(qn.reshape(b, s, -1), kn.reshape(b, s, -1), v.reshape(b, s, -1), rel_bias, q_norm_g[l],
                           k_norm_g[l], lam_vecs, attn_sub_g[l].reshape(1, ATTN_V_DIM))

    pad_h = (0, DT_PAD - SSM_HEADS)
    y = _ssd(z.reshape(b, s, -1), xbc.reshape(b, s, -1), dt_raw.reshape(b, s, -1), conv_w[l],
             conv_b[l].reshape(1, -1), jnp.pad(dt_bias[l], pad_h).reshape(1, -1),
             jnp.pad(a_log[l], pad_h).reshape(1, -1), jnp.repeat(d_skip[l], SSM_HEAD_DIM).reshape(1, -1),
             ssm_norm_g[l].reshape(1, -1))

    x1, h2, pos, top_w, tile_cnt = _merge(
        x2, attn.reshape(t, -1), y.reshape(t, -1), gm, w_gate, w_attn_o[l].astype(BF16), w_ssm_o[l].astype(BF16),
        w_out[l].astype(BF16), g_ffn[l].reshape(1, d), w_router[l].T, b_router[l].reshape(-1, 1))

    n_tiles = t // min(ROW_TILE, t)
    cnt = tile_cnt.reshape(n_tiles, N_EXPERTS, LANES)[:, :, 0]
    cnt_al = _round_up(cnt, ROW_ALIGN)
    run_start = jnp.cumsum(cnt_al, axis=1) - cnt_al
    padded = _round_up(jnp.sum(cnt_al, axis=0), MOE_TILE)
    end_pad = jnp.cumsum(padded)
    run_off = (end_pad - padded)[None, :] + jnp.cumsum(cnt_al, axis=0) - cnt_al
    n_rows = _round_up(t * TOP_K + n_tiles * N_EXPERTS * (ROW_ALIGN - 1), MOE_TILE) + N_EXPERTS * MOE_TILE
    block_start = jnp.arange(n_rows // MOE_TILE, dtype=jnp.int32) * MOE_TILE
    block_e = jnp.minimum(jnp.sum(block_start[:, None] >= end_pad[None, :], axis=1), N_EXPERTS - 1).astype(jnp.int32)
    n_used = (end_pad[-1:] // MOE_TILE).astype(jnp.int32)
    last_block = jnp.concatenate([jnp.maximum(end_pad - MOE_TILE, 0), end_pad[-1:] // MOE_TILE]).astype(jnp.int32)
    tables = [a.reshape(-1).astype(jnp.int32) for a in (run_off, cnt_al, run_start)]

    xs = _dispatch(*tables, last_block, h2, pos, n_rows)
    ys = _experts(block_e, n_used, xs, w1[l].astype(BF16), b1[l][:, None, :], w2[l].astype(BF16), b2[l][:, None, :])
    out = _combine(*tables, ys, x1, pos.T, top_w.T)
    return out.reshape(b, s, d)
```

```python
import functools
import math

import jax
import jax.numpy as jnp
from jax import lax
from jax.experimental import pallas as pl
from jax.experimental.pallas import tpu as pltpu

F32 = jnp.float32
BF16 = jnp.bfloat16

D_MODEL = 1024
ATTN_HEADS = 8
ATTN_HEAD_DIM = 64
ATTN_V_DIM = 2 * ATTN_HEAD_DIM
LAMBDA_INIT = 0.8 - 0.6 * math.exp(-0.3 * 0)
NUM_BUCKETS = 32
MAX_DISTANCE = 128
D_INNER = 2 * D_MODEL
SSM_HEAD_DIM = 64
SSM_HEADS = D_INNER // SSM_HEAD_DIM
SSM_GROUPS = 8
SSM_HEADS_PER_GROUP = SSM_HEADS // SSM_GROUPS
D_STATE = 128
CONV_WIDTH = 4
SSM_CHUNK = 128
CONV_DIM = D_INNER + 2 * SSM_GROUPS * D_STATE
N_EXPERTS = 32
TOP_K = 4
D_EXPERT = D_MODEL
SWIGLU_LIMIT = 7.0
SWIGLU_ALPHA = 1.702
RMS_EPS = 1e-6
SSM_EPS = 1e-5
Q_COLS = ATTN_HEADS * 2 * ATTN_HEAD_DIM
K_COLS = Q_COLS
V_COLS = ATTN_HEADS * ATTN_V_DIM

LANES = 128
MXU_DIM = 256
DT_PAD = LANES
NEG_BIG = -1e30
LOG2E = math.log2(math.e)
EXP2_SAFE_BOUND = 80.0
VMEM_LIMIT = 56 * 1024 * 1024

ROW_TILE = 512
ATTN_TILE = 512
MOE_TILE = 256
ROW_ALIGN = 16


def _rms(x, eps):
    return x * lax.rsqrt(jnp.mean(x * x, axis=-1, keepdims=True) + eps)


def _sigmoid(x):
    return 1.0 / (1.0 + jnp.exp(-x))


def _params(sem):
    return pltpu.CompilerParams(dimension_semantics=sem, vmem_limit_bytes=VMEM_LIMIT)


def _resident(shape):
    return pl.BlockSpec(shape, lambda *_: (0,) * len(shape), pipeline_mode=pl.Buffered(1))


def _qkv_kernel(x_ref, g_ref, w_ref, gq_ref, gk_ref, q_ref, k_ref, v_ref):
    h = (_rms(x_ref[...], RMS_EPS) * g_ref[...]).astype(BF16)
    qkv = jnp.dot(h, w_ref[...], preferred_element_type=F32)
    r = lax.broadcasted_iota(jnp.int32, (MXU_DIM, MXU_DIM), 0) // ATTN_HEAD_DIM
    c = lax.broadcasted_iota(jnp.int32, (MXU_DIM, MXU_DIM), 1) // ATTN_HEAD_DIM
    group_ones = jnp.where(r == c, 1.0, 0.0).astype(BF16)

    def head_norm(t, gain_ref, out_ref):
        for cc in range(Q_COLS // MXU_DIM):
            sl = slice(cc * MXU_DIM, (cc + 1) * MXU_DIM)
            tc = t[:, sl]
            ss = jnp.dot((tc * tc).astype(BF16), group_ones, preferred_element_type=F32)
            out_ref[:, sl] = (tc * lax.rsqrt(ss * (1.0 / ATTN_HEAD_DIM) + RMS_EPS) * gain_ref[:, sl]).astype(BF16)

    head_norm(qkv[:, :Q_COLS], gq_ref, q_ref)
    head_norm(qkv[:, Q_COLS:Q_COLS + K_COLS], gk_ref, k_ref)
    v_ref[...] = qkv[:, Q_COLS + K_COLS:].astype(BF16)


def _qkv_proj(x2, g_mix, w_qkv, gq, gk):
    t = x2.shape[0]
    tm = min(ROW_TILE, t)
    row = lambda i: (i, 0)
    fix = lambda i: (0, 0)
    out = jax.ShapeDtypeStruct((t, D_MODEL), BF16)
    return pl.pallas_call(
        _qkv_kernel,
        grid=(t // tm,),
        in_specs=[pl.BlockSpec((tm, D_MODEL), row), pl.BlockSpec((1, D_MODEL), fix),
                  _resident((D_MODEL, 3 * D_MODEL)), pl.BlockSpec((1, D_MODEL), fix),
                  pl.BlockSpec((1, D_MODEL), fix)],
        out_specs=[pl.BlockSpec((tm, D_MODEL), row)] * 3,
        out_shape=[out, out, out],
        compiler_params=_params(("arbitrary",)),
        name="qkv_proj",
    )(x2, g_mix, w_qkv, gq, gk)


CONV_COLS = 1024
HIST_ROWS = 8


def _ssm_proj_kernel(x_ref, g_ref, w_ref, cw_ref, cb_ref, dtb_ref, z_ref, xbc_ref, dt_ref, tail_s,
                     *, tiles_per_seq):
    tm = x_ref.shape[0]

    @pl.when(pl.program_id(0) % tiles_per_seq == 0)
    def _sequence_start():
        tail_s[...] = jnp.zeros(tail_s.shape, F32)

    h = (_rms(x_ref[...], RMS_EPS) * g_ref[...]).astype(BF16)
    zf = jnp.dot(h, w_ref[:, :D_INNER], preferred_element_type=F32)
    z_ref[...] = (zf * _sigmoid(zf)).astype(BF16)
    for c in range(CONV_DIM // CONV_COLS):
        cols = slice(c * CONV_COLS, (c + 1) * CONV_COLS)
        p = jnp.dot(h, w_ref[:, D_INNER + c * CONV_COLS:D_INNER + (c + 1) * CONV_COLS], preferred_element_type=F32)
        prev = tail_s[:, cols]
        tail_s[:, cols] = p[tm - HIST_ROWS:, :]
        groups = jnp.concatenate([prev, p], axis=0).reshape(tm // HIST_ROWS + 1, HIST_ROWS, CONV_COLS)
        sub = lax.broadcasted_iota(jnp.int32, (tm // HIST_ROWS, HIST_ROWS, CONV_COLS), 1)
        conv = cb_ref[:, cols] + cw_ref[CONV_WIDTH - 1:CONV_WIDTH, cols] * p
        for back in range(1, CONV_WIDTH):
            rolled = pltpu.roll(groups, back, axis=1)
            shifted = jnp.where(sub < back, rolled[:-1], rolled[1:]).reshape(tm, CONV_COLS)
            tap = CONV_WIDTH - 1 - back
            conv = conv + cw_ref[tap:tap + 1, cols] * shifted
        xbc_ref[:, cols] = (conv * _sigmoid(conv)).astype(BF16)
    dtl = jnp.dot(h, w_ref[:, D_INNER + CONV_DIM:], preferred_element_type=F32) + dtb_ref[...]
    dt_ref[...] = jnp.maximum(dtl, 0.0) + jnp.log(1.0 + jnp.exp(-jnp.abs(dtl)))


def _ssm_proj(x2, g_mix, w_ssm, conv_w, conv_b, dt_bias, seq_len):
    t = x2.shape[0]
    tm = min(ROW_TILE, seq_len)
    assert seq_len % tm == 0
    ncol = D_INNER + CONV_DIM + DT_PAD
    row = lambda i: (i, 0)
    fix = lambda i: (0, 0)
    return pl.pallas_call(
        functools.partial(_ssm_proj_kernel, tiles_per_seq=seq_len // tm),
        grid=(t // tm,),
        in_specs=[pl.BlockSpec((tm, D_MODEL), row), pl.BlockSpec((1, D_MODEL), fix),
                  _resident((D_MODEL, ncol)), pl.BlockSpec((CONV_WIDTH, CONV_DIM), fix),
                  pl.BlockSpec((1, CONV_DIM), fix), pl.BlockSpec((1, DT_PAD), fix)],
        out_specs=[pl.BlockSpec((tm, D_INNER), row), pl.BlockSpec((tm, CONV_DIM), row),
                   pl.BlockSpec((tm, DT_PAD), row)],
        out_shape=[jax.ShapeDtypeStruct((t, D_INNER), BF16), jax.ShapeDtypeStruct((t, CONV_DIM), BF16),
                   jax.ShapeDtypeStruct((t, DT_PAD), F32)],
        scratch_shapes=[pltpu.VMEM((HIST_ROWS, CONV_DIM), F32)],
        compiler_params=_params(("arbitrary",)),
        name="ssm_proj",
    )(x2, g_mix, w_ssm, conv_w, conv_b, dt_bias)


def _split_maps(q):
    lane = lax.broadcasted_iota(jnp.int32, q.shape, 1)
    zero = jnp.zeros_like(q)
    return jnp.where(lane < ATTN_HEAD_DIM, q, zero), jnp.where(lane >= ATTN_HEAD_DIM, q, zero)


def _attn_finalize(acc1, l1, acc2, l2, lam_ref, subg_ref, o_ref):
    lam_v = lam_ref[...]
    lam = (jnp.exp(jnp.sum(lam_v[0:1] * lam_v[1:2], axis=-1, keepdims=True))
           - jnp.exp(jnp.sum(lam_v[2:3] * lam_v[3:4], axis=-1, keepdims=True)) + LAMBDA_INIT)
    o = acc1 / l1 - lam * (acc2 / l2)
    o_ref[...] = (_rms(o, RMS_EPS) * subg_ref[...] * (1.0 - LAMBDA_INIT)).astype(BF16)


def _attn_bounded_kernel(q_ref, k_ref, v_ref, bias_ref, lam_ref, subg_ref, o_ref, qq_s, vv_s, acc_s):
    i = pl.program_id(2)
    tq = q_ref.shape[0]
    tk = tq

    @pl.when(i == 0)
    def _extend_v():
        vv_s[:, :ATTN_V_DIM] = v_ref[...]
        vv_s[:, ATTN_V_DIM:] = jnp.ones((vv_s.shape[0], ATTN_V_DIM), BF16)

    q1, q2 = _split_maps(q_ref[...])
    qq_s[0:tq, :] = q1
    qq_s[tq:2 * tq, :] = q2
    acc_s[...] = jnp.zeros(acc_s.shape, F32)
    contract_last = (((1,), (1,)), ((), ()))

    def block(j, bias):
        rows = pl.ds(pl.multiple_of(j * tk, tk), tk)
        k = k_ref[rows, :]
        vv = vv_s[rows, :]
        for half in range(2):
            hs = slice(half * tq, (half + 1) * tq)
            s = lax.dot_general(qq_s[hs, :], k, contract_last, preferred_element_type=F32)
            if bias is not None:
                s = s + bias_ref[bias]
            acc_s[hs, :] += jnp.dot(jnp.exp2(s).astype(BF16), vv, preferred_element_type=F32)

    def far(j, carry):
        block(j, None)
        return carry

    lax.fori_loop(0, i - 1, far, 0)

    @pl.when(i >= 1)
    def _prev():
        block(i - 1, 1)

    block(i, 0)
    _attn_finalize(acc_s[0:tq, :ATTN_V_DIM], acc_s[0:tq, ATTN_V_DIM:], acc_s[tq:2 * tq, :ATTN_V_DIM],
                   acc_s[tq:2 * tq, ATTN_V_DIM:], lam_ref, subg_ref, o_ref)


def _attn_online_kernel(it_ref, jt_ref, q_ref, k_ref, v_ref, bias_ref, lam_ref, subg_ref, o_ref,
                        q1_s, q2_s, m1_s, m2_s, l1_s, l2_s, acc1_s, acc2_s):
    step = pl.program_id(2)
    i = it_ref[step]
    j = jt_ref[step]

    @pl.when(j == 0)
    def _init():
        q1_s[...], q2_s[...] = _split_maps(q_ref[...])
        for m_s, l_s, acc_s in ((m1_s, l1_s, acc1_s), (m2_s, l2_s, acc2_s)):
            m_s[...] = jnp.full(m_s.shape, NEG_BIG, F32)
            l_s[...] = jnp.zeros(l_s.shape, F32)
            acc_s[...] = jnp.zeros(acc_s.shape, F32)

    def update(bias):
        k = k_ref[...]
        v = v_ref[...]
        contract_last = (((1,), (1,)), ((), ()))
        for q_s, m_s, l_s, acc_s in ((q1_s, m1_s, l1_s, acc1_s), (q2_s, m2_s, l2_s, acc2_s)):
            s = lax.dot_general(q_s[...], k, contract_last, preferred_element_type=F32)
            if bias is not None:
                s = s + bias_ref[bias]
            m_old = m_s[...]
            m_new = jnp.maximum(m_old, jnp.max(s, axis=-1, keepdims=True))
            alpha = jnp.exp2(m_old - m_new)
            p = jnp.exp2(s - m_new)
            l_s[...] = alpha * l_s[...] + jnp.sum(p, axis=-1, keepdims=True)
            acc_s[...] = alpha * acc_s[...] + jnp.dot(p.astype(BF16), v, preferred_element_type=F32)
            m_s[...] = m_new

    @pl.when(j == i)
    def _diag():
        update(0)

    @pl.when(j == i - 1)
    def _prev():
        update(1)

    @pl.when(j < i - 1)
    def _far():
        update(None)

    @pl.when(j == i)
    def _finalize():
        _attn_finalize(acc1_s[...], l1_s[...], acc2_s[...], l2_s[...], lam_ref, subg_ref, o_ref)


def _t5_bucket(dist):
    n = jnp.maximum(dist, 0)
    max_exact = NUM_BUCKETS // 2
    scaled = jnp.log(jnp.maximum(n, 1).astype(F32) / max_exact) / math.log(MAX_DISTANCE / max_exact)
    large = max_exact + (scaled * (NUM_BUCKETS - max_exact)).astype(jnp.int32)
    large = jnp.minimum(large, NUM_BUCKETS - 1)
    return jnp.where(n < max_exact, n, large)


def _bias_tiles(rel_bias, tile):
    blk = MAX_DISTANCE
    assert tile % blk == 0
    nb = tile // blk
    table = (rel_bias - rel_bias[NUM_BUCKETS - 1]).astype(F32) * LOG2E
    r = jnp.arange(blk, dtype=jnp.int32)
    d0 = r[:, None] - r[None, :]

    def lookup(dist):
        onehot = (_t5_bucket(dist)[..., None] == jnp.arange(NUM_BUCKETS, dtype=jnp.int32)).astype(F32)
        return jnp.einsum('qkn,nh->hqk', onehot, table, precision=lax.Precision.HIGHEST)

    on_diag = jnp.where(d0[None] >= 0, lookup(d0), NEG_BIG)
    sub_diag = lookup(d0 + blk)
    zeros = jnp.zeros_like(sub_diag)
    masked = jnp.full_like(sub_diag, NEG_BIG)

    def assemble(pick):
        return jnp.concatenate(
            [jnp.concatenate([pick(bi, bj) for bj in range(nb)], axis=-1) for bi in range(nb)], axis=-2)

    diag_tile = assemble(lambda bi, bj: on_diag if bi == bj else sub_diag if bi == bj + 1 else zeros if bi > bj else masked)
    prev_tile = assemble(lambda bi, bj: sub_diag if (bi == 0 and bj == nb - 1) else zeros)
    return jnp.stack([diag_tile, prev_tile], axis=1)


def _attn_bounded(qn, kn, v, bias, lam_vecs, sub_g):
    b, s, _ = qn.shape
    tile = min(ATTN_TILE, s)
    q_map = lambda bb, h, i: (bb, i, h)
    kv_map = lambda bb, h, i: (bb, 0, h)
    return pl.pallas_call(
        _attn_bounded_kernel,
        grid=(b, ATTN_HEADS, s // tile),
        in_specs=[pl.BlockSpec((None, tile, ATTN_V_DIM), q_map),
                  pl.BlockSpec((None, s, ATTN_V_DIM), kv_map),
                  pl.BlockSpec((None, s, ATTN_V_DIM), kv_map),
                  pl.BlockSpec((None, 2, tile, tile), lambda bb, h, i: (h, 0, 0, 0)),
                  pl.BlockSpec((4, ATTN_HEAD_DIM), lambda bb, h, i: (0, 0)),
                  pl.BlockSpec((1, ATTN_V_DIM), lambda bb, h, i: (0, 0))],
        out_specs=pl.BlockSpec((None, tile, ATTN_V_DIM), q_map),
        out_shape=jax.ShapeDtypeStruct((b, s, V_COLS), BF16),
        scratch_shapes=[pltpu.VMEM((2 * tile, ATTN_V_DIM), BF16), pltpu.VMEM((s, 2 * ATTN_V_DIM), BF16),
                        pltpu.VMEM((2 * tile, 2 * ATTN_V_DIM), F32)],
        compiler_params=_params(("arbitrary", "arbitrary", "arbitrary")),
        name="diff_attn_bounded",
    )(qn, kn, v, bias, lam_vecs, sub_g)


def _attn_online(qn, kn, v, bias, lam_vecs, sub_g):
    b, s, _ = qn.shape
    tile = min(ATTN_TILE, s)
    nq = s // tile
    it = jnp.asarray([i for i in range(nq) for _ in range(i + 1)], jnp.int32)
    jt = jnp.asarray([j for i in range(nq) for j in range(i + 1)], jnp.int32)
    q_map = lambda bb, h, st, it_r, jt_r: (bb, it_r[st], h)
    kv_map = lambda bb, h, st, it_r, jt_r: (bb, jt_r[st], h)
    grid_spec = pltpu.PrefetchScalarGridSpec(
        num_scalar_prefetch=2,
        grid=(b, ATTN_HEADS, int(it.shape[0])),
        in_specs=[pl.BlockSpec((None, tile, ATTN_V_DIM), q_map),
                  pl.BlockSpec((None, tile, ATTN_V_DIM), kv_map),
                  pl.BlockSpec((None, tile, ATTN_V_DIM), kv_map),
                  pl.BlockSpec((None, 2, tile, tile), lambda bb, h, st, it_r, jt_r: (h, 0, 0, 0)),
                  pl.BlockSpec((4, ATTN_HEAD_DIM), lambda bb, h, st, it_r, jt_r: (0, 0)),
                  pl.BlockSpec((1, ATTN_V_DIM), lambda bb, h, st, it_r, jt_r: (0, 0))],
        out_specs=pl.BlockSpec((None, tile, ATTN_V_DIM), q_map),
        scratch_shapes=[pltpu.VMEM((tile, ATTN_V_DIM), BF16), pltpu.VMEM((tile, ATTN_V_DIM), BF16),
                        pltpu.VMEM((tile, 1), F32), pltpu.VMEM((tile, 1), F32),
                        pltpu.VMEM((tile, 1), F32), pltpu.VMEM((tile, 1), F32),
                        pltpu.VMEM((tile, ATTN_V_DIM), F32), pltpu.VMEM((tile, ATTN_V_DIM), F32)],
    )
    return pl.pallas_call(
        _attn_online_kernel,
        grid_spec=grid_spec,
        out_shape=jax.ShapeDtypeStruct((b, s, V_COLS), BF16),
        compiler_params=_params(("arbitrary", "arbitrary", "arbitrary")),
        name="diff_attn_online",
    )(it, jt, qn, kn, v, bias, lam_vecs, sub_g)


def _diff_attention(qn, kn, v, rel_bias, q_gain, k_gain, lam_vecs, sub_g):
    tile = min(ATTN_TILE, qn.shape[1])
    bias = _bias_tiles(rel_bias, tile)
    spread = jnp.max(jnp.abs(rel_bias - rel_bias[NUM_BUCKETS - 1]))
    bound = LOG2E * (1.05 * math.sqrt(ATTN_HEAD_DIM) * jnp.max(jnp.abs(q_gain)) * jnp.max(jnp.abs(k_gain)) + spread)
    args = (qn, kn, v, bias, lam_vecs, sub_g)
    return lax.cond(bound < EXP2_SAFE_BOUND, lambda a: _attn_bounded(*a), lambda a: _attn_online(*a), args)


def _ssd_kernel(z_ref, xbc_ref, dt_ref, alog_ref, dskip_ref, ng_ref, y_ref, state_s):
    L = SSM_CHUNK

    @pl.when(pl.program_id(1) == 0)
    def _reset():
        state_s[...] = jnp.zeros(state_s.shape, F32)

    xs = xbc_ref[:, :D_INNER].astype(F32)
    bm = xbc_ref[:, D_INNER:D_INNER + SSM_GROUPS * D_STATE]
    cm = xbc_ref[:, D_INNER + SSM_GROUPS * D_STATE:]

    dt = dt_ref[...]
    a = dt * (-jnp.exp(alog_ref[...]))
    row = lax.broadcasted_iota(jnp.int32, (L, L), 0)
    col = lax.broadcasted_iota(jnp.int32, (L, L), 1)
    causal = row >= col
    tril = jnp.where(causal, 1.0, 0.0).astype(F32)
    acum = jnp.dot(tril, a, preferred_element_type=F32, precision=lax.Precision.HIGHEST)
    acum_t = acum.T
    a_end = acum[L - 1:L, :]
    w_end = jnp.exp(a_end - acum) * dt
    e_acum = jnp.exp(acum)
    e_end = jnp.exp(a_end)
    lane = lax.broadcasted_iota(jnp.int32, (L, LANES), 1)
    low = lane < SSM_HEAD_DIM

    def pair_cols(arr, h0):
        return jnp.where(low, arr[:, h0:h0 + 1], arr[:, h0 + 1:h0 + 2])

    contract_last = (((1,), (1,)), ((), ()))
    contract_first = (((0,), (0,)), ((), ()))
    y_parts = []
    for g in range(SSM_GROUPS):
        bg = bm[:, g * D_STATE:(g + 1) * D_STATE]
        cg = cm[:, g * D_STATE:(g + 1) * D_STATE]
        cb = lax.dot_general(cg, bg, contract_last, preferred_element_type=F32)
        st = state_s[g]
        y_off = jnp.dot(cg, st.astype(BF16), preferred_element_type=F32)
        xw_parts = []
        dec_parts = []
        for pr in range(SSM_HEADS_PER_GROUP // 2):
            h0 = g * SSM_HEADS_PER_GROUP + 2 * pr
            ch = slice(h0 * SSM_HEAD_DIM, (h0 + 2) * SSM_HEAD_DIM)
            x_pair = xs[:, ch]
            xdt = (x_pair * pair_cols(dt, h0)).astype(BF16)
            yd = []
            for hh in (h0, h0 + 1):
                seg = acum[:, hh:hh + 1] - acum_t[hh:hh + 1, :]
                decay = jnp.exp(jnp.where(causal, seg, NEG_BIG))
                yd.append(jnp.dot((cb * decay).astype(BF16), xdt, preferred_element_type=F32))
            y_diag = jnp.where(low, yd[0], yd[1])
            off = y_off[:, 2 * pr * SSM_HEAD_DIM:(2 * pr + 2) * SSM_HEAD_DIM]
            y_parts.append(y_diag + off * pair_cols(e_acum, h0))
            xw_parts.append(x_pair * pair_cols(w_end, h0))
            dec_parts.append(jnp.where(low[0:1], e_end[:, h0:h0 + 1], e_end[:, h0 + 1:h0 + 2]))
        xw = jnp.concatenate(xw_parts, axis=-1).astype(BF16)
        dec = jnp.concatenate(dec_parts, axis=-1)
        state_s[g] = st * dec + lax.dot_general(bg, xw, contract_first, preferred_element_type=F32)
    y = (jnp.concatenate(y_parts, axis=-1) + dskip_ref[...] * xs) * z_ref[...].astype(F32)
    gsz = D_INNER // SSM_GROUPS
    for g in range(SSM_GROUPS):
        sl = slice(g * gsz, (g + 1) * gsz)
        y_ref[:, sl] = (_rms(y[:, sl], SSM_EPS) * ng_ref[:, sl]).astype(BF16)


def _ssd(z, xbc, dt, a_log, d_skip_ch, norm_g):
    b, s, _ = z.shape
    nc = s // SSM_CHUNK
    blk = lambda bb, c: (bb, c, 0)
    fix = lambda bb, c: (0, 0)
    return pl.pallas_call(
        _ssd_kernel,
        grid=(b, nc),
        in_specs=[pl.BlockSpec((None, SSM_CHUNK, D_INNER), blk), pl.BlockSpec((None, SSM_CHUNK, CONV_DIM), blk),
                  pl.BlockSpec((None, SSM_CHUNK, DT_PAD), blk), pl.BlockSpec((1, DT_PAD), fix),
                  pl.BlockSpec((1, D_INNER), fix), pl.BlockSpec((1, D_INNER), fix)],
        out_specs=pl.BlockSpec((None, SSM_CHUNK, D_INNER), blk),
        out_shape=jax.ShapeDtypeStruct((b, s, D_INNER), BF16),
        scratch_shapes=[pltpu.VMEM((SSM_GROUPS, D_STATE, SSM_HEADS_PER_GROUP * SSM_HEAD_DIM), F32)],
        compiler_params=_params(("arbitrary", "arbitrary")),
        name="ssd",
    )(z, xbc, dt, a_log, d_skip_ch, norm_g)


def _round_up(n, m):
    return (n + m - 1) // m * m


def _merge_kernel(x_ref, attn_ref, y_ref, gmix_ref, wg_ref, wao_ref, wso_ref, wout_ref, gffn_ref, wr_ref, br_ref,
                  x1_ref, h2_ref, pos_ref, topw_ref, cnt_ref):
    x = x_ref[...]
    tm = x.shape[0]
    h = (_rms(x, RMS_EPS) * gmix_ref[...]).astype(BF16)
    gates = _sigmoid(jnp.dot(h, wg_ref[...], preferred_element_type=F32))
    attn_out = jnp.dot(attn_ref[...], wao_ref[...], preferred_element_type=F32)
    ssm_out = jnp.dot(y_ref[...], wso_ref[...], preferred_element_type=F32)
    merged = gates[:, :D_MODEL] * attn_out + gates[:, D_MODEL:] * ssm_out
    x1 = x + jnp.dot(merged.astype(BF16), wout_ref[...], preferred_element_type=F32)
    x1_ref[...] = x1
    h2 = _rms(x1, RMS_EPS) * gffn_ref[...]
    h2_ref[...] = h2.astype(BF16)

    logits = lax.dot_general(wr_ref[...], h2, (((1,), (1,)), ((), ())), preferred_element_type=F32,
                             precision=lax.Precision.HIGHEST) + br_ref[...]
    eid = lax.broadcasted_iota(jnp.int32, logits.shape, 0)
    vals, hits = [], []
    member = jnp.zeros(logits.shape, F32)
    work = logits
    for kk in range(TOP_K):
        m = jnp.max(work, axis=0, keepdims=True)
        idx = jnp.min(jnp.where(work == m, eid, N_EXPERTS), axis=0, keepdims=True)
        hit = eid == idx
        vals.append(m)
        hits.append(hit)
        member = jnp.where(hit, 1.0, member)
        work = jnp.where(hit, -jnp.inf, work)
    ex = [jnp.exp(v - vals[0]) for v in vals]
    denom = ex[0] + ex[1] + ex[2] + ex[3]
    for kk in range(TOP_K):
        topw_ref[kk:kk + 1, :] = ex[kk] / denom

    r = lax.broadcasted_iota(jnp.int32, (tm, tm), 0)
    c = lax.broadcasted_iota(jnp.int32, (tm, tm), 1)
    before = jnp.where(r < c, 1.0, 0.0).astype(BF16)
    prefix = jnp.dot(member.astype(BF16), before, preferred_element_type=F32)
    cnt = jnp.sum(member, axis=1, keepdims=True).astype(jnp.int32)
    cnt_al = jnp.bitwise_and(cnt + (ROW_ALIGN - 1), -ROW_ALIGN)
    cnt_al = jnp.broadcast_to(cnt_al, (N_EXPERTS, LANES)).astype(F32)
    er = lax.broadcasted_iota(jnp.int32, (N_EXPERTS, N_EXPERTS), 0)
    ec = lax.broadcasted_iota(jnp.int32, (N_EXPERTS, N_EXPERTS), 1)
    lower = jnp.where(ec < er, 1.0, 0.0).astype(F32)
    run_start = jnp.dot(lower, cnt_al, preferred_element_type=F32, precision=lax.Precision.HIGHEST)[:, 0:1]
    base = prefix + run_start
    for kk in range(TOP_K):
        pos_ref[kk:kk + 1, :] = jnp.sum(jnp.where(hits[kk], base, 0.0), axis=0, keepdims=True).astype(jnp.int32)
    cnt_ref[...] = jnp.broadcast_to(cnt, cnt_ref.shape)


def _merge(x2, attn, y, g_mix, w_gate, w_ao, w_so, w_out, g_ffn, w_r_t, b_r):
    t = x2.shape[0]
    tm = min(ROW_TILE, t)
    row = lambda i: (i, 0)
    colb = lambda i: (0, i)
    fix = lambda i: (0, 0)
    return pl.pallas_call(
        _merge_kernel,
        grid=(t // tm,),
        in_specs=[pl.BlockSpec((tm, D_MODEL), row), pl.BlockSpec((tm, V_COLS), row), pl.BlockSpec((tm, D_INNER), row),
                  pl.BlockSpec((1, D_MODEL), fix), _resident((D_MODEL, 2 * D_MODEL)),
                  _resident((V_COLS, D_MODEL)), _resident((D_INNER, D_MODEL)),
                  _resident((D_MODEL, D_MODEL)), pl.BlockSpec((1, D_MODEL), fix),
                  pl.BlockSpec((N_EXPERTS, D_MODEL), fix), pl.BlockSpec((N_EXPERTS, 1), fix)],
        out_specs=[pl.BlockSpec((tm, D_MODEL), row), pl.BlockSpec((tm, D_MODEL), row),
                   pl.BlockSpec((TOP_K, tm), colb), pl.BlockSpec((TOP_K, tm), colb),
                   pl.BlockSpec((N_EXPERTS, LANES), row)],
        out_shape=[jax.ShapeDtypeStruct((t, D_MODEL), F32), jax.ShapeDtypeStruct((t, D_MODEL), BF16),
                   jax.ShapeDtypeStruct((TOP_K, t), jnp.int32), jax.ShapeDtypeStruct((TOP_K, t), F32),
                   jax.ShapeDtypeStruct((t // tm * N_EXPERTS, LANES), jnp.int32)],
        compiler_params=_params(("arbitrary",)),
        name="merge_router",
    )(x2, attn, y, g_mix, w_gate, w_ao, w_so, w_out, g_ffn, w_r_t, b_r)


def _sorted_rows(tm):
    return _round_up(TOP_K * tm + N_EXPERTS * (ROW_ALIGN - 1), LANES)


def _run_copies(n, max_n, vmem_ref, vmem_off, hbm_ref, hbm_off, sem, to_hbm, wait):
    done = 0
    pieces = [ROW_ALIGN << p for p in range((max_n // ROW_ALIGN).bit_length())]
    for bit in reversed(pieces):
        take = (n & bit) != 0

        @pl.when(take)
        def _piece(bit=bit, done=done):
            v = vmem_ref.at[pl.ds(pl.multiple_of(vmem_off + done, ROW_ALIGN), bit)]
            h = hbm_ref.at[pl.ds(pl.multiple_of(hbm_off + done, ROW_ALIGN), bit)]
            cp = pltpu.make_async_copy(v, h, sem) if to_hbm else pltpu.make_async_copy(h, v, sem)
            cp.wait() if wait else cp.start()

        done = done + jnp.where(take, bit, 0)


def _dispatch_kernel(off_ref, cnt_ref, start_ref, last_ref, h2_ref, pos_ref, xs_hbm, buf_s, zero_s, sem, zsem):
    i = pl.program_id(0)
    tm = h2_ref.shape[0]
    rows = buf_s.shape[0]

    @pl.when(i == 0)
    def _zero_last_blocks():
        zero_s[...] = jnp.zeros(zero_s.shape, BF16)
        for e in range(N_EXPERTS):
            start = pl.multiple_of(last_ref[e], MOE_TILE)
            pltpu.make_async_copy(zero_s, xs_hbm.at[pl.ds(start, MOE_TILE)], zsem).start()
        for e in range(N_EXPERTS):
            pltpu.make_async_copy(zero_s, xs_hbm.at[pl.ds(0, MOE_TILE)], zsem).wait()

        def slack(wait):
            def body(blk, carry):
                cp = pltpu.make_async_copy(zero_s, xs_hbm.at[pl.ds(pl.multiple_of(blk * MOE_TILE, MOE_TILE), MOE_TILE)],
                                           zsem)
                cp.wait() if wait else cp.start()
                return carry
            lax.fori_loop(last_ref[N_EXPERTS], xs_hbm.shape[0] // MOE_TILE, body, 0)

        slack(False)
        slack(True)

    pos = pos_ref[...]
    rid = lax.broadcasted_iota(jnp.int32, (rows, tm), 0)
    sel = jnp.zeros((rows, tm), F32)
    for kk in range(TOP_K):
        sel = sel + jnp.where(rid == pos[kk:kk + 1, :], 1.0, 0.0)
    buf_s[...] = jnp.dot(sel.astype(BF16), h2_ref[...], preferred_element_type=F32).astype(BF16)

    for wait in (False, True):
        for e in range(N_EXPERTS):
            idx = i * N_EXPERTS + e
            _run_copies(cnt_ref[idx], tm, buf_s, start_ref[idx], xs_hbm, off_ref[idx], sem, True, wait)


def _dispatch(run_off, run_cnt, run_start, last_block, h2, pos, n_rows):
    t = h2.shape[0]
    tm = min(ROW_TILE, t)
    grid_spec = pltpu.PrefetchScalarGridSpec(
        num_scalar_prefetch=4,
        grid=(t // tm,),
        in_specs=[pl.BlockSpec((tm, D_MODEL), lambda i, *_: (i, 0)),
                  pl.BlockSpec((TOP_K, tm), lambda i, *_: (0, i))],
        out_specs=pl.BlockSpec(memory_space=pl.ANY),
        scratch_shapes=[pltpu.VMEM((_sorted_rows(tm), D_MODEL), BF16), pltpu.VMEM((MOE_TILE, D_MODEL), BF16),
                        pltpu.SemaphoreType.DMA(()), pltpu.SemaphoreType.DMA(())],
    )
    return pl.pallas_call(
        _dispatch_kernel,
        grid_spec=grid_spec,
        out_shape=jax.ShapeDtypeStruct((n_rows, D_MODEL), BF16),
        compiler_params=_params(("arbitrary",)),
        name="moe_dispatch",
    )(run_off, run_cnt, run_start, last_block, h2, pos)


def _expert_kernel(be_ref, nb_ref, x_ref, w1_ref, b1_ref, w2_ref, b2_ref, y_ref):
    @pl.when(pl.program_id(0) < nb_ref[0])
    def _run():
        gu = jnp.dot(x_ref[...], w1_ref[...], preferred_element_type=F32) + b1_ref[...]
        gate = jnp.minimum(gu[:, :D_EXPERT], SWIGLU_LIMIT)
        up = jnp.clip(gu[:, D_EXPERT:], -SWIGLU_LIMIT, SWIGLU_LIMIT)
        act = (up + 1.0) * (gate * _sigmoid(SWIGLU_ALPHA * gate))
        y = jnp.dot(act.astype(BF16), w2_ref[...], preferred_element_type=F32) + b2_ref[...]
        y_ref[...] = y.astype(BF16)

    @pl.when(pl.program_id(0) >= nb_ref[0])
    def _slack():
        y_ref[...] = jnp.zeros(y_ref.shape, BF16)


def _experts(block_e, n_used, xs, w1, b1, w2, b2):
    n_rows = xs.shape[0]
    nb = n_rows // MOE_TILE
    row_in = lambda i, be, nu: (jnp.minimum(i, nu[0] - 1), 0)
    row = lambda i, be, nu: (i, 0)
    wsel = lambda i, be, nu: (be[i], 0, 0)
    grid_spec = pltpu.PrefetchScalarGridSpec(
        num_scalar_prefetch=2,
        grid=(nb,),
        in_specs=[pl.BlockSpec((MOE_TILE, D_MODEL), row_in),
                  pl.BlockSpec((None, D_MODEL, 2 * D_EXPERT), wsel), pl.BlockSpec((None, 1, 2 * D_EXPERT), wsel),
                  pl.BlockSpec((None, D_EXPERT, D_MODEL), wsel), pl.BlockSpec((None, 1, D_MODEL), wsel)],
        out_specs=pl.BlockSpec((MOE_TILE, D_MODEL), row),
    )
    return pl.pallas_call(
        _expert_kernel,
        grid_spec=grid_spec,
        out_shape=jax.ShapeDtypeStruct((n_rows, D_MODEL), BF16),
        compiler_params=_params(("arbitrary",)),
        name="moe_experts",
    )(block_e, n_used, xs, w1, b1, w2, b2)


def _combine_kernel(off_ref, cnt_ref, start_ref, y_hbm, x1_ref, pos_ref, w_ref, o_ref, buf_s, sem):
    i = pl.program_id(0)
    tm = x1_ref.shape[0]
    rows = buf_s.shape[0]

    @pl.when(i == 0)
    def _clear():
        buf_s[...] = jnp.zeros(buf_s.shape, BF16)

    def copies(wait):
        for e in range(N_EXPERTS):
            idx = i * N_EXPERTS + e
            _run_copies(cnt_ref[idx], tm, buf_s, start_ref[idx], y_hbm, off_ref[idx], sem, False, wait)

    copies(False)
    pos = pos_ref[...]
    w = w_ref[...]
    cid = lax.broadcasted_iota(jnp.int32, (tm, rows), 1)
    wsel = jnp.zeros((tm, rows), F32)
    for kk in range(TOP_K):
        wsel = wsel + jnp.where(cid == pos[:, kk:kk + 1], w[:, kk:kk + 1], 0.0)
    w_hi = wsel.astype(BF16)
    w_lo = (wsel - w_hi.astype(F32)).astype(BF16)
    copies(True)
    ys = buf_s[...]
    o_ref[...] = (x1_ref[...] + jnp.dot(w_hi, ys, preferred_element_type=F32)
                  + jnp.dot(w_lo, ys, preferred_element_type=F32))


def _combine(run_off, run_cnt, run_start, y, x1, pos_tok, w_tok):
    t = x1.shape[0]
    tm = min(ROW_TILE, t)
    grid_spec = pltpu.PrefetchScalarGridSpec(
        num_scalar_prefetch=3,
        grid=(t // tm,),
        in_specs=[pl.BlockSpec(memory_space=pl.ANY),
                  pl.BlockSpec((tm, D_MODEL), lambda i, *_: (i, 0)),
                  pl.BlockSpec((tm, TOP_K), lambda i, *_: (i, 0)), pl.BlockSpec((tm, TOP_K), lambda i, *_: (i, 0))],
        out_specs=pl.BlockSpec((tm, D_MODEL), lambda i, *_: (i, 0)),
        scratch_shapes=[pltpu.VMEM((_sorted_rows(tm), D_MODEL), BF16), pltpu.SemaphoreType.DMA(())],
    )
    return pl.pallas_call(
        _combine_kernel,
        grid_spec=grid_spec,
        out_shape=jax.ShapeDtypeStruct((t, D_MODEL), F32),
        compiler_params=_params(("arbitrary",)),
        name="moe_combine",
    )(run_off, run_cnt, run_start, y, x1, pos_tok, w_tok)


def kernel(x, g_mix, w_in, q_norm_g, k_norm_g, lambda_q1, lambda_k1, lambda_q2, lambda_k2, attn_sub_g, rel_bias,
           w_attn_o, conv_w, conv_b, dt_bias, a_log, d_skip, ssm_norm_g, w_ssm_o, w_out, g_ffn, w_router, b_router,
           w1, b1, w2, b2):
    b, s, d = x.shape
    t = b * s
    l = 0
    x2 = x.reshape(t, d)

    w = w_in[l]
    c0 = Q_COLS + K_COLS + V_COLS
    c1 = c0 + D_INNER + CONV_DIM
    w_qkv = w[:, :c0].astype(BF16)
    w_dt = jnp.pad(w[:, c1:c1 + SSM_HEADS], ((0, 0), (0, DT_PAD - SSM_HEADS)))
    w_ssm = jnp.concatenate([w[:, c0:c1], w_dt], axis=1).astype(BF16)
    w_gate = w[:, c1 + SSM_HEADS:].astype(BF16)
    n_hd = Q_COLS // ATTN_HEAD_DIM
    gq = (jnp.tile(q_norm_g[l], n_hd) * (ATTN_HEAD_DIM ** -0.5 * LOG2E)).reshape(1, Q_COLS)
    gk = jnp.tile(k_norm_g[l], n_hd).reshape(1, K_COLS)
    gm = g_mix[l].reshape(1, d)

    qn, kn, v = _qkv_proj(x2, gm, w_qkv, gq, gk)
    pad_h = (0, DT_PAD - SSM_HEADS)
    z, xbc, dt = _ssm_proj(x2, gm, w_ssm, conv_w[l], conv_b[l].reshape(1, -1),
                           jnp.pad(dt_bias[l], pad_h).reshape(1, -1), s)

    lam_vecs = jnp.stack([lambda_q1[l], lambda_k1[l], lambda_q2[l], lambda_k2[l]]).astype(F32)
    attn = _diff_attention(qn.reshape(b, s, -1), kn.reshape(b, s, -1), v.reshape(b, s, -1), rel_bias, q_norm_g[l],
                           k_norm_g[l], lam_vecs, attn_sub_g[l].reshape(1, ATTN_V_DIM))

    y = _ssd(z.reshape(b, s, -1), xbc.reshape(b, s, -1), dt.reshape(b, s, -1),
             jnp.pad(a_log[l], pad_h).reshape(1, -1), jnp.repeat(d_skip[l], SSM_HEAD_DIM).reshape(1, -1),
             ssm_norm_g[l].reshape(1, -1))

    x1, h2, pos, top_w, tile_cnt = _merge(
        x2, attn.reshape(t, -1), y.reshape(t, -1), gm, w_gate, w_attn_o[l].astype(BF16), w_ssm_o[l].astype(BF16),
        w_out[l].astype(BF16), g_ffn[l].reshape(1, d), w_router[l].T, b_router[l].reshape(-1, 1))

    n_tiles = t // min(ROW_TILE, t)
    cnt = tile_cnt.reshape(n_tiles, N_EXPERTS, LANES)[:, :, 0]
    cnt_al = _round_up(cnt, ROW_ALIGN)
    run_start = jnp.cumsum(cnt_al, axis=1) - cnt_al
    padded = _round_up(jnp.sum(cnt_al, axis=0), MOE_TILE)
    end_pad = jnp.cumsum(padded)
    run_off = (end_pad - padded)[None, :] + jnp.cumsum(cnt_al, axis=0) - cnt_al
    n_rows = _round_up(t * TOP_K + n_tiles * N_EXPERTS * (ROW_ALIGN - 1), MOE_TILE) + N_EXPERTS * MOE_TILE
    block_start = jnp.arange(n_rows // MOE_TILE, dtype=jnp.int32) * MOE_TILE
    block_e = jnp.minimum(jnp.sum(block_start[:, None] >= end_pad[None, :], axis=1), N_EXPERTS - 1).astype(jnp.int32)
    n_used = (end_pad[-1:] // MOE_TILE).astype(jnp.int32)
    last_block = jnp.concatenate([jnp.maximum(end_pad - MOE_TILE, 0), end_pad[-1:] // MOE_TILE]).astype(jnp.int32)
    tables = [a.reshape(-1).astype(jnp.int32) for a in (run_off, cnt_al, run_start)]

    xs = _dispatch(*tables, last_block, h2, pos, n_rows)
    ys = _experts(block_e, n_used, xs, w1[l].astype(BF16), b1[l][:, None, :], w2[l].astype(BF16), b2[l][:, None, :])
    out = _combine(*tables, ys, x1, pos.T, top_w.T)
    return out.reshape(b, s, d)
```

```python
import functools
import math

import jax
import jax.numpy as jnp
from jax import lax
from jax.experimental import pallas as pl
from jax.experimental.pallas import tpu as pltpu

F32 = jnp.float32
BF16 = jnp.bfloat16

D_MODEL = 1024
ATTN_HEADS = 8
ATTN_HEAD_DIM = 64
ATTN_V_DIM = 2 * ATTN_HEAD_DIM
LAMBDA_INIT = 0.8 - 0.6 * math.exp(-0.3 * 0)
NUM_BUCKETS = 32
MAX_DISTANCE = 128
D_INNER = 2 * D_MODEL
SSM_HEAD_DIM = 64
SSM_HEADS = D_INNER // SSM_HEAD_DIM
SSM_GROUPS = 8
SSM_HEADS_PER_GROUP = SSM_HEADS // SSM_GROUPS
D_STATE = 128
CONV_WIDTH = 4
SSM_CHUNK = 128
CONV_DIM = D_INNER + 2 * SSM_GROUPS * D_STATE
N_EXPERTS = 32
TOP_K = 4
D_EXPERT = D_MODEL
SWIGLU_LIMIT = 7.0
SWIGLU_ALPHA = 1.702
RMS_EPS = 1e-6
SSM_EPS = 1e-5
Q_COLS = ATTN_HEADS * 2 * ATTN_HEAD_DIM
K_COLS = Q_COLS
V_COLS = ATTN_HEADS * ATTN_V_DIM

LANES = 128
MXU_DIM = 256
DT_PAD = LANES
NEG_BIG = -1e30
LOG2E = math.log2(math.e)
EXP2_SAFE_BOUND = 80.0
VMEM_LIMIT = 56 * 1024 * 1024

ROW_TILE = 512
ATTN_TILE = 512
MOE_TILE = 256
ROW_ALIGN = 16


def _rms(x, eps):
    return x * lax.rsqrt(jnp.mean(x * x, axis=-1, keepdims=True) + eps)


def _sigmoid(x):
    return 1.0 / (1.0 + jnp.exp(-x))


def _params(sem):
    return pltpu.CompilerParams(dimension_semantics=sem, vmem_limit_bytes=VMEM_LIMIT)


def _resident(shape):
    return pl.BlockSpec(shape, lambda *_: (0,) * len(shape), pipeline_mode=pl.Buffered(1))


def _qkv_kernel(x_ref, g_ref, w_ref, gq_ref, gk_ref, q_ref, k_ref, v_ref):
    h = (_rms(x_ref[...], RMS_EPS) * g_ref[...]).astype(BF16)
    qkv = jnp.dot(h, w_ref[...], preferred_element_type=F32)
    r = lax.broadcasted_iota(jnp.int32, (MXU_DIM, MXU_DIM), 0) // ATTN_HEAD_DIM
    c = lax.broadcasted_iota(jnp.int32, (MXU_DIM, MXU_DIM), 1) // ATTN_HEAD_DIM
    group_ones = jnp.where(r == c, 1.0, 0.0).astype(BF16)

    def head_norm(t, gain_ref, out_ref):
        for cc in range(Q_COLS // MXU_DIM):
            sl = slice(cc * MXU_DIM, (cc + 1) * MXU_DIM)
            tc = t[:, sl]
            ss = jnp.dot((tc * tc).astype(BF16), group_ones, preferred_element_type=F32)
            out_ref[:, sl] = (tc * lax.rsqrt(ss * (1.0 / ATTN_HEAD_DIM) + RMS_EPS) * gain_ref[:, sl]).astype(BF16)

    head_norm(qkv[:, :Q_COLS], gq_ref, q_ref)
    head_norm(qkv[:, Q_COLS:Q_COLS + K_COLS], gk_ref, k_ref)
    v_ref[...] = qkv[:, Q_COLS + K_COLS:].astype(BF16)


def _qkv_proj(x2, g_mix, w_qkv, gq, gk):
    t = x2.shape[0]
    tm = min(ROW_TILE, t)
    row = lambda i: (i, 0)
    fix = lambda i: (0, 0)
    out = jax.ShapeDtypeStruct((t, D_MODEL), BF16)
    return pl.pallas_call(
        _qkv_kernel,
        grid=(t // tm,),
        in_specs=[pl.BlockSpec((tm, D_MODEL), row), pl.BlockSpec((1, D_MODEL), fix),
                  _resident((D_MODEL, 3 * D_MODEL)), pl.BlockSpec((1, D_MODEL), fix),
                  pl.BlockSpec((1, D_MODEL), fix)],
        out_specs=[pl.BlockSpec((tm, D_MODEL), row)] * 3,
        out_shape=[out, out, out],
        compiler_params=_params(("arbitrary",)),
        name="qkv_proj",
    )(x2, g_mix, w_qkv, gq, gk)


CONV_COLS = 1024
HIST_ROWS = 8


def _ssm_proj_kernel(x_ref, g_ref, w_ref, cw_ref, cb_ref, dtb_ref, z_ref, xbc_ref, dt_ref, tail_s,
                     *, tiles_per_seq):
    tm = x_ref.shape[0]

    @pl.when(pl.program_id(0) % tiles_per_seq == 0)
    def _sequence_start():
        tail_s[...] = jnp.zeros(tail_s.shape, F32)

    h = (_rms(x_ref[...], RMS_EPS) * g_ref[...]).astype(BF16)
    zf = jnp.dot(h, w_ref[:, :D_INNER], preferred_element_type=F32)
    z_ref[...] = (zf * _sigmoid(zf)).astype(BF16)
    for c in range(CONV_DIM // CONV_COLS):
        cols = slice(c * CONV_COLS, (c + 1) * CONV_COLS)
        p = jnp.dot(h, w_ref[:, D_INNER + c * CONV_COLS:D_INNER + (c + 1) * CONV_COLS], preferred_element_type=F32)
        prev = tail_s[:, cols]
        tail_s[:, cols] = p[tm - HIST_ROWS:, :]
        groups = jnp.concatenate([prev, p], axis=0).reshape(tm // HIST_ROWS + 1, HIST_ROWS, CONV_COLS)
        sub = lax.broadcasted_iota(jnp.int32, (tm // HIST_ROWS, HIST_ROWS, CONV_COLS), 1)
        conv = cb_ref[:, cols] + cw_ref[CONV_WIDTH - 1:CONV_WIDTH, cols] * p
        for back in range(1, CONV_WIDTH):
            rolled = pltpu.roll(groups, back, axis=1)
            shifted = jnp.where(sub < back, rolled[:-1], rolled[1:]).reshape(tm, CONV_COLS)
            tap = CONV_WIDTH - 1 - back
            conv = conv + cw_ref[tap:tap + 1, cols] * shifted
        xbc_ref[:, cols] = (conv * _sigmoid(conv)).astype(BF16)
    dtl = jnp.dot(h, w_ref[:, D_INNER + CONV_DIM:], preferred_element_type=F32) + dtb_ref[...]
    dt_ref[...] = jnp.maximum(dtl, 0.0) + jnp.log(1.0 + jnp.exp(-jnp.abs(dtl)))


def _ssm_proj(x2, g_mix, w_ssm, conv_w, conv_b, dt_bias, seq_len):
    t = x2.shape[0]
    tm = min(ROW_TILE, seq_len)
    assert seq_len % tm == 0
    ncol = D_INNER + CONV_DIM + DT_PAD
    row = lambda i: (i, 0)
    fix = lambda i: (0, 0)
    return pl.pallas_call(
        functools.partial(_ssm_proj_kernel, tiles_per_seq=seq_len // tm),
        grid=(t // tm,),
        in_specs=[pl.BlockSpec((tm, D_MODEL), row), pl.BlockSpec((1, D_MODEL), fix),
                  _resident((D_MODEL, ncol)), pl.BlockSpec((CONV_WIDTH, CONV_DIM), fix),
                  pl.BlockSpec((1, CONV_DIM), fix), pl.BlockSpec((1, DT_PAD), fix)],
        out_specs=[pl.BlockSpec((tm, D_INNER), row), pl.BlockSpec((tm, CONV_DIM), row),
                   pl.BlockSpec((tm, DT_PAD), row)],
        out_shape=[jax.ShapeDtypeStruct((t, D_INNER), BF16), jax.ShapeDtypeStruct((t, CONV_DIM), BF16),
                   jax.ShapeDtypeStruct((t, DT_PAD), F32)],
        scratch_shapes=[pltpu.VMEM((HIST_ROWS, CONV_DIM), F32)],
        compiler_params=_params(("arbitrary",)),
        name="ssm_proj",
    )(x2, g_mix, w_ssm, conv_w, conv_b, dt_bias)


def _split_maps(q):
    lane = lax.broadcasted_iota(jnp.int32, q.shape, 1)
    zero = jnp.zeros_like(q)
    return jnp.where(lane < ATTN_HEAD_DIM, q, zero), jnp.where(lane >= ATTN_HEAD_DIM, q, zero)


def _attn_finalize(acc1, l1, acc2, l2, lam_ref, subg_ref, o_ref):
    lam_v = lam_ref[...]
    lam = (jnp.exp(jnp.sum(lam_v[0:1] * lam_v[1:2], axis=-1, keepdims=True))
           - jnp.exp(jnp.sum(lam_v[2:3] * lam_v[3:4], axis=-1, keepdims=True)) + LAMBDA_INIT)
    o = acc1 / l1 - lam * (acc2 / l2)
    o_ref[...] = (_rms(o, RMS_EPS) * subg_ref[...] * (1.0 - LAMBDA_INIT)).astype(BF16)


def _attn_bounded_kernel(q_ref, k_ref, v_ref, bias_ref, lam_ref, subg_ref, o_ref, qq_s, vv_s, acc_s, s_s):
    i = pl.program_id(2)
    tq = q_ref.shape[0]
    tk = tq

    @pl.when(i == 0)
    def _extend_v():
        vv_s[:, :ATTN_V_DIM] = v_ref[...]
        vv_s[:, ATTN_V_DIM:] = jnp.ones((vv_s.shape[0], ATTN_V_DIM), BF16)

    q1, q2 = _split_maps(q_ref[...])
    qq_s[0:tq, :] = q1
    qq_s[tq:2 * tq, :] = q2
    acc_s[...] = jnp.zeros(acc_s.shape, F32)
    contract_last = (((1,), (1,)), ((), ()))

    def logits(j, kind):
        rows = pl.ds(pl.multiple_of(j * tk, tk), tk)
        s = lax.dot_general(qq_s[...], k_ref[rows, :], contract_last, preferred_element_type=F32)
        if kind is not None:
            b = bias_ref[kind]
            s = jnp.concatenate([s[0:tq] + b, s[tq:2 * tq] + b], axis=0)
        return s

    def accumulate(j, slot):
        rows = pl.ds(pl.multiple_of(j * tk, tk), tk)
        acc_s[...] += jnp.dot(jnp.exp2(s_s[slot]).astype(BF16), vv_s[rows, :], preferred_element_type=F32)

    def step(t, slot, next_kind):
        s_s[1 - slot] = logits(t + 1, next_kind)
        accumulate(t, slot)

    odd = (i & 1) == 1

    @pl.when(i == 0)
    def _first_is_diag():
        s_s[0] = logits(0, 0)

    @pl.when(i == 1)
    def _first_is_prev():
        s_s[1] = logits(0, 1)

    @pl.when(jnp.logical_and(i >= 2, jnp.logical_not(odd)))
    def _first_far_even():
        s_s[0] = logits(0, None)

    @pl.when(jnp.logical_and(i >= 2, odd))
    def _first_far_odd():
        s_s[1] = logits(0, None)

    @pl.when(jnp.logical_and(i >= 3, odd))
    def _unpaired_far_step():
        step(0, 1, None)

    def far_pair(pp, carry):
        t = (i & 1) + 2 * pp
        step(t, 0, None)
        step(t + 1, 1, None)
        return carry

    lax.fori_loop(0, jnp.maximum(i - 2, 0) // 2, far_pair, 0)

    @pl.when(i >= 2)
    def _next_is_prev():
        step(i - 2, 0, 1)

    @pl.when(i >= 1)
    def _next_is_diag():
        step(i - 1, 1, 0)

    accumulate(i, 0)
    _attn_finalize(acc_s[0:tq, :ATTN_V_DIM], acc_s[0:tq, ATTN_V_DIM:], acc_s[tq:2 * tq, :ATTN_V_DIM],
                   acc_s[tq:2 * tq, ATTN_V_DIM:], lam_ref, subg_ref, o_ref)


def _attn_online_kernel(it_ref, jt_ref, q_ref, k_ref, v_ref, bias_ref, lam_ref, subg_ref, o_ref,
                        q1_s, q2_s, m1_s, m2_s, l1_s, l2_s, acc1_s, acc2_s):
    step = pl.program_id(2)
    i = it_ref[step]
    j = jt_ref[step]

    @pl.when(j == 0)
    def _init():
        q1_s[...], q2_s[...] = _split_maps(q_ref[...])
        for m_s, l_s, acc_s in ((m1_s, l1_s, acc1_s), (m2_s, l2_s, acc2_s)):
            m_s[...] = jnp.full(m_s.shape, NEG_BIG, F32)
            l_s[...] = jnp.zeros(l_s.shape, F32)
            acc_s[...] = jnp.zeros(acc_s.shape, F32)

    def update(bias):
        k = k_ref[...]
        v = v_ref[...]
        contract_last = (((1,), (1,)), ((), ()))
        for q_s, m_s, l_s, acc_s in ((q1_s, m1_s, l1_s, acc1_s), (q2_s, m2_s, l2_s, acc2_s)):
            s = lax.dot_general(q_s[...], k, contract_last, preferred_element_type=F32)
            if bias is not None:
                s = s + bias_ref[bias]
            m_old = m_s[...]
            m_new = jnp.maximum(m_old, jnp.max(s, axis=-1, keepdims=True))
            alpha = jnp.exp2(m_old - m_new)
            p = jnp.exp2(s - m_new)
            l_s[...] = alpha * l_s[...] + jnp.sum(p, axis=-1, keepdims=True)
            acc_s[...] = alpha * acc_s[...] + jnp.dot(p.astype(BF16), v, preferred_element_type=F32)
            m_s[...] = m_new

    @pl.when(j == i)
    def _diag():
        update(0)

    @pl.when(j == i - 1)
    def _prev():
        update(1)

    @pl.when(j < i - 1)
    def _far():
        update(None)

    @pl.when(j == i)
    def _finalize():
        _attn_finalize(acc1_s[...], l1_s[...], acc2_s[...], l2_s[...], lam_ref, subg_ref, o_ref)


def _t5_bucket(dist):
    n = jnp.maximum(dist, 0)
    max_exact = NUM_BUCKETS // 2
    scaled = jnp.log(jnp.maximum(n, 1).astype(F32) / max_exact) / math.log(MAX_DISTANCE / max_exact)
    large = max_exact + (scaled * (NUM_BUCKETS - max_exact)).astype(jnp.int32)
    large = jnp.minimum(large, NUM_BUCKETS - 1)
    return jnp.where(n < max_exact, n, large)


def _bias_tiles(rel_bias, tile):
    blk = MAX_DISTANCE
    assert tile % blk == 0
    nb = tile // blk
    table = (rel_bias - rel_bias[NUM_BUCKETS - 1]).astype(F32) * LOG2E
    r = jnp.arange(blk, dtype=jnp.int32)
    d0 = r[:, None] - r[None, :]

    def lookup(dist):
        onehot = (_t5_bucket(dist)[..., None] == jnp.arange(NUM_BUCKETS, dtype=jnp.int32)).astype(F32)
        return jnp.einsum('qkn,nh->hqk', onehot, table, precision=lax.Precision.HIGHEST)

    on_diag = jnp.where(d0[None] >= 0, lookup(d0), NEG_BIG)
    sub_diag = lookup(d0 + blk)
    zeros = jnp.zeros_like(sub_diag)
    masked = jnp.full_like(sub_diag, NEG_BIG)

    def assemble(pick):
        return jnp.concatenate(
            [jnp.concatenate([pick(bi, bj) for bj in range(nb)], axis=-1) for bi in range(nb)], axis=-2)

    diag_tile = assemble(lambda bi, bj: on_diag if bi == bj else sub_diag if bi == bj + 1 else zeros if bi > bj else masked)
    prev_tile = assemble(lambda bi, bj: sub_diag if (bi == 0 and bj == nb - 1) else zeros)
    return jnp.stack([diag_tile, prev_tile], axis=1)


def _attn_bounded(qn, kn, v, bias, lam_vecs, sub_g):
    b, s, _ = qn.shape
    tile = min(ATTN_TILE, s)
    q_map = lambda bb, h, i: (bb, i, h)
    kv_map = lambda bb, h, i: (bb, 0, h)
    return pl.pallas_call(
        _attn_bounded_kernel,
        grid=(b, ATTN_HEADS, s // tile),
        in_specs=[pl.BlockSpec((None, tile, ATTN_V_DIM), q_map),
                  pl.BlockSpec((None, s, ATTN_V_DIM), kv_map),
                  pl.BlockSpec((None, s, ATTN_V_DIM), kv_map),
                  pl.BlockSpec((None, 2, tile, tile), lambda bb, h, i: (h, 0, 0, 0)),
                  pl.BlockSpec((4, ATTN_HEAD_DIM), lambda bb, h, i: (0, 0)),
                  pl.BlockSpec((1, ATTN_V_DIM), lambda bb, h, i: (0, 0))],
        out_specs=pl.BlockSpec((None, tile, ATTN_V_DIM), q_map),
        out_shape=jax.ShapeDtypeStruct((b, s, V_COLS), BF16),
        scratch_shapes=[pltpu.VMEM((2 * tile, ATTN_V_DIM), BF16), pltpu.VMEM((s, 2 * ATTN_V_DIM), BF16),
                        pltpu.VMEM((2 * tile, 2 * ATTN_V_DIM), F32), pltpu.VMEM((2, 2 * tile, tile), F32)],
        compiler_params=_params(("arbitrary", "arbitrary", "arbitrary")),
        name="diff_attn_bounded",
    )(qn, kn, v, bias, lam_vecs, sub_g)


def _attn_online(qn, kn, v, bias, lam_vecs, sub_g):
    b, s, _ = qn.shape
    tile = min(ATTN_TILE, s)
    nq = s // tile
    it = jnp.asarray([i for i in range(nq) for _ in range(i + 1)], jnp.int32)
    jt = jnp.asarray([j for i in range(nq) for j in range(i + 1)], jnp.int32)
    q_map = lambda bb, h, st, it_r, jt_r: (bb, it_r[st], h)
    kv_map = lambda bb, h, st, it_r, jt_r: (bb, jt_r[st], h)
    grid_spec = pltpu.PrefetchScalarGridSpec(
        num_scalar_prefetch=2,
        grid=(b, ATTN_HEADS, int(it.shape[0])),
        in_specs=[pl.BlockSpec((None, tile, ATTN_V_DIM), q_map),
                  pl.BlockSpec((None, tile, ATTN_V_DIM), kv_map),
                  pl.BlockSpec((None, tile, ATTN_V_DIM), kv_map),
                  pl.BlockSpec((None, 2, tile, tile), lambda bb, h, st, it_r, jt_r: (h, 0, 0, 0)),
                  pl.BlockSpec((4, ATTN_HEAD_DIM), lambda bb, h, st, it_r, jt_r: (0, 0)),
                  pl.BlockSpec((1, ATTN_V_DIM), lambda bb, h, st, it_r, jt_r: (0, 0))],
        out_specs=pl.BlockSpec((None, tile, ATTN_V_DIM), q_map),
        scratch_shapes=[pltpu.VMEM((tile, ATTN_V_DIM), BF16), pltpu.VMEM((tile, ATTN_V_DIM), BF16),
                        pltpu.VMEM((tile, 1), F32), pltpu.VMEM((tile, 1), F32),
                        pltpu.VMEM((tile, 1), F32), pltpu.VMEM((tile, 1), F32),
                        pltpu.VMEM((tile, ATTN_V_DIM), F32), pltpu.VMEM((tile, ATTN_V_DIM), F32)],
    )
    return pl.pallas_call(
        _attn_online_kernel,
        grid_spec=grid_spec,
        out_shape=jax.ShapeDtypeStruct((b, s, V_COLS), BF16),
        compiler_params=_params(("arbitrary", "arbitrary", "arbitrary")),
        name="diff_attn_online",
    )(it, jt, qn, kn, v, bias, lam_vecs, sub_g)


def _diff_attention(qn, kn, v, rel_bias, q_gain, k_gain, lam_vecs, sub_g):
    tile = min(ATTN_TILE, qn.shape[1])
    bias = _bias_tiles(rel_bias, tile)
    spread = jnp.max(jnp.abs(rel_bias - rel_bias[NUM_BUCKETS - 1]))
    bound = LOG2E * (1.05 * math.sqrt(ATTN_HEAD_DIM) * jnp.max(jnp.abs(q_gain)) * jnp.max(jnp.abs(k_gain)) + spread)
    args = (qn, kn, v, bias, lam_vecs, sub_g)
    return lax.cond(bound < EXP2_SAFE_BOUND, lambda a: _attn_bounded(*a), lambda a: _attn_online(*a), args)


def _ssd_kernel(z_ref, xbc_ref, dt_ref, alog_ref, dskip_ref, ng_ref, y_ref, state_s):
    L = SSM_CHUNK

    @pl.when(pl.program_id(1) == 0)
    def _reset():
        state_s[...] = jnp.zeros(state_s.shape, F32)

    xs = xbc_ref[:, :D_INNER].astype(F32)
    bm = xbc_ref[:, D_INNER:D_INNER + SSM_GROUPS * D_STATE]
    cm = xbc_ref[:, D_INNER + SSM_GROUPS * D_STATE:]

    dt = dt_ref[...]
    a = dt * (-jnp.exp(alog_ref[...]))
    row = lax.broadcasted_iota(jnp.int32, (L, L), 0)
    col = lax.broadcasted_iota(jnp.int32, (L, L), 1)
    causal = row >= col
    tril = jnp.where(causal, 1.0, 0.0).astype(F32)
    acum = jnp.dot(tril, a, preferred_element_type=F32, precision=lax.Precision.HIGHEST)
    acum_t = acum.T
    a_end = acum[L - 1:L, :]
    w_end = jnp.exp(a_end - acum) * dt
    e_acum = jnp.exp(acum)
    e_end = jnp.exp(a_end)
    lane = lax.broadcasted_iota(jnp.int32, (L, LANES), 1)
    low = lane < SSM_HEAD_DIM

    def pair_cols(arr, h0):
        return jnp.where(low, arr[:, h0:h0 + 1], arr[:, h0 + 1:h0 + 2])

    contract_last = (((1,), (1,)), ((), ()))
    contract_first = (((0,), (0,)), ((), ()))
    y_parts = []
    for g in range(SSM_GROUPS):
        bg = bm[:, g * D_STATE:(g + 1) * D_STATE]
        cg = cm[:, g * D_STATE:(g + 1) * D_STATE]
        cb = lax.dot_general(cg, bg, contract_last, preferred_element_type=F32)
        st = state_s[g]
        y_off = jnp.dot(cg, st.astype(BF16), preferred_element_type=F32)
        xw_parts = []
        dec_parts = []
        for pr in range(SSM_HEADS_PER_GROUP // 2):
            h0 = g * SSM_HEADS_PER_GROUP + 2 * pr
            ch = slice(h0 * SSM_HEAD_DIM, (h0 + 2) * SSM_HEAD_DIM)
            x_pair = xs[:, ch]
            xdt = (x_pair * pair_cols(dt, h0)).astype(BF16)
            yd = []
            for hh in (h0, h0 + 1):
                seg = acum[:, hh:hh + 1] - acum_t[hh:hh + 1, :]
                decay = jnp.exp(jnp.where(causal, seg, NEG_BIG))
                yd.append(jnp.dot((cb * decay).astype(BF16), xdt, preferred_element_type=F32))
            y_diag = jnp.where(low, yd[0], yd[1])
            off = y_off[:, 2 * pr * SSM_HEAD_DIM:(2 * pr + 2) * SSM_HEAD_DIM]
            y_parts.append(y_diag + off * pair_cols(e_acum, h0))
            xw_parts.append(x_pair * pair_cols(w_end, h0))
            dec_parts.append(jnp.where(low[0:1], e_end[:, h0:h0 + 1], e_end[:, h0 + 1:h0 + 2]))
        xw = jnp.concatenate(xw_parts, axis=-1).astype(BF16)
        dec = jnp.concatenate(dec_parts, axis=-1)
        state_s[g] = st * dec + lax.dot_general(bg, xw, contract_first, preferred_element_type=F32)
    y = (jnp.concatenate(y_parts, axis=-1) + dskip_ref[...] * xs) * z_ref[...].astype(F32)
    gsz = D_INNER // SSM_GROUPS
    for g in range(SSM_GROUPS):
        sl = slice(g * gsz, (g + 1) * gsz)
        y_ref[:, sl] = (_rms(y[:, sl], SSM_EPS) * ng_ref[:, sl]).astype(BF16)


def _ssd(z, xbc, dt, a_log, d_skip_ch, norm_g):
    b, s, _ = z.shape
    nc = s // SSM_CHUNK
    blk = lambda bb, c: (bb, c, 0)
    fix = lambda bb, c: (0, 0)
    return pl.pallas_call(
        _ssd_kernel,
        grid=(b, nc),
        in_specs=[pl.BlockSpec((None, SSM_CHUNK, D_INNER), blk), pl.BlockSpec((None, SSM_CHUNK, CONV_DIM), blk),
                  pl.BlockSpec((None, SSM_CHUNK, DT_PAD), blk), pl.BlockSpec((1, DT_PAD), fix),
                  pl.BlockSpec((1, D_INNER), fix), pl.BlockSpec((1, D_INNER), fix)],
        out_specs=pl.BlockSpec((None, SSM_CHUNK, D_INNER), blk),
        out_shape=jax.ShapeDtypeStruct((b, s, D_INNER), BF16),
        scratch_shapes=[pltpu.VMEM((SSM_GROUPS, D_STATE, SSM_HEADS_PER_GROUP * SSM_HEAD_DIM), F32)],
        compiler_params=_params(("arbitrary", "arbitrary")),
        name="ssd",
    )(z, xbc, dt, a_log, d_skip_ch, norm_g)


def _round_up(n, m):
    return (n + m - 1) // m * m


def _merge_kernel(x_ref, attn_ref, y_ref, gmix_ref, wg_ref, wao_ref, wso_ref, wout_ref, gffn_ref, wr_ref, br_ref,
                  x1_ref, h2_ref, pos_ref, topw_ref, cnt_ref):
    x = x_ref[...]
    tm = x.shape[0]
    h = (_rms(x, RMS_EPS) * gmix_ref[...]).astype(BF16)
    gates = _sigmoid(jnp.dot(h, wg_ref[...], preferred_element_type=F32))
    attn_out = jnp.dot(attn_ref[...], wao_ref[...], preferred_element_type=F32)
    ssm_out = jnp.dot(y_ref[...], wso_ref[...], preferred_element_type=F32)
    merged = gates[:, :D_MODEL] * attn_out + gates[:, D_MODEL:] * ssm_out
    x1 = x + jnp.dot(merged.astype(BF16), wout_ref[...], preferred_element_type=F32)
    x1_ref[...] = x1
    h2 = _rms(x1, RMS_EPS) * gffn_ref[...]
    h2_ref[...] = h2.astype(BF16)

    logits = lax.dot_general(wr_ref[...], h2, (((1,), (1,)), ((), ())), preferred_element_type=F32,
                             precision=lax.Precision.HIGHEST) + br_ref[...]
    eid = lax.broadcasted_iota(jnp.int32, logits.shape, 0)
    vals, hits = [], []
    member = jnp.zeros(logits.shape, F32)
    work = logits
    for kk in range(TOP_K):
        m = jnp.max(work, axis=0, keepdims=True)
        idx = jnp.min(jnp.where(work == m, eid, N_EXPERTS), axis=0, keepdims=True)
        hit = eid == idx
        vals.append(m)
        hits.append(hit)
        member = jnp.where(hit, 1.0, member)
        work = jnp.where(hit, -jnp.inf, work)
    ex = [jnp.exp(v - vals[0]) for v in vals]
    denom = ex[0] + ex[1] + ex[2] + ex[3]
    for kk in range(TOP_K):
        topw_ref[kk:kk + 1, :] = ex[kk] / denom

    r = lax.broadcasted_iota(jnp.int32, (tm, tm), 0)
    c = lax.broadcasted_iota(jnp.int32, (tm, tm), 1)
    before = jnp.where(r < c, 1.0, 0.0).astype(BF16)
    prefix = jnp.dot(member.astype(BF16), before, preferred_element_type=F32)
    cnt = jnp.sum(member, axis=1, keepdims=True).astype(jnp.int32)
    cnt_al = jnp.bitwise_and(cnt + (ROW_ALIGN - 1), -ROW_ALIGN)
    cnt_al = jnp.broadcast_to(cnt_al, (N_EXPERTS, LANES)).astype(F32)
    er = lax.broadcasted_iota(jnp.int32, (N_EXPERTS, N_EXPERTS), 0)
    ec = lax.broadcasted_iota(jnp.int32, (N_EXPERTS, N_EXPERTS), 1)
    lower = jnp.where(ec < er, 1.0, 0.0).astype(F32)
    run_start = jnp.dot(lower, cnt_al, preferred_element_type=F32, precision=lax.Precision.HIGHEST)[:, 0:1]
    base = prefix + run_start
    for kk in range(TOP_K):
        pos_ref[kk:kk + 1, :] = jnp.sum(jnp.where(hits[kk], base, 0.0), axis=0, keepdims=True).astype(jnp.int32)
    cnt_ref[...] = jnp.broadcast_to(cnt, cnt_ref.shape)


def _merge(x2, attn, y, g_mix, w_gate, w_ao, w_so, w_out, g_ffn, w_r_t, b_r):
    t = x2.shape[0]
    tm = min(ROW_TILE, t)
    row = lambda i: (i, 0)
    colb = lambda i: (0, i)
    fix = lambda i: (0, 0)
    return pl.pallas_call(
        _merge_kernel,
        grid=(t // tm,),
        in_specs=[pl.BlockSpec((tm, D_MODEL), row), pl.BlockSpec((tm, V_COLS), row), pl.BlockSpec((tm, D_INNER), row),
                  pl.BlockSpec((1, D_MODEL), fix), _resident((D_MODEL, 2 * D_MODEL)),
                  _resident((V_COLS, D_MODEL)), _resident((D_INNER, D_MODEL)),
                  _resident((D_MODEL, D_MODEL)), pl.BlockSpec((1, D_MODEL), fix),
                  pl.BlockSpec((N_EXPERTS, D_MODEL), fix), pl.BlockSpec((N_EXPERTS, 1), fix)],
        out_specs=[pl.BlockSpec((tm, D_MODEL), row), pl.BlockSpec((tm, D_MODEL), row),
                   pl.BlockSpec((TOP_K, tm), colb), pl.BlockSpec((TOP_K, tm), colb),
                   pl.BlockSpec((N_EXPERTS, LANES), row)],
        out_shape=[jax.ShapeDtypeStruct((t, D_MODEL), F32), jax.ShapeDtypeStruct((t, D_MODEL), BF16),
                   jax.ShapeDtypeStruct((TOP_K, t), jnp.int32), jax.ShapeDtypeStruct((TOP_K, t), F32),
                   jax.ShapeDtypeStruct((t // tm * N_EXPERTS, LANES), jnp.int32)],
        compiler_params=_params(("arbitrary",)),
        name="merge_router",
    )(x2, attn, y, g_mix, w_gate, w_ao, w_so, w_out, g_ffn, w_r_t, b_r)


def _sorted_rows(tm):
    return _round_up(TOP_K * tm + N_EXPERTS * (ROW_ALIGN - 1), LANES)


def _run_copies(n, max_n, vmem_ref, vmem_off, hbm_ref, hbm_off, sem, to_hbm, wait):
    done = 0
    pieces = [ROW_ALIGN << p for p in range((max_n // ROW_ALIGN).bit_length())]
    for bit in reversed(pieces):
        take = (n & bit) != 0

        @pl.when(take)
        def _piece(bit=bit, done=done):
            v = vmem_ref.at[pl.ds(pl.multiple_of(vmem_off + done, ROW_ALIGN), bit)]
            h = hbm_ref.at[pl.ds(pl.multiple_of(hbm_off + done, ROW_ALIGN), bit)]
            cp = pltpu.make_async_copy(v, h, sem) if to_hbm else pltpu.make_async_copy(h, v, sem)
            cp.wait() if wait else cp.start()

        done = done + jnp.where(take, bit, 0)


def _dispatch_kernel(off_ref, cnt_ref, start_ref, last_ref, h2_ref, pos_ref, xs_hbm, buf_s, zero_s, sem, zsem):
    i = pl.program_id(0)
    tm = h2_ref.shape[0]
    rows = buf_s.shape[0]

    @pl.when(i == 0)
    def _zero_last_blocks():
        zero_s[...] = jnp.zeros(zero_s.shape, BF16)
        for e in range(N_EXPERTS):
            start = pl.multiple_of(last_ref[e], MOE_TILE)
            pltpu.make_async_copy(zero_s, xs_hbm.at[pl.ds(start, MOE_TILE)], zsem).start()
        for e in range(N_EXPERTS):
            pltpu.make_async_copy(zero_s, xs_hbm.at[pl.ds(0, MOE_TILE)], zsem).wait()

        def slack(wait):
            def body(blk, carry):
                cp = pltpu.make_async_copy(zero_s, xs_hbm.at[pl.ds(pl.multiple_of(blk * MOE_TILE, MOE_TILE), MOE_TILE)],
                                           zsem)
                cp.wait() if wait else cp.start()
                return carry
            lax.fori_loop(last_ref[N_EXPERTS], xs_hbm.shape[0] // MOE_TILE, body, 0)

        slack(False)
        slack(True)

    pos = pos_ref[...]
    rid = lax.broadcasted_iota(jnp.int32, (rows, tm), 0)
    sel = jnp.zeros((rows, tm), F32)
    for kk in range(TOP_K):
        sel = sel + jnp.where(rid == pos[kk:kk + 1, :], 1.0, 0.0)
    buf_s[...] = jnp.dot(sel.astype(BF16), h2_ref[...], preferred_element_type=F32).astype(BF16)

    for wait in (False, True):
        for e in range(N_EXPERTS):
            idx = i * N_EXPERTS + e
            _run_copies(cnt_ref[idx], tm, buf_s, start_ref[idx], xs_hbm, off_ref[idx], sem, True, wait)


def _dispatch(run_off, run_cnt, run_start, last_block, h2, pos, n_rows):
    t = h2.shape[0]
    tm = min(ROW_TILE, t)
    grid_spec = pltpu.PrefetchScalarGridSpec(
        num_scalar_prefetch=4,
        grid=(t // tm,),
        in_specs=[pl.BlockSpec((tm, D_MODEL), lambda i, *_: (i, 0)),
                  pl.BlockSpec((TOP_K, tm), lambda i, *_: (0, i))],
        out_specs=pl.BlockSpec(memory_space=pl.ANY),
        scratch_shapes=[pltpu.VMEM((_sorted_rows(tm), D_MODEL), BF16), pltpu.VMEM((MOE_TILE, D_MODEL), BF16),
                        pltpu.SemaphoreType.DMA(()), pltpu.SemaphoreType.DMA(())],
    )
    return pl.pallas_call(
        _dispatch_kernel,
        grid_spec=grid_spec,
        out_shape=jax.ShapeDtypeStruct((n_rows, D_MODEL), BF16),
        compiler_params=_params(("arbitrary",)),
        name="moe_dispatch",
    )(run_off, run_cnt, run_start, last_block, h2, pos)


def _expert_kernel(be_ref, nb_ref, x_ref, w1_ref, b1_ref, w2_ref, b2_ref, y_ref):
    @pl.when(pl.program_id(0) < nb_ref[0])
    def _run():
        gu = jnp.dot(x_ref[...], w1_ref[...], preferred_element_type=F32) + b1_ref[...]
        gate = jnp.minimum(gu[:, :D_EXPERT], SWIGLU_LIMIT)
        up = jnp.clip(gu[:, D_EXPERT:], -SWIGLU_LIMIT, SWIGLU_LIMIT)
        act = (up + 1.0) * (gate * _sigmoid(SWIGLU_ALPHA * gate))
        y = jnp.dot(act.astype(BF16), w2_ref[...], preferred_element_type=F32) + b2_ref[...]
        y_ref[...] = y.astype(BF16)

    @pl.when(pl.program_id(0) >= nb_ref[0])
    def _slack():
        y_ref[...] = jnp.zeros(y_ref.shape, BF16)


def _experts(block_e, n_used, xs, w1, b1, w2, b2):
    n_rows = xs.shape[0]
    nb = n_rows // MOE_TILE
    row_in = lambda i, be, nu: (jnp.minimum(i, nu[0] - 1), 0)
    row = lambda i, be, nu: (i, 0)
    wsel = lambda i, be, nu: (be[i], 0, 0)
    grid_spec = pltpu.PrefetchScalarGridSpec(
        num_scalar_prefetch=2,
        grid=(nb,),
        in_specs=[pl.BlockSpec((MOE_TILE, D_MODEL), row_in),
                  pl.BlockSpec((None, D_MODEL, 2 * D_EXPERT), wsel), pl.BlockSpec((None, 1, 2 * D_EXPERT), wsel),
                  pl.BlockSpec((None, D_EXPERT, D_MODEL), wsel), pl.BlockSpec((None, 1, D_MODEL), wsel)],
        out_specs=pl.BlockSpec((MOE_TILE, D_MODEL), row),
    )
    return pl.pallas_call(
        _expert_kernel,
        grid_spec=grid_spec,
        out_shape=jax.ShapeDtypeStruct((n_rows, D_MODEL), BF16),
        compiler_params=_params(("arbitrary",)),
        name="moe_experts",
    )(block_e, n_used, xs, w1, b1, w2, b2)


def _combine_kernel(off_ref, cnt_ref, start_ref, y_hbm, x1_ref, pos_ref, w_ref, o_ref, buf_s, sem):
    i = pl.program_id(0)
    tm = x1_ref.shape[0]
    rows = buf_s.shape[0]

    @pl.when(i == 0)
    def _clear():
        buf_s[...] = jnp.zeros(buf_s.shape, BF16)

    def copies(wait):
        for e in range(N_EXPERTS):
            idx = i * N_EXPERTS + e
            _run_copies(cnt_ref[idx], tm, buf_s, start_ref[idx], y_hbm, off_ref[idx], sem, False, wait)

    copies(False)
    pos = pos_ref[...]
    w = w_ref[...]
    cid = lax.broadcasted_iota(jnp.int32, (tm, rows), 1)
    wsel = jnp.zeros((tm, rows), F32)
    for kk in range(TOP_K):
        wsel = wsel + jnp.where(cid == pos[:, kk:kk + 1], w[:, kk:kk + 1], 0.0)
    w_hi = wsel.astype(BF16)
    w_lo = (wsel - w_hi.astype(F32)).astype(BF16)
    copies(True)
    ys = buf_s[...]
    o_ref[...] = (x1_ref[...] + jnp.dot(w_hi, ys, preferred_element_type=F32)
                  + jnp.dot(w_lo, ys, preferred_element_type=F32))


def _combine(run_off, run_cnt, run_start, y, x1, pos_tok, w_tok):
    t = x1.shape[0]
    tm = min(ROW_TILE, t)
    grid_spec = pltpu.PrefetchScalarGridSpec(
        num_scalar_prefetch=3,
        grid=(t // tm,),
        in_specs=[pl.BlockSpec(memory_space=pl.ANY),
                  pl.BlockSpec((tm, D_MODEL), lambda i, *_: (i, 0)),
                  pl.BlockSpec((tm, TOP_K), lambda i, *_: (i, 0)), pl.BlockSpec((tm, TOP_K), lambda i, *_: (i, 0))],
        out_specs=pl.BlockSpec((tm, D_MODEL), lambda i, *_: (i, 0)),
        scratch_shapes=[pltpu.VMEM((_sorted_rows(tm), D_MODEL), BF16), pltpu.SemaphoreType.DMA(())],
    )
    return pl.pallas_call(
        _combine_kernel,
        grid_spec=grid_spec,
        out_shape=jax.ShapeDtypeStruct((t, D_MODEL), F32),
        compiler_params=_params(("arbitrary",)),
        name="moe_combine",
    )(run_off, run_cnt, run_start, y, x1, pos_tok, w_tok)


def kernel(x, g_mix, w_in, q_norm_g, k_norm_g, lambda_q1, lambda_k1, lambda_q2, lambda_k2, attn_sub_g, rel_bias,
           w_attn_o, conv_w, conv_b, dt_bias, a_log, d_skip, ssm_norm_g, w_ssm_o, w_out, g_ffn, w_router, b_router,
           w1, b1, w2, b2):
    b, s, d = x.shape
    t = b * s
    l = 0
    x2 = x.reshape(t, d)

    w = w_in[l]
    c0 = Q_COLS + K_COLS + V_COLS
    c1 = c0 + D_INNER + CONV_DIM
    w_qkv = w[:, :c0].astype(BF16)
    w_dt = jnp.pad(w[:, c1:c1 + SSM_HEADS], ((0, 0), (0, DT_PAD - SSM_HEADS)))
    w_ssm = jnp.concatenate([w[:, c0:c1], w_dt], axis=1).astype(BF16)
    w_gate = w[:, c1 + SSM_HEADS:].astype(BF16)
    n_hd = Q_COLS // ATTN_HEAD_DIM
    gq = (jnp.tile(q_norm_g[l], n_hd) * (ATTN_HEAD_DIM ** -0.5 * LOG2E)).reshape(1, Q_COLS)
    gk = jnp.tile(k_norm_g[l], n_hd).reshape(1, K_COLS)
    gm = g_mix[l].reshape(1, d)

    qn, kn, v = _qkv_proj(x2, gm, w_qkv, gq, gk)
    pad_h = (0, DT_PAD - SSM_HEADS)
    z, xbc, dt = _ssm_proj(x2, gm, w_ssm, conv_w[l], conv_b[l].reshape(1, -1),
                           jnp.pad(dt_bias[l], pad_h).reshape(1, -1), s)

    lam_vecs = jnp.stack([lambda_q1[l], lambda_k1[l], lambda_q2[l], lambda_k2[l]]).astype(F32)
    attn = _diff_attention(qn.reshape(b, s, -1), kn.reshape(b, s, -1), v.reshape(b, s, -1), rel_bias, q_norm_g[l],
                           k_norm_g[l], lam_vecs, attn_sub_g[l].reshape(1, ATTN_V_DIM))

    y = _ssd(z.reshape(b, s, -1), xbc.reshape(b, s, -1), dt.reshape(b, s, -1),
             jnp.pad(a_log[l], pad_h).reshape(1, -1), jnp.repeat(d_skip[l], SSM_HEAD_DIM).reshape(1, -1),
             ssm_norm_g[l].reshape(1, -1))

    x1, h2, pos, top_w, tile_cnt = _merge(
        x2, attn.reshape(t, -1), y.reshape(t, -1), gm, w_gate, w_attn_o[l].astype(BF16), w_ssm_o[l].astype(BF16),
        w_out[l].astype(BF16), g_ffn[l].reshape(1, d), w_router[l].T, b_router[l].reshape(-1, 1))

    n_tiles = t // min(ROW_TILE, t)
    cnt = tile_cnt.reshape(n_tiles, N_EXPERTS, LANES)[:, :, 0]
    cnt_al = _round_up(cnt, ROW_ALIGN)
    run_start = jnp.cumsum(cnt_al, axis=1) - cnt_al
    padded = _round_up(jnp.sum(cnt_al, axis=0), MOE_TILE)
    end_pad = jnp.cumsum(padded)
    run_off = (end_pad - padded)[None, :] + jnp.cumsum(cnt_al, axis=0) - cnt_al
    n_rows = _round_up(t * TOP_K + n_tiles * N_EXPERTS * (ROW_ALIGN - 1), MOE_TILE) + N_EXPERTS * MOE_TILE
    block_start = jnp.arange(n_rows // MOE_TILE, dtype=jnp.int32) * MOE_TILE
    block_e = jnp.minimum(jnp.sum(block_start[:, None] >= end_pad[None, :], axis=1), N_EXPERTS - 1).astype(jnp.int32)
    n_used = (end_pad[-1:] // MOE_TILE).astype(jnp.int32)
    last_block = jnp.concatenate([jnp.maximum(end_pad - MOE_TILE, 0), end_pad[-1:] // MOE_TILE]).astype(jnp.int32)
    tables = [a.reshape(-1).astype(jnp.int32) for a in (run_off, cnt_al, run_start)]

    xs = _dispatch(*tables, last_block, h2, pos, n_rows)
    ys = _experts(block_e, n_used, xs, w1[l].astype(BF16), b1[l][:, None, :], w2[l].astype(BF16), b2[l][:, None, :])
    out = _combine(*tables, ys, x1, pos.T, top_w.T)
    return out.reshape(b, s, d)
```

```python
import functools
import math

import jax
import jax.numpy as jnp
from jax import lax
from jax.experimental import pallas as pl
from jax.experimental.pallas import tpu as pltpu

F32 = jnp.float32
BF16 = jnp.bfloat16

D_MODEL = 1024
ATTN_HEADS = 8
ATTN_HEAD_DIM = 64
ATTN_V_DIM = 2 * ATTN_HEAD_DIM
LAMBDA_INIT = 0.8 - 0.6 * math.exp(-0.3 * 0)
NUM_BUCKETS = 32
MAX_DISTANCE = 128
D_INNER = 2 * D_MODEL
SSM_HEAD_DIM = 64
SSM_HEADS = D_INNER // SSM_HEAD_DIM
SSM_GROUPS = 8
SSM_HEADS_PER_GROUP = SSM_HEADS // SSM_GROUPS
D_STATE = 128
CONV_WIDTH = 4
SSM_CHUNK = 128
CONV_DIM = D_INNER + 2 * SSM_GROUPS * D_STATE
N_EXPERTS = 32
TOP_K = 4
D_EXPERT = D_MODEL
SWIGLU_LIMIT = 7.0
SWIGLU_ALPHA = 1.702
RMS_EPS = 1e-6
SSM_EPS = 1e-5
Q_COLS = ATTN_HEADS * 2 * ATTN_HEAD_DIM
K_COLS = Q_COLS
V_COLS = ATTN_HEADS * ATTN_V_DIM

LANES = 128
MXU_DIM = 256
DT_PAD = LANES
NEG_BIG = -1e30
LOG2E = math.log2(math.e)
EXP2_SAFE_BOUND = 80.0
VMEM_LIMIT = 56 * 1024 * 1024

ROW_TILE = 512
ATTN_TILE = 512
MOE_TILE = 512
ROW_ALIGN = 16


def _rms(x, eps):
    return x * lax.rsqrt(jnp.mean(x * x, axis=-1, keepdims=True) + eps)


def _sigmoid(x):
    return 1.0 / (1.0 + jnp.exp(-x))


def _params(sem):
    return pltpu.CompilerParams(dimension_semantics=sem, vmem_limit_bytes=VMEM_LIMIT)


def _resident(shape):
    return pl.BlockSpec(shape, lambda *_: (0,) * len(shape), pipeline_mode=pl.Buffered(1))


def _qkv_kernel(x_ref, g_ref, w_ref, gq_ref, gk_ref, q_ref, k_ref, v_ref):
    h = (_rms(x_ref[...], RMS_EPS) * g_ref[...]).astype(BF16)
    qkv = jnp.dot(h, w_ref[...], preferred_element_type=F32)
    r = lax.broadcasted_iota(jnp.int32, (MXU_DIM, MXU_DIM), 0) // ATTN_HEAD_DIM
    c = lax.broadcasted_iota(jnp.int32, (MXU_DIM, MXU_DIM), 1) // ATTN_HEAD_DIM
    group_ones = jnp.where(r == c, 1.0, 0.0).astype(BF16)

    def head_norm(t, gain_ref, out_ref):
        for cc in range(Q_COLS // MXU_DIM):
            sl = slice(cc * MXU_DIM, (cc + 1) * MXU_DIM)
            tc = t[:, sl]
            ss = jnp.dot((tc * tc).astype(BF16), group_ones, preferred_element_type=F32)
            out_ref[:, sl] = (tc * lax.rsqrt(ss * (1.0 / ATTN_HEAD_DIM) + RMS_EPS) * gain_ref[:, sl]).astype(BF16)

    head_norm(qkv[:, :Q_COLS], gq_ref, q_ref)
    head_norm(qkv[:, Q_COLS:Q_COLS + K_COLS], gk_ref, k_ref)
    v_ref[...] = qkv[:, Q_COLS + K_COLS:].astype(BF16)


def _qkv_proj(x2, g_mix, w_qkv, gq, gk):
    t = x2.shape[0]
    tm = min(ROW_TILE, t)
    row = lambda i: (i, 0)
    fix = lambda i: (0, 0)
    out = jax.ShapeDtypeStruct((t, D_MODEL), BF16)
    return pl.pallas_call(
        _qkv_kernel,
        grid=(t // tm,),
        in_specs=[pl.BlockSpec((tm, D_MODEL), row), pl.BlockSpec((1, D_MODEL), fix),
                  _resident((D_MODEL, 3 * D_MODEL)), pl.BlockSpec((1, D_MODEL), fix),
                  pl.BlockSpec((1, D_MODEL), fix)],
        out_specs=[pl.BlockSpec((tm, D_MODEL), row)] * 3,
        out_shape=[out, out, out],
        compiler_params=_params(("arbitrary",)),
        name="qkv_proj",
    )(x2, g_mix, w_qkv, gq, gk)


CONV_COLS = 1024
SUBLANES = 8
CHUNK_VREGS = SSM_CHUNK // SUBLANES


def _chunk_time(row):
    return (row >> 3) + CHUNK_VREGS * (row & (SUBLANES - 1))


def _ssm_proj_kernel(x_ref, g_ref, w_ref, cw_ref, cb_ref, dtb_ref, alog_ref, z_ref, xbc_ref, dt_ref, acum_ref,
                     acumt_ref, tail_s, h_s, p_s, *, tiles_per_seq):
    tm = x_ref.shape[0]
    nb = tm // SSM_CHUNK
    taps = CONV_WIDTH - 1

    @pl.when(pl.program_id(0) % tiles_per_seq == 0)
    def _sequence_start():
        tail_s[...] = jnp.zeros(tail_s.shape, F32)

    h = (_rms(x_ref[...], RMS_EPS) * g_ref[...]).astype(BF16)
    rr = lax.broadcasted_iota(jnp.int32, (SSM_CHUNK, SSM_CHUNK), 0)
    cc = lax.broadcasted_iota(jnp.int32, (SSM_CHUNK, SSM_CHUNK), 1)
    pick = jnp.where(cc == _chunk_time(rr), 1.0, 0.0).astype(BF16)
    h = jnp.concatenate(
        [jnp.dot(pick, h[b * SSM_CHUNK:(b + 1) * SSM_CHUNK], preferred_element_type=F32).astype(BF16)
         for b in range(nb)], axis=0)

    h_s[...] = h

    def project(stage):
        return jnp.dot(h_s[...], w_ref[:, stage * CONV_COLS:(stage + 1) * CONV_COLS], preferred_element_type=F32)

    z_stages = D_INNER // CONV_COLS
    n_stages = z_stages + CONV_DIM // CONV_COLS
    sub = lax.broadcasted_iota(jnp.int32, (nb * taps, SUBLANES, CONV_COLS), 1)
    p_s[0] = project(0)
    for stage in range(n_stages):
        if stage + 1 < n_stages:
            p_s[(stage + 1) % 2] = project(stage + 1)
        p = p_s[stage % 2]
        if stage < z_stages:
            z_ref[:, stage * CONV_COLS:(stage + 1) * CONV_COLS] = (p * _sigmoid(p)).astype(BF16)
            continue
        c = stage - z_stages
        cols = slice(c * CONV_COLS, (c + 1) * CONV_COLS)
        p4 = p.reshape(nb, CHUNK_VREGS, SUBLANES, CONV_COLS)
        last = p4[:, CHUNK_VREGS - taps:]
        seq = jnp.concatenate([tail_s[c][None], last], axis=0).reshape((nb + 1) * taps, SUBLANES, CONV_COLS)
        tail_s[c] = last[nb - 1]
        rolled = pltpu.roll(seq, 1, axis=1)
        wrapped = jnp.where(sub == 0, rolled[:nb * taps], rolled[taps:]).reshape(nb, taps, SUBLANES, CONV_COLS)
        conv = cb_ref[:, cols] + cw_ref[taps:taps + 1, cols] * p4
        for back in range(1, CONV_WIDTH):
            shifted = jnp.concatenate([wrapped[:, taps - back:], p4[:, :CHUNK_VREGS - back]], axis=1)
            conv = conv + cw_ref[taps - back:taps - back + 1, cols] * shifted
        xbc_ref[:, cols] = (conv * _sigmoid(conv)).reshape(tm, CONV_COLS).astype(BF16)
    dtl = jnp.dot(h_s[...], w_ref[:, D_INNER + CONV_DIM:], preferred_element_type=F32) + dtb_ref[...]
    dt = jnp.maximum(dtl, 0.0) + jnp.log(1.0 + jnp.exp(-jnp.abs(dtl)))
    dt_ref[...] = dt
    a = dt * (-jnp.exp(alog_ref[...]))
    upto = jnp.where(_chunk_time(rr) >= _chunk_time(cc), 1.0, 0.0).astype(F32)
    for b in range(nb):
        rows = slice(b * SSM_CHUNK, (b + 1) * SSM_CHUNK)
        acum = jnp.dot(upto, a[rows], preferred_element_type=F32, precision=lax.Precision.HIGHEST)
        acum_ref[rows, :] = acum
        acumt_ref[rows, :] = acum.T


def _ssm_proj(x2, g_mix, w_ssm, conv_w, conv_b, dt_bias, a_log, seq_len):
    t = x2.shape[0]
    tm = min(ROW_TILE, seq_len)
    assert seq_len % tm == 0
    ncol = D_INNER + CONV_DIM + DT_PAD
    row = lambda i: (i, 0)
    fix = lambda i: (0, 0)
    heads = jax.ShapeDtypeStruct((t, DT_PAD), F32)
    return pl.pallas_call(
        functools.partial(_ssm_proj_kernel, tiles_per_seq=seq_len // tm),
        grid=(t // tm,),
        in_specs=[pl.BlockSpec((tm, D_MODEL), row), pl.BlockSpec((1, D_MODEL), fix),
                  _resident((D_MODEL, ncol)), pl.BlockSpec((CONV_WIDTH, CONV_DIM), fix),
                  pl.BlockSpec((1, CONV_DIM), fix), pl.BlockSpec((1, DT_PAD), fix), pl.BlockSpec((1, DT_PAD), fix)],
        out_specs=[pl.BlockSpec((tm, D_INNER), row), pl.BlockSpec((tm, CONV_DIM), row)]
        + [pl.BlockSpec((tm, DT_PAD), row)] * 3,
        out_shape=[jax.ShapeDtypeStruct((t, D_INNER), BF16), jax.ShapeDtypeStruct((t, CONV_DIM), BF16),
                   heads, heads, heads],
        scratch_shapes=[pltpu.VMEM((CONV_DIM // CONV_COLS, CONV_WIDTH - 1, SUBLANES, CONV_COLS), F32),
                        pltpu.VMEM((tm, D_MODEL), BF16), pltpu.VMEM((2, tm, CONV_COLS), F32)],
        compiler_params=_params(("arbitrary",)),
        name="ssm_proj",
    )(x2, g_mix, w_ssm, conv_w, conv_b, dt_bias, a_log)


def _split_maps(q):
    lane = lax.broadcasted_iota(jnp.int32, q.shape, 1)
    zero = jnp.zeros_like(q)
    return jnp.where(lane < ATTN_HEAD_DIM, q, zero), jnp.where(lane >= ATTN_HEAD_DIM, q, zero)


def _attn_finalize(acc1, l1, acc2, l2, lam_ref, subg_ref, o_ref):
    lam_v = lam_ref[...]
    lam = (jnp.exp(jnp.sum(lam_v[0:1] * lam_v[1:2], axis=-1, keepdims=True))
           - jnp.exp(jnp.sum(lam_v[2:3] * lam_v[3:4], axis=-1, keepdims=True)) + LAMBDA_INIT)
    o = acc1 / l1 - lam * (acc2 / l2)
    o_ref[...] = (_rms(o, RMS_EPS) * subg_ref[...] * (1.0 - LAMBDA_INIT)).astype(BF16)


def _attn_bounded_kernel(q_ref, k_ref, v_ref, bias_ref, lam_ref, subg_ref, o_ref, qq_s, vv_s, acc_s, s_s):
    i = pl.program_id(2)
    tq = q_ref.shape[0]
    tk = tq

    @pl.when(i == 0)
    def _extend_v():
        vv_s[:, :ATTN_V_DIM] = v_ref[...]
        vv_s[:, ATTN_V_DIM:] = jnp.ones((vv_s.shape[0], ATTN_V_DIM), BF16)

    q1, q2 = _split_maps(q_ref[...])
    qq_s[0:tq, :] = q1
    qq_s[tq:2 * tq, :] = q2
    acc_s[...] = jnp.zeros(acc_s.shape, F32)
    contract_last = (((1,), (1,)), ((), ()))

    def logits(j, kind):
        rows = pl.ds(pl.multiple_of(j * tk, tk), tk)
        s = lax.dot_general(qq_s[...], k_ref[rows, :], contract_last, preferred_element_type=F32)
        if kind is not None:
            b = bias_ref[kind]
            s = jnp.concatenate([s[0:tq] + b, s[tq:2 * tq] + b], axis=0)
        return s

    def accumulate(j, slot):
        rows = pl.ds(pl.multiple_of(j * tk, tk), tk)
        acc_s[...] += jnp.dot(jnp.exp2(s_s[slot]).astype(BF16), vv_s[rows, :], preferred_element_type=F32)

    def step(t, slot, next_kind):
        s_s[1 - slot] = logits(t + 1, next_kind)
        accumulate(t, slot)

    odd = (i & 1) == 1

    @pl.when(i == 0)
    def _first_is_diag():
        s_s[0] = logits(0, 0)

    @pl.when(i == 1)
    def _first_is_prev():
        s_s[1] = logits(0, 1)

    @pl.when(jnp.logical_and(i >= 2, jnp.logical_not(odd)))
    def _first_far_even():
        s_s[0] = logits(0, None)

    @pl.when(jnp.logical_and(i >= 2, odd))
    def _first_far_odd():
        s_s[1] = logits(0, None)

    @pl.when(jnp.logical_and(i >= 3, odd))
    def _unpaired_far_step():
        step(0, 1, None)

    def far_pair(pp, carry):
        t = (i & 1) + 2 * pp
        step(t, 0, None)
        step(t + 1, 1, None)
        return carry

    lax.fori_loop(0, jnp.maximum(i - 2, 0) // 2, far_pair, 0)

    @pl.when(i >= 2)
    def _next_is_prev():
        step(i - 2, 0, 1)

    @pl.when(i >= 1)
    def _next_is_diag():
        step(i - 1, 1, 0)

    accumulate(i, 0)
    _attn_finalize(acc_s[0:tq, :ATTN_V_DIM], acc_s[0:tq, ATTN_V_DIM:], acc_s[tq:2 * tq, :ATTN_V_DIM],
                   acc_s[tq:2 * tq, ATTN_V_DIM:], lam_ref, subg_ref, o_ref)


def _attn_online_kernel(it_ref, jt_ref, q_ref, k_ref, v_ref, bias_ref, lam_ref, subg_ref, o_ref,
                        q1_s, q2_s, m1_s, m2_s, l1_s, l2_s, acc1_s, acc2_s):
    step = pl.program_id(2)
    i = it_ref[step]
    j = jt_ref[step]

    @pl.when(j == 0)
    def _init():
        q1_s[...], q2_s[...] = _split_maps(q_ref[...])
        for m_s, l_s, acc_s in ((m1_s, l1_s, acc1_s), (m2_s, l2_s, acc2_s)):
            m_s[...] = jnp.full(m_s.shape, NEG_BIG, F32)
            l_s[...] = jnp.zeros(l_s.shape, F32)
            acc_s[...] = jnp.zeros(acc_s.shape, F32)

    def update(bias):
        k = k_ref[...]
        v = v_ref[...]
        contract_last = (((1,), (1,)), ((), ()))
        for q_s, m_s, l_s, acc_s in ((q1_s, m1_s, l1_s, acc1_s), (q2_s, m2_s, l2_s, acc2_s)):
            s = lax.dot_general(q_s[...], k, contract_last, preferred_element_type=F32)
            if bias is not None:
                s = s + bias_ref[bias]
            m_old = m_s[...]
            m_new = jnp.maximum(m_old, jnp.max(s, axis=-1, keepdims=True))
            alpha = jnp.exp2(m_old - m_new)
            p = jnp.exp2(s - m_new)
            l_s[...] = alpha * l_s[...] + jnp.sum(p, axis=-1, keepdims=True)
            acc_s[...] = alpha * acc_s[...] + jnp.dot(p.astype(BF16), v, preferred_element_type=F32)
            m_s[...] = m_new

    @pl.when(j == i)
    def _diag():
        update(0)

    @pl.when(j == i - 1)
    def _prev():
        update(1)

    @pl.when(j < i - 1)
    def _far():
        update(None)

    @pl.when(j == i)
    def _finalize():
        _attn_finalize(acc1_s[...], l1_s[...], acc2_s[...], l2_s[...], lam_ref, subg_ref, o_ref)


def _t5_bucket(dist):
    n = jnp.maximum(dist, 0)
    max_exact = NUM_BUCKETS // 2
    scaled = jnp.log(jnp.maximum(n, 1).astype(F32) / max_exact) / math.log(MAX_DISTANCE / max_exact)
    large = max_exact + (scaled * (NUM_BUCKETS - max_exact)).astype(jnp.int32)
    large = jnp.minimum(large, NUM_BUCKETS - 1)
    return jnp.where(n < max_exact, n, large)


def _bias_tiles(rel_bias, tile):
    blk = MAX_DISTANCE
    assert tile % blk == 0
    nb = tile // blk
    table = (rel_bias - rel_bias[NUM_BUCKETS - 1]).astype(F32) * LOG2E
    r = jnp.arange(blk, dtype=jnp.int32)
    d0 = r[:, None] - r[None, :]

    def lookup(dist):
        onehot = (_t5_bucket(dist)[..., None] == jnp.arange(NUM_BUCKETS, dtype=jnp.int32)).astype(F32)
        return jnp.einsum('qkn,nh->hqk', onehot, table, precision=lax.Precision.HIGHEST)

    on_diag = jnp.where(d0[None] >= 0, lookup(d0), NEG_BIG)
    sub_diag = lookup(d0 + blk)
    zeros = jnp.zeros_like(sub_diag)
    masked = jnp.full_like(sub_diag, NEG_BIG)

    def assemble(pick):
        return jnp.concatenate(
            [jnp.concatenate([pick(bi, bj) for bj in range(nb)], axis=-1) for bi in range(nb)], axis=-2)

    diag_tile = assemble(lambda bi, bj: on_diag if bi == bj else sub_diag if bi == bj + 1 else zeros if bi > bj else masked)
    prev_tile = assemble(lambda bi, bj: sub_diag if (bi == 0 and bj == nb - 1) else zeros)
    return jnp.stack([diag_tile, prev_tile], axis=1)


def _attn_bounded(qn, kn, v, bias, lam_vecs, sub_g):
    b, s, _ = qn.shape
    tile = min(ATTN_TILE, s)
    q_map = lambda bb, h, i: (bb, i, h)
    kv_map = lambda bb, h, i: (bb, 0, h)
    return pl.pallas_call(
        _attn_bounded_kernel,
        grid=(b, ATTN_HEADS, s // tile),
        in_specs=[pl.BlockSpec((None, tile, ATTN_V_DIM), q_map),
                  pl.BlockSpec((None, s, ATTN_V_DIM), kv_map),
                  pl.BlockSpec((None, s, ATTN_V_DIM), kv_map),
                  pl.BlockSpec((None, 2, tile, tile), lambda bb, h, i: (h, 0, 0, 0)),
                  pl.BlockSpec((4, ATTN_HEAD_DIM), lambda bb, h, i: (0, 0)),
                  pl.BlockSpec((1, ATTN_V_DIM), lambda bb, h, i: (0, 0))],
        out_specs=pl.BlockSpec((None, tile, ATTN_V_DIM), q_map),
        out_shape=jax.ShapeDtypeStruct((b, s, V_COLS), BF16),
        scratch_shapes=[pltpu.VMEM((2 * tile, ATTN_V_DIM), BF16), pltpu.VMEM((s, 2 * ATTN_V_DIM), BF16),
                        pltpu.VMEM((2 * tile, 2 * ATTN_V_DIM), F32), pltpu.VMEM((2, 2 * tile, tile), F32)],
        compiler_params=_params(("arbitrary", "arbitrary", "arbitrary")),
        name="diff_attn_bounded",
    )(qn, kn, v, bias, lam_vecs, sub_g)


def _attn_online(qn, kn, v, bias, lam_vecs, sub_g):
    b, s, _ = qn.shape
    tile = min(ATTN_TILE, s)
    nq = s // tile
    it = jnp.asarray([i for i in range(nq) for _ in range(i + 1)], jnp.int32)
    jt = jnp.asarray([j for i in range(nq) for j in range(i + 1)], jnp.int32)
    q_map = lambda bb, h, st, it_r, jt_r: (bb, it_r[st], h)
    kv_map = lambda bb, h, st, it_r, jt_r: (bb, jt_r[st], h)
    grid_spec = pltpu.PrefetchScalarGridSpec(
        num_scalar_prefetch=2,
        grid=(b, ATTN_HEADS, int(it.shape[0])),
        in_specs=[pl.BlockSpec((None, tile, ATTN_V_DIM), q_map),
                  pl.BlockSpec((None, tile, ATTN_V_DIM), kv_map),
                  pl.BlockSpec((None, tile, ATTN_V_DIM), kv_map),
                  pl.BlockSpec((None, 2, tile, tile), lambda bb, h, st, it_r, jt_r: (h, 0, 0, 0)),
                  pl.BlockSpec((4, ATTN_HEAD_DIM), lambda bb, h, st, it_r, jt_r: (0, 0)),
                  pl.BlockSpec((1, ATTN_V_DIM), lambda bb, h, st, it_r, jt_r: (0, 0))],
        out_specs=pl.BlockSpec((None, tile, ATTN_V_DIM), q_map),
        scratch_shapes=[pltpu.VMEM((tile, ATTN_V_DIM), BF16), pltpu.VMEM((tile, ATTN_V_DIM), BF16),
                        pltpu.VMEM((tile, 1), F32), pltpu.VMEM((tile, 1), F32),
                        pltpu.VMEM((tile, 1), F32), pltpu.VMEM((tile, 1), F32),
                        pltpu.VMEM((tile, ATTN_V_DIM), F32), pltpu.VMEM((tile, ATTN_V_DIM), F32)],
    )
    return pl.pallas_call(
        _attn_online_kernel,
        grid_spec=grid_spec,
        out_shape=jax.ShapeDtypeStruct((b, s, V_COLS), BF16),
        compiler_params=_params(("arbitrary", "arbitrary", "arbitrary")),
        name="diff_attn_online",
    )(it, jt, qn, kn, v, bias, lam_vecs, sub_g)


def _diff_attention(qn, kn, v, rel_bias, q_gain, k_gain, lam_vecs, sub_g):
    tile = min(ATTN_TILE, qn.shape[1])
    bias = _bias_tiles(rel_bias, tile)
    spread = jnp.max(jnp.abs(rel_bias - rel_bias[NUM_BUCKETS - 1]))
    bound = LOG2E * (1.05 * math.sqrt(ATTN_HEAD_DIM) * jnp.max(jnp.abs(q_gain)) * jnp.max(jnp.abs(k_gain)) + spread)
    args = (qn, kn, v, bias, lam_vecs, sub_g)
    return lax.cond(bound < EXP2_SAFE_BOUND, lambda a: _attn_bounded(*a), lambda a: _attn_online(*a), args)


def _ssd_kernel(z_ref, xbc_ref, dt_ref, acum_ref, acumt_ref, dskip_ref, ng_ref, y_ref, state_s):
    L = SSM_CHUNK

    @pl.when(pl.program_id(1) == 0)
    def _reset():
        state_s[...] = jnp.zeros(state_s.shape, F32)

    xs = xbc_ref[:, :D_INNER].astype(F32)
    bm = xbc_ref[:, D_INNER:D_INNER + SSM_GROUPS * D_STATE]
    cm = xbc_ref[:, D_INNER + SSM_GROUPS * D_STATE:]

    dt = dt_ref[...]
    row = lax.broadcasted_iota(jnp.int32, (L, L), 0)
    col = lax.broadcasted_iota(jnp.int32, (L, L), 1)
    causal = _chunk_time(row) >= _chunk_time(col)
    acum = acum_ref[...]
    acum_t = acumt_ref[...]
    a_end = acum[L - 1:L, :]
    w_end = jnp.exp(a_end - acum) * dt
    e_acum = jnp.exp(acum)
    e_end = jnp.exp(a_end)
    lane = lax.broadcasted_iota(jnp.int32, (L, LANES), 1)
    low = lane < SSM_HEAD_DIM

    def pair_cols(arr, h0):
        return jnp.where(low, arr[:, h0:h0 + 1], arr[:, h0 + 1:h0 + 2])

    contract_last = (((1,), (1,)), ((), ()))
    contract_first = (((0,), (0,)), ((), ()))
    y_parts = []
    for g in range(SSM_GROUPS):
        bg = bm[:, g * D_STATE:(g + 1) * D_STATE]
        cg = cm[:, g * D_STATE:(g + 1) * D_STATE]
        cb = lax.dot_general(cg, bg, contract_last, preferred_element_type=F32)
        st = state_s[g]
        y_off = jnp.dot(cg, st.astype(BF16), preferred_element_type=F32)
        xw_parts = []
        dec_parts = []
        for pr in range(SSM_HEADS_PER_GROUP // 2):
            h0 = g * SSM_HEADS_PER_GROUP + 2 * pr
            ch = slice(h0 * SSM_HEAD_DIM, (h0 + 2) * SSM_HEAD_DIM)
            x_pair = xs[:, ch]
            xdt = (x_pair * pair_cols(dt, h0)).astype(BF16)
            yd = []
            for hh in (h0, h0 + 1):
                seg = acum[:, hh:hh + 1] - acum_t[hh:hh + 1, :]
                decay = jnp.exp(jnp.where(causal, seg, NEG_BIG))
                yd.append(jnp.dot((cb * decay).astype(BF16), xdt, preferred_element_type=F32))
            y_diag = jnp.where(low, yd[0], yd[1])
            off = y_off[:, 2 * pr * SSM_HEAD_DIM:(2 * pr + 2) * SSM_HEAD_DIM]
            y_parts.append(y_diag + off * pair_cols(e_acum, h0))
            xw_parts.append(x_pair * pair_cols(w_end, h0))
            dec_parts.append(jnp.where(low[0:1], e_end[:, h0:h0 + 1], e_end[:, h0 + 1:h0 + 2]))
        xw = jnp.concatenate(xw_parts, axis=-1).astype(BF16)
        dec = jnp.concatenate(dec_parts, axis=-1)
        state_s[g] = st * dec + lax.dot_general(bg, xw, contract_first, preferred_element_type=F32)
    y = (jnp.concatenate(y_parts, axis=-1) + dskip_ref[...] * xs) * z_ref[...].astype(F32)
    gsz = D_INNER // SSM_GROUPS
    unpick = jnp.where(row == _chunk_time(col), 1.0, 0.0).astype(BF16)
    for g in range(SSM_GROUPS):
        sl = slice(g * gsz, (g + 1) * gsz)
        yn = (_rms(y[:, sl], SSM_EPS) * ng_ref[:, sl]).astype(BF16)
        y_ref[:, sl] = jnp.dot(unpick, yn, preferred_element_type=F32).astype(BF16)


def _ssd(z, xbc, dt, acum, acum_t, d_skip_ch, norm_g):
    b, s, _ = z.shape
    nc = s // SSM_CHUNK
    blk = lambda bb, c: (bb, c, 0)
    fix = lambda bb, c: (0, 0)
    return pl.pallas_call(
        _ssd_kernel,
        grid=(b, nc),
        in_specs=[pl.BlockSpec((None, SSM_CHUNK, D_INNER), blk), pl.BlockSpec((None, SSM_CHUNK, CONV_DIM), blk)]
        + [pl.BlockSpec((None, SSM_CHUNK, DT_PAD), blk)] * 3
        + [pl.BlockSpec((1, D_INNER), fix), pl.BlockSpec((1, D_INNER), fix)],
        out_specs=pl.BlockSpec((None, SSM_CHUNK, D_INNER), blk),
        out_shape=jax.ShapeDtypeStruct((b, s, D_INNER), BF16),
        scratch_shapes=[pltpu.VMEM((SSM_GROUPS, D_STATE, SSM_HEADS_PER_GROUP * SSM_HEAD_DIM), F32)],
        compiler_params=_params(("arbitrary", "arbitrary")),
        name="ssd",
    )(z, xbc, dt, acum, acum_t, d_skip_ch, norm_g)


def _round_up(n, m):
    return (n + m - 1) // m * m


def _merge_kernel(x_ref, attn_ref, y_ref, gmix_ref, wg_ref, wao_ref, wso_ref, wout_ref, gffn_ref, wr_ref, br_ref,
                  x1_ref, h2_ref, pos_ref, topw_ref, cnt_ref):
    x = x_ref[...]
    tm = x.shape[0]
    h = (_rms(x, RMS_EPS) * gmix_ref[...]).astype(BF16)
    gates = _sigmoid(jnp.dot(h, wg_ref[...], preferred_element_type=F32))
    attn_out = jnp.dot(attn_ref[...], wao_ref[...], preferred_element_type=F32)
    ssm_out = jnp.dot(y_ref[...], wso_ref[...], preferred_element_type=F32)
    merged = gates[:, :D_MODEL] * attn_out + gates[:, D_MODEL:] * ssm_out
    x1 = x + jnp.dot(merged.astype(BF16), wout_ref[...], preferred_element_type=F32)
    x1_ref[...] = x1
    h2 = _rms(x1, RMS_EPS) * gffn_ref[...]
    h2_ref[...] = h2.astype(BF16)

    logits = lax.dot_general(wr_ref[...], h2, (((1,), (1,)), ((), ())), preferred_element_type=F32,
                             precision=lax.Precision.HIGHEST) + br_ref[...]
    eid = lax.broadcasted_iota(jnp.int32, logits.shape, 0)
    vals, hits = [], []
    member = jnp.zeros(logits.shape, F32)
    work = logits
    for kk in range(TOP_K):
        m = jnp.max(work, axis=0, keepdims=True)
        idx = jnp.min(jnp.where(work == m, eid, N_EXPERTS), axis=0, keepdims=True)
        hit = eid == idx
        vals.append(m)
        hits.append(hit)
        member = jnp.where(hit, 1.0, member)
        work = jnp.where(hit, -jnp.inf, work)
    ex = [jnp.exp(v - vals[0]) for v in vals]
    denom = ex[0] + ex[1] + ex[2] + ex[3]
    for kk in range(TOP_K):
        topw_ref[kk:kk + 1, :] = ex[kk] / denom

    r = lax.broadcasted_iota(jnp.int32, (tm, tm), 0)
    c = lax.broadcasted_iota(jnp.int32, (tm, tm), 1)
    before = jnp.where(r < c, 1.0, 0.0).astype(BF16)
    prefix = jnp.dot(member.astype(BF16), before, preferred_element_type=F32)
    cnt = jnp.sum(member, axis=1, keepdims=True).astype(jnp.int32)
    cnt_al = jnp.bitwise_and(cnt + (ROW_ALIGN - 1), -ROW_ALIGN)
    cnt_al = jnp.broadcast_to(cnt_al, (N_EXPERTS, LANES)).astype(F32)
    er = lax.broadcasted_iota(jnp.int32, (N_EXPERTS, N_EXPERTS), 0)
    ec = lax.broadcasted_iota(jnp.int32, (N_EXPERTS, N_EXPERTS), 1)
    lower = jnp.where(ec < er, 1.0, 0.0).astype(F32)
    run_start = jnp.dot(lower, cnt_al, preferred_element_type=F32, precision=lax.Precision.HIGHEST)[:, 0:1]
    base = prefix + run_start
    for kk in range(TOP_K):
        pos_ref[kk:kk + 1, :] = jnp.sum(jnp.where(hits[kk], base, 0.0), axis=0, keepdims=True).astype(jnp.int32)
    cnt_ref[...] = jnp.broadcast_to(cnt, cnt_ref.shape)


def _merge(x2, attn, y, g_mix, w_gate, w_ao, w_so, w_out, g_ffn, w_r_t, b_r):
    t = x2.shape[0]
    tm = min(ROW_TILE, t)
    row = lambda i: (i, 0)
    colb = lambda i: (0, i)
    fix = lambda i: (0, 0)
    return pl.pallas_call(
        _merge_kernel,
        grid=(t // tm,),
        in_specs=[pl.BlockSpec((tm, D_MODEL), row), pl.BlockSpec((tm, V_COLS), row), pl.BlockSpec((tm, D_INNER), row),
                  pl.BlockSpec((1, D_MODEL), fix), _resident((D_MODEL, 2 * D_MODEL)),
                  _resident((V_COLS, D_MODEL)), _resident((D_INNER, D_MODEL)),
                  _resident((D_MODEL, D_MODEL)), pl.BlockSpec((1, D_MODEL), fix),
                  pl.BlockSpec((N_EXPERTS, D_MODEL), fix), pl.BlockSpec((N_EXPERTS, 1), fix)],
        out_specs=[pl.BlockSpec((tm, D_MODEL), row), pl.BlockSpec((tm, D_MODEL), row),
                   pl.BlockSpec((TOP_K, tm), colb), pl.BlockSpec((TOP_K, tm), colb),
                   pl.BlockSpec((N_EXPERTS, LANES), row)],
        out_shape=[jax.ShapeDtypeStruct((t, D_MODEL), F32), jax.ShapeDtypeStruct((t, D_MODEL), BF16),
                   jax.ShapeDtypeStruct((TOP_K, t), jnp.int32), jax.ShapeDtypeStruct((TOP_K, t), F32),
                   jax.ShapeDtypeStruct((t // tm * N_EXPERTS, LANES), jnp.int32)],
        compiler_params=_params(("arbitrary",)),
        name="merge_router",
    )(x2, attn, y, g_mix, w_gate, w_ao, w_so, w_out, g_ffn, w_r_t, b_r)


def _sorted_rows(tm):
    return _round_up(TOP_K * tm + N_EXPERTS * (ROW_ALIGN - 1), LANES)


def _run_copies(n, max_n, vmem_ref, vmem_off, hbm_ref, hbm_off, sem, to_hbm, wait):
    done = 0
    pieces = [ROW_ALIGN << p for p in range((max_n // ROW_ALIGN).bit_length())]
    for bit in reversed(pieces):
        take = (n & bit) != 0

        @pl.when(take)
        def _piece(bit=bit, done=done):
            v = vmem_ref.at[pl.ds(pl.multiple_of(vmem_off + done, ROW_ALIGN), bit)]
            h = hbm_ref.at[pl.ds(pl.multiple_of(hbm_off + done, ROW_ALIGN), bit)]
            cp = pltpu.make_async_copy(v, h, sem) if to_hbm else pltpu.make_async_copy(h, v, sem)
            cp.wait() if wait else cp.start()

        done = done + jnp.where(take, bit, 0)


def _dispatch_kernel(off_ref, cnt_ref, start_ref, last_ref, h2_ref, pos_ref, xs_hbm, buf_s, zero_s, sem, zsem):
    i = pl.program_id(0)
    tm = h2_ref.shape[0]
    rows = buf_s.shape[0]

    @pl.when(i == 0)
    def _zero_last_blocks():
        zero_s[...] = jnp.zeros(zero_s.shape, BF16)
        for e in range(N_EXPERTS):
            start = pl.multiple_of(last_ref[e], MOE_TILE)
            pltpu.make_async_copy(zero_s, xs_hbm.at[pl.ds(start, MOE_TILE)], zsem).start()
        for e in range(N_EXPERTS):
            pltpu.make_async_copy(zero_s, xs_hbm.at[pl.ds(0, MOE_TILE)], zsem).wait()

        def slack(wait):
            def body(blk, carry):
                cp = pltpu.make_async_copy(zero_s, xs_hbm.at[pl.ds(pl.multiple_of(blk * MOE_TILE, MOE_TILE), MOE_TILE)],
                                           zsem)
                cp.wait() if wait else cp.start()
                return carry
            lax.fori_loop(last_ref[N_EXPERTS], xs_hbm.shape[0] // MOE_TILE, body, 0)

        slack(False)
        slack(True)

    pos = pos_ref[...]
    rid = lax.broadcasted_iota(jnp.int32, (rows, tm), 0)
    sel = jnp.zeros((rows, tm), F32)
    for kk in range(TOP_K):
        sel = sel + jnp.where(rid == pos[kk:kk + 1, :], 1.0, 0.0)
    buf_s[...] = jnp.dot(sel.astype(BF16), h2_ref[...], preferred_element_type=F32).astype(BF16)

    for wait in (False, True):
        for e in range(N_EXPERTS):
            idx = i * N_EXPERTS + e
            _run_copies(cnt_ref[idx], tm, buf_s, start_ref[idx], xs_hbm, off_ref[idx], sem, True, wait)


def _dispatch(run_off, run_cnt, run_start, last_block, h2, pos, n_rows):
    t = h2.shape[0]
    tm = min(ROW_TILE, t)
    grid_spec = pltpu.PrefetchScalarGridSpec(
        num_scalar_prefetch=4,
        grid=(t // tm,),
        in_specs=[pl.BlockSpec((tm, D_MODEL), lambda i, *_: (i, 0)),
                  pl.BlockSpec((TOP_K, tm), lambda i, *_: (0, i))],
        out_specs=pl.BlockSpec(memory_space=pl.ANY),
        scratch_shapes=[pltpu.VMEM((_sorted_rows(tm), D_MODEL), BF16), pltpu.VMEM((MOE_TILE, D_MODEL), BF16),
                        pltpu.SemaphoreType.DMA(()), pltpu.SemaphoreType.DMA(())],
    )
    return pl.pallas_call(
        _dispatch_kernel,
        grid_spec=grid_spec,
        out_shape=jax.ShapeDtypeStruct((n_rows, D_MODEL), BF16),
        compiler_params=_params(("arbitrary",)),
        name="moe_dispatch",
    )(run_off, run_cnt, run_start, last_block, h2, pos)


def _expert_kernel(be_ref, nb_ref, x_ref, w1_ref, b1_ref, w2_ref, b2_ref, y_ref):
    @pl.when(pl.program_id(0) < nb_ref[0])
    def _run():
        gu = jnp.dot(x_ref[...], w1_ref[...], preferred_element_type=F32) + b1_ref[...]
        gate = jnp.minimum(gu[:, :D_EXPERT], SWIGLU_LIMIT)
        up = jnp.clip(gu[:, D_EXPERT:], -SWIGLU_LIMIT, SWIGLU_LIMIT)
        act = (up + 1.0) * (gate * _sigmoid(SWIGLU_ALPHA * gate))
        y = jnp.dot(act.astype(BF16), w2_ref[...], preferred_element_type=F32) + b2_ref[...]
        y_ref[...] = y.astype(BF16)

    @pl.when(pl.program_id(0) >= nb_ref[0])
    def _slack():
        y_ref[...] = jnp.zeros(y_ref.shape, BF16)


def _experts(block_e, n_used, xs, w1, b1, w2, b2):
    n_rows = xs.shape[0]
    nb = n_rows // MOE_TILE
    row_in = lambda i, be, nu: (jnp.minimum(i, nu[0] - 1), 0)
    row = lambda i, be, nu: (i, 0)
    wsel = lambda i, be, nu: (be[i], 0, 0)
    grid_spec = pltpu.PrefetchScalarGridSpec(
        num_scalar_prefetch=2,
        grid=(nb,),
        in_specs=[pl.BlockSpec((MOE_TILE, D_MODEL), row_in),
                  pl.BlockSpec((None, D_MODEL, 2 * D_EXPERT), wsel), pl.BlockSpec((None, 1, 2 * D_EXPERT), wsel),
                  pl.BlockSpec((None, D_EXPERT, D_MODEL), wsel), pl.BlockSpec((None, 1, D_MODEL), wsel)],
        out_specs=pl.BlockSpec((MOE_TILE, D_MODEL), row),
    )
    return pl.pallas_call(
        _expert_kernel,
        grid_spec=grid_spec,
        out_shape=jax.ShapeDtypeStruct((n_rows, D_MODEL), BF16),
        compiler_params=_params(("arbitrary",)),
        name="moe_experts",
    )(block_e, n_used, xs, w1, b1, w2, b2)


def _combine_kernel(off_ref, cnt_ref, start_ref, y_hbm, x1_ref, pos_ref, w_ref, o_ref, buf_s, sem):
    i = pl.program_id(0)
    tm = x1_ref.shape[0]
    rows = buf_s.shape[0]

    @pl.when(i == 0)
    def _clear():
        buf_s[...] = jnp.zeros(buf_s.shape, BF16)

    def copies(wait):
        for e in range(N_EXPERTS):
            idx = i * N_EXPERTS + e
            _run_copies(cnt_ref[idx], tm, buf_s, start_ref[idx], y_hbm, off_ref[idx], sem, False, wait)

    copies(False)
    pos = pos_ref[...]
    w = w_ref[...]
    cid = lax.broadcasted_iota(jnp.int32, (tm, rows), 1)
    wsel = jnp.zeros((tm, rows), F32)
    for kk in range(TOP_K):
        wsel = wsel + jnp.where(cid == pos[:, kk:kk + 1], w[:, kk:kk + 1], 0.0)
    w_hi = wsel.astype(BF16)
    w_lo = (wsel - w_hi.astype(F32)).astype(BF16)
    copies(True)
    ys = buf_s[...]
    o_ref[...] = (x1_ref[...] + jnp.dot(w_hi, ys, preferred_element_type=F32)
                  + jnp.dot(w_lo, ys, preferred_element_type=F32))


def _combine(run_off, run_cnt, run_start, y, x1, pos_tok, w_tok):
    t = x1.shape[0]
    tm = min(ROW_TILE, t)
    grid_spec = pltpu.PrefetchScalarGridSpec(
        num_scalar_prefetch=3,
        grid=(t // tm,),
        in_specs=[pl.BlockSpec(memory_space=pl.ANY),
                  pl.BlockSpec((tm, D_MODEL), lambda i, *_: (i, 0)),
                  pl.BlockSpec((tm, TOP_K), lambda i, *_: (i, 0)), pl.BlockSpec((tm, TOP_K), lambda i, *_: (i, 0))],
        out_specs=pl.BlockSpec((tm, D_MODEL), lambda i, *_: (i, 0)),
        scratch_shapes=[pltpu.VMEM((_sorted_rows(tm), D_MODEL), BF16), pltpu.SemaphoreType.DMA(())],
    )
    return pl.pallas_call(
        _combine_kernel,
        grid_spec=grid_spec,
        out_shape=jax.ShapeDtypeStruct((t, D_MODEL), F32),
        compiler_params=_params(("arbitrary",)),
        name="moe_combine",
    )(run_off, run_cnt, run_start, y, x1, pos_tok, w_tok)


def kernel(x, g_mix, w_in, q_norm_g, k_norm_g, lambda_q1, lambda_k1, lambda_q2, lambda_k2, attn_sub_g, rel_bias,
           w_attn_o, conv_w, conv_b, dt_bias, a_log, d_skip, ssm_norm_g, w_ssm_o, w_out, g_ffn, w_router, b_router,
           w1, b1, w2, b2):
    b, s, d = x.shape
    t = b * s
    l = 0
    x2 = x.reshape(t, d)

    w = w_in[l]
    c0 = Q_COLS + K_COLS + V_COLS
    c1 = c0 + D_INNER + CONV_DIM
    w_qkv = w[:, :c0].astype(BF16)
    w_dt = jnp.pad(w[:, c1:c1 + SSM_HEADS], ((0, 0), (0, DT_PAD - SSM_HEADS)))
    w_ssm = jnp.concatenate([w[:, c0:c1], w_dt], axis=1).astype(BF16)
    w_gate = w[:, c1 + SSM_HEADS:].astype(BF16)
    n_hd = Q_COLS // ATTN_HEAD_DIM
    gq = (jnp.tile(q_norm_g[l], n_hd) * (ATTN_HEAD_DIM ** -0.5 * LOG2E)).reshape(1, Q_COLS)
    gk = jnp.tile(k_norm_g[l], n_hd).reshape(1, K_COLS)
    gm = g_mix[l].reshape(1, d)

    qn, kn, v = _qkv_proj(x2, gm, w_qkv, gq, gk)
    pad_h = (0, DT_PAD - SSM_HEADS)
    z, xbc, dt, acum, acum_t = _ssm_proj(x2, gm, w_ssm, conv_w[l], conv_b[l].reshape(1, -1),
                                         jnp.pad(dt_bias[l], pad_h).reshape(1, -1),
                                         jnp.pad(a_log[l], pad_h).reshape(1, -1), s)

    lam_vecs = jnp.stack([lambda_q1[l], lambda_k1[l], lambda_q2[l], lambda_k2[l]]).astype(F32)
    attn = _diff_attention(qn.reshape(b, s, -1), kn.reshape(b, s, -1), v.reshape(b, s, -1), rel_bias, q_norm_g[l],
                           k_norm_g[l], lam_vecs, attn_sub_g[l].reshape(1, ATTN_V_DIM))

    y = _ssd(z.reshape(b, s, -1), xbc.reshape(b, s, -1), dt.reshape(b, s, -1), acum.reshape(b, s, -1),
             acum_t.reshape(b, s, -1), jnp.repeat(d_skip[l], SSM_HEAD_DIM).reshape(1, -1),
             ssm_norm_g[l].reshape(1, -1))

    x1, h2, pos, top_w, tile_cnt = _merge(
        x2, attn.reshape(t, -1), y.reshape(t, -1), gm, w_gate, w_attn_o[l].astype(BF16), w_ssm_o[l].astype(BF16),
        w_out[l].astype(BF16), g_ffn[l].reshape(1, d), w_router[l].T, b_router[l].reshape(-1, 1))

    n_tiles = t // min(ROW_TILE, t)
    cnt = tile_cnt.reshape(n_tiles, N_EXPERTS, LANES)[:, :, 0]
    cnt_al = _round_up(cnt, ROW_ALIGN)
    run_start = jnp.cumsum(cnt_al, axis=1) - cnt_al
    padded = _round_up(jnp.sum(cnt_al, axis=0), MOE_TILE)
    end_pad = jnp.cumsum(padded)
    run_off = (end_pad - padded)[None, :] + jnp.cumsum(cnt_al, axis=0) - cnt_al
    n_rows = _round_up(t * TOP_K + n_tiles * N_EXPERTS * (ROW_ALIGN - 1), MOE_TILE) + N_EXPERTS * MOE_TILE
    block_start = jnp.arange(n_rows // MOE_TILE, dtype=jnp.int32) * MOE_TILE
    block_e = jnp.minimum(jnp.sum(block_start[:, None] >= end_pad[None, :], axis=1), N_EXPERTS - 1).astype(jnp.int32)
    n_used = (end_pad[-1:] // MOE_TILE).astype(jnp.int32)
    last_block = jnp.concatenate([jnp.maximum(end_pad - MOE_TILE, 0), end_pad[-1:] // MOE_TILE]).astype(jnp.int32)
    tables = [a.reshape(-1).astype(jnp.int32) for a in (run_off, cnt_al, run_start)]

    xs = _dispatch(*tables, last_block, h2, pos, n_rows)
    ys = _experts(block_e, n_used, xs, w1[l].astype(BF16), b1[l][:, None, :], w2[l].astype(BF16), b2[l][:, None, :])
    out = _combine(*tables, ys, x1, pos.T, top_w.T)
    return out.reshape(b, s, d)
```

```python
import functools
import math

import jax
import jax.numpy as jnp
from jax import lax
from jax.experimental import pallas as pl
from jax.experimental.pallas import tpu as pltpu

F32 = jnp.float32
BF16 = jnp.bfloat16

D_MODEL = 1024
ATTN_HEADS = 8
ATTN_HEAD_DIM = 64
ATTN_V_DIM = 2 * ATTN_HEAD_DIM
LAMBDA_INIT = 0.8 - 0.6 * math.exp(-0.3 * 0)
NUM_BUCKETS = 32
MAX_DISTANCE = 128
D_INNER = 2 * D_MODEL
SSM_HEAD_DIM = 64
SSM_HEADS = D_INNER // SSM_HEAD_DIM
SSM_GROUPS = 8
SSM_HEADS_PER_GROUP = SSM_HEADS // SSM_GROUPS
D_STATE = 128
CONV_WIDTH = 4
SSM_CHUNK = 128
CONV_DIM = D_INNER + 2 * SSM_GROUPS * D_STATE
N_EXPERTS = 32
TOP_K = 4
D_EXPERT = D_MODEL
SWIGLU_LIMIT = 7.0
SWIGLU_ALPHA = 1.702
RMS_EPS = 1e-6
SSM_EPS = 1e-5
Q_COLS = ATTN_HEADS * 2 * ATTN_HEAD_DIM
K_COLS = Q_COLS
V_COLS = ATTN_HEADS * ATTN_V_DIM

LANES = 128
MXU_DIM = 256
DT_PAD = LANES
NEG_BIG = -1e30
LOG2E = math.log2(math.e)
EXP2_SAFE_BOUND = 80.0
VMEM_LIMIT = 56 * 1024 * 1024

ROW_TILE = 512
ATTN_TILE = 512
MOE_TILE = 512
ROW_ALIGN = 16


def _rms(x, eps):
    return x * lax.rsqrt(jnp.mean(x * x, axis=-1, keepdims=True) + eps)


def _sigmoid(x):
    return 1.0 / (1.0 + jnp.exp(-x))


def _params(sem):
    return pltpu.CompilerParams(dimension_semantics=sem, vmem_limit_bytes=VMEM_LIMIT)


def _resident(shape):
    return pl.BlockSpec(shape, lambda *_: (0,) * len(shape), pipeline_mode=pl.Buffered(1))


def _qkv_kernel(x_ref, g_ref, w_ref, gq_ref, gk_ref, q_ref, k_ref, v_ref):
    h = (_rms(x_ref[...], RMS_EPS) * g_ref[...]).astype(BF16)
    qkv = jnp.dot(h, w_ref[...], preferred_element_type=F32)
    r = lax.broadcasted_iota(jnp.int32, (MXU_DIM, MXU_DIM), 0) // ATTN_HEAD_DIM
    c = lax.broadcasted_iota(jnp.int32, (MXU_DIM, MXU_DIM), 1) // ATTN_HEAD_DIM
    group_ones = jnp.where(r == c, 1.0, 0.0).astype(BF16)

    def head_norm(t, gain_ref, out_ref):
        for cc in range(Q_COLS // MXU_DIM):
            sl = slice(cc * MXU_DIM, (cc + 1) * MXU_DIM)
            tc = t[:, sl]
            ss = jnp.dot((tc * tc).astype(BF16), group_ones, preferred_element_type=F32)
            out_ref[:, sl] = (tc * lax.rsqrt(ss * (1.0 / ATTN_HEAD_DIM) + RMS_EPS) * gain_ref[:, sl]).astype(BF16)

    head_norm(qkv[:, :Q_COLS], gq_ref, q_ref)
    head_norm(qkv[:, Q_COLS:Q_COLS + K_COLS], gk_ref, k_ref)
    v_ref[...] = qkv[:, Q_COLS + K_COLS:].astype(BF16)


def _qkv_proj(x2, g_mix, w_qkv, gq, gk):
    t = x2.shape[0]
    tm = min(ROW_TILE, t)
    row = lambda i: (i, 0)
    fix = lambda i: (0, 0)
    out = jax.ShapeDtypeStruct((t, D_MODEL), BF16)
    return pl.pallas_call(
        _qkv_kernel,
        grid=(t // tm,),
        in_specs=[pl.BlockSpec((tm, D_MODEL), row), pl.BlockSpec((1, D_MODEL), fix),
                  _resident((D_MODEL, 3 * D_MODEL)), pl.BlockSpec((1, D_MODEL), fix),
                  pl.BlockSpec((1, D_MODEL), fix)],
        out_specs=[pl.BlockSpec((tm, D_MODEL), row)] * 3,
        out_shape=[out, out, out],
        compiler_params=_params(("arbitrary",)),
        name="qkv_proj",
    )(x2, g_mix, w_qkv, gq, gk)


CONV_COLS = 1024
SUBLANES = 8
CHUNK_VREGS = SSM_CHUNK // SUBLANES


def _chunk_time(row):
    return (row >> 3) + CHUNK_VREGS * (row & (SUBLANES - 1))


def _ssm_proj_kernel(x_ref, g_ref, w_ref, cw_ref, cb_ref, dtb_ref, alog_ref, z_ref, xbc_ref, dt_ref, acum_ref,
                     acumt_ref, tail_s, h_s, p_s, *, tiles_per_seq):
    tm = x_ref.shape[0]
    nb = tm // SSM_CHUNK
    taps = CONV_WIDTH - 1

    @pl.when(pl.program_id(0) % tiles_per_seq == 0)
    def _sequence_start():
        tail_s[...] = jnp.zeros(tail_s.shape, F32)

    h = (_rms(x_ref[...], RMS_EPS) * g_ref[...]).astype(BF16)
    rr = lax.broadcasted_iota(jnp.int32, (SSM_CHUNK, SSM_CHUNK), 0)
    cc = lax.broadcasted_iota(jnp.int32, (SSM_CHUNK, SSM_CHUNK), 1)
    pick = jnp.where(cc == _chunk_time(rr), 1.0, 0.0).astype(BF16)
    h = jnp.concatenate(
        [jnp.dot(pick, h[b * SSM_CHUNK:(b + 1) * SSM_CHUNK], preferred_element_type=F32).astype(BF16)
         for b in range(nb)], axis=0)

    h_s[...] = h

    def project(stage):
        return jnp.dot(h_s[...], w_ref[:, stage * CONV_COLS:(stage + 1) * CONV_COLS], preferred_element_type=F32)

    z_stages = D_INNER // CONV_COLS
    n_stages = z_stages + CONV_DIM // CONV_COLS
    sub = lax.broadcasted_iota(jnp.int32, (nb * taps, SUBLANES, CONV_COLS), 1)
    p_s[0] = project(0)
    for stage in range(n_stages):
        if stage + 1 < n_stages:
            p_s[(stage + 1) % 2] = project(stage + 1)
        p = p_s[stage % 2]
        if stage < z_stages:
            z_ref[:, stage * CONV_COLS:(stage + 1) * CONV_COLS] = (p * _sigmoid(p)).astype(BF16)
            continue
        c = stage - z_stages
        cols = slice(c * CONV_COLS, (c + 1) * CONV_COLS)
        p4 = p.reshape(nb, CHUNK_VREGS, SUBLANES, CONV_COLS)
        last = p4[:, CHUNK_VREGS - taps:]
        seq = jnp.concatenate([tail_s[c][None], last], axis=0).reshape((nb + 1) * taps, SUBLANES, CONV_COLS)
        tail_s[c] = last[nb - 1]
        rolled = pltpu.roll(seq, 1, axis=1)
        wrapped = jnp.where(sub == 0, rolled[:nb * taps], rolled[taps:]).reshape(nb, taps, SUBLANES, CONV_COLS)
        conv = cb_ref[:, cols] + cw_ref[taps:taps + 1, cols] * p4
        for back in range(1, CONV_WIDTH):
            shifted = jnp.concatenate([wrapped[:, taps - back:], p4[:, :CHUNK_VREGS - back]], axis=1)
            conv = conv + cw_ref[taps - back:taps - back + 1, cols] * shifted
        xbc_ref[:, cols] = (conv * _sigmoid(conv)).reshape(tm, CONV_COLS).astype(BF16)
    dtl = jnp.dot(h_s[...], w_ref[:, D_INNER + CONV_DIM:], preferred_element_type=F32) + dtb_ref[...]
    dt = jnp.maximum(dtl, 0.0) + jnp.log(1.0 + jnp.exp(-jnp.abs(dtl)))
    dt_ref[...] = dt
    a = dt * (-jnp.exp(alog_ref[...]))
    upto = jnp.where(_chunk_time(rr) >= _chunk_time(cc), 1.0, 0.0).astype(F32)
    for b in range(nb):
        rows = slice(b * SSM_CHUNK, (b + 1) * SSM_CHUNK)
        acum = jnp.dot(upto, a[rows], preferred_element_type=F32, precision=lax.Precision.HIGHEST)
        acum_ref[rows, :] = acum
        acumt_ref[rows, :] = acum.T


def _ssm_proj(x2, g_mix, w_ssm, conv_w, conv_b, dt_bias, a_log, seq_len):
    t = x2.shape[0]
    tm = min(ROW_TILE, seq_len)
    assert seq_len % tm == 0
    ncol = D_INNER + CONV_DIM + DT_PAD
    row = lambda i: (i, 0)
    fix = lambda i: (0, 0)
    heads = jax.ShapeDtypeStruct((t, DT_PAD), F32)
    return pl.pallas_call(
        functools.partial(_ssm_proj_kernel, tiles_per_seq=seq_len // tm),
        grid=(t // tm,),
        in_specs=[pl.BlockSpec((tm, D_MODEL), row), pl.BlockSpec((1, D_MODEL), fix),
                  _resident((D_MODEL, ncol)), pl.BlockSpec((CONV_WIDTH, CONV_DIM), fix),
                  pl.BlockSpec((1, CONV_DIM), fix), pl.BlockSpec((1, DT_PAD), fix), pl.BlockSpec((1, DT_PAD), fix)],
        out_specs=[pl.BlockSpec((tm, D_INNER), row), pl.BlockSpec((tm, CONV_DIM), row)]
        + [pl.BlockSpec((tm, DT_PAD), row)] * 3,
        out_shape=[jax.ShapeDtypeStruct((t, D_INNER), BF16), jax.ShapeDtypeStruct((t, CONV_DIM), BF16),
                   heads, heads, heads],
        scratch_shapes=[pltpu.VMEM((CONV_DIM // CONV_COLS, CONV_WIDTH - 1, SUBLANES, CONV_COLS), F32),
                        pltpu.VMEM((tm, D_MODEL), BF16), pltpu.VMEM((2, tm, CONV_COLS), F32)],
        compiler_params=_params(("arbitrary",)),
        name="ssm_proj",
    )(x2, g_mix, w_ssm, conv_w, conv_b, dt_bias, a_log)


def _split_maps(q):
    lane = lax.broadcasted_iota(jnp.int32, q.shape, 1)
    zero = jnp.zeros_like(q)
    return jnp.where(lane < ATTN_HEAD_DIM, q, zero), jnp.where(lane >= ATTN_HEAD_DIM, q, zero)


def _attn_finalize(acc1, l1, acc2, l2, lam_ref, subg_ref, o_ref):
    lam_v = lam_ref[...]
    lam = (jnp.exp(jnp.sum(lam_v[0:1] * lam_v[1:2], axis=-1, keepdims=True))
           - jnp.exp(jnp.sum(lam_v[2:3] * lam_v[3:4], axis=-1, keepdims=True)) + LAMBDA_INIT)
    o = acc1 / l1 - lam * (acc2 / l2)
    o_ref[...] = (_rms(o, RMS_EPS) * subg_ref[...] * (1.0 - LAMBDA_INIT)).astype(BF16)


def _attn_bounded_kernel(q_ref, k_ref, v_ref, bias_ref, lam_ref, subg_ref, o_ref, qq_s, vv_s, acc_s, s_s):
    i = pl.program_id(2)
    tq = q_ref.shape[0]
    tk = tq

    @pl.when(i == 0)
    def _extend_v():
        vv_s[:, :ATTN_V_DIM] = v_ref[...]
        vv_s[:, ATTN_V_DIM:] = jnp.ones((vv_s.shape[0], ATTN_V_DIM), BF16)

    q1, q2 = _split_maps(q_ref[...])
    qq_s[0:tq, :] = q1
    qq_s[tq:2 * tq, :] = q2
    acc_s[...] = jnp.zeros(acc_s.shape, F32)
    contract_last = (((1,), (1,)), ((), ()))

    def logits(j, kind):
        rows = pl.ds(pl.multiple_of(j * tk, tk), tk)
        s = lax.dot_general(qq_s[...], k_ref[rows, :], contract_last, preferred_element_type=F32)
        if kind is not None:
            b = bias_ref[kind]
            s = jnp.concatenate([s[0:tq] + b, s[tq:2 * tq] + b], axis=0)
        return s

    def accumulate(j, slot):
        rows = pl.ds(pl.multiple_of(j * tk, tk), tk)
        acc_s[...] += jnp.dot(jnp.exp2(s_s[slot]).astype(BF16), vv_s[rows, :], preferred_element_type=F32)

    def step(t, slot, next_kind):
        s_s[1 - slot] = logits(t + 1, next_kind)
        accumulate(t, slot)

    odd = (i & 1) == 1

    @pl.when(i == 0)
    def _first_is_diag():
        s_s[0] = logits(0, 0)

    @pl.when(i == 1)
    def _first_is_prev():
        s_s[1] = logits(0, 1)

    @pl.when(jnp.logical_and(i >= 2, jnp.logical_not(odd)))
    def _first_far_even():
        s_s[0] = logits(0, None)

    @pl.when(jnp.logical_and(i >= 2, odd))
    def _first_far_odd():
        s_s[1] = logits(0, None)

    @pl.when(jnp.logical_and(i >= 3, odd))
    def _unpaired_far_step():
        step(0, 1, None)

    def far_pair(pp, carry):
        t = (i & 1) + 2 * pp
        step(t, 0, None)
        step(t + 1, 1, None)
        return carry

    lax.fori_loop(0, jnp.maximum(i - 2, 0) // 2, far_pair, 0)

    @pl.when(i >= 2)
    def _next_is_prev():
        step(i - 2, 0, 1)

    @pl.when(i >= 1)
    def _next_is_diag():
        step(i - 1, 1, 0)

    accumulate(i, 0)
    _attn_finalize(acc_s[0:tq, :ATTN_V_DIM], acc_s[0:tq, ATTN_V_DIM:], acc_s[tq:2 * tq, :ATTN_V_DIM],
                   acc_s[tq:2 * tq, ATTN_V_DIM:], lam_ref, subg_ref, o_ref)


def _attn_online_kernel(it_ref, jt_ref, q_ref, k_ref, v_ref, bias_ref, lam_ref, subg_ref, o_ref,
                        q1_s, q2_s, m1_s, m2_s, l1_s, l2_s, acc1_s, acc2_s):
    step = pl.program_id(2)
    i = it_ref[step]
    j = jt_ref[step]

    @pl.when(j == 0)
    def _init():
        q1_s[...], q2_s[...] = _split_maps(q_ref[...])
        for m_s, l_s, acc_s in ((m1_s, l1_s, acc1_s), (m2_s, l2_s, acc2_s)):
            m_s[...] = jnp.full(m_s.shape, NEG_BIG, F32)
            l_s[...] = jnp.zeros(l_s.shape, F32)
            acc_s[...] = jnp.zeros(acc_s.shape, F32)

    def update(bias):
        k = k_ref[...]
        v = v_ref[...]
        contract_last = (((1,), (1,)), ((), ()))
        for q_s, m_s, l_s, acc_s in ((q1_s, m1_s, l1_s, acc1_s), (q2_s, m2_s, l2_s, acc2_s)):
            s = lax.dot_general(q_s[...], k, contract_last, preferred_element_type=F32)
            if bias is not None:
                s = s + bias_ref[bias]
            m_old = m_s[...]
            m_new = jnp.maximum(m_old, jnp.max(s, axis=-1, keepdims=True))
            alpha = jnp.exp2(m_old - m_new)
            p = jnp.exp2(s - m_new)
            l_s[...] = alpha * l_s[...] + jnp.sum(p, axis=-1, keepdims=True)
            acc_s[...] = alpha * acc_s[...] + jnp.dot(p.astype(BF16), v, preferred_element_type=F32)
            m_s[...] = m_new

    @pl.when(j == i)
    def _diag():
        update(0)

    @pl.when(j == i - 1)
    def _prev():
        update(1)

    @pl.when(j < i - 1)
    def _far():
        update(None)

    @pl.when(j == i)
    def _finalize():
        _attn_finalize(acc1_s[...], l1_s[...], acc2_s[...], l2_s[...], lam_ref, subg_ref, o_ref)


def _t5_bucket(dist):
    n = jnp.maximum(dist, 0)
    max_exact = NUM_BUCKETS // 2
    scaled = jnp.log(jnp.maximum(n, 1).astype(F32) / max_exact) / math.log(MAX_DISTANCE / max_exact)
    large = max_exact + (scaled * (NUM_BUCKETS - max_exact)).astype(jnp.int32)
    large = jnp.minimum(large, NUM_BUCKETS - 1)
    return jnp.where(n < max_exact, n, large)


def _bias_tiles(rel_bias, tile):
    blk = MAX_DISTANCE
    assert tile % blk == 0
    nb = tile // blk
    table = (rel_bias - rel_bias[NUM_BUCKETS - 1]).astype(F32) * LOG2E
    r = jnp.arange(blk, dtype=jnp.int32)
    d0 = r[:, None] - r[None, :]

    def lookup(dist):
        onehot = (_t5_bucket(dist)[..., None] == jnp.arange(NUM_BUCKETS, dtype=jnp.int32)).astype(F32)
        return jnp.einsum('qkn,nh->hqk', onehot, table, precision=lax.Precision.HIGHEST)

    on_diag = jnp.where(d0[None] >= 0, lookup(d0), NEG_BIG)
    sub_diag = lookup(d0 + blk)
    zeros = jnp.zeros_like(sub_diag)
    masked = jnp.full_like(sub_diag, NEG_BIG)

    def assemble(pick):
        return jnp.concatenate(
            [jnp.concatenate([pick(bi, bj) for bj in range(nb)], axis=-1) for bi in range(nb)], axis=-2)

    diag_tile = assemble(lambda bi, bj: on_diag if bi == bj else sub_diag if bi == bj + 1 else zeros if bi > bj else masked)
    prev_tile = assemble(lambda bi, bj: sub_diag if (bi == 0 and bj == nb - 1) else zeros)
    return jnp.stack([diag_tile, prev_tile], axis=1)


def _attn_bounded(qn, kn, v, bias, lam_vecs, sub_g):
    b, s, _ = qn.shape
    tile = min(ATTN_TILE, s)
    q_map = lambda bb, h, i: (bb, i, h)
    kv_map = lambda bb, h, i: (bb, 0, h)
    return pl.pallas_call(
        _attn_bounded_kernel,
        grid=(b, ATTN_HEADS, s // tile),
        in_specs=[pl.BlockSpec((None, tile, ATTN_V_DIM), q_map),
                  pl.BlockSpec((None, s, ATTN_V_DIM), kv_map),
                  pl.BlockSpec((None, s, ATTN_V_DIM), kv_map),
                  pl.BlockSpec((None, 2, tile, tile), lambda bb, h, i: (h, 0, 0, 0)),
                  pl.BlockSpec((4, ATTN_HEAD_DIM), lambda bb, h, i: (0, 0)),
                  pl.BlockSpec((1, ATTN_V_DIM), lambda bb, h, i: (0, 0))],
        out_specs=pl.BlockSpec((None, tile, ATTN_V_DIM), q_map),
        out_shape=jax.ShapeDtypeStruct((b, s, V_COLS), BF16),
        scratch_shapes=[pltpu.VMEM((2 * tile, ATTN_V_DIM), BF16), pltpu.VMEM((s, 2 * ATTN_V_DIM), BF16),
                        pltpu.VMEM((2 * tile, 2 * ATTN_V_DIM), F32), pltpu.VMEM((2, 2 * tile, tile), F32)],
        compiler_params=_params(("arbitrary", "arbitrary", "arbitrary")),
        name="diff_attn_bounded",
    )(qn, kn, v, bias, lam_vecs, sub_g)


def _attn_online(qn, kn, v, bias, lam_vecs, sub_g):
    b, s, _ = qn.shape
    tile = min(ATTN_TILE, s)
    nq = s // tile
    it = jnp.asarray([i for i in range(nq) for _ in range(i + 1)], jnp.int32)
    jt = jnp.asarray([j for i in range(nq) for j in range(i + 1)], jnp.int32)
    q_map = lambda bb, h, st, it_r, jt_r: (bb, it_r[st], h)
    kv_map = lambda bb, h, st, it_r, jt_r: (bb, jt_r[st], h)
    grid_spec = pltpu.PrefetchScalarGridSpec(
        num_scalar_prefetch=2,
        grid=(b, ATTN_HEADS, int(it.shape[0])),
        in_specs=[pl.BlockSpec((None, tile, ATTN_V_DIM), q_map),
                  pl.BlockSpec((None, tile, ATTN_V_DIM), kv_map),
                  pl.BlockSpec((None, tile, ATTN_V_DIM), kv_map),
                  pl.BlockSpec((None, 2, tile, tile), lambda bb, h, st, it_r, jt_r: (h, 0, 0, 0)),
                  pl.BlockSpec((4, ATTN_HEAD_DIM), lambda bb, h, st, it_r, jt_r: (0, 0)),
                  pl.BlockSpec((1, ATTN_V_DIM), lambda bb, h, st, it_r, jt_r: (0, 0))],
        out_specs=pl.BlockSpec((None, tile, ATTN_V_DIM), q_map),
        scratch_shapes=[pltpu.VMEM((tile, ATTN_V_DIM), BF16), pltpu.VMEM((tile, ATTN_V_DIM), BF16),
                        pltpu.VMEM((tile, 1), F32), pltpu.VMEM((tile, 1), F32),
                        pltpu.VMEM((tile, 1), F32), pltpu.VMEM((tile, 1), F32),
                        pltpu.VMEM((tile, ATTN_V_DIM), F32), pltpu.VMEM((tile, ATTN_V_DIM), F32)],
    )
    return pl.pallas_call(
        _attn_online_kernel,
        grid_spec=grid_spec,
        out_shape=jax.ShapeDtypeStruct((b, s, V_COLS), BF16),
        compiler_params=_params(("arbitrary", "arbitrary", "arbitrary")),
        name="diff_attn_online",
    )(it, jt, qn, kn, v, bias, lam_vecs, sub_g)


def _diff_attention(qn, kn, v, rel_bias, q_gain, k_gain, lam_vecs, sub_g):
    tile = min(ATTN_TILE, qn.shape[1])
    bias = _bias_tiles(rel_bias, tile)
    spread = jnp.max(jnp.abs(rel_bias - rel_bias[NUM_BUCKETS - 1]))
    bound = LOG2E * (1.05 * math.sqrt(ATTN_HEAD_DIM) * jnp.max(jnp.abs(q_gain)) * jnp.max(jnp.abs(k_gain)) + spread)
    args = (qn, kn, v, bias, lam_vecs, sub_g)
    return lax.cond(bound < EXP2_SAFE_BOUND, lambda a: _attn_bounded(*a), lambda a: _attn_online(*a), args)


def _ssd_kernel(z_ref, xbc_ref, dt_ref, acum_ref, acumt_ref, dskip_ref, ng_ref, y_ref, state_s):
    L = SSM_CHUNK

    @pl.when(pl.program_id(1) == 0)
    def _reset():
        state_s[...] = jnp.zeros(state_s.shape, F32)

    xs = xbc_ref[:, :D_INNER].astype(F32)
    bm = xbc_ref[:, D_INNER:D_INNER + SSM_GROUPS * D_STATE]
    cm = xbc_ref[:, D_INNER + SSM_GROUPS * D_STATE:]

    dt = dt_ref[...]
    row = lax.broadcasted_iota(jnp.int32, (L, L), 0)
    col = lax.broadcasted_iota(jnp.int32, (L, L), 1)
    causal = _chunk_time(row) >= _chunk_time(col)
    acum = acum_ref[...]
    acum_t = acumt_ref[...]
    a_end = acum[L - 1:L, :]
    w_end = jnp.exp(a_end - acum) * dt
    e_acum = jnp.exp(acum)
    e_end = jnp.exp(a_end)
    lane = lax.broadcasted_iota(jnp.int32, (L, LANES), 1)
    low = lane < SSM_HEAD_DIM

    def pair_cols(arr, h0):
        return jnp.where(low, arr[:, h0:h0 + 1], arr[:, h0 + 1:h0 + 2])

    contract_last = (((1,), (1,)), ((), ()))
    contract_first = (((0,), (0,)), ((), ()))
    y_parts = []
    for g in range(SSM_GROUPS):
        bg = bm[:, g * D_STATE:(g + 1) * D_STATE]
        cg = cm[:, g * D_STATE:(g + 1) * D_STATE]
        cb = lax.dot_general(cg, bg, contract_last, preferred_element_type=F32)
        st = state_s[g]
        y_off = jnp.dot(cg, st.astype(BF16), preferred_element_type=F32)
        xw_parts = []
        dec_parts = []
        for pr in range(SSM_HEADS_PER_GROUP // 2):
            h0 = g * SSM_HEADS_PER_GROUP + 2 * pr
            ch = slice(h0 * SSM_HEAD_DIM, (h0 + 2) * SSM_HEAD_DIM)
            x_pair = xs[:, ch]
            xdt = (x_pair * pair_cols(dt, h0)).astype(BF16)
            yd = []
            for hh in (h0, h0 + 1):
                seg = acum[:, hh:hh + 1] - acum_t[hh:hh + 1, :]
                decay = jnp.exp(jnp.where(causal, seg, NEG_BIG))
                yd.append(jnp.dot((cb * decay).astype(BF16), xdt, preferred_element_type=F32))
            y_diag = jnp.where(low, yd[0], yd[1])
            off = y_off[:, 2 * pr * SSM_HEAD_DIM:(2 * pr + 2) * SSM_HEAD_DIM]
            y_parts.append(y_diag + off * pair_cols(e_acum, h0))
            xw_parts.append(x_pair * pair_cols(w_end, h0))
            dec_parts.append(jnp.where(low[0:1], e_end[:, h0:h0 + 1], e_end[:, h0 + 1:h0 + 2]))
        xw = jnp.concatenate(xw_parts, axis=-1).astype(BF16)
        dec = jnp.concatenate(dec_parts, axis=-1)
        state_s[g] = st * dec + lax.dot_general(bg, xw, contract_first, preferred_element_type=F32)
    y = (jnp.concatenate(y_parts, axis=-1) + dskip_ref[...] * xs) * z_ref[...].astype(F32)
    gsz = D_INNER // SSM_GROUPS
    unpick = jnp.where(row == _chunk_time(col), 1.0, 0.0).astype(BF16)
    for g in range(SSM_GROUPS):
        sl = slice(g * gsz, (g + 1) * gsz)
        yn = (_rms(y[:, sl], SSM_EPS) * ng_ref[:, sl]).astype(BF16)
        y_ref[:, sl] = jnp.dot(unpick, yn, preferred_element_type=F32).astype(BF16)


def _ssd(z, xbc, dt, acum, acum_t, d_skip_ch, norm_g):
    b, s, _ = z.shape
    nc = s // SSM_CHUNK
    blk = lambda bb, c: (bb, c, 0)
    fix = lambda bb, c: (0, 0)
    return pl.pallas_call(
        _ssd_kernel,
        grid=(b, nc),
        in_specs=[pl.BlockSpec((None, SSM_CHUNK, D_INNER), blk), pl.BlockSpec((None, SSM_CHUNK, CONV_DIM), blk)]
        + [pl.BlockSpec((None, SSM_CHUNK, DT_PAD), blk)] * 3
        + [pl.BlockSpec((1, D_INNER), fix), pl.BlockSpec((1, D_INNER), fix)],
        out_specs=pl.BlockSpec((None, SSM_CHUNK, D_INNER), blk),
        out_shape=jax.ShapeDtypeStruct((b, s, D_INNER), BF16),
        scratch_shapes=[pltpu.VMEM((SSM_GROUPS, D_STATE, SSM_HEADS_PER_GROUP * SSM_HEAD_DIM), F32)],
        compiler_params=_params(("arbitrary", "arbitrary")),
        name="ssd",
    )(z, xbc, dt, acum, acum_t, d_skip_ch, norm_g)


def _round_up(n, m):
    return (n + m - 1) // m * m


def _merge_kernel(x_ref, attn_ref, y_ref, gmix_ref, wg_ref, wao_ref, wso_ref, wout_ref, gffn_ref, wr_ref, br_ref,
                  x1_ref, h2_ref, pos_ref, topw_ref, cnt_ref):
    x = x_ref[...]
    tm = x.shape[0]
    h = (_rms(x, RMS_EPS) * gmix_ref[...]).astype(BF16)
    gates = _sigmoid(jnp.dot(h, wg_ref[...], preferred_element_type=F32))
    attn_out = jnp.dot(attn_ref[...], wao_ref[...], preferred_element_type=F32)
    ssm_out = jnp.dot(y_ref[...], wso_ref[...], preferred_element_type=F32)
    merged = gates[:, :D_MODEL] * attn_out + gates[:, D_MODEL:] * ssm_out
    x1 = x + jnp.dot(merged.astype(BF16), wout_ref[...], preferred_element_type=F32)
    x1_ref[...] = x1
    h2 = _rms(x1, RMS_EPS) * gffn_ref[...]
    h2_ref[...] = h2.astype(BF16)

    logits = lax.dot_general(wr_ref[...], h2, (((1,), (1,)), ((), ())), preferred_element_type=F32,
                             precision=lax.Precision.HIGHEST) + br_ref[...]
    eid = lax.broadcasted_iota(jnp.int32, logits.shape, 0)
    vals, hits = [], []
    member = jnp.zeros(logits.shape, F32)
    work = logits
    for kk in range(TOP_K):
        m = jnp.max(work, axis=0, keepdims=True)
        idx = jnp.min(jnp.where(work == m, eid, N_EXPERTS), axis=0, keepdims=True)
        hit = eid == idx
        vals.append(m)
        hits.append(hit)
        member = jnp.where(hit, 1.0, member)
        work = jnp.where(hit, -jnp.inf, work)
    ex = [jnp.exp(v - vals[0]) for v in vals]
    denom = ex[0] + ex[1] + ex[2] + ex[3]
    for kk in range(TOP_K):
        topw_ref[kk:kk + 1, :] = ex[kk] / denom

    r = lax.broadcasted_iota(jnp.int32, (tm, tm), 0)
    c = lax.broadcasted_iota(jnp.int32, (tm, tm), 1)
    before = jnp.where(r < c, 1.0, 0.0).astype(BF16)
    prefix = jnp.dot(member.astype(BF16), before, preferred_element_type=F32)
    cnt = jnp.sum(member, axis=1, keepdims=True).astype(jnp.int32)
    cnt_al = jnp.bitwise_and(cnt + (ROW_ALIGN - 1), -ROW_ALIGN)
    cnt_al = jnp.broadcast_to(cnt_al, (N_EXPERTS, LANES)).astype(F32)
    er = lax.broadcasted_iota(jnp.int32, (N_EXPERTS, N_EXPERTS), 0)
    ec = lax.broadcasted_iota(jnp.int32, (N_EXPERTS, N_EXPERTS), 1)
    lower = jnp.where(ec < er, 1.0, 0.0).astype(F32)
    run_start = jnp.dot(lower, cnt_al, preferred_element_type=F32, precision=lax.Precision.HIGHEST)[:, 0:1]
    base = prefix + run_start
    for kk in range(TOP_K):
        pos_ref[kk:kk + 1, :] = jnp.sum(jnp.where(hits[kk], base, 0.0), axis=0, keepdims=True).astype(jnp.int32)
    cnt_ref[...] = jnp.broadcast_to(cnt, cnt_ref.shape)


def _merge(x2, attn, y, g_mix, w_gate, w_ao, w_so, w_out, g_ffn, w_r_t, b_r):
    t = x2.shape[0]
    tm = min(ROW_TILE, t)
    row = lambda i: (i, 0)
    colb = lambda i: (0, i)
    fix = lambda i: (0, 0)
    return pl.pallas_call(
        _merge_kernel,
        grid=(t // tm,),
        in_specs=[pl.BlockSpec((tm, D_MODEL), row), pl.BlockSpec((tm, V_COLS), row), pl.BlockSpec((tm, D_INNER), row),
                  pl.BlockSpec((1, D_MODEL), fix), _resident((D_MODEL, 2 * D_MODEL)),
                  _resident((V_COLS, D_MODEL)), _resident((D_INNER, D_MODEL)),
                  _resident((D_MODEL, D_MODEL)), pl.BlockSpec((1, D_MODEL), fix),
                  pl.BlockSpec((N_EXPERTS, D_MODEL), fix), pl.BlockSpec((N_EXPERTS, 1), fix)],
        out_specs=[pl.BlockSpec((tm, D_MODEL), row), pl.BlockSpec((tm, D_MODEL), row),
                   pl.BlockSpec((TOP_K, tm), colb), pl.BlockSpec((TOP_K, tm), colb),
                   pl.BlockSpec((N_EXPERTS, LANES), row)],
        out_shape=[jax.ShapeDtypeStruct((t, D_MODEL), F32), jax.ShapeDtypeStruct((t, D_MODEL), BF16),
                   jax.ShapeDtypeStruct((TOP_K, t), jnp.int32), jax.ShapeDtypeStruct((TOP_K, t), F32),
                   jax.ShapeDtypeStruct((t // tm * N_EXPERTS, LANES), jnp.int32)],
        compiler_params=_params(("arbitrary",)),
        name="merge_router",
    )(x2, attn, y, g_mix, w_gate, w_ao, w_so, w_out, g_ffn, w_r_t, b_r)


def _sorted_rows(tm):
    return _round_up(TOP_K * tm + N_EXPERTS * (ROW_ALIGN - 1), LANES)


def _run_copies(n, max_n, vmem_ref, vmem_off, hbm_ref, hbm_off, sem, to_hbm, wait):
    done = 0
    pieces = [ROW_ALIGN << p for p in range((max_n // ROW_ALIGN).bit_length())]
    for bit in reversed(pieces):
        take = (n & bit) != 0

        @pl.when(take)
        def _piece(bit=bit, done=done):
            v = vmem_ref.at[pl.ds(pl.multiple_of(vmem_off + done, ROW_ALIGN), bit)]
            h = hbm_ref.at[pl.ds(pl.multiple_of(hbm_off + done, ROW_ALIGN), bit)]
            cp = pltpu.make_async_copy(v, h, sem) if to_hbm else pltpu.make_async_copy(h, v, sem)
            cp.wait() if wait else cp.start()

        done = done + jnp.where(take, bit, 0)


def _dispatch_kernel(off_ref, cnt_ref, start_ref, last_ref, h2_ref, pos_ref, xs_hbm, buf_s, zero_s, sem, zsem):
    i = pl.program_id(0)
    tm = h2_ref.shape[0]
    rows = buf_s.shape[0]

    @pl.when(i == 0)
    def _zero_last_blocks():
        zero_s[...] = jnp.zeros(zero_s.shape, BF16)
        for e in range(N_EXPERTS):
            start = pl.multiple_of(last_ref[e], MOE_TILE)
            pltpu.make_async_copy(zero_s, xs_hbm.at[pl.ds(start, MOE_TILE)], zsem).start()
        for e in range(N_EXPERTS):
            pltpu.make_async_copy(zero_s, xs_hbm.at[pl.ds(0, MOE_TILE)], zsem).wait()

        def slack(wait):
            def body(blk, carry):
                cp = pltpu.make_async_copy(zero_s, xs_hbm.at[pl.ds(pl.multiple_of(blk * MOE_TILE, MOE_TILE), MOE_TILE)],
                                           zsem)
                cp.wait() if wait else cp.start()
                return carry
            lax.fori_loop(last_ref[N_EXPERTS], xs_hbm.shape[0] // MOE_TILE, body, 0)

        slack(False)
        slack(True)

    pos = pos_ref[...]
    rid = lax.broadcasted_iota(jnp.int32, (rows, tm), 0)
    sel = jnp.zeros((rows, tm), F32)
    for kk in range(TOP_K):
        sel = sel + jnp.where(rid == pos[kk:kk + 1, :], 1.0, 0.0)
    buf_s[...] = jnp.dot(sel.astype(BF16), h2_ref[...], preferred_element_type=F32).astype(BF16)

    for wait in (False, True):
        for e in range(N_EXPERTS):
            idx = i * N_EXPERTS + e
            _run_copies(cnt_ref[idx], tm, buf_s, start_ref[idx], xs_hbm, off_ref[idx], sem, True, wait)


def _dispatch(run_off, run_cnt, run_start, last_block, h2, pos, n_rows):
    t = h2.shape[0]
    tm = min(ROW_TILE, t)
    grid_spec = pltpu.PrefetchScalarGridSpec(
        num_scalar_prefetch=4,
        grid=(t // tm,),
        in_specs=[pl.BlockSpec((tm, D_MODEL), lambda i, *_: (i, 0)),
                  pl.BlockSpec((TOP_K, tm), lambda i, *_: (0, i))],
        out_specs=pl.BlockSpec(memory_space=pl.ANY),
        scratch_shapes=[pltpu.VMEM((_sorted_rows(tm), D_MODEL), BF16), pltpu.VMEM((MOE_TILE, D_MODEL), BF16),
                        pltpu.SemaphoreType.DMA(()), pltpu.SemaphoreType.DMA(())],
    )
    return pl.pallas_call(
        _dispatch_kernel,
        grid_spec=grid_spec,
        out_shape=jax.ShapeDtypeStruct((n_rows, D_MODEL), BF16),
        compiler_params=_params(("arbitrary",)),
        name="moe_dispatch",
    )(run_off, run_cnt, run_start, last_block, h2, pos)


def _expert_kernel(be_ref, nb_ref, x_ref, w1_ref, b1_ref, w2_ref, b2_ref, y_ref, w1_s, w2_s):
    i = pl.program_id(0)

    @pl.when(jnp.logical_or(i == 0, be_ref[i] != be_ref[jnp.maximum(i - 1, 0)]))
    def _new_expert():
        w1_s[...] = w1_ref[...].astype(BF16)
        w2_s[...] = w2_ref[...].astype(BF16)

    @pl.when(i < nb_ref[0])
    def _run():
        gu = jnp.dot(x_ref[...], w1_s[...], preferred_element_type=F32) + b1_ref[...]
        gate = jnp.minimum(gu[:, :D_EXPERT], SWIGLU_LIMIT)
        up = jnp.clip(gu[:, D_EXPERT:], -SWIGLU_LIMIT, SWIGLU_LIMIT)
        act = (up + 1.0) * (gate * _sigmoid(SWIGLU_ALPHA * gate))
        y = jnp.dot(act.astype(BF16), w2_s[...], preferred_element_type=F32) + b2_ref[...]
        y_ref[...] = y.astype(BF16)

    @pl.when(i >= nb_ref[0])
    def _slack():
        y_ref[...] = jnp.zeros(y_ref.shape, BF16)


def _experts(block_e, n_used, xs, w1, b1, w2, b2):
    n_rows = xs.shape[0]
    nb = n_rows // MOE_TILE
    row_in = lambda i, be, nu: (jnp.minimum(i, nu[0] - 1), 0)
    row = lambda i, be, nu: (i, 0)
    wsel = lambda i, be, nu: (be[i], 0, 0)
    grid_spec = pltpu.PrefetchScalarGridSpec(
        num_scalar_prefetch=2,
        grid=(nb,),
        in_specs=[pl.BlockSpec((MOE_TILE, D_MODEL), row_in),
                  pl.BlockSpec((None, D_MODEL, 2 * D_EXPERT), wsel), pl.BlockSpec((None, 1, 2 * D_EXPERT), wsel),
                  pl.BlockSpec((None, D_EXPERT, D_MODEL), wsel), pl.BlockSpec((None, 1, D_MODEL), wsel)],
        out_specs=pl.BlockSpec((MOE_TILE, D_MODEL), row),
        scratch_shapes=[pltpu.VMEM((D_MODEL, 2 * D_EXPERT), BF16), pltpu.VMEM((D_EXPERT, D_MODEL), BF16)],
    )
    return pl.pallas_call(
        _expert_kernel,
        grid_spec=grid_spec,
        out_shape=jax.ShapeDtypeStruct((n_rows, D_MODEL), BF16),
        compiler_params=_params(("arbitrary",)),
        name="moe_experts",
    )(block_e, n_used, xs, w1, b1, w2, b2)


def _combine_kernel(off_ref, cnt_ref, start_ref, y_hbm, x1_ref, pos_ref, w_ref, o_ref, buf_s, sem):
    i = pl.program_id(0)
    tm = x1_ref.shape[0]
    rows = buf_s.shape[0]

    @pl.when(i == 0)
    def _clear():
        buf_s[...] = jnp.zeros(buf_s.shape, BF16)

    def copies(wait):
        for e in range(N_EXPERTS):
            idx = i * N_EXPERTS + e
            _run_copies(cnt_ref[idx], tm, buf_s, start_ref[idx], y_hbm, off_ref[idx], sem, False, wait)

    copies(False)
    pos = pos_ref[...]
    w = w_ref[...]
    cid = lax.broadcasted_iota(jnp.int32, (tm, rows), 1)
    wsel = jnp.zeros((tm, rows), F32)
    for kk in range(TOP_K):
        wsel = wsel + jnp.where(cid == pos[:, kk:kk + 1], w[:, kk:kk + 1], 0.0)
    copies(True)
    o_ref[...] = x1_ref[...] + jnp.dot(wsel.astype(BF16), buf_s[...], preferred_element_type=F32)


def _combine(run_off, run_cnt, run_start, y, x1, pos_tok, w_tok):
    t = x1.shape[0]
    tm = min(ROW_TILE, t)
    grid_spec = pltpu.PrefetchScalarGridSpec(
        num_scalar_prefetch=3,
        grid=(t // tm,),
        in_specs=[pl.BlockSpec(memory_space=pl.ANY),
                  pl.BlockSpec((tm, D_MODEL), lambda i, *_: (i, 0)),
                  pl.BlockSpec((tm, TOP_K), lambda i, *_: (i, 0)), pl.BlockSpec((tm, TOP_K), lambda i, *_: (i, 0))],
        out_specs=pl.BlockSpec((tm, D_MODEL), lambda i, *_: (i, 0)),
        scratch_shapes=[pltpu.VMEM((_sorted_rows(tm), D_MODEL), BF16), pltpu.SemaphoreType.DMA(())],
    )
    return pl.pallas_call(
        _combine_kernel,
        grid_spec=grid_spec,
        out_shape=jax.ShapeDtypeStruct((t, D_MODEL), F32),
        compiler_params=_params(("arbitrary",)),
        name="moe_combine",
    )(run_off, run_cnt, run_start, y, x1, pos_tok, w_tok)


def kernel(x, g_mix, w_in, q_norm_g, k_norm_g, lambda_q1, lambda_k1, lambda_q2, lambda_k2, attn_sub_g, rel_bias,
           w_attn_o, conv_w, conv_b, dt_bias, a_log, d_skip, ssm_norm_g, w_ssm_o, w_out, g_ffn, w_router, b_router,
           w1, b1, w2, b2):
    b, s, d = x.shape
    t = b * s
    l = 0
    x2 = x.reshape(t, d)

    w = w_in[l]
    c0 = Q_COLS + K_COLS + V_COLS
    c1 = c0 + D_INNER + CONV_DIM
    w_qkv = w[:, :c0].astype(BF16)
    w_dt = jnp.pad(w[:, c1:c1 + SSM_HEADS], ((0, 0), (0, DT_PAD - SSM_HEADS)))
    w_ssm = jnp.concatenate([w[:, c0:c1], w_dt], axis=1).astype(BF16)
    w_gate = w[:, c1 + SSM_HEADS:].astype(BF16)
    n_hd = Q_COLS // ATTN_HEAD_DIM
    gq = (jnp.tile(q_norm_g[l], n_hd) * (ATTN_HEAD_DIM ** -0.5 * LOG2E)).reshape(1, Q_COLS)
    gk = jnp.tile(k_norm_g[l], n_hd).reshape(1, K_COLS)
    gm = g_mix[l].reshape(1, d)

    qn, kn, v = _qkv_proj(x2, gm, w_qkv, gq, gk)
    pad_h = (0, DT_PAD - SSM_HEADS)
    z, xbc, dt, acum, acum_t = _ssm_proj(x2, gm, w_ssm, conv_w[l], conv_b[l].reshape(1, -1),
                                         jnp.pad(dt_bias[l], pad_h).reshape(1, -1),
                                         jnp.pad(a_log[l], pad_h).reshape(1, -1), s)

    lam_vecs = jnp.stack([lambda_q1[l], lambda_k1[l], lambda_q2[l], lambda_k2[l]]).astype(F32)
    attn = _diff_attention(qn.reshape(b, s, -1), kn.reshape(b, s, -1), v.reshape(b, s, -1), rel_bias, q_norm_g[l],
                           k_norm_g[l], lam_vecs, attn_sub_g[l].reshape(1, ATTN_V_DIM))

    y = _ssd(z.reshape(b, s, -1), xbc.reshape(b, s, -1), dt.reshape(b, s, -1), acum.reshape(b, s, -1),
             acum_t.reshape(b, s, -1), jnp.repeat(d_skip[l], SSM_HEAD_DIM).reshape(1, -1),
             ssm_norm_g[l].reshape(1, -1))

    x1, h2, pos, top_w, tile_cnt = _merge(
        x2, attn.reshape(t, -1), y.reshape(t, -1), gm, w_gate, w_attn_o[l].astype(BF16), w_ssm_o[l].astype(BF16),
        w_out[l].astype(BF16), g_ffn[l].reshape(1, d), w_router[l].T, b_router[l].reshape(-1, 1))

    n_tiles = t // min(ROW_TILE, t)
    cnt = tile_cnt.reshape(n_tiles, N_EXPERTS, LANES)[:, :, 0]
    cnt_al = _round_up(cnt, ROW_ALIGN)
    run_start = jnp.cumsum(cnt_al, axis=1) - cnt_al
    padded = _round_up(jnp.sum(cnt_al, axis=0), MOE_TILE)
    end_pad = jnp.cumsum(padded)
    run_off = (end_pad - padded)[None, :] + jnp.cumsum(cnt_al, axis=0) - cnt_al
    n_rows = _round_up(t * TOP_K + n_tiles * N_EXPERTS * (ROW_ALIGN - 1), MOE_TILE) + N_EXPERTS * MOE_TILE
    block_start = jnp.arange(n_rows // MOE_TILE, dtype=jnp.int32) * MOE_TILE
    block_e = jnp.minimum(jnp.sum(block_start[:, None] >= end_pad[None, :], axis=1), N_EXPERTS - 1).astype(jnp.int32)
    n_used = (end_pad[-1:] // MOE_TILE).astype(jnp.int32)
    last_block = jnp.concatenate([jnp.maximum(end_pad - MOE_TILE, 0), end_pad[-1:] // MOE_TILE]).astype(jnp.int32)
    tables = [a.reshape(-1).astype(jnp.int32) for a in (run_off, cnt_al, run_start)]

    xs = _dispatch(*tables, last_block, h2, pos, n_rows)
    ys = _experts(block_e, n_used, xs, w1[l], b1[l][:, None, :], w2[l], b2[l][:, None, :])
    out = _combine(*tables, ys, x1, pos.T, top_w.T)
    return out.reshape(b, s, d)
```

```python
import functools
import math

import jax
import jax.numpy as jnp
from jax import lax
from jax.experimental import pallas as pl
from jax.experimental.pallas import tpu as pltpu

F32 = jnp.float32
BF16 = jnp.bfloat16

D_MODEL = 1024
ATTN_HEADS = 8
ATTN_HEAD_DIM = 64
ATTN_V_DIM = 2 * ATTN_HEAD_DIM
LAMBDA_INIT = 0.8 - 0.6 * math.exp(-0.3 * 0)
NUM_BUCKETS = 32
MAX_DISTANCE = 128
D_INNER = 2 * D_MODEL
SSM_HEAD_DIM = 64
SSM_HEADS = D_INNER // SSM_HEAD_DIM
SSM_GROUPS = 8
SSM_HEADS_PER_GROUP = SSM_HEADS // SSM_GROUPS
D_STATE = 128
CONV_WIDTH = 4
SSM_CHUNK = 128
CONV_DIM = D_INNER + 2 * SSM_GROUPS * D_STATE
N_EXPERTS = 32
TOP_K = 4
D_EXPERT = D_MODEL
SWIGLU_LIMIT = 7.0
SWIGLU_ALPHA = 1.702
RMS_EPS = 1e-6
SSM_EPS = 1e-5
Q_COLS = ATTN_HEADS * 2 * ATTN_HEAD_DIM
K_COLS = Q_COLS
V_COLS = ATTN_HEADS * ATTN_V_DIM

LANES = 128
MXU_DIM = 256
DT_PAD = LANES
NEG_BIG = -1e30
LOG2E = math.log2(math.e)
EXP2_SAFE_BOUND = 80.0
VMEM_LIMIT = 56 * 1024 * 1024

ROW_TILE = 512
ATTN_TILE = 512
MOE_TILE = 512
ROW_ALIGN = 16


def _rms(x, eps):
    return x * lax.rsqrt(jnp.mean(x * x, axis=-1, keepdims=True) + eps)


def _sigmoid(x):
    return 1.0 / (1.0 + jnp.exp(-x))


def _params(sem):
    return pltpu.CompilerParams(dimension_semantics=sem, vmem_limit_bytes=VMEM_LIMIT)


def _resident(shape):
    return pl.BlockSpec(shape, lambda *_: (0,) * len(shape), pipeline_mode=pl.Buffered(1))


def _qkv_kernel(x_ref, g_ref, w_ref, gq_ref, gk_ref, q_ref, k_ref, v_ref):
    h = (_rms(x_ref[...], RMS_EPS) * g_ref[...]).astype(BF16)
    qkv = jnp.dot(h, w_ref[...], preferred_element_type=F32)
    r = lax.broadcasted_iota(jnp.int32, (MXU_DIM, MXU_DIM), 0) // ATTN_HEAD_DIM
    c = lax.broadcasted_iota(jnp.int32, (MXU_DIM, MXU_DIM), 1) // ATTN_HEAD_DIM
    group_ones = jnp.where(r == c, 1.0, 0.0).astype(BF16)

    def head_norm(t, gain_ref, out_ref):
        for cc in range(Q_COLS // MXU_DIM):
            sl = slice(cc * MXU_DIM, (cc + 1) * MXU_DIM)
            tc = t[:, sl]
            ss = jnp.dot((tc * tc).astype(BF16), group_ones, preferred_element_type=F32)
            out_ref[:, sl] = (tc * lax.rsqrt(ss * (1.0 / ATTN_HEAD_DIM) + RMS_EPS) * gain_ref[:, sl]).astype(BF16)

    head_norm(qkv[:, :Q_COLS], gq_ref, q_ref)
    head_norm(qkv[:, Q_COLS:Q_COLS + K_COLS], gk_ref, k_ref)
    v_ref[...] = qkv[:, Q_COLS + K_COLS:].astype(BF16)


def _qkv_proj(x2, g_mix, w_qkv, gq, gk):
    t = x2.shape[0]
    tm = min(ROW_TILE, t)
    row = lambda i: (i, 0)
    fix = lambda i: (0, 0)
    out = jax.ShapeDtypeStruct((t, D_MODEL), BF16)
    return pl.pallas_call(
        _qkv_kernel,
        grid=(t // tm,),
        in_specs=[pl.BlockSpec((tm, D_MODEL), row), pl.BlockSpec((1, D_MODEL), fix),
                  _resident((D_MODEL, 3 * D_MODEL)), pl.BlockSpec((1, D_MODEL), fix),
                  pl.BlockSpec((1, D_MODEL), fix)],
        out_specs=[pl.BlockSpec((tm, D_MODEL), row)] * 3,
        out_shape=[out, out, out],
        compiler_params=_params(("arbitrary",)),
        name="qkv_proj",
    )(x2, g_mix, w_qkv, gq, gk)


CONV_COLS = 1024
SUBLANES = 8
CHUNK_VREGS = SSM_CHUNK // SUBLANES


def _chunk_time(row):
    return (row >> 3) + CHUNK_VREGS * (row & (SUBLANES - 1))


def _ssm_proj_kernel(x_ref, g_ref, w_ref, cw_ref, cb_ref, dtb_ref, alog_ref, z_ref, xbc_ref, dt_ref, acum_ref,
                     acumt_ref, tail_s, h_s, p_s, *, tiles_per_seq):
    tm = x_ref.shape[0]
    nb = tm // SSM_CHUNK
    taps = CONV_WIDTH - 1

    @pl.when(pl.program_id(0) % tiles_per_seq == 0)
    def _sequence_start():
        tail_s[...] = jnp.zeros(tail_s.shape, F32)

    h = (_rms(x_ref[...], RMS_EPS) * g_ref[...]).astype(BF16)
    rr = lax.broadcasted_iota(jnp.int32, (SSM_CHUNK, SSM_CHUNK), 0)
    cc = lax.broadcasted_iota(jnp.int32, (SSM_CHUNK, SSM_CHUNK), 1)
    pick = jnp.where(cc == _chunk_time(rr), 1.0, 0.0).astype(BF16)
    h = jnp.concatenate(
        [jnp.dot(pick, h[b * SSM_CHUNK:(b + 1) * SSM_CHUNK], preferred_element_type=F32).astype(BF16)
         for b in range(nb)], axis=0)

    h_s[...] = h

    def project(stage):
        return jnp.dot(h_s[...], w_ref[:, stage * CONV_COLS:(stage + 1) * CONV_COLS], preferred_element_type=F32)

    z_stages = D_INNER // CONV_COLS
    n_stages = z_stages + CONV_DIM // CONV_COLS
    sub = lax.broadcasted_iota(jnp.int32, (nb * taps, SUBLANES, CONV_COLS), 1)
    p_s[0] = project(0)
    for stage in range(n_stages):
        if stage + 1 < n_stages:
            p_s[(stage + 1) % 2] = project(stage + 1)
        p = p_s[stage % 2]
        if stage < z_stages:
            z_ref[:, stage * CONV_COLS:(stage + 1) * CONV_COLS] = (p * _sigmoid(p)).astype(BF16)
            continue
        c = stage - z_stages
        cols = slice(c * CONV_COLS, (c + 1) * CONV_COLS)
        p4 = p.reshape(nb, CHUNK_VREGS, SUBLANES, CONV_COLS)
        last = p4[:, CHUNK_VREGS - taps:]
        seq = jnp.concatenate([tail_s[c][None], last], axis=0).reshape((nb + 1) * taps, SUBLANES, CONV_COLS)
        tail_s[c] = last[nb - 1]
        rolled = pltpu.roll(seq, 1, axis=1)
        wrapped = jnp.where(sub == 0, rolled[:nb * taps], rolled[taps:]).reshape(nb, taps, SUBLANES, CONV_COLS)
        conv = cb_ref[:, cols] + cw_ref[taps:taps + 1, cols] * p4
        for back in range(1, CONV_WIDTH):
            shifted = jnp.concatenate([wrapped[:, taps - back:], p4[:, :CHUNK_VREGS - back]], axis=1)
            conv = conv + cw_ref[taps - back:taps - back + 1, cols] * shifted
        xbc_ref[:, cols] = (conv * _sigmoid(conv)).reshape(tm, CONV_COLS).astype(BF16)
    dtl = jnp.dot(h_s[...], w_ref[:, D_INNER + CONV_DIM:], preferred_element_type=F32) + dtb_ref[...]
    dt = jnp.maximum(dtl, 0.0) + jnp.log(1.0 + jnp.exp(-jnp.abs(dtl)))
    dt_ref[...] = dt
    a = dt * (-jnp.exp(alog_ref[...]))
    upto = jnp.where(_chunk_time(rr) >= _chunk_time(cc), 1.0, 0.0).astype(F32)
    for b in range(nb):
        rows = slice(b * SSM_CHUNK, (b + 1) * SSM_CHUNK)
        acum = jnp.dot(upto, a[rows], preferred_element_type=F32, precision=lax.Precision.HIGHEST)
        acum_ref[rows, :] = acum
        acumt_ref[rows, :] = acum.T


def _ssm_proj(x2, g_mix, w_ssm, conv_w, conv_b, dt_bias, a_log, seq_len):
    t = x2.shape[0]
    tm = min(ROW_TILE, seq_len)
    assert seq_len % tm == 0
    ncol = D_INNER + CONV_DIM + DT_PAD
    row = lambda i: (i, 0)
    fix = lambda i: (0, 0)
    heads = jax.ShapeDtypeStruct((t, DT_PAD), F32)
    return pl.pallas_call(
        functools.partial(_ssm_proj_kernel, tiles_per_seq=seq_len // tm),
        grid=(t // tm,),
        in_specs=[pl.BlockSpec((tm, D_MODEL), row), pl.BlockSpec((1, D_MODEL), fix),
                  _resident((D_MODEL, ncol)), pl.BlockSpec((CONV_WIDTH, CONV_DIM), fix),
                  pl.BlockSpec((1, CONV_DIM), fix), pl.BlockSpec((1, DT_PAD), fix), pl.BlockSpec((1, DT_PAD), fix)],
        out_specs=[pl.BlockSpec((tm, D_INNER), row), pl.BlockSpec((tm, CONV_DIM), row)]
        + [pl.BlockSpec((tm, DT_PAD), row)] * 3,
        out_shape=[jax.ShapeDtypeStruct((t, D_INNER), BF16), jax.ShapeDtypeStruct((t, CONV_DIM), BF16),
                   heads, heads, heads],
        scratch_shapes=[pltpu.VMEM((CONV_DIM // CONV_COLS, CONV_WIDTH - 1, SUBLANES, CONV_COLS), F32),
                        pltpu.VMEM((tm, D_MODEL), BF16), pltpu.VMEM((2, tm, CONV_COLS), F32)],
        compiler_params=_params(("arbitrary",)),
        name="ssm_proj",
    )(x2, g_mix, w_ssm, conv_w, conv_b, dt_bias, a_log)


def _split_maps(q):
    lane = lax.broadcasted_iota(jnp.int32, q.shape, 1)
    zero = jnp.zeros_like(q)
    return jnp.where(lane < ATTN_HEAD_DIM, q, zero), jnp.where(lane >= ATTN_HEAD_DIM, q, zero)


def _attn_finalize(acc1, l1, acc2, l2, lam_ref, subg_ref, o_ref):
    lam_v = lam_ref[...]
    lam = (jnp.exp(jnp.sum(lam_v[0:1] * lam_v[1:2], axis=-1, keepdims=True))
           - jnp.exp(jnp.sum(lam_v[2:3] * lam_v[3:4], axis=-1, keepdims=True)) + LAMBDA_INIT)
    o = acc1 / l1 - lam * (acc2 / l2)
    o_ref[...] = (_rms(o, RMS_EPS) * subg_ref[...] * (1.0 - LAMBDA_INIT)).astype(BF16)


def _attn_bounded_kernel(q_ref, k_ref, v_ref, bias_ref, lam_ref, subg_ref, o_ref, qq_s, vv_s, acc_s, s_s, *, nq):
    pair_id = pl.program_id(2)
    tq = qq_s.shape[1] // 2
    tk = tq
    blocks = (pair_id, nq - 1 - pair_id)

    @pl.when(pair_id == 0)
    def _extend_v():
        vv_s[:, :ATTN_V_DIM] = v_ref[...]
        vv_s[:, ATTN_V_DIM:] = jnp.ones((vv_s.shape[0], ATTN_V_DIM), BF16)

    for side in range(2):
        q1, q2 = _split_maps(q_ref[pl.ds(pl.multiple_of(blocks[side] * tq, tq), tq), :])
        qq_s[side, 0:tq, :] = q1
        qq_s[side, tq:2 * tq, :] = q2
    acc_s[...] = jnp.zeros(acc_s.shape, F32)
    contract_last = (((1,), (1,)), ((), ()))

    def tile(t):
        first = t <= pair_id
        side = jnp.where(first, 0, 1)
        j = jnp.where(first, t, t - pair_id - 1)
        diag = jnp.where(first, blocks[0], blocks[1])
        kind = jnp.where(j == diag, 0, jnp.where(j == diag - 1, 1, 2))
        return side, pl.ds(pl.multiple_of(j * tk, tk), tk), kind

    def logits(t):
        side, rows, kind = tile(t)
        s = lax.dot_general(qq_s[side], k_ref[rows, :], contract_last, preferred_element_type=F32)
        b = bias_ref[kind]
        return jnp.concatenate([s[0:tq] + b, s[tq:2 * tq] + b], axis=0)

    def accumulate(t, slot):
        side, rows, _ = tile(t)
        acc_s[side] += jnp.dot(jnp.exp2(s_s[slot]).astype(BF16), vv_s[rows, :], preferred_element_type=F32)

    s_s[0] = logits(0)
    for t in range(nq + 1):
        if t < nq:
            s_s[(t + 1) % 2] = logits(t + 1)
        accumulate(t, t % 2)

    for side in range(2):
        acc = acc_s[side]
        _attn_finalize(acc[0:tq, :ATTN_V_DIM], acc[0:tq, ATTN_V_DIM:], acc[tq:2 * tq, :ATTN_V_DIM],
                       acc[tq:2 * tq, ATTN_V_DIM:], lam_ref, subg_ref,
                       o_ref.at[pl.ds(pl.multiple_of(blocks[side] * tq, tq), tq), :])


def _attn_online_kernel(it_ref, jt_ref, q_ref, k_ref, v_ref, bias_ref, lam_ref, subg_ref, o_ref,
                        q1_s, q2_s, m1_s, m2_s, l1_s, l2_s, acc1_s, acc2_s):
    step = pl.program_id(2)
    i = it_ref[step]
    j = jt_ref[step]

    @pl.when(j == 0)
    def _init():
        q1_s[...], q2_s[...] = _split_maps(q_ref[...])
        for m_s, l_s, acc_s in ((m1_s, l1_s, acc1_s), (m2_s, l2_s, acc2_s)):
            m_s[...] = jnp.full(m_s.shape, NEG_BIG, F32)
            l_s[...] = jnp.zeros(l_s.shape, F32)
            acc_s[...] = jnp.zeros(acc_s.shape, F32)

    def update(bias):
        k = k_ref[...]
        v = v_ref[...]
        contract_last = (((1,), (1,)), ((), ()))
        for q_s, m_s, l_s, acc_s in ((q1_s, m1_s, l1_s, acc1_s), (q2_s, m2_s, l2_s, acc2_s)):
            s = lax.dot_general(q_s[...], k, contract_last, preferred_element_type=F32)
            if bias is not None:
                s = s + bias_ref[bias]
            m_old = m_s[...]
            m_new = jnp.maximum(m_old, jnp.max(s, axis=-1, keepdims=True))
            alpha = jnp.exp2(m_old - m_new)
            p = jnp.exp2(s - m_new)
            l_s[...] = alpha * l_s[...] + jnp.sum(p, axis=-1, keepdims=True)
            acc_s[...] = alpha * acc_s[...] + jnp.dot(p.astype(BF16), v, preferred_element_type=F32)
            m_s[...] = m_new

    @pl.when(j == i)
    def _diag():
        update(0)

    @pl.when(j == i - 1)
    def _prev():
        update(1)

    @pl.when(j < i - 1)
    def _far():
        update(None)

    @pl.when(j == i)
    def _finalize():
        _attn_finalize(acc1_s[...], l1_s[...], acc2_s[...], l2_s[...], lam_ref, subg_ref, o_ref)


def _t5_bucket(dist):
    n = jnp.maximum(dist, 0)
    max_exact = NUM_BUCKETS // 2
    scaled = jnp.log(jnp.maximum(n, 1).astype(F32) / max_exact) / math.log(MAX_DISTANCE / max_exact)
    large = max_exact + (scaled * (NUM_BUCKETS - max_exact)).astype(jnp.int32)
    large = jnp.minimum(large, NUM_BUCKETS - 1)
    return jnp.where(n < max_exact, n, large)


def _bias_tiles(rel_bias, tile):
    blk = MAX_DISTANCE
    assert tile % blk == 0
    nb = tile // blk
    table = (rel_bias - rel_bias[NUM_BUCKETS - 1]).astype(F32) * LOG2E
    r = jnp.arange(blk, dtype=jnp.int32)
    d0 = r[:, None] - r[None, :]

    def lookup(dist):
        onehot = (_t5_bucket(dist)[..., None] == jnp.arange(NUM_BUCKETS, dtype=jnp.int32)).astype(F32)
        return jnp.einsum('qkn,nh->hqk', onehot, table, precision=lax.Precision.HIGHEST)

    on_diag = jnp.where(d0[None] >= 0, lookup(d0), NEG_BIG)
    sub_diag = lookup(d0 + blk)
    zeros = jnp.zeros_like(sub_diag)
    masked = jnp.full_like(sub_diag, NEG_BIG)

    def assemble(pick):
        return jnp.concatenate(
            [jnp.concatenate([pick(bi, bj) for bj in range(nb)], axis=-1) for bi in range(nb)], axis=-2)

    diag_tile = assemble(lambda bi, bj: on_diag if bi == bj else sub_diag if bi == bj + 1 else zeros if bi > bj else masked)
    prev_tile = assemble(lambda bi, bj: sub_diag if (bi == 0 and bj == nb - 1) else zeros)
    return jnp.stack([diag_tile, prev_tile, jnp.zeros_like(prev_tile)], axis=1)


def _attn_bounded(qn, kn, v, bias, lam_vecs, sub_g):
    b, s, _ = qn.shape
    tile = min(ATTN_TILE, s)
    nq = s // tile
    assert nq % 2 == 0
    seq_map = lambda bb, h, i: (bb, 0, h)
    return pl.pallas_call(
        functools.partial(_attn_bounded_kernel, nq=nq),
        grid=(b, ATTN_HEADS, nq // 2),
        in_specs=[pl.BlockSpec((None, s, ATTN_V_DIM), seq_map),
                  pl.BlockSpec((None, s, ATTN_V_DIM), seq_map),
                  pl.BlockSpec((None, s, ATTN_V_DIM), seq_map),
                  pl.BlockSpec((None, 3, tile, tile), lambda bb, h, i: (h, 0, 0, 0)),
                  pl.BlockSpec((4, ATTN_HEAD_DIM), lambda bb, h, i: (0, 0)),
                  pl.BlockSpec((1, ATTN_V_DIM), lambda bb, h, i: (0, 0))],
        out_specs=pl.BlockSpec((None, s, ATTN_V_DIM), seq_map),
        out_shape=jax.ShapeDtypeStruct((b, s, V_COLS), BF16),
        scratch_shapes=[pltpu.VMEM((2, 2 * tile, ATTN_V_DIM), BF16), pltpu.VMEM((s, 2 * ATTN_V_DIM), BF16),
                        pltpu.VMEM((2, 2 * tile, 2 * ATTN_V_DIM), F32), pltpu.VMEM((2, 2 * tile, tile), F32)],
        compiler_params=_params(("arbitrary", "arbitrary", "arbitrary")),
        name="diff_attn_bounded",
    )(qn, kn, v, bias, lam_vecs, sub_g)


def _attn_online(qn, kn, v, bias, lam_vecs, sub_g):
    b, s, _ = qn.shape
    tile = min(ATTN_TILE, s)
    nq = s // tile
    it = jnp.asarray([i for i in range(nq) for _ in range(i + 1)], jnp.int32)
    jt = jnp.asarray([j for i in range(nq) for j in range(i + 1)], jnp.int32)
    q_map = lambda bb, h, st, it_r, jt_r: (bb, it_r[st], h)
    kv_map = lambda bb, h, st, it_r, jt_r: (bb, jt_r[st], h)
    grid_spec = pltpu.PrefetchScalarGridSpec(
        num_scalar_prefetch=2,
        grid=(b, ATTN_HEADS, int(it.shape[0])),
        in_specs=[pl.BlockSpec((None, tile, ATTN_V_DIM), q_map),
                  pl.BlockSpec((None, tile, ATTN_V_DIM), kv_map),
                  pl.BlockSpec((None, tile, ATTN_V_DIM), kv_map),
                  pl.BlockSpec((None, 3, tile, tile), lambda bb, h, st, it_r, jt_r: (h, 0, 0, 0)),
                  pl.BlockSpec((4, ATTN_HEAD_DIM), lambda bb, h, st, it_r, jt_r: (0, 0)),
                  pl.BlockSpec((1, ATTN_V_DIM), lambda bb, h, st, it_r, jt_r: (0, 0))],
        out_specs=pl.BlockSpec((None, tile, ATTN_V_DIM), q_map),
        scratch_shapes=[pltpu.VMEM((tile, ATTN_V_DIM), BF16), pltpu.VMEM((tile, ATTN_V_DIM), BF16),
                        pltpu.VMEM((tile, 1), F32), pltpu.VMEM((tile, 1), F32),
                        pltpu.VMEM((tile, 1), F32), pltpu.VMEM((tile, 1), F32),
                        pltpu.VMEM((tile, ATTN_V_DIM), F32), pltpu.VMEM((tile, ATTN_V_DIM), F32)],
    )
    return pl.pallas_call(
        _attn_online_kernel,
        grid_spec=grid_spec,
        out_shape=jax.ShapeDtypeStruct((b, s, V_COLS), BF16),
        compiler_params=_params(("arbitrary", "arbitrary", "arbitrary")),
        name="diff_attn_online",
    )(it, jt, qn, kn, v, bias, lam_vecs, sub_g)


def _diff_attention(qn, kn, v, rel_bias, q_gain, k_gain, lam_vecs, sub_g):
    tile = min(ATTN_TILE, qn.shape[1])
    bias = _bias_tiles(rel_bias, tile)
    spread = jnp.max(jnp.abs(rel_bias - rel_bias[NUM_BUCKETS - 1]))
    bound = LOG2E * (1.05 * math.sqrt(ATTN_HEAD_DIM) * jnp.max(jnp.abs(q_gain)) * jnp.max(jnp.abs(k_gain)) + spread)
    args = (qn, kn, v, bias, lam_vecs, sub_g)
    return lax.cond(bound < EXP2_SAFE_BOUND, lambda a: _attn_bounded(*a), lambda a: _attn_online(*a), args)


def _ssd_kernel(z_ref, xbc_ref, dt_ref, acum_ref, acumt_ref, dskip_ref, ng_ref, y_ref, state_s):
    L = SSM_CHUNK

    @pl.when(pl.program_id(1) == 0)
    def _reset():
        state_s[...] = jnp.zeros(state_s.shape, F32)

    xs = xbc_ref[:, :D_INNER].astype(F32)
    bm = xbc_ref[:, D_INNER:D_INNER + SSM_GROUPS * D_STATE]
    cm = xbc_ref[:, D_INNER + SSM_GROUPS * D_STATE:]

    dt = dt_ref[...]
    row = lax.broadcasted_iota(jnp.int32, (L, L), 0)
    col = lax.broadcasted_iota(jnp.int32, (L, L), 1)
    causal = _chunk_time(row) >= _chunk_time(col)
    acum = acum_ref[...]
    acum_t = acumt_ref[...]
    a_end = acum[L - 1:L, :]
    lane = lax.broadcasted_iota(jnp.int32, (L, LANES), 1)
    low = lane < SSM_HEAD_DIM

    def pair(lo, hi):
        return jnp.where(low[:lo.shape[0]], lo, hi)

    def spread(arr, h):
        return jnp.broadcast_to(arr[:, h:h + 1], (arr.shape[0], LANES))

    contract_last = (((1,), (1,)), ((), ()))
    contract_first = (((0,), (0,)), ((), ()))
    y_parts = []
    for g in range(SSM_GROUPS):
        bg = bm[:, g * D_STATE:(g + 1) * D_STATE]
        cg = cm[:, g * D_STATE:(g + 1) * D_STATE]
        cb = lax.dot_general(cg, bg, contract_last, preferred_element_type=F32)
        st = state_s[g]
        y_off = jnp.dot(cg, st.astype(BF16), preferred_element_type=F32)
        xw_parts = []
        dec_parts = []
        for pr in range(SSM_HEADS_PER_GROUP // 2):
            h0 = g * SSM_HEADS_PER_GROUP + 2 * pr
            ch = slice(h0 * SSM_HEAD_DIM, (h0 + 2) * SSM_HEAD_DIM)
            x_pair = xs[:, ch]
            acols = (spread(acum, h0), spread(acum, h0 + 1))
            acol = pair(*acols)
            aend = pair(spread(a_end, h0), spread(a_end, h0 + 1))
            xdt32 = x_pair * pair(spread(dt, h0), spread(dt, h0 + 1))
            xdt = xdt32.astype(BF16)
            yd = []
            for hh, ac in zip((h0, h0 + 1), acols):
                seg = ac - acum_t[hh:hh + 1, :]
                decay = jnp.exp(jnp.where(causal, seg, NEG_BIG))
                yd.append(jnp.dot((cb * decay).astype(BF16), xdt, preferred_element_type=F32))
            y_diag = pair(yd[0], yd[1])
            off = y_off[:, 2 * pr * SSM_HEAD_DIM:(2 * pr + 2) * SSM_HEAD_DIM]
            y_parts.append(y_diag + off * jnp.exp(acol))
            xw_parts.append(xdt32 * jnp.exp(aend - acol))
            dec_parts.append(jnp.exp(aend))
        xw = jnp.concatenate(xw_parts, axis=-1).astype(BF16)
        dec = jnp.concatenate(dec_parts, axis=-1)
        state_s[g] = st * dec + lax.dot_general(bg, xw, contract_first, preferred_element_type=F32)
    y = (jnp.concatenate(y_parts, axis=-1) + dskip_ref[...] * xs) * z_ref[...].astype(F32)
    gsz = D_INNER // SSM_GROUPS
    unpick = jnp.where(row == _chunk_time(col), 1.0, 0.0).astype(BF16)
    for g in range(SSM_GROUPS):
        sl = slice(g * gsz, (g + 1) * gsz)
        yn = (_rms(y[:, sl], SSM_EPS) * ng_ref[:, sl]).astype(BF16)
        y_ref[:, sl] = jnp.dot(unpick, yn, preferred_element_type=F32).astype(BF16)


def _ssd(z, xbc, dt, acum, acum_t, d_skip_ch, norm_g):
    b, s, _ = z.shape
    nc = s // SSM_CHUNK
    blk = lambda bb, c: (bb, c, 0)
    fix = lambda bb, c: (0, 0)
    return pl.pallas_call(
        _ssd_kernel,
        grid=(b, nc),
        in_specs=[pl.BlockSpec((None, SSM_CHUNK, D_INNER), blk), pl.BlockSpec((None, SSM_CHUNK, CONV_DIM), blk)]
        + [pl.BlockSpec((None, SSM_CHUNK, DT_PAD), blk)] * 3
        + [pl.BlockSpec((1, D_INNER), fix), pl.BlockSpec((1, D_INNER), fix)],
        out_specs=pl.BlockSpec((None, SSM_CHUNK, D_INNER), blk),
        out_shape=jax.ShapeDtypeStruct((b, s, D_INNER), BF16),
        scratch_shapes=[pltpu.VMEM((SSM_GROUPS, D_STATE, SSM_HEADS_PER_GROUP * SSM_HEAD_DIM), F32)],
        compiler_params=_params(("arbitrary", "arbitrary")),
        name="ssd",
    )(z, xbc, dt, acum, acum_t, d_skip_ch, norm_g)


def _round_up(n, m):
    return (n + m - 1) // m * m


def _merge_kernel(x_ref, attn_ref, y_ref, gmix_ref, wg_ref, wao_ref, wso_ref, wout_ref, gffn_ref, wr_ref, br_ref,
                  x1_ref, h2_ref, pos_ref, topw_ref, cnt_ref):
    x = x_ref[...]
    tm = x.shape[0]
    h = (_rms(x, RMS_EPS) * gmix_ref[...]).astype(BF16)
    gates = _sigmoid(jnp.dot(h, wg_ref[...], preferred_element_type=F32))
    attn_out = jnp.dot(attn_ref[...], wao_ref[...], preferred_element_type=F32)
    ssm_out = jnp.dot(y_ref[...], wso_ref[...], preferred_element_type=F32)
    merged = gates[:, :D_MODEL] * attn_out + gates[:, D_MODEL:] * ssm_out
    x1 = x + jnp.dot(merged.astype(BF16), wout_ref[...], preferred_element_type=F32)
    x1_ref[...] = x1
    h2 = _rms(x1, RMS_EPS) * gffn_ref[...]
    h2_ref[...] = h2.astype(BF16)

    logits = lax.dot_general(wr_ref[...], h2, (((1,), (1,)), ((), ())), preferred_element_type=F32,
                             precision=lax.Precision.HIGHEST) + br_ref[...]
    eid = lax.broadcasted_iota(jnp.int32, logits.shape, 0)
    vals, hits = [], []
    member = jnp.zeros(logits.shape, F32)
    work = logits
    for kk in range(TOP_K):
        m = jnp.max(work, axis=0, keepdims=True)
        idx = jnp.min(jnp.where(work == m, eid, N_EXPERTS), axis=0, keepdims=True)
        hit = eid == idx
        vals.append(m)
        hits.append(hit)
        member = jnp.where(hit, 1.0, member)
        work = jnp.where(hit, -jnp.inf, work)
    ex = [jnp.exp(v - vals[0]) for v in vals]
    denom = ex[0] + ex[1] + ex[2] + ex[3]
    for kk in range(TOP_K):
        topw_ref[kk:kk + 1, :] = ex[kk] / denom

    r = lax.broadcasted_iota(jnp.int32, (tm, tm), 0)
    c = lax.broadcasted_iota(jnp.int32, (tm, tm), 1)
    before = jnp.where(r < c, 1.0, 0.0).astype(BF16)
    prefix = jnp.dot(member.astype(BF16), before, preferred_element_type=F32)
    cnt = jnp.sum(member, axis=1, keepdims=True).astype(jnp.int32)
    cnt_al = jnp.bitwise_and(cnt + (ROW_ALIGN - 1), -ROW_ALIGN)
    cnt_al = jnp.broadcast_to(cnt_al, (N_EXPERTS, LANES)).astype(F32)
    er = lax.broadcasted_iota(jnp.int32, (N_EXPERTS, N_EXPERTS), 0)
    ec = lax.broadcasted_iota(jnp.int32, (N_EXPERTS, N_EXPERTS), 1)
    lower = jnp.where(ec < er, 1.0, 0.0).astype(F32)
    run_start = jnp.dot(lower, cnt_al, preferred_element_type=F32, precision=lax.Precision.HIGHEST)[:, 0:1]
    base = prefix + run_start
    for kk in range(TOP_K):
        pos_ref[kk:kk + 1, :] = jnp.sum(jnp.where(hits[kk], base, 0.0), axis=0, keepdims=True).astype(jnp.int32)
    cnt_ref[...] = jnp.broadcast_to(cnt, cnt_ref.shape)


def _merge(x2, attn, y, g_mix, w_gate, w_ao, w_so, w_out, g_ffn, w_r_t, b_r):
    t = x2.shape[0]
    tm = min(ROW_TILE, t)
    row = lambda i: (i, 0)
    colb = lambda i: (0, i)
    fix = lambda i: (0, 0)
    return pl.pallas_call(
        _merge_kernel,
        grid=(t // tm,),
        in_specs=[pl.BlockSpec((tm, D_MODEL), row), pl.BlockSpec((tm, V_COLS), row), pl.BlockSpec((tm, D_INNER), row),
                  pl.BlockSpec((1, D_MODEL), fix), _resident((D_MODEL, 2 * D_MODEL)),
                  _resident((V_COLS, D_MODEL)), _resident((D_INNER, D_MODEL)),
                  _resident((D_MODEL, D_MODEL)), pl.BlockSpec((1, D_MODEL), fix),
                  pl.BlockSpec((N_EXPERTS, D_MODEL), fix), pl.BlockSpec((N_EXPERTS, 1), fix)],
        out_specs=[pl.BlockSpec((tm, D_MODEL), row), pl.BlockSpec((tm, D_MODEL), row),
                   pl.BlockSpec((TOP_K, tm), colb), pl.BlockSpec((TOP_K, tm), colb),
                   pl.BlockSpec((N_EXPERTS, LANES), row)],
        out_shape=[jax.ShapeDtypeStruct((t, D_MODEL), F32), jax.ShapeDtypeStruct((t, D_MODEL), BF16),
                   jax.ShapeDtypeStruct((TOP_K, t), jnp.int32), jax.ShapeDtypeStruct((TOP_K, t), F32),
                   jax.ShapeDtypeStruct((t // tm * N_EXPERTS, LANES), jnp.int32)],
        compiler_params=_params(("arbitrary",)),
        name="merge_router",
    )(x2, attn, y, g_mix, w_gate, w_ao, w_so, w_out, g_ffn, w_r_t, b_r)


def _sorted_rows(tm):
    return _round_up(TOP_K * tm + N_EXPERTS * (ROW_ALIGN - 1), LANES)


def _run_copies(n, max_n, vmem_ref, vmem_off, hbm_ref, hbm_off, sem, to_hbm, wait):
    done = 0
    pieces = [ROW_ALIGN << p for p in range((max_n // ROW_ALIGN).bit_length())]
    for bit in reversed(pieces):
        take = (n & bit) != 0

        @pl.when(take)
        def _piece(bit=bit, done=done):
            v = vmem_ref.at[pl.ds(pl.multiple_of(vmem_off + done, ROW_ALIGN), bit)]
            h = hbm_ref.at[pl.ds(pl.multiple_of(hbm_off + done, ROW_ALIGN), bit)]
            cp = pltpu.make_async_copy(v, h, sem) if to_hbm else pltpu.make_async_copy(h, v, sem)
            cp.wait() if wait else cp.start()

        done = done + jnp.where(take, bit, 0)


def _dispatch_kernel(off_ref, cnt_ref, start_ref, last_ref, h2_ref, pos_ref, xs_hbm, buf_s, zero_s, sem, zsem):
    i = pl.program_id(0)
    tm = h2_ref.shape[0]
    rows = buf_s.shape[0]

    @pl.when(i == 0)
    def _zero_last_blocks():
        zero_s[...] = jnp.zeros(zero_s.shape, BF16)
        for e in range(N_EXPERTS):
            start = pl.multiple_of(last_ref[e], MOE_TILE)
            pltpu.make_async_copy(zero_s, xs_hbm.at[pl.ds(start, MOE_TILE)], zsem).start()
        for e in range(N_EXPERTS):
            pltpu.make_async_copy(zero_s, xs_hbm.at[pl.ds(0, MOE_TILE)], zsem).wait()

        def slack(wait):
            def body(blk, carry):
                cp = pltpu.make_async_copy(zero_s, xs_hbm.at[pl.ds(pl.multiple_of(blk * MOE_TILE, MOE_TILE), MOE_TILE)],
                                           zsem)
                cp.wait() if wait else cp.start()
                return carry
            lax.fori_loop(last_ref[N_EXPERTS], xs_hbm.shape[0] // MOE_TILE, body, 0)

        slack(False)
        slack(True)

    pos = pos_ref[...]
    rid = lax.broadcasted_iota(jnp.int32, (rows, tm), 0)
    sel = jnp.zeros((rows, tm), F32)
    for kk in range(TOP_K):
        sel = sel + jnp.where(rid == pos[kk:kk + 1, :], 1.0, 0.0)
    buf_s[...] = jnp.dot(sel.astype(BF16), h2_ref[...], preferred_element_type=F32).astype(BF16)

    for wait in (False, True):
        for e in range(N_EXPERTS):
            idx = i * N_EXPERTS + e
            _run_copies(cnt_ref[idx], tm, buf_s, start_ref[idx], xs_hbm, off_ref[idx], sem, True, wait)


def _dispatch(run_off, run_cnt, run_start, last_block, h2, pos, n_rows):
    t = h2.shape[0]
    tm = min(ROW_TILE, t)
    grid_spec = pltpu.PrefetchScalarGridSpec(
        num_scalar_prefetch=4,
        grid=(t // tm,),
        in_specs=[pl.BlockSpec((tm, D_MODEL), lambda i, *_: (i, 0)),
                  pl.BlockSpec((TOP_K, tm), lambda i, *_: (0, i))],
        out_specs=pl.BlockSpec(memory_space=pl.ANY),
        scratch_shapes=[pltpu.VMEM((_sorted_rows(tm), D_MODEL), BF16), pltpu.VMEM((MOE_TILE, D_MODEL), BF16),
                        pltpu.SemaphoreType.DMA(()), pltpu.SemaphoreType.DMA(())],
    )
    return pl.pallas_call(
        _dispatch_kernel,
        grid_spec=grid_spec,
        out_shape=jax.ShapeDtypeStruct((n_rows, D_MODEL), BF16),
        compiler_params=_params(("arbitrary",)),
        name="moe_dispatch",
    )(run_off, run_cnt, run_start, last_block, h2, pos)


def _expert_kernel(be_ref, nb_ref, x_ref, w1_ref, b1_ref, w2_ref, b2_ref, y_ref, w1_s, w2_s):
    i = pl.program_id(0)

    @pl.when(jnp.logical_or(i == 0, be_ref[i] != be_ref[jnp.maximum(i - 1, 0)]))
    def _new_expert():
        w1_s[...] = w1_ref[...].astype(BF16)
        w2_s[...] = w2_ref[...].astype(BF16)

    @pl.when(i < nb_ref[0])
    def _run():
        gu = jnp.dot(x_ref[...], w1_s[...], preferred_element_type=F32) + b1_ref[...]
        gate = jnp.minimum(gu[:, :D_EXPERT], SWIGLU_LIMIT)
        up = jnp.clip(gu[:, D_EXPERT:], -SWIGLU_LIMIT, SWIGLU_LIMIT)
        act = (up + 1.0) * (gate * _sigmoid(SWIGLU_ALPHA * gate))
        y = jnp.dot(act.astype(BF16), w2_s[...], preferred_element_type=F32) + b2_ref[...]
        y_ref[...] = y.astype(BF16)

    @pl.when(i >= nb_ref[0])
    def _slack():
        y_ref[...] = jnp.zeros(y_ref.shape, BF16)


def _experts(block_e, n_used, xs, w1, b1, w2, b2):
    n_rows = xs.shape[0]
    nb = n_rows // MOE_TILE
    row_in = lambda i, be, nu: (jnp.minimum(i, nu[0] - 1), 0)
    row = lambda i, be, nu: (i, 0)
    wsel = lambda i, be, nu: (be[i], 0, 0)
    grid_spec = pltpu.PrefetchScalarGridSpec(
        num_scalar_prefetch=2,
        grid=(nb,),
        in_specs=[pl.BlockSpec((MOE_TILE, D_MODEL), row_in),
                  pl.BlockSpec((None, D_MODEL, 2 * D_EXPERT), wsel), pl.BlockSpec((None, 1, 2 * D_EXPERT), wsel),
                  pl.BlockSpec((None, D_EXPERT, D_MODEL), wsel), pl.BlockSpec((None, 1, D_MODEL), wsel)],
        out_specs=pl.BlockSpec((MOE_TILE, D_MODEL), row),
        scratch_shapes=[pltpu.VMEM((D_MODEL, 2 * D_EXPERT), BF16), pltpu.VMEM((D_EXPERT, D_MODEL), BF16)],
    )
    return pl.pallas_call(
        _expert_kernel,
        grid_spec=grid_spec,
        out_shape=jax.ShapeDtypeStruct((n_rows, D_MODEL), BF16),
        compiler_params=_params(("arbitrary",)),
        name="moe_experts",
    )(block_e, n_used, xs, w1, b1, w2, b2)


def _combine_kernel(off_ref, cnt_ref, start_ref, y_hbm, x1_ref, pos_ref, w_ref, o_ref, buf_s, sem):
    i = pl.program_id(0)
    tm = x1_ref.shape[0]
    rows = buf_s.shape[0]

    @pl.when(i == 0)
    def _clear():
        buf_s[...] = jnp.zeros(buf_s.shape, BF16)

    def copies(wait):
        for e in range(N_EXPERTS):
            idx = i * N_EXPERTS + e
            _run_copies(cnt_ref[idx], tm, buf_s, start_ref[idx], y_hbm, off_ref[idx], sem, False, wait)

    copies(False)
    pos = pos_ref[...]
    w = w_ref[...]
    cid = lax.broadcasted_iota(jnp.int32, (tm, rows), 1)
    wsel = jnp.zeros((tm, rows), F32)
    for kk in range(TOP_K):
        wsel = wsel + jnp.where(cid == pos[:, kk:kk + 1], w[:, kk:kk + 1], 0.0)
    copies(True)
    o_ref[...] = x1_ref[...] + jnp.dot(wsel.astype(BF16), buf_s[...], preferred_element_type=F32)


def _combine(run_off, run_cnt, run_start, y, x1, pos_tok, w_tok):
    t = x1.shape[0]
    tm = min(ROW_TILE, t)
    grid_spec = pltpu.PrefetchScalarGridSpec(
        num_scalar_prefetch=3,
        grid=(t // tm,),
        in_specs=[pl.BlockSpec(memory_space=pl.ANY),
                  pl.BlockSpec((tm, D_MODEL), lambda i, *_: (i, 0)),
                  pl.BlockSpec((tm, TOP_K), lambda i, *_: (i, 0)), pl.BlockSpec((tm, TOP_K), lambda i, *_: (i, 0))],
        out_specs=pl.BlockSpec((tm, D_MODEL), lambda i, *_: (i, 0)),
        scratch_shapes=[pltpu.VMEM((_sorted_rows(tm), D_MODEL), BF16), pltpu.SemaphoreType.DMA(())],
    )
    return pl.pallas_call(
        _combine_kernel,
        grid_spec=grid_spec,
        out_shape=jax.ShapeDtypeStruct((t, D_MODEL), F32),
        compiler_params=_params(("arbitrary",)),
        name="moe_combine",
    )(run_off, run_cnt, run_start, y, x1, pos_tok, w_tok)


def kernel(x, g_mix, w_in, q_norm_g, k_norm_g, lambda_q1, lambda_k1, lambda_q2, lambda_k2, attn_sub_g, rel_bias,
           w_attn_o, conv_w, conv_b, dt_bias, a_log, d_skip, ssm_norm_g, w_ssm_o, w_out, g_ffn, w_router, b_router,
           w1, b1, w2, b2):
    b, s, d = x.shape
    t = b * s
    l = 0
    x2 = x.reshape(t, d)

    w = w_in[l]
    c0 = Q_COLS + K_COLS + V_COLS
    c1 = c0 + D_INNER + CONV_DIM
    w_qkv = w[:, :c0].astype(BF16)
    w_dt = jnp.pad(w[:, c1:c1 + SSM_HEADS], ((0, 0), (0, DT_PAD - SSM_HEADS)))
    w_ssm = jnp.concatenate([w[:, c0:c1], w_dt], axis=1).astype(BF16)
    w_gate = w[:, c1 + SSM_HEADS:].astype(BF16)
    n_hd = Q_COLS // ATTN_HEAD_DIM
    gq = (jnp.tile(q_norm_g[l], n_hd) * (ATTN_HEAD_DIM ** -0.5 * LOG2E)).reshape(1, Q_COLS)
    gk = jnp.tile(k_norm_g[l], n_hd).reshape(1, K_COLS)
    gm = g_mix[l].reshape(1, d)

    qn, kn, v = _qkv_proj(x2, gm, w_qkv, gq, gk)
    pad_h = (0, DT_PAD - SSM_HEADS)
    z, xbc, dt, acum, acum_t = _ssm_proj(x2, gm, w_ssm, conv_w[l], conv_b[l].reshape(1, -1),
                                         jnp.pad(dt_bias[l], pad_h).reshape(1, -1),
                                         jnp.pad(a_log[l], pad_h).reshape(1, -1), s)

    lam_vecs = jnp.stack([lambda_q1[l], lambda_k1[l], lambda_q2[l], lambda_k2[l]]).astype(F32)
    attn = _diff_attention(qn.reshape(b, s, -1), kn.reshape(b, s, -1), v.reshape(b, s, -1), rel_bias, q_norm_g[l],
                           k_norm_g[l], lam_vecs, attn_sub_g[l].reshape(1, ATTN_V_DIM))

    y = _ssd(z.reshape(b, s, -1), xbc.reshape(b, s, -1), dt.reshape(b, s, -1), acum.reshape(b, s, -1),
             acum_t.reshape(b, s, -1), jnp.repeat(d_skip[l], SSM_HEAD_DIM).reshape(1, -1),
             ssm_norm_g[l].reshape(1, -1))

    x1, h2, pos, top_w, tile_cnt = _merge(
        x2, attn.reshape(t, -1), y.reshape(t, -1), gm, w_gate, w_attn_o[l].astype(BF16), w_ssm_o[l].astype(BF16),
        w_out[l].astype(BF16), g_ffn[l].reshape(1, d), w_router[l].T, b_router[l].reshape(-1, 1))

    n_tiles = t // min(ROW_TILE, t)
    cnt = tile_cnt.reshape(n_tiles, N_EXPERTS, LANES)[:, :, 0]
    cnt_al = _round_up(cnt, ROW_ALIGN)
    run_start = jnp.cumsum(cnt_al, axis=1) - cnt_al
    padded = _round_up(jnp.sum(cnt_al, axis=0), MOE_TILE)
    end_pad = jnp.cumsum(padded)
    run_off = (end_pad - padded)[None, :] + jnp.cumsum(cnt_al, axis=0) - cnt_al
    n_rows = _round_up(t * TOP_K + n_tiles * N_EXPERTS * (ROW_ALIGN - 1), MOE_TILE) + N_EXPERTS * MOE_TILE
    block_start = jnp.arange(n_rows // MOE_TILE, dtype=jnp.int32) * MOE_TILE
    block_e = jnp.minimum(jnp.sum(block_start[:, None] >= end_pad[None, :], axis=1), N_EXPERTS - 1).astype(jnp.int32)
    n_used = (end_pad[-1:] // MOE_TILE).astype(jnp.int32)
    last_block = jnp.concatenate([jnp.maximum(end_pad - MOE_TILE, 0), end_pad[-1:] // MOE_TILE]).astype(jnp.int32)
    tables = [a.reshape(-1).astype(jnp.int32) for a in (run_off, cnt_al, run_start)]

    xs = _dispatch(*tables, last_block, h2, pos, n_rows)
    ys = _experts(block_e, n_used, xs, w1[l], b1[l][:, None, :], w2[l], b2[l][:, None, :])
    out = _combine(*tables, ys, x1, pos.T, top_w.T)
    return out.reshape(b, s, d)
```

```python
import functools
import math

import jax
import jax.numpy as jnp
from jax import lax
from jax.experimental import pallas as pl
from jax.experimental.pallas import tpu as pltpu

F32 = jnp.float32
BF16 = jnp.bfloat16

D_MODEL = 1024
ATTN_HEADS = 8
ATTN_HEAD_DIM = 64
ATTN_V_DIM = 2 * ATTN_HEAD_DIM
LAMBDA_INIT = 0.8 - 0.6 * math.exp(-0.3 * 0)
NUM_BUCKETS = 32
MAX_DISTANCE = 128
D_INNER = 2 * D_MODEL
SSM_HEAD_DIM = 64
SSM_HEADS = D_INNER // SSM_HEAD_DIM
SSM_GROUPS = 8
SSM_HEADS_PER_GROUP = SSM_HEADS // SSM_GROUPS
D_STATE = 128
CONV_WIDTH = 4
SSM_CHUNK = 128
CONV_DIM = D_INNER + 2 * SSM_GROUPS * D_STATE
N_EXPERTS = 32
TOP_K = 4
D_EXPERT = D_MODEL
SWIGLU_LIMIT = 7.0
SWIGLU_ALPHA = 1.702
RMS_EPS = 1e-6
SSM_EPS = 1e-5
Q_COLS = ATTN_HEADS * 2 * ATTN_HEAD_DIM
K_COLS = Q_COLS
V_COLS = ATTN_HEADS * ATTN_V_DIM

LANES = 128
MXU_DIM = 256
DT_PAD = LANES
NEG_BIG = -1e30
LOG2E = math.log2(math.e)
EXP2_SAFE_BOUND = 80.0
VMEM_LIMIT = 56 * 1024 * 1024

ROW_TILE = 512
ATTN_TILE = 512
MOE_TILE = 512
ROW_ALIGN = 16


def _rms(x, eps):
    return x * lax.rsqrt(jnp.mean(x * x, axis=-1, keepdims=True) + eps)


def _sigmoid(x):
    return 1.0 / (1.0 + jnp.exp(-x))


def _params(sem):
    return pltpu.CompilerParams(dimension_semantics=sem, vmem_limit_bytes=VMEM_LIMIT)


def _resident(shape):
    return pl.BlockSpec(shape, lambda *_: (0,) * len(shape), pipeline_mode=pl.Buffered(1))


def _qkv_kernel(x_ref, g_ref, w_ref, gq_ref, gk_ref, q_ref, k_ref, v_ref):
    h = (_rms(x_ref[...], RMS_EPS) * g_ref[...]).astype(BF16)
    qkv = jnp.dot(h, w_ref[...], preferred_element_type=F32)
    r = lax.broadcasted_iota(jnp.int32, (MXU_DIM, MXU_DIM), 0) // ATTN_HEAD_DIM
    c = lax.broadcasted_iota(jnp.int32, (MXU_DIM, MXU_DIM), 1) // ATTN_HEAD_DIM
    group_ones = jnp.where(r == c, 1.0, 0.0).astype(BF16)

    def head_norm(t, gain_ref, out_ref):
        for cc in range(Q_COLS // MXU_DIM):
            sl = slice(cc * MXU_DIM, (cc + 1) * MXU_DIM)
            tc = t[:, sl]
            ss = jnp.dot((tc * tc).astype(BF16), group_ones, preferred_element_type=F32)
            out_ref[:, sl] = (tc * lax.rsqrt(ss * (1.0 / ATTN_HEAD_DIM) + RMS_EPS) * gain_ref[:, sl]).astype(BF16)

    head_norm(qkv[:, :Q_COLS], gq_ref, q_ref)
    head_norm(qkv[:, Q_COLS:Q_COLS + K_COLS], gk_ref, k_ref)
    v_ref[...] = qkv[:, Q_COLS + K_COLS:].astype(BF16)


def _qkv_proj(x2, g_mix, w_qkv, gq, gk):
    t = x2.shape[0]
    tm = min(ROW_TILE, t)
    row = lambda i: (i, 0)
    fix = lambda i: (0, 0)
    out = jax.ShapeDtypeStruct((t, D_MODEL), BF16)
    return pl.pallas_call(
        _qkv_kernel,
        grid=(t // tm,),
        in_specs=[pl.BlockSpec((tm, D_MODEL), row), pl.BlockSpec((1, D_MODEL), fix),
                  _resident((D_MODEL, 3 * D_MODEL)), pl.BlockSpec((1, D_MODEL), fix),
                  pl.BlockSpec((1, D_MODEL), fix)],
        out_specs=[pl.BlockSpec((tm, D_MODEL), row)] * 3,
        out_shape=[out, out, out],
        compiler_params=_params(("arbitrary",)),
        name="qkv_proj",
    )(x2, g_mix, w_qkv, gq, gk)


CONV_COLS = 1024
SUBLANES = 8
CHUNK_VREGS = SSM_CHUNK // SUBLANES


def _chunk_time(row):
    return (row >> 3) + CHUNK_VREGS * (row & (SUBLANES - 1))


def _ssm_proj_kernel(x_ref, g_ref, w_ref, cw_ref, cb_ref, dtb_ref, alog_ref, z_ref, xbc_ref, dt_ref, acum_ref,
                     acumt_ref, tail_s, h_s, p_s, *, tiles_per_seq):
    tm = x_ref.shape[0]
    nb = tm // SSM_CHUNK
    taps = CONV_WIDTH - 1

    @pl.when(pl.program_id(0) % tiles_per_seq == 0)
    def _sequence_start():
        tail_s[...] = jnp.zeros(tail_s.shape, F32)

    h = (_rms(x_ref[...], RMS_EPS) * g_ref[...]).astype(BF16)
    rr = lax.broadcasted_iota(jnp.int32, (SSM_CHUNK, SSM_CHUNK), 0)
    cc = lax.broadcasted_iota(jnp.int32, (SSM_CHUNK, SSM_CHUNK), 1)
    pick = jnp.where(cc == _chunk_time(rr), 1.0, 0.0).astype(BF16)
    h = jnp.concatenate(
        [jnp.dot(pick, h[b * SSM_CHUNK:(b + 1) * SSM_CHUNK], preferred_element_type=F32).astype(BF16)
         for b in range(nb)], axis=0)

    h_s[...] = h

    def project(stage):
        return jnp.dot(h_s[...], w_ref[:, stage * CONV_COLS:(stage + 1) * CONV_COLS], preferred_element_type=F32)

    z_stages = D_INNER // CONV_COLS
    n_stages = z_stages + CONV_DIM // CONV_COLS
    sub = lax.broadcasted_iota(jnp.int32, (nb * taps, SUBLANES, CONV_COLS), 1)
    p_s[0] = project(0)
    for stage in range(n_stages):
        if stage + 1 < n_stages:
            p_s[(stage + 1) % 2] = project(stage + 1)
        p = p_s[stage % 2]
        if stage < z_stages:
            z_ref[:, stage * CONV_COLS:(stage + 1) * CONV_COLS] = (p * _sigmoid(p)).astype(BF16)
            continue
        c = stage - z_stages
        cols = slice(c * CONV_COLS, (c + 1) * CONV_COLS)
        p4 = p.reshape(nb, CHUNK_VREGS, SUBLANES, CONV_COLS)
        last = p4[:, CHUNK_VREGS - taps:]
        seq = jnp.concatenate([tail_s[c][None], last], axis=0).reshape((nb + 1) * taps, SUBLANES, CONV_COLS)
        tail_s[c] = last[nb - 1]
        rolled = pltpu.roll(seq, 1, axis=1)
        wrapped = jnp.where(sub == 0, rolled[:nb * taps], rolled[taps:]).reshape(nb, taps, SUBLANES, CONV_COLS)
        conv = cb_ref[:, cols] + cw_ref[taps:taps + 1, cols] * p4
        for back in range(1, CONV_WIDTH):
            shifted = jnp.concatenate([wrapped[:, taps - back:], p4[:, :CHUNK_VREGS - back]], axis=1)
            conv = conv + cw_ref[taps - back:taps - back + 1, cols] * shifted
        xbc_ref[:, cols] = (conv * _sigmoid(conv)).reshape(tm, CONV_COLS).astype(BF16)
    dtl = jnp.dot(h_s[...], w_ref[:, D_INNER + CONV_DIM:], preferred_element_type=F32) + dtb_ref[...]
    dt = jnp.maximum(dtl, 0.0) + jnp.log(1.0 + jnp.exp(-jnp.abs(dtl)))
    dt_ref[...] = dt
    a = dt * (-jnp.exp(alog_ref[...]))
    upto = jnp.where(_chunk_time(rr) >= _chunk_time(cc), 1.0, 0.0).astype(F32)
    for b in range(nb):
        rows = slice(b * SSM_CHUNK, (b + 1) * SSM_CHUNK)
        acum = jnp.dot(upto, a[rows], preferred_element_type=F32, precision=lax.Precision.HIGHEST)
        acum_ref[rows, :] = acum
        acumt_ref[rows, :] = acum.T


def _ssm_proj(x2, g_mix, w_ssm, conv_w, conv_b, dt_bias, a_log, seq_len):
    t = x2.shape[0]
    tm = min(ROW_TILE, seq_len)
    assert seq_len % tm == 0
    ncol = D_INNER + CONV_DIM + DT_PAD
    row = lambda i: (i, 0)
    fix = lambda i: (0, 0)
    heads = jax.ShapeDtypeStruct((t, DT_PAD), F32)
    return pl.pallas_call(
        functools.partial(_ssm_proj_kernel, tiles_per_seq=seq_len // tm),
        grid=(t // tm,),
        in_specs=[pl.BlockSpec((tm, D_MODEL), row), pl.BlockSpec((1, D_MODEL), fix),
                  _resident((D_MODEL, ncol)), pl.BlockSpec((CONV_WIDTH, CONV_DIM), fix),
                  pl.BlockSpec((1, CONV_DIM), fix), pl.BlockSpec((1, DT_PAD), fix), pl.BlockSpec((1, DT_PAD), fix)],
        out_specs=[pl.BlockSpec((tm, D_INNER), row), pl.BlockSpec((tm, CONV_DIM), row)]
        + [pl.BlockSpec((tm, DT_PAD), row)] * 3,
        out_shape=[jax.ShapeDtypeStruct((t, D_INNER), BF16), jax.ShapeDtypeStruct((t, CONV_DIM), BF16),
                   heads, heads, heads],
        scratch_shapes=[pltpu.VMEM((CONV_DIM // CONV_COLS, CONV_WIDTH - 1, SUBLANES, CONV_COLS), F32),
                        pltpu.VMEM((tm, D_MODEL), BF16), pltpu.VMEM((2, tm, CONV_COLS), F32)],
        compiler_params=_params(("arbitrary",)),
        name="ssm_proj",
    )(x2, g_mix, w_ssm, conv_w, conv_b, dt_bias, a_log)


def _split_maps(q):
    lane = lax.broadcasted_iota(jnp.int32, q.shape, 1)
    zero = jnp.zeros_like(q)
    return jnp.where(lane < ATTN_HEAD_DIM, q, zero), jnp.where(lane >= ATTN_HEAD_DIM, q, zero)


def _attn_finalize(acc1, l1, acc2, l2, lam_ref, subg_ref, o_ref):
    lam_v = lam_ref[...]
    lam = (jnp.exp(jnp.sum(lam_v[0:1] * lam_v[1:2], axis=-1, keepdims=True))
           - jnp.exp(jnp.sum(lam_v[2:3] * lam_v[3:4], axis=-1, keepdims=True)) + LAMBDA_INIT)
    o = acc1 / l1 - lam * (acc2 / l2)
    o_ref[...] = (_rms(o, RMS_EPS) * subg_ref[...] * (1.0 - LAMBDA_INIT)).astype(BF16)


def _attn_bounded_kernel(q_ref, k_ref, v_ref, bias_ref, lam_ref, subg_ref, o_ref, qq_s, vv_s, acc_s, prev_s, s_s,
                         *, nq):
    pair_id = pl.program_id(2)
    tq = qq_s.shape[1] // 2
    tk = tq
    blocks = (pair_id, nq - 1 - pair_id)

    @pl.when(pair_id == 0)
    def _extend_v():
        vv_s[:, :ATTN_V_DIM] = v_ref[...]
        vv_s[:, ATTN_V_DIM:] = jnp.ones((vv_s.shape[0], ATTN_V_DIM), BF16)

    def finalize(acc, block):
        _attn_finalize(acc[0:tq, :ATTN_V_DIM], acc[0:tq, ATTN_V_DIM:], acc[tq:2 * tq, :ATTN_V_DIM],
                       acc[tq:2 * tq, ATTN_V_DIM:], lam_ref, subg_ref,
                       o_ref.at[pl.ds(pl.multiple_of(block * tq, tq), tq), :])

    @pl.when(jnp.logical_and(jnp.logical_and(pl.program_id(0) == 0, pl.program_id(1) == 0), pair_id == 0))
    def _first_step():
        acc_s[...] = jnp.ones(acc_s.shape, F32)

    prev_s[...] = acc_s[...]
    acc_s[...] = jnp.zeros(acc_s.shape, F32)

    for side in range(2):
        q1, q2 = _split_maps(q_ref[pl.ds(pl.multiple_of(blocks[side] * tq, tq), tq), :])
        qq_s[side, 0:tq, :] = q1
        qq_s[side, tq:2 * tq, :] = q2
    contract_last = (((1,), (1,)), ((), ()))

    def tile(t):
        first = t <= pair_id
        side = jnp.where(first, 0, 1)
        j = jnp.where(first, t, t - pair_id - 1)
        diag = jnp.where(first, blocks[0], blocks[1])
        kind = jnp.where(j == diag, 0, jnp.where(j == diag - 1, 1, 2))
        return side, pl.ds(pl.multiple_of(j * tk, tk), tk), kind

    def logits(t):
        side, rows, kind = tile(t)
        s = lax.dot_general(qq_s[side], k_ref[rows, :], contract_last, preferred_element_type=F32)
        b = bias_ref[kind]
        return jnp.concatenate([s[0:tq] + b, s[tq:2 * tq] + b], axis=0)

    def accumulate(t, slot):
        side, rows, _ = tile(t)
        acc_s[side] += jnp.dot(jnp.exp2(s_s[slot]).astype(BF16), vv_s[rows, :], preferred_element_type=F32)

    s_s[0] = logits(0)
    for t in range(nq + 1):
        if t < nq:
            s_s[(t + 1) % 2] = logits(t + 1)
        accumulate(t, t % 2)
    finalize(prev_s[0], jnp.maximum(pair_id - 1, 0))
    finalize(prev_s[1], nq - jnp.maximum(pair_id, 1))

    @pl.when(pair_id == nq // 2 - 1)
    def _last_step():
        finalize(acc_s[0], blocks[0])
        finalize(acc_s[1], blocks[1])


def _attn_online_kernel(it_ref, jt_ref, q_ref, k_ref, v_ref, bias_ref, lam_ref, subg_ref, o_ref,
                        q1_s, q2_s, m1_s, m2_s, l1_s, l2_s, acc1_s, acc2_s):
    step = pl.program_id(2)
    i = it_ref[step]
    j = jt_ref[step]

    @pl.when(j == 0)
    def _init():
        q1_s[...], q2_s[...] = _split_maps(q_ref[...])
        for m_s, l_s, acc_s in ((m1_s, l1_s, acc1_s), (m2_s, l2_s, acc2_s)):
            m_s[...] = jnp.full(m_s.shape, NEG_BIG, F32)
            l_s[...] = jnp.zeros(l_s.shape, F32)
            acc_s[...] = jnp.zeros(acc_s.shape, F32)

    def update(bias):
        k = k_ref[...]
        v = v_ref[...]
        contract_last = (((1,), (1,)), ((), ()))
        for q_s, m_s, l_s, acc_s in ((q1_s, m1_s, l1_s, acc1_s), (q2_s, m2_s, l2_s, acc2_s)):
            s = lax.dot_general(q_s[...], k, contract_last, preferred_element_type=F32)
            if bias is not None:
                s = s + bias_ref[bias]
            m_old = m_s[...]
            m_new = jnp.maximum(m_old, jnp.max(s, axis=-1, keepdims=True))
            alpha = jnp.exp2(m_old - m_new)
            p = jnp.exp2(s - m_new)
            l_s[...] = alpha * l_s[...] + jnp.sum(p, axis=-1, keepdims=True)
            acc_s[...] = alpha * acc_s[...] + jnp.dot(p.astype(BF16), v, preferred_element_type=F32)
            m_s[...] = m_new

    @pl.when(j == i)
    def _diag():
        update(0)

    @pl.when(j == i - 1)
    def _prev():
        update(1)

    @pl.when(j < i - 1)
    def _far():
        update(None)

    @pl.when(j == i)
    def _finalize():
        _attn_finalize(acc1_s[...], l1_s[...], acc2_s[...], l2_s[...], lam_ref, subg_ref, o_ref)


def _t5_bucket(dist):
    n = jnp.maximum(dist, 0)
    max_exact = NUM_BUCKETS // 2
    scaled = jnp.log(jnp.maximum(n, 1).astype(F32) / max_exact) / math.log(MAX_DISTANCE / max_exact)
    large = max_exact + (scaled * (NUM_BUCKETS - max_exact)).astype(jnp.int32)
    large = jnp.minimum(large, NUM_BUCKETS - 1)
    return jnp.where(n < max_exact, n, large)


def _bias_tiles(rel_bias, tile):
    blk = MAX_DISTANCE
    assert tile % blk == 0
    nb = tile // blk
    table = (rel_bias - rel_bias[NUM_BUCKETS - 1]).astype(F32) * LOG2E
    r = jnp.arange(blk, dtype=jnp.int32)
    d0 = r[:, None] - r[None, :]

    def lookup(dist):
        onehot = (_t5_bucket(dist)[..., None] == jnp.arange(NUM_BUCKETS, dtype=jnp.int32)).astype(F32)
        return jnp.einsum('qkn,nh->hqk', onehot, table, precision=lax.Precision.HIGHEST)

    on_diag = jnp.where(d0[None] >= 0, lookup(d0), NEG_BIG)
    sub_diag = lookup(d0 + blk)
    zeros = jnp.zeros_like(sub_diag)
    masked = jnp.full_like(sub_diag, NEG_BIG)

    def assemble(pick):
        return jnp.concatenate(
            [jnp.concatenate([pick(bi, bj) for bj in range(nb)], axis=-1) for bi in range(nb)], axis=-2)

    diag_tile = assemble(lambda bi, bj: on_diag if bi == bj else sub_diag if bi == bj + 1 else zeros if bi > bj else masked)
    prev_tile = assemble(lambda bi, bj: sub_diag if (bi == 0 and bj == nb - 1) else zeros)
    return jnp.stack([diag_tile, prev_tile, jnp.zeros_like(prev_tile)], axis=1)


def _attn_bounded(qn, kn, v, bias, lam_vecs, sub_g):
    b, s, _ = qn.shape
    tile = min(ATTN_TILE, s)
    nq = s // tile
    assert nq % 2 == 0
    seq_map = lambda bb, h, i: (bb, 0, h)
    return pl.pallas_call(
        functools.partial(_attn_bounded_kernel, nq=nq),
        grid=(b, ATTN_HEADS, nq // 2),
        in_specs=[pl.BlockSpec((None, s, ATTN_V_DIM), seq_map),
                  pl.BlockSpec((None, s, ATTN_V_DIM), seq_map),
                  pl.BlockSpec((None, s, ATTN_V_DIM), seq_map),
                  pl.BlockSpec((None, 3, tile, tile), lambda bb, h, i: (h, 0, 0, 0)),
                  pl.BlockSpec((4, ATTN_HEAD_DIM), lambda bb, h, i: (0, 0)),
                  pl.BlockSpec((1, ATTN_V_DIM), lambda bb, h, i: (0, 0))],
        out_specs=pl.BlockSpec((None, s, ATTN_V_DIM), seq_map),
        out_shape=jax.ShapeDtypeStruct((b, s, V_COLS), BF16),
        scratch_shapes=[pltpu.VMEM((2, 2 * tile, ATTN_V_DIM), BF16), pltpu.VMEM((s, 2 * ATTN_V_DIM), BF16),
                        pltpu.VMEM((2, 2 * tile, 2 * ATTN_V_DIM), F32), pltpu.VMEM((2, 2 * tile, 2 * ATTN_V_DIM), F32),
                        pltpu.VMEM((2, 2 * tile, tile), F32)],
        compiler_params=_params(("arbitrary", "arbitrary", "arbitrary")),
        name="diff_attn_bounded",
    )(qn, kn, v, bias, lam_vecs, sub_g)


def _attn_online(qn, kn, v, bias, lam_vecs, sub_g):
    b, s, _ = qn.shape
    tile = min(ATTN_TILE, s)
    nq = s // tile
    it = jnp.asarray([i for i in range(nq) for _ in range(i + 1)], jnp.int32)
    jt = jnp.asarray([j for i in range(nq) for j in range(i + 1)], jnp.int32)
    q_map = lambda bb, h, st, it_r, jt_r: (bb, it_r[st], h)
    kv_map = lambda bb, h, st, it_r, jt_r: (bb, jt_r[st], h)
    grid_spec = pltpu.PrefetchScalarGridSpec(
        num_scalar_prefetch=2,
        grid=(b, ATTN_HEADS, int(it.shape[0])),
        in_specs=[pl.BlockSpec((None, tile, ATTN_V_DIM), q_map),
                  pl.BlockSpec((None, tile, ATTN_V_DIM), kv_map),
                  pl.BlockSpec((None, tile, ATTN_V_DIM), kv_map),
                  pl.BlockSpec((None, 3, tile, tile), lambda bb, h, st, it_r, jt_r: (h, 0, 0, 0)),
                  pl.BlockSpec((4, ATTN_HEAD_DIM), lambda bb, h, st, it_r, jt_r: (0, 0)),
                  pl.BlockSpec((1, ATTN_V_DIM), lambda bb, h, st, it_r, jt_r: (0, 0))],
        out_specs=pl.BlockSpec((None, tile, ATTN_V_DIM), q_map),
        scratch_shapes=[pltpu.VMEM((tile, ATTN_V_DIM), BF16), pltpu.VMEM((tile, ATTN_V_DIM), BF16),
                        pltpu.VMEM((tile, 1), F32), pltpu.VMEM((tile, 1), F32),
                        pltpu.VMEM((tile, 1), F32), pltpu.VMEM((tile, 1), F32),
                        pltpu.VMEM((tile, ATTN_V_DIM), F32), pltpu.VMEM((tile, ATTN_V_DIM), F32)],
    )
    return pl.pallas_call(
        _attn_online_kernel,
        grid_spec=grid_spec,
        out_shape=jax.ShapeDtypeStruct((b, s, V_COLS), BF16),
        compiler_params=_params(("arbitrary", "arbitrary", "arbitrary")),
        name="diff_attn_online",
    )(it, jt, qn, kn, v, bias, lam_vecs, sub_g)


def _diff_attention(qn, kn, v, rel_bias, q_gain, k_gain, lam_vecs, sub_g):
    tile = min(ATTN_TILE, qn.shape[1])
    bias = _bias_tiles(rel_bias, tile)
    spread = jnp.max(jnp.abs(rel_bias - rel_bias[NUM_BUCKETS - 1]))
    bound = LOG2E * (1.05 * math.sqrt(ATTN_HEAD_DIM) * jnp.max(jnp.abs(q_gain)) * jnp.max(jnp.abs(k_gain)) + spread)
    args = (qn, kn, v, bias, lam_vecs, sub_g)
    return lax.cond(bound < EXP2_SAFE_BOUND, lambda a: _attn_bounded(*a), lambda a: _attn_online(*a), args)


def _ssd_kernel(z_ref, xbc_ref, dt_ref, acum_ref, acumt_ref, dskip_ref, ng_ref, y_ref, state_s):
    L = SSM_CHUNK

    @pl.when(pl.program_id(1) == 0)
    def _reset():
        state_s[...] = jnp.zeros(state_s.shape, F32)

    xs = xbc_ref[:, :D_INNER].astype(F32)
    bm = xbc_ref[:, D_INNER:D_INNER + SSM_GROUPS * D_STATE]
    cm = xbc_ref[:, D_INNER + SSM_GROUPS * D_STATE:]

    dt = dt_ref[...]
    row = lax.broadcasted_iota(jnp.int32, (L, L), 0)
    col = lax.broadcasted_iota(jnp.int32, (L, L), 1)
    causal = _chunk_time(row) >= _chunk_time(col)
    acum = acum_ref[...]
    acum_t = acumt_ref[...]
    a_end = acum[L - 1:L, :]
    lane = lax.broadcasted_iota(jnp.int32, (L, LANES), 1)
    low = lane < SSM_HEAD_DIM

    def pair(lo, hi):
        return jnp.where(low[:lo.shape[0]], lo, hi)

    def spread(arr, h):
        return jnp.broadcast_to(arr[:, h:h + 1], (arr.shape[0], LANES))

    contract_last = (((1,), (1,)), ((), ()))
    contract_first = (((0,), (0,)), ((), ()))
    y_parts = []
    for g in range(SSM_GROUPS):
        bg = bm[:, g * D_STATE:(g + 1) * D_STATE]
        cg = cm[:, g * D_STATE:(g + 1) * D_STATE]
        cb = lax.dot_general(cg, bg, contract_last, preferred_element_type=F32)
        st = state_s[g]
        y_off = jnp.dot(cg, st.astype(BF16), preferred_element_type=F32)
        xw_parts = []
        dec_parts = []
        for pr in range(SSM_HEADS_PER_GROUP // 2):
            h0 = g * SSM_HEADS_PER_GROUP + 2 * pr
            ch = slice(h0 * SSM_HEAD_DIM, (h0 + 2) * SSM_HEAD_DIM)
            x_pair = xs[:, ch]
            acols = (spread(acum, h0), spread(acum, h0 + 1))
            acol = pair(*acols)
            aend = pair(spread(a_end, h0), spread(a_end, h0 + 1))
            xdt32 = x_pair * pair(spread(dt, h0), spread(dt, h0 + 1))
            xdt = xdt32.astype(BF16)
            yd = []
            for hh, ac in zip((h0, h0 + 1), acols):
                seg = ac - acum_t[hh:hh + 1, :]
                decay = jnp.exp(jnp.where(causal, seg, NEG_BIG))
                yd.append(jnp.dot((cb * decay).astype(BF16), xdt, preferred_element_type=F32))
            y_diag = pair(yd[0], yd[1])
            off = y_off[:, 2 * pr * SSM_HEAD_DIM:(2 * pr + 2) * SSM_HEAD_DIM]
            y_parts.append(y_diag + off * jnp.exp(acol))
            xw_parts.append(xdt32 * jnp.exp(aend - acol))
            dec_parts.append(jnp.exp(aend))
        xw = jnp.concatenate(xw_parts, axis=-1).astype(BF16)
        dec = jnp.concatenate(dec_parts, axis=-1)
        state_s[g] = st * dec + lax.dot_general(bg, xw, contract_first, preferred_element_type=F32)
    y = (jnp.concatenate(y_parts, axis=-1) + dskip_ref[...] * xs) * z_ref[...].astype(F32)
    gsz = D_INNER // SSM_GROUPS
    unpick = jnp.where(row == _chunk_time(col), 1.0, 0.0).astype(BF16)
    for g in range(SSM_GROUPS):
        sl = slice(g * gsz, (g + 1) * gsz)
        yn = (_rms(y[:, sl], SSM_EPS) * ng_ref[:, sl]).astype(BF16)
        y_ref[:, sl] = jnp.dot(unpick, yn, preferred_element_type=F32).astype(BF16)


def _ssd(z, xbc, dt, acum, acum_t, d_skip_ch, norm_g):
    b, s, _ = z.shape
    nc = s // SSM_CHUNK
    blk = lambda bb, c: (bb, c, 0)
    fix = lambda bb, c: (0, 0)
    return pl.pallas_call(
        _ssd_kernel,
        grid=(b, nc),
        in_specs=[pl.BlockSpec((None, SSM_CHUNK, D_INNER), blk), pl.BlockSpec((None, SSM_CHUNK, CONV_DIM), blk)]
        + [pl.BlockSpec((None, SSM_CHUNK, DT_PAD), blk)] * 3
        + [pl.BlockSpec((1, D_INNER), fix), pl.BlockSpec((1, D_INNER), fix)],
        out_specs=pl.BlockSpec((None, SSM_CHUNK, D_INNER), blk),
        out_shape=jax.ShapeDtypeStruct((b, s, D_INNER), BF16),
        scratch_shapes=[pltpu.VMEM((SSM_GROUPS, D_STATE, SSM_HEADS_PER_GROUP * SSM_HEAD_DIM), F32)],
        compiler_params=_params(("arbitrary", "arbitrary")),
        name="ssd",
    )(z, xbc, dt, acum, acum_t, d_skip_ch, norm_g)


def _round_up(n, m):
    return (n + m - 1) // m * m


def _merge_kernel(x_ref, attn_ref, y_ref, gmix_ref, wg_ref, wao_ref, wso_ref, wout_ref, gffn_ref, wr_ref, br_ref,
                  x1_ref, h2_ref, pos_ref, topw_ref, cnt_ref):
    x = x_ref[...]
    tm = x.shape[0]
    h = (_rms(x, RMS_EPS) * gmix_ref[...]).astype(BF16)
    gates = _sigmoid(jnp.dot(h, wg_ref[...], preferred_element_type=F32))
    attn_out = jnp.dot(attn_ref[...], wao_ref[...], preferred_element_type=F32)
    ssm_out = jnp.dot(y_ref[...], wso_ref[...], preferred_element_type=F32)
    merged = gates[:, :D_MODEL] * attn_out + gates[:, D_MODEL:] * ssm_out
    x1 = x + jnp.dot(merged.astype(BF16), wout_ref[...], preferred_element_type=F32)
    x1_ref[...] = x1
    h2 = _rms(x1, RMS_EPS) * gffn_ref[...]
    h2_ref[...] = h2.astype(BF16)

    logits = lax.dot_general(wr_ref[...], h2, (((1,), (1,)), ((), ())), preferred_element_type=F32,
                             precision=lax.Precision.HIGHEST) + br_ref[...]
    eid = lax.broadcasted_iota(jnp.int32, logits.shape, 0)
    vals, hits = [], []
    member = jnp.zeros(logits.shape, F32)
    work = logits
    for kk in range(TOP_K):
        m = jnp.max(work, axis=0, keepdims=True)
        idx = jnp.min(jnp.where(work == m, eid, N_EXPERTS), axis=0, keepdims=True)
        hit = eid == idx
        vals.append(m)
        hits.append(hit)
        member = jnp.where(hit, 1.0, member)
        work = jnp.where(hit, -jnp.inf, work)
    ex = [jnp.exp(v - vals[0]) for v in vals]
    denom = ex[0] + ex[1] + ex[2] + ex[3]
    for kk in range(TOP_K):
        topw_ref[kk:kk + 1, :] = ex[kk] / denom

    r = lax.broadcasted_iota(jnp.int32, (tm, tm), 0)
    c = lax.broadcasted_iota(jnp.int32, (tm, tm), 1)
    before = jnp.where(r < c, 1.0, 0.0).astype(BF16)
    prefix = jnp.dot(member.astype(BF16), before, preferred_element_type=F32)
    cnt = jnp.sum(member, axis=1, keepdims=True).astype(jnp.int32)
    cnt_al = jnp.bitwise_and(cnt + (ROW_ALIGN - 1), -ROW_ALIGN)
    cnt_al = jnp.broadcast_to(cnt_al, (N_EXPERTS, LANES)).astype(F32)
    er = lax.broadcasted_iota(jnp.int32, (N_EXPERTS, N_EXPERTS), 0)
    ec = lax.broadcasted_iota(jnp.int32, (N_EXPERTS, N_EXPERTS), 1)
    lower = jnp.where(ec < er, 1.0, 0.0).astype(F32)
    run_start = jnp.dot(lower, cnt_al, preferred_element_type=F32, precision=lax.Precision.HIGHEST)[:, 0:1]
    base = prefix + run_start
    for kk in range(TOP_K):
        pos_ref[kk:kk + 1, :] = jnp.sum(jnp.where(hits[kk], base, 0.0), axis=0, keepdims=True).astype(jnp.int32)
    cnt_ref[...] = jnp.broadcast_to(cnt, cnt_ref.shape)


def _merge(x2, attn, y, g_mix, w_gate, w_ao, w_so, w_out, g_ffn, w_r_t, b_r):
    t = x2.shape[0]
    tm = min(ROW_TILE, t)
    row = lambda i: (i, 0)
    colb = lambda i: (0, i)
    fix = lambda i: (0, 0)
    return pl.pallas_call(
        _merge_kernel,
        grid=(t // tm,),
        in_specs=[pl.BlockSpec((tm, D_MODEL), row), pl.BlockSpec((tm, V_COLS), row), pl.BlockSpec((tm, D_INNER), row),
                  pl.BlockSpec((1, D_MODEL), fix), _resident((D_MODEL, 2 * D_MODEL)),
                  _resident((V_COLS, D_MODEL)), _resident((D_INNER, D_MODEL)),
                  _resident((D_MODEL, D_MODEL)), pl.BlockSpec((1, D_MODEL), fix),
                  pl.BlockSpec((N_EXPERTS, D_MODEL), fix), pl.BlockSpec((N_EXPERTS, 1), fix)],
        out_specs=[pl.BlockSpec((tm, D_MODEL), row), pl.BlockSpec((tm, D_MODEL), row),
                   pl.BlockSpec((TOP_K, tm), colb), pl.BlockSpec((TOP_K, tm), colb),
                   pl.BlockSpec((N_EXPERTS, LANES), row)],
        out_shape=[jax.ShapeDtypeStruct((t, D_MODEL), F32), jax.ShapeDtypeStruct((t, D_MODEL), BF16),
                   jax.ShapeDtypeStruct((TOP_K, t), jnp.int32), jax.ShapeDtypeStruct((TOP_K, t), F32),
                   jax.ShapeDtypeStruct((t // tm * N_EXPERTS, LANES), jnp.int32)],
        compiler_params=_params(("arbitrary",)),
        name="merge_router",
    )(x2, attn, y, g_mix, w_gate, w_ao, w_so, w_out, g_ffn, w_r_t, b_r)


def _sorted_rows(tm):
    return _round_up(TOP_K * tm + N_EXPERTS * (ROW_ALIGN - 1), LANES)


def _run_copies(n, max_n, vmem_ref, vmem_off, hbm_ref, hbm_off, sem, to_hbm, wait):
    done = 0
    pieces = [ROW_ALIGN << p for p in range((max_n // ROW_ALIGN).bit_length())]
    for bit in reversed(pieces):
        take = (n & bit) != 0

        @pl.when(take)
        def _piece(bit=bit, done=done):
            v = vmem_ref.at[pl.ds(pl.multiple_of(vmem_off + done, ROW_ALIGN), bit)]
            h = hbm_ref.at[pl.ds(pl.multiple_of(hbm_off + done, ROW_ALIGN), bit)]
            cp = pltpu.make_async_copy(v, h, sem) if to_hbm else pltpu.make_async_copy(h, v, sem)
            cp.wait() if wait else cp.start()

        done = done + jnp.where(take, bit, 0)


def _dispatch_kernel(off_ref, cnt_ref, start_ref, last_ref, h2_ref, pos_ref, xs_hbm, buf_s, zero_s, sem, zsem):
    i = pl.program_id(0)
    tm = h2_ref.shape[0]
    rows = buf_s.shape[0]

    @pl.when(i == 0)
    def _zero_last_blocks():
        zero_s[...] = jnp.zeros(zero_s.shape, BF16)
        for e in range(N_EXPERTS):
            start = pl.multiple_of(last_ref[e], MOE_TILE)
            pltpu.make_async_copy(zero_s, xs_hbm.at[pl.ds(start, MOE_TILE)], zsem).start()
        for e in range(N_EXPERTS):
            pltpu.make_async_copy(zero_s, xs_hbm.at[pl.ds(0, MOE_TILE)], zsem).wait()

        def slack(wait):
            def body(blk, carry):
                cp = pltpu.make_async_copy(zero_s, xs_hbm.at[pl.ds(pl.multiple_of(blk * MOE_TILE, MOE_TILE), MOE_TILE)],
                                           zsem)
                cp.wait() if wait else cp.start()
                return carry
            lax.fori_loop(last_ref[N_EXPERTS], xs_hbm.shape[0] // MOE_TILE, body, 0)

        slack(False)
        slack(True)

    pos = pos_ref[...]
    rid = lax.broadcasted_iota(jnp.int32, (rows, tm), 0)
    sel = jnp.zeros((rows, tm), F32)
    for kk in range(TOP_K):
        sel = sel + jnp.where(rid == pos[kk:kk + 1, :], 1.0, 0.0)
    buf_s[...] = jnp.dot(sel.astype(BF16), h2_ref[...], preferred_element_type=F32).astype(BF16)

    for wait in (False, True):
        for e in range(N_EXPERTS):
            idx = i * N_EXPERTS + e
            _run_copies(cnt_ref[idx], tm, buf_s, start_ref[idx], xs_hbm, off_ref[idx], sem, True, wait)


def _dispatch(run_off, run_cnt, run_start, last_block, h2, pos, n_rows):
    t = h2.shape[0]
    tm = min(ROW_TILE, t)
    grid_spec = pltpu.PrefetchScalarGridSpec(
        num_scalar_prefetch=4,
        grid=(t // tm,),
        in_specs=[pl.BlockSpec((tm, D_MODEL), lambda i, *_: (i, 0)),
                  pl.BlockSpec((TOP_K, tm), lambda i, *_: (0, i))],
        out_specs=pl.BlockSpec(memory_space=pl.ANY),
        scratch_shapes=[pltpu.VMEM((_sorted_rows(tm), D_MODEL), BF16), pltpu.VMEM((MOE_TILE, D_MODEL), BF16),
                        pltpu.SemaphoreType.DMA(()), pltpu.SemaphoreType.DMA(())],
    )
    return pl.pallas_call(
        _dispatch_kernel,
        grid_spec=grid_spec,
        out_shape=jax.ShapeDtypeStruct((n_rows, D_MODEL), BF16),
        compiler_params=_params(("arbitrary",)),
        name="moe_dispatch",
    )(run_off, run_cnt, run_start, last_block, h2, pos)


def _expert_kernel(be_ref, nb_ref, x_ref, w1_ref, b1_ref, w2_ref, b2_ref, y_ref, w1_s, w2_s):
    i = pl.program_id(0)

    @pl.when(jnp.logical_or(i == 0, be_ref[i] != be_ref[jnp.maximum(i - 1, 0)]))
    def _new_expert():
        w1_s[...] = w1_ref[...].astype(BF16)
        w2_s[...] = w2_ref[...].astype(BF16)

    @pl.when(i < nb_ref[0])
    def _run():
        gu = jnp.dot(x_ref[...], w1_s[...], preferred_element_type=F32) + b1_ref[...]
        gate = jnp.minimum(gu[:, :D_EXPERT], SWIGLU_LIMIT)
        up = jnp.clip(gu[:, D_EXPERT:], -SWIGLU_LIMIT, SWIGLU_LIMIT)
        act = (up + 1.0) * (gate * _sigmoid(SWIGLU_ALPHA * gate))
        y = jnp.dot(act.astype(BF16), w2_s[...], preferred_element_type=F32) + b2_ref[...]
        y_ref[...] = y.astype(BF16)

    @pl.when(i >= nb_ref[0])
    def _slack():
        y_ref[...] = jnp.zeros(y_ref.shape, BF16)


def _experts(block_e, n_used, xs, w1, b1, w2, b2):
    n_rows = xs.shape[0]
    nb = n_rows // MOE_TILE
    row_in = lambda i, be, nu: (jnp.minimum(i, nu[0] - 1), 0)
    row = lambda i, be, nu: (i, 0)
    wsel = lambda i, be, nu: (be[i], 0, 0)
    grid_spec = pltpu.PrefetchScalarGridSpec(
        num_scalar_prefetch=2,
        grid=(nb,),
        in_specs=[pl.BlockSpec((MOE_TILE, D_MODEL), row_in),
                  pl.BlockSpec((None, D_MODEL, 2 * D_EXPERT), wsel), pl.BlockSpec((None, 1, 2 * D_EXPERT), wsel),
                  pl.BlockSpec((None, D_EXPERT, D_MODEL), wsel), pl.BlockSpec((None, 1, D_MODEL), wsel)],
        out_specs=pl.BlockSpec((MOE_TILE, D_MODEL), row),
        scratch_shapes=[pltpu.VMEM((D_MODEL, 2 * D_EXPERT), BF16), pltpu.VMEM((D_EXPERT, D_MODEL), BF16)],
    )
    return pl.pallas_call(
        _expert_kernel,
        grid_spec=grid_spec,
        out_shape=jax.ShapeDtypeStruct((n_rows, D_MODEL), BF16),
        compiler_params=_params(("arbitrary",)),
        name="moe_experts",
    )(block_e, n_used, xs, w1, b1, w2, b2)


def _combine_kernel(off_ref, cnt_ref, start_ref, y_hbm, x1_ref, pos_ref, w_ref, o_ref, buf_s, sem):
    i = pl.program_id(0)
    tm = x1_ref.shape[0]
    rows = buf_s.shape[0]

    @pl.when(i == 0)
    def _clear():
        buf_s[...] = jnp.zeros(buf_s.shape, BF16)

    def copies(wait):
        for e in range(N_EXPERTS):
            idx = i * N_EXPERTS + e
            _run_copies(cnt_ref[idx], tm, buf_s, start_ref[idx], y_hbm, off_ref[idx], sem, False, wait)

    copies(False)
    pos = pos_ref[...]
    w = w_ref[...]
    cid = lax.broadcasted_iota(jnp.int32, (tm, rows), 1)
    wsel = jnp.zeros((tm, rows), F32)
    for kk in range(TOP_K):
        wsel = wsel + jnp.where(cid == pos[:, kk:kk + 1], w[:, kk:kk + 1], 0.0)
    copies(True)
    o_ref[...] = x1_ref[...] + jnp.dot(wsel.astype(BF16), buf_s[...], preferred_element_type=F32)


def _combine(run_off, run_cnt, run_start, y, x1, pos_tok, w_tok):
    t = x1.shape[0]
    tm = min(ROW_TILE, t)
    grid_spec = pltpu.PrefetchScalarGridSpec(
        num_scalar_prefetch=3,
        grid=(t // tm,),
        in_specs=[pl.BlockSpec(memory_space=pl.ANY),
                  pl.BlockSpec((tm, D_MODEL), lambda i, *_: (i, 0)),
                  pl.BlockSpec((tm, TOP_K), lambda i, *_: (i, 0)), pl.BlockSpec((tm, TOP_K), lambda i, *_: (i, 0))],
        out_specs=pl.BlockSpec((tm, D_MODEL), lambda i, *_: (i, 0)),
        scratch_shapes=[pltpu.VMEM((_sorted_rows(tm), D_MODEL), BF16), pltpu.SemaphoreType.DMA(())],
    )
    return pl.pallas_call(
        _combine_kernel,
        grid_spec=grid_spec,
        out_shape=jax.ShapeDtypeStruct((t, D_MODEL), F32),
        compiler_params=_params(("arbitrary",)),
        name="moe_combine",
    )(run_off, run_cnt, run_start, y, x1, pos_tok, w_tok)


def kernel(x, g_mix, w_in, q_norm_g, k_norm_g, lambda_q1, lambda_k1, lambda_q2, lambda_k2, attn_sub_g, rel_bias,
           w_attn_o, conv_w, conv_b, dt_bias, a_log, d_skip, ssm_norm_g, w_ssm_o, w_out, g_ffn, w_router, b_router,
           w1, b1, w2, b2):
    b, s, d = x.shape
    t = b * s
    l = 0
    x2 = x.reshape(t, d)

    w = w_in[l]
    c0 = Q_COLS + K_COLS + V_COLS
    c1 = c0 + D_INNER + CONV_DIM
    w_qkv = w[:, :c0].astype(BF16)
    w_dt = jnp.pad(w[:, c1:c1 + SSM_HEADS], ((0, 0), (0, DT_PAD - SSM_HEADS)))
    w_ssm = jnp.concatenate([w[:, c0:c1], w_dt], axis=1).astype(BF16)
    w_gate = w[:, c1 + SSM_HEADS:].astype(BF16)
    n_hd = Q_COLS // ATTN_HEAD_DIM
    gq = (jnp.tile(q_norm_g[l], n_hd) * (ATTN_HEAD_DIM ** -0.5 * LOG2E)).reshape(1, Q_COLS)
    gk = jnp.tile(k_norm_g[l], n_hd).reshape(1, K_COLS)
    gm = g_mix[l].reshape(1, d)

    qn, kn, v = _qkv_proj(x2, gm, w_qkv, gq, gk)
    pad_h = (0, DT_PAD - SSM_HEADS)
    z, xbc, dt, acum, acum_t = _ssm_proj(x2, gm, w_ssm, conv_w[l], conv_b[l].reshape(1, -1),
                                         jnp.pad(dt_bias[l], pad_h).reshape(1, -1),
                                         jnp.pad(a_log[l], pad_h).reshape(1, -1), s)

    lam_vecs = jnp.stack([lambda_q1[l], lambda_k1[l], lambda_q2[l], lambda_k2[l]]).astype(F32)
    attn = _diff_attention(qn.reshape(b, s, -1), kn.reshape(b, s, -1), v.reshape(b, s, -1), rel_bias, q_norm_g[l],
                           k_norm_g[l], lam_vecs, attn_sub_g[l].reshape(1, ATTN_V_DIM))

    y = _ssd(z.reshape(b, s, -1), xbc.reshape(b, s, -1), dt.reshape(b, s, -1), acum.reshape(b, s, -1),
             acum_t.reshape(b, s, -1), jnp.repeat(d_skip[l], SSM_HEAD_DIM).reshape(1, -1),
             ssm_norm_g[l].reshape(1, -1))

    x1, h2, pos, top_w, tile_cnt = _merge(
        x2, attn.reshape(t, -1), y.reshape(t, -1), gm, w_gate, w_attn_o[l].astype(BF16), w_ssm_o[l].astype(BF16),
        w_out[l].astype(BF16), g_ffn[l].reshape(1, d), w_router[l].T, b_router[l].reshape(-1, 1))

    n_tiles = t // min(ROW_TILE, t)
    cnt = tile_cnt.reshape(n_tiles, N_EXPERTS, LANES)[:, :, 0]
    cnt_al = _round_up(cnt, ROW_ALIGN)
    run_start = jnp.cumsum(cnt_al, axis=1) - cnt_al
    padded = _round_up(jnp.sum(cnt_al, axis=0), MOE_TILE)
    end_pad = jnp.cumsum(padded)
    run_off = (end_pad - padded)[None, :] + jnp.cumsum(cnt_al, axis=0) - cnt_al
    n_rows = _round_up(t * TOP_K + n_tiles * N_EXPERTS * (ROW_ALIGN - 1), MOE_TILE) + N_EXPERTS * MOE_TILE
    block_start = jnp.arange(n_rows // MOE_TILE, dtype=jnp.int32) * MOE_TILE
    block_e = jnp.minimum(jnp.sum(block_start[:, None] >= end_pad[None, :], axis=1), N_EXPERTS - 1).astype(jnp.int32)
    n_used = (end_pad[-1:] // MOE_TILE).astype(jnp.int32)
    last_block = jnp.concatenate([jnp.maximum(end_pad - MOE_TILE, 0), end_pad[-1:] // MOE_TILE]).astype(jnp.int32)
    tables = [a.reshape(-1).astype(jnp.int32) for a in (run_off, cnt_al, run_start)]

    xs = _dispatch(*tables, last_block, h2, pos, n_rows)
    ys = _experts(block_e, n_used, xs, w1[l], b1[l][:, None, :], w2[l], b2[l][:, None, :])
    out = _combine(*tables, ys, x1, pos.T, top_w.T)
    return out.reshape(b, s, d)
```

```python
import functools
import math

import jax
import jax.numpy as jnp
from jax import lax
from jax.experimental import pallas as pl
from jax.experimental.pallas import tpu as pltpu

F32 = jnp.float32
BF16 = jnp.bfloat16

D_MODEL = 1024
ATTN_HEADS = 8
ATTN_HEAD_DIM = 64
ATTN_V_DIM = 2 * ATTN_HEAD_DIM
LAMBDA_INIT = 0.8 - 0.6 * math.exp(-0.3 * 0)
NUM_BUCKETS = 32
MAX_DISTANCE = 128
D_INNER = 2 * D_MODEL
SSM_HEAD_DIM = 64
SSM_HEADS = D_INNER // SSM_HEAD_DIM
SSM_GROUPS = 8
SSM_HEADS_PER_GROUP = SSM_HEADS // SSM_GROUPS
D_STATE = 128
CONV_WIDTH = 4
SSM_CHUNK = 128
CONV_DIM = D_INNER + 2 * SSM_GROUPS * D_STATE
N_EXPERTS = 32
TOP_K = 4
D_EXPERT = D_MODEL
SWIGLU_LIMIT = 7.0
SWIGLU_ALPHA = 1.702
RMS_EPS = 1e-6
SSM_EPS = 1e-5
Q_COLS = ATTN_HEADS * 2 * ATTN_HEAD_DIM
K_COLS = Q_COLS
V_COLS = ATTN_HEADS * ATTN_V_DIM

LANES = 128
MXU_DIM = 256
DT_PAD = LANES
NEG_BIG = -1e30
LOG2E = math.log2(math.e)
EXP2_SAFE_BOUND = 80.0
VMEM_LIMIT = 56 * 1024 * 1024

ROW_TILE = 512
ATTN_TILE = 512
MOE_TILE = 512
ROW_ALIGN = 16


def _rms(x, eps):
    return x * lax.rsqrt(jnp.mean(x * x, axis=-1, keepdims=True) + eps)


def _sigmoid(x):
    return 1.0 / (1.0 + jnp.exp(-x))


def _params(sem):
    return pltpu.CompilerParams(dimension_semantics=sem, vmem_limit_bytes=VMEM_LIMIT)


def _resident(shape):
    return pl.BlockSpec(shape, lambda *_: (0,) * len(shape), pipeline_mode=pl.Buffered(1))


def _qkv_kernel(x_ref, g_ref, w_ref, gq_ref, gk_ref, q_ref, k_ref, v_ref):
    h = (_rms(x_ref[...], RMS_EPS) * g_ref[...]).astype(BF16)
    qkv = jnp.dot(h, w_ref[...], preferred_element_type=F32)
    r = lax.broadcasted_iota(jnp.int32, (MXU_DIM, MXU_DIM), 0) // ATTN_HEAD_DIM
    c = lax.broadcasted_iota(jnp.int32, (MXU_DIM, MXU_DIM), 1) // ATTN_HEAD_DIM
    group_ones = jnp.where(r == c, 1.0, 0.0).astype(BF16)

    def head_norm(t, gain_ref, out_ref):
        for cc in range(Q_COLS // MXU_DIM):
            sl = slice(cc * MXU_DIM, (cc + 1) * MXU_DIM)
            tc = t[:, sl]
            ss = jnp.dot((tc * tc).astype(BF16), group_ones, preferred_element_type=F32)
            out_ref[:, sl] = (tc * lax.rsqrt(ss * (1.0 / ATTN_HEAD_DIM) + RMS_EPS) * gain_ref[:, sl]).astype(BF16)

    head_norm(qkv[:, :Q_COLS], gq_ref, q_ref)
    head_norm(qkv[:, Q_COLS:Q_COLS + K_COLS], gk_ref, k_ref)
    v_ref[...] = qkv[:, Q_COLS + K_COLS:].astype(BF16)


def _qkv_proj(x2, g_mix, w_qkv, gq, gk):
    t = x2.shape[0]
    tm = min(ROW_TILE, t)
    row = lambda i: (i, 0)
    fix = lambda i: (0, 0)
    out = jax.ShapeDtypeStruct((t, D_MODEL), BF16)
    return pl.pallas_call(
        _qkv_kernel,
        grid=(t // tm,),
        in_specs=[pl.BlockSpec((tm, D_MODEL), row), pl.BlockSpec((1, D_MODEL), fix),
                  _resident((D_MODEL, 3 * D_MODEL)), pl.BlockSpec((1, D_MODEL), fix),
                  pl.BlockSpec((1, D_MODEL), fix)],
        out_specs=[pl.BlockSpec((tm, D_MODEL), row)] * 3,
        out_shape=[out, out, out],
        compiler_params=_params(("arbitrary",)),
        name="qkv_proj",
    )(x2, g_mix, w_qkv, gq, gk)


CONV_COLS = 1024
SUBLANES = 8
CHUNK_VREGS = SSM_CHUNK // SUBLANES


def _chunk_time(row):
    return (row >> 3) + CHUNK_VREGS * (row & (SUBLANES - 1))


def _ssm_proj_kernel(x_ref, g_ref, w_ref, cw_ref, cb_ref, dtb_ref, alog_ref, z_ref, xbc_ref, dt_ref, acum_ref,
                     acumt_ref, tail_s, h_s, p_s, *, tiles_per_seq):
    tm = x_ref.shape[0]
    nb = tm // SSM_CHUNK
    taps = CONV_WIDTH - 1

    @pl.when(pl.program_id(0) % tiles_per_seq == 0)
    def _sequence_start():
        tail_s[...] = jnp.zeros(tail_s.shape, F32)

    h = (_rms(x_ref[...], RMS_EPS) * g_ref[...]).astype(BF16)
    rr = lax.broadcasted_iota(jnp.int32, (SSM_CHUNK, SSM_CHUNK), 0)
    cc = lax.broadcasted_iota(jnp.int32, (SSM_CHUNK, SSM_CHUNK), 1)
    pick = jnp.where(cc == _chunk_time(rr), 1.0, 0.0).astype(BF16)
    h = jnp.concatenate(
        [jnp.dot(pick, h[b * SSM_CHUNK:(b + 1) * SSM_CHUNK], preferred_element_type=F32).astype(BF16)
         for b in range(nb)], axis=0)

    h_s[...] = h

    def project(stage):
        return jnp.dot(h_s[...], w_ref[:, stage * CONV_COLS:(stage + 1) * CONV_COLS], preferred_element_type=F32)

    z_stages = D_INNER // CONV_COLS
    n_stages = z_stages + CONV_DIM // CONV_COLS
    sub = lax.broadcasted_iota(jnp.int32, (nb * taps, SUBLANES, CONV_COLS), 1)
    p_s[0] = project(0)
    for stage in range(n_stages):
        if stage + 1 < n_stages:
            p_s[(stage + 1) % 2] = project(stage + 1)
        p = p_s[stage % 2]
        if stage < z_stages:
            z_ref[:, stage * CONV_COLS:(stage + 1) * CONV_COLS] = (p * _sigmoid(p)).astype(BF16)
            continue
        c = stage - z_stages
        cols = slice(c * CONV_COLS, (c + 1) * CONV_COLS)
        p4 = p.reshape(nb, CHUNK_VREGS, SUBLANES, CONV_COLS)
        last = p4[:, CHUNK_VREGS - taps:]
        seq = jnp.concatenate([tail_s[c][None], last], axis=0).reshape((nb + 1) * taps, SUBLANES, CONV_COLS)
        tail_s[c] = last[nb - 1]
        rolled = pltpu.roll(seq, 1, axis=1)
        wrapped = jnp.where(sub == 0, rolled[:nb * taps], rolled[taps:]).reshape(nb, taps, SUBLANES, CONV_COLS)
        conv = cb_ref[:, cols] + cw_ref[taps:taps + 1, cols] * p4
        for back in range(1, CONV_WIDTH):
            shifted = jnp.concatenate([wrapped[:, taps - back:], p4[:, :CHUNK_VREGS - back]], axis=1)
            conv = conv + cw_ref[taps - back:taps - back + 1, cols] * shifted
        xbc_ref[:, cols] = (conv * _sigmoid(conv)).reshape(tm, CONV_COLS).astype(BF16)
    dtl = jnp.dot(h_s[...], w_ref[:, D_INNER + CONV_DIM:], preferred_element_type=F32) + dtb_ref[...]
    dt = jnp.maximum(dtl, 0.0) + jnp.log(1.0 + jnp.exp(-jnp.abs(dtl)))
    dt_ref[...] = dt
    a = dt * (-jnp.exp(alog_ref[...]))
    upto = jnp.where(_chunk_time(rr) >= _chunk_time(cc), 1.0, 0.0).astype(F32)
    for b in range(nb):
        rows = slice(b * SSM_CHUNK, (b + 1) * SSM_CHUNK)
        acum = jnp.dot(upto, a[rows], preferred_element_type=F32, precision=lax.Precision.HIGHEST)
        acum_ref[rows, :] = acum
        acumt_ref[rows, :] = acum.T


def _ssm_proj(x2, g_mix, w_ssm, conv_w, conv_b, dt_bias, a_log, seq_len):
    t = x2.shape[0]
    tm = min(ROW_TILE, seq_len)
    assert seq_len % tm == 0
    ncol = D_INNER + CONV_DIM + DT_PAD
    row = lambda i: (i, 0)
    fix = lambda i: (0, 0)
    heads = jax.ShapeDtypeStruct((t, DT_PAD), F32)
    return pl.pallas_call(
        functools.partial(_ssm_proj_kernel, tiles_per_seq=seq_len // tm),
        grid=(t // tm,),
        in_specs=[pl.BlockSpec((tm, D_MODEL), row), pl.BlockSpec((1, D_MODEL), fix),
                  _resident((D_MODEL, ncol)), pl.BlockSpec((CONV_WIDTH, CONV_DIM), fix),
                  pl.BlockSpec((1, CONV_DIM), fix), pl.BlockSpec((1, DT_PAD), fix), pl.BlockSpec((1, DT_PAD), fix)],
        out_specs=[pl.BlockSpec((tm, D_INNER), row), pl.BlockSpec((tm, CONV_DIM), row)]
        + [pl.BlockSpec((tm, DT_PAD), row)] * 3,
        out_shape=[jax.ShapeDtypeStruct((t, D_INNER), BF16), jax.ShapeDtypeStruct((t, CONV_DIM), BF16),
                   heads, heads, heads],
        scratch_shapes=[pltpu.VMEM((CONV_DIM // CONV_COLS, CONV_WIDTH - 1, SUBLANES, CONV_COLS), F32),
                        pltpu.VMEM((tm, D_MODEL), BF16), pltpu.VMEM((2, tm, CONV_COLS), F32)],
        compiler_params=_params(("arbitrary",)),
        name="ssm_proj",
    )(x2, g_mix, w_ssm, conv_w, conv_b, dt_bias, a_log)


def _split_maps(q):
    lane = lax.broadcasted_iota(jnp.int32, q.shape, 1)
    zero = jnp.zeros_like(q)
    return jnp.where(lane < ATTN_HEAD_DIM, q, zero), jnp.where(lane >= ATTN_HEAD_DIM, q, zero)


def _attn_finalize(acc1, l1, acc2, l2, lam_ref, subg_ref, o_ref):
    lam_v = lam_ref[...]
    lam = (jnp.exp(jnp.sum(lam_v[0:1] * lam_v[1:2], axis=-1, keepdims=True))
           - jnp.exp(jnp.sum(lam_v[2:3] * lam_v[3:4], axis=-1, keepdims=True)) + LAMBDA_INIT)
    o = acc1 / l1 - lam * (acc2 / l2)
    o_ref[...] = (_rms(o, RMS_EPS) * subg_ref[...] * (1.0 - LAMBDA_INIT)).astype(BF16)


def _attn_bounded_kernel(q_ref, k_ref, v_ref, bias_ref, lam_ref, subg_ref, o_ref, qq_s, vv_s, acc_s, s_s, *, nq):
    pair_id = pl.program_id(2)
    tq = qq_s.shape[1] // 2
    tk = tq
    blocks = (pair_id, nq - 1 - pair_id)

    @pl.when(pair_id == 0)
    def _extend_v():
        vv_s[:, :ATTN_V_DIM] = v_ref[...]
        vv_s[:, ATTN_V_DIM:] = jnp.ones((vv_s.shape[0], ATTN_V_DIM), BF16)

    for side in range(2):
        q1, q2 = _split_maps(q_ref[pl.ds(pl.multiple_of(blocks[side] * tq, tq), tq), :])
        qq_s[side, 0:tq, :] = q1
        qq_s[side, tq:2 * tq, :] = q2
    acc_s[...] = jnp.zeros(acc_s.shape, F32)
    contract_last = (((1,), (1,)), ((), ()))

    def tile(t):
        first = t <= pair_id
        side = jnp.where(first, 0, 1)
        j = jnp.where(first, t, t - pair_id - 1)
        diag = jnp.where(first, blocks[0], blocks[1])
        kind = jnp.where(j == diag, 0, jnp.where(j == diag - 1, 1, 2))
        return side, pl.ds(pl.multiple_of(j * tk, tk), tk), kind

    def logits(t):
        side, rows, kind = tile(t)
        s = lax.dot_general(qq_s[side], k_ref[rows, :], contract_last, preferred_element_type=F32)
        b = bias_ref[kind]
        return jnp.concatenate([s[0:tq] + b, s[tq:2 * tq] + b], axis=0)

    def accumulate(t, slot):
        side, rows, _ = tile(t)
        acc_s[side] += jnp.dot(jnp.exp2(s_s[slot]).astype(BF16), vv_s[rows, :], preferred_element_type=F32)

    s_s[0] = logits(0)
    for t in range(nq + 1):
        if t < nq:
            s_s[(t + 1) % 2] = logits(t + 1)
        accumulate(t, t % 2)

    for side in range(2):
        acc = acc_s[side]
        _attn_finalize(acc[0:tq, :ATTN_V_DIM], acc[0:tq, ATTN_V_DIM:], acc[tq:2 * tq, :ATTN_V_DIM],
                       acc[tq:2 * tq, ATTN_V_DIM:], lam_ref, subg_ref,
                       o_ref.at[pl.ds(pl.multiple_of(blocks[side] * tq, tq), tq), :])


def _attn_online_kernel(it_ref, jt_ref, q_ref, k_ref, v_ref, bias_ref, lam_ref, subg_ref, o_ref,
                        q1_s, q2_s, m1_s, m2_s, l1_s, l2_s, acc1_s, acc2_s):
    step = pl.program_id(2)
    i = it_ref[step]
    j = jt_ref[step]

    @pl.when(j == 0)
    def _init():
        q1_s[...], q2_s[...] = _split_maps(q_ref[...])
        for m_s, l_s, acc_s in ((m1_s, l1_s, acc1_s), (m2_s, l2_s, acc2_s)):
            m_s[...] = jnp.full(m_s.shape, NEG_BIG, F32)
            l_s[...] = jnp.zeros(l_s.shape, F32)
            acc_s[...] = jnp.zeros(acc_s.shape, F32)

    def update(bias):
        k = k_ref[...]
        v = v_ref[...]
        contract_last = (((1,), (1,)), ((), ()))
        for q_s, m_s, l_s, acc_s in ((q1_s, m1_s, l1_s, acc1_s), (q2_s, m2_s, l2_s, acc2_s)):
            s = lax.dot_general(q_s[...], k, contract_last, preferred_element_type=F32)
            if bias is not None:
                s = s + bias_ref[bias]
            m_old = m_s[...]
            m_new = jnp.maximum(m_old, jnp.max(s, axis=-1, keepdims=True))
            alpha = jnp.exp2(m_old - m_new)
            p = jnp.exp2(s - m_new)
            l_s[...] = alpha * l_s[...] + jnp.sum(p, axis=-1, keepdims=True)
            acc_s[...] = alpha * acc_s[...] + jnp.dot(p.astype(BF16), v, preferred_element_type=F32)
            m_s[...] = m_new

    @pl.when(j == i)
    def _diag():
        update(0)

    @pl.when(j == i - 1)
    def _prev():
        update(1)

    @pl.when(j < i - 1)
    def _far():
        update(None)

    @pl.when(j == i)
    def _finalize():
        _attn_finalize(acc1_s[...], l1_s[...], acc2_s[...], l2_s[...], lam_ref, subg_ref, o_ref)


def _t5_bucket(dist):
    n = jnp.maximum(dist, 0)
    max_exact = NUM_BUCKETS // 2
    scaled = jnp.log(jnp.maximum(n, 1).astype(F32) / max_exact) / math.log(MAX_DISTANCE / max_exact)
    large = max_exact + (scaled * (NUM_BUCKETS - max_exact)).astype(jnp.int32)
    large = jnp.minimum(large, NUM_BUCKETS - 1)
    return jnp.where(n < max_exact, n, large)


def _bias_tiles(rel_bias, tile):
    blk = MAX_DISTANCE
    assert tile % blk == 0
    nb = tile // blk
    table = (rel_bias - rel_bias[NUM_BUCKETS - 1]).astype(F32) * LOG2E
    r = jnp.arange(blk, dtype=jnp.int32)
    d0 = r[:, None] - r[None, :]

    def lookup(dist):
        onehot = (_t5_bucket(dist)[..., None] == jnp.arange(NUM_BUCKETS, dtype=jnp.int32)).astype(F32)
        return jnp.einsum('qkn,nh->hqk', onehot, table, precision=lax.Precision.HIGHEST)

    on_diag = jnp.where(d0[None] >= 0, lookup(d0), NEG_BIG)
    sub_diag = lookup(d0 + blk)
    zeros = jnp.zeros_like(sub_diag)
    masked = jnp.full_like(sub_diag, NEG_BIG)

    def assemble(pick):
        return jnp.concatenate(
            [jnp.concatenate([pick(bi, bj) for bj in range(nb)], axis=-1) for bi in range(nb)], axis=-2)

    diag_tile = assemble(lambda bi, bj: on_diag if bi == bj else sub_diag if bi == bj + 1 else zeros if bi > bj else masked)
    prev_tile = assemble(lambda bi, bj: sub_diag if (bi == 0 and bj == nb - 1) else zeros)
    return jnp.stack([diag_tile, prev_tile, jnp.zeros_like(prev_tile)], axis=1)


def _attn_bounded(qn, kn, v, bias, lam_vecs, sub_g):
    b, s, _ = qn.shape
    tile = min(ATTN_TILE, s)
    nq = s // tile
    assert nq % 2 == 0
    seq_map = lambda h, bb, i: (bb, 0, h)
    return pl.pallas_call(
        functools.partial(_attn_bounded_kernel, nq=nq),
        grid=(ATTN_HEADS, b, nq // 2),
        in_specs=[pl.BlockSpec((None, s, ATTN_V_DIM), seq_map),
                  pl.BlockSpec((None, s, ATTN_V_DIM), seq_map),
                  pl.BlockSpec((None, s, ATTN_V_DIM), seq_map),
                  pl.BlockSpec((None, 3, tile, tile), lambda h, bb, i: (h, 0, 0, 0)),
                  pl.BlockSpec((4, ATTN_HEAD_DIM), lambda h, bb, i: (0, 0)),
                  pl.BlockSpec((1, ATTN_V_DIM), lambda h, bb, i: (0, 0))],
        out_specs=pl.BlockSpec((None, s, ATTN_V_DIM), seq_map),
        out_shape=jax.ShapeDtypeStruct((b, s, V_COLS), BF16),
        scratch_shapes=[pltpu.VMEM((2, 2 * tile, ATTN_V_DIM), BF16), pltpu.VMEM((s, 2 * ATTN_V_DIM), BF16),
                        pltpu.VMEM((2, 2 * tile, 2 * ATTN_V_DIM), F32), pltpu.VMEM((2, 2 * tile, tile), F32)],
        compiler_params=_params(("arbitrary", "arbitrary", "arbitrary")),
        name="diff_attn_bounded",
    )(qn, kn, v, bias, lam_vecs, sub_g)


def _attn_online(qn, kn, v, bias, lam_vecs, sub_g):
    b, s, _ = qn.shape
    tile = min(ATTN_TILE, s)
    nq = s // tile
    it = jnp.asarray([i for i in range(nq) for _ in range(i + 1)], jnp.int32)
    jt = jnp.asarray([j for i in range(nq) for j in range(i + 1)], jnp.int32)
    q_map = lambda bb, h, st, it_r, jt_r: (bb, it_r[st], h)
    kv_map = lambda bb, h, st, it_r, jt_r: (bb, jt_r[st], h)
    grid_spec = pltpu.PrefetchScalarGridSpec(
        num_scalar_prefetch=2,
        grid=(b, ATTN_HEADS, int(it.shape[0])),
        in_specs=[pl.BlockSpec((None, tile, ATTN_V_DIM), q_map),
                  pl.BlockSpec((None, tile, ATTN_V_DIM), kv_map),
                  pl.BlockSpec((None, tile, ATTN_V_DIM), kv_map),
                  pl.BlockSpec((None, 3, tile, tile), lambda bb, h, st, it_r, jt_r: (h, 0, 0, 0)),
                  pl.BlockSpec((4, ATTN_HEAD_DIM), lambda bb, h, st, it_r, jt_r: (0, 0)),
                  pl.BlockSpec((1, ATTN_V_DIM), lambda bb, h, st, it_r, jt_r: (0, 0))],
        out_specs=pl.BlockSpec((None, tile, ATTN_V_DIM), q_map),
        scratch_shapes=[pltpu.VMEM((tile, ATTN_V_DIM), BF16), pltpu.VMEM((tile, ATTN_V_DIM), BF16),
                        pltpu.VMEM((tile, 1), F32), pltpu.VMEM((tile, 1), F32),
                        pltpu.VMEM((tile, 1), F32), pltpu.VMEM((tile, 1), F32),
                        pltpu.VMEM((tile, ATTN_V_DIM), F32), pltpu.VMEM((tile, ATTN_V_DIM), F32)],
    )
    return pl.pallas_call(
        _attn_online_kernel,
        grid_spec=grid_spec,
        out_shape=jax.ShapeDtypeStruct((b, s, V_COLS), BF16),
        compiler_params=_params(("arbitrary", "arbitrary", "arbitrary")),
        name="diff_attn_online",
    )(it, jt, qn, kn, v, bias, lam_vecs, sub_g)


def _diff_attention(qn, kn, v, rel_bias, q_gain, k_gain, lam_vecs, sub_g):
    tile = min(ATTN_TILE, qn.shape[1])
    bias = _bias_tiles(rel_bias, tile)
    spread = jnp.max(jnp.abs(rel_bias - rel_bias[NUM_BUCKETS - 1]))
    bound = LOG2E * (1.05 * math.sqrt(ATTN_HEAD_DIM) * jnp.max(jnp.abs(q_gain)) * jnp.max(jnp.abs(k_gain)) + spread)
    args = (qn, kn, v, bias, lam_vecs, sub_g)
    return lax.cond(bound < EXP2_SAFE_BOUND, lambda a: _attn_bounded(*a), lambda a: _attn_online(*a), args)


def _ssd_kernel(z_ref, xbc_ref, dt_ref, acum_ref, acumt_ref, dskip_ref, ng_ref, y_ref, state_s):
    L = SSM_CHUNK

    @pl.when(pl.program_id(1) == 0)
    def _reset():
        state_s[...] = jnp.zeros(state_s.shape, F32)

    xs = xbc_ref[:, :D_INNER].astype(F32)
    bm = xbc_ref[:, D_INNER:D_INNER + SSM_GROUPS * D_STATE]
    cm = xbc_ref[:, D_INNER + SSM_GROUPS * D_STATE:]

    dt = dt_ref[...]
    row = lax.broadcasted_iota(jnp.int32, (L, L), 0)
    col = lax.broadcasted_iota(jnp.int32, (L, L), 1)
    causal = _chunk_time(row) >= _chunk_time(col)
    acum = acum_ref[...]
    acum_t = acumt_ref[...]
    a_end = acum[L - 1:L, :]
    lane = lax.broadcasted_iota(jnp.int32, (L, LANES), 1)
    low = lane < SSM_HEAD_DIM

    def pair(lo, hi):
        return jnp.where(low[:lo.shape[0]], lo, hi)

    def spread(arr, h):
        return jnp.broadcast_to(arr[:, h:h + 1], (arr.shape[0], LANES))

    contract_last = (((1,), (1,)), ((), ()))
    contract_first = (((0,), (0,)), ((), ()))
    y_parts = []
    for g in range(SSM_GROUPS):
        bg = bm[:, g * D_STATE:(g + 1) * D_STATE]
        cg = cm[:, g * D_STATE:(g + 1) * D_STATE]
        cb = lax.dot_general(cg, bg, contract_last, preferred_element_type=F32)
        st = state_s[g]
        y_off = jnp.dot(cg, st.astype(BF16), preferred_element_type=F32)
        xw_parts = []
        dec_parts = []
        for pr in range(SSM_HEADS_PER_GROUP // 2):
            h0 = g * SSM_HEADS_PER_GROUP + 2 * pr
            ch = slice(h0 * SSM_HEAD_DIM, (h0 + 2) * SSM_HEAD_DIM)
            x_pair = xs[:, ch]
            acols = (spread(acum, h0), spread(acum, h0 + 1))
            acol = pair(*acols)
            aend = pair(spread(a_end, h0), spread(a_end, h0 + 1))
            xdt32 = x_pair * pair(spread(dt, h0), spread(dt, h0 + 1))
            xdt = xdt32.astype(BF16)
            yd = []
            for hh, ac in zip((h0, h0 + 1), acols):
                seg = ac - acum_t[hh:hh + 1, :]
                decay = jnp.exp(jnp.where(causal, seg, NEG_BIG))
                yd.append(jnp.dot((cb * decay).astype(BF16), xdt, preferred_element_type=F32))
            y_diag = pair(yd[0], yd[1])
            off = y_off[:, 2 * pr * SSM_HEAD_DIM:(2 * pr + 2) * SSM_HEAD_DIM]
            y_parts.append(y_diag + off * jnp.exp(acol))
            xw_parts.append(xdt32 * jnp.exp(aend - acol))
            dec_parts.append(jnp.exp(aend))
        xw = jnp.concatenate(xw_parts, axis=-1).astype(BF16)
        dec = jnp.concatenate(dec_parts, axis=-1)
        state_s[g] = st * dec + lax.dot_general(bg, xw, contract_first, preferred_element_type=F32)
    y = (jnp.concatenate(y_parts, axis=-1) + dskip_ref[...] * xs) * z_ref[...].astype(F32)
    gsz = D_INNER // SSM_GROUPS
    unpick = jnp.where(row == _chunk_time(col), 1.0, 0.0).astype(BF16)
    for g in range(SSM_GROUPS):
        sl = slice(g * gsz, (g + 1) * gsz)
        yn = (_rms(y[:, sl], SSM_EPS) * ng_ref[:, sl]).astype(BF16)
        y_ref[:, sl] = jnp.dot(unpick, yn, preferred_element_type=F32).astype(BF16)


def _ssd(z, xbc, dt, acum, acum_t, d_skip_ch, norm_g):
    b, s, _ = z.shape
    nc = s // SSM_CHUNK
    blk = lambda bb, c: (bb, c, 0)
    fix = lambda bb, c: (0, 0)
    return pl.pallas_call(
        _ssd_kernel,
        grid=(b, nc),
        in_specs=[pl.BlockSpec((None, SSM_CHUNK, D_INNER), blk), pl.BlockSpec((None, SSM_CHUNK, CONV_DIM), blk)]
        + [pl.BlockSpec((None, SSM_CHUNK, DT_PAD), blk)] * 3
        + [pl.BlockSpec((1, D_INNER), fix), pl.BlockSpec((1, D_INNER), fix)],
        out_specs=pl.BlockSpec((None, SSM_CHUNK, D_INNER), blk),
        out_shape=jax.ShapeDtypeStruct((b, s, D_INNER), BF16),
        scratch_shapes=[pltpu.VMEM((SSM_GROUPS, D_STATE, SSM_HEADS_PER_GROUP * SSM_HEAD_DIM), F32)],
        compiler_params=_params(("arbitrary", "arbitrary")),
        name="ssd",
    )(z, xbc, dt, acum, acum_t, d_skip_ch, norm_g)


def _round_up(n, m):
    return (n + m - 1) // m * m


def _merge_kernel(x_ref, attn_ref, y_ref, gmix_ref, wg_ref, wao_ref, wso_ref, wout_ref, gffn_ref, wr_ref, br_ref,
                  x1_ref, h2_ref, pos_ref, topw_ref, cnt_ref):
    x = x_ref[...]
    tm = x.shape[0]
    h = (_rms(x, RMS_EPS) * gmix_ref[...]).astype(BF16)
    gates = _sigmoid(jnp.dot(h, wg_ref[...], preferred_element_type=F32))
    attn_out = jnp.dot(attn_ref[...], wao_ref[...], preferred_element_type=F32)
    ssm_out = jnp.dot(y_ref[...], wso_ref[...], preferred_element_type=F32)
    merged = gates[:, :D_MODEL] * attn_out + gates[:, D_MODEL:] * ssm_out
    x1 = x + jnp.dot(merged.astype(BF16), wout_ref[...], preferred_element_type=F32)
    x1_ref[...] = x1
    h2 = _rms(x1, RMS_EPS) * gffn_ref[...]
    h2_ref[...] = h2.astype(BF16)

    logits = lax.dot_general(wr_ref[...], h2, (((1,), (1,)), ((), ())), preferred_element_type=F32,
                             precision=lax.Precision.HIGHEST) + br_ref[...]
    eid = lax.broadcasted_iota(jnp.int32, logits.shape, 0)
    vals, hits = [], []
    member = jnp.zeros(logits.shape, F32)
    work = logits
    for kk in range(TOP_K):
        m = jnp.max(work, axis=0, keepdims=True)
        idx = jnp.min(jnp.where(work == m, eid, N_EXPERTS), axis=0, keepdims=True)
        hit = eid == idx
        vals.append(m)
        hits.append(hit)
        member = jnp.where(hit, 1.0, member)
        work = jnp.where(hit, -jnp.inf, work)
    ex = [jnp.exp(v - vals[0]) for v in vals]
    denom = ex[0] + ex[1] + ex[2] + ex[3]
    for kk in range(TOP_K):
        topw_ref[kk:kk + 1, :] = ex[kk] / denom

    r = lax.broadcasted_iota(jnp.int32, (tm, tm), 0)
    c = lax.broadcasted_iota(jnp.int32, (tm, tm), 1)
    before = jnp.where(r < c, 1.0, 0.0).astype(BF16)
    prefix = jnp.dot(member.astype(BF16), before, preferred_element_type=F32)
    cnt = jnp.sum(member, axis=1, keepdims=True).astype(jnp.int32)
    cnt_al = jnp.bitwise_and(cnt + (ROW_ALIGN - 1), -ROW_ALIGN)
    cnt_al = jnp.broadcast_to(cnt_al, (N_EXPERTS, LANES)).astype(F32)
    er = lax.broadcasted_iota(jnp.int32, (N_EXPERTS, N_EXPERTS), 0)
    ec = lax.broadcasted_iota(jnp.int32, (N_EXPERTS, N_EXPERTS), 1)
    lower = jnp.where(ec < er, 1.0, 0.0).astype(F32)
    run_start = jnp.dot(lower, cnt_al, preferred_element_type=F32, precision=lax.Precision.HIGHEST)[:, 0:1]
    base = prefix + run_start
    for kk in range(TOP_K):
        pos_ref[kk:kk + 1, :] = jnp.sum(jnp.where(hits[kk], base, 0.0), axis=0, keepdims=True).astype(jnp.int32)
    cnt_ref[...] = jnp.broadcast_to(cnt, cnt_ref.shape)


def _merge(x2, attn, y, g_mix, w_gate, w_ao, w_so, w_out, g_ffn, w_r_t, b_r):
    t = x2.shape[0]
    tm = min(ROW_TILE, t)
    row = lambda i: (i, 0)
    colb = lambda i: (0, i)
    fix = lambda i: (0, 0)
    return pl.pallas_call(
        _merge_kernel,
        grid=(t // tm,),
        in_specs=[pl.BlockSpec((tm, D_MODEL), row), pl.BlockSpec((tm, V_COLS), row), pl.BlockSpec((tm, D_INNER), row),
                  pl.BlockSpec((1, D_MODEL), fix), _resident((D_MODEL, 2 * D_MODEL)),
                  _resident((V_COLS, D_MODEL)), _resident((D_INNER, D_MODEL)),
                  _resident((D_MODEL, D_MODEL)), pl.BlockSpec((1, D_MODEL), fix),
                  pl.BlockSpec((N_EXPERTS, D_MODEL), fix), pl.BlockSpec((N_EXPERTS, 1), fix)],
        out_specs=[pl.BlockSpec((tm, D_MODEL), row), pl.BlockSpec((tm, D_MODEL), row),
                   pl.BlockSpec((TOP_K, tm), colb), pl.BlockSpec((TOP_K, tm), colb),
                   pl.BlockSpec((N_EXPERTS, LANES), row)],
        out_shape=[jax.ShapeDtypeStruct((t, D_MODEL), F32), jax.ShapeDtypeStruct((t, D_MODEL), BF16),
                   jax.ShapeDtypeStruct((TOP_K, t), jnp.int32), jax.ShapeDtypeStruct((TOP_K, t), F32),
                   jax.ShapeDtypeStruct((t // tm * N_EXPERTS, LANES), jnp.int32)],
        compiler_params=_params(("arbitrary",)),
        name="merge_router",
    )(x2, attn, y, g_mix, w_gate, w_ao, w_so, w_out, g_ffn, w_r_t, b_r)


def _sorted_rows(tm):
    return _round_up(TOP_K * tm + N_EXPERTS * (ROW_ALIGN - 1), LANES)


def _run_copies(n, max_n, vmem_ref, vmem_off, hbm_ref, hbm_off, sem, to_hbm, wait):
    done = 0
    pieces = [ROW_ALIGN << p for p in range((max_n // ROW_ALIGN).bit_length())]
    for bit in reversed(pieces):
        take = (n & bit) != 0

        @pl.when(take)
        def _piece(bit=bit, done=done):
            v = vmem_ref.at[pl.ds(pl.multiple_of(vmem_off + done, ROW_ALIGN), bit)]
            h = hbm_ref.at[pl.ds(pl.multiple_of(hbm_off + done, ROW_ALIGN), bit)]
            cp = pltpu.make_async_copy(v, h, sem) if to_hbm else pltpu.make_async_copy(h, v, sem)
            cp.wait() if wait else cp.start()

        done = done + jnp.where(take, bit, 0)


def _dispatch_kernel(off_ref, cnt_ref, start_ref, last_ref, h2_ref, pos_ref, xs_hbm, buf_s, zero_s, sem, zsem):
    i = pl.program_id(0)
    tm = h2_ref.shape[0]
    rows = buf_s.shape[0]

    @pl.when(i == 0)
    def _zero_last_blocks():
        zero_s[...] = jnp.zeros(zero_s.shape, BF16)
        for e in range(N_EXPERTS):
            start = pl.multiple_of(last_ref[e], MOE_TILE)
            pltpu.make_async_copy(zero_s, xs_hbm.at[pl.ds(start, MOE_TILE)], zsem).start()
        for e in range(N_EXPERTS):
            pltpu.make_async_copy(zero_s, xs_hbm.at[pl.ds(0, MOE_TILE)], zsem).wait()

        def slack(wait):
            def body(blk, carry):
                cp = pltpu.make_async_copy(zero_s, xs_hbm.at[pl.ds(pl.multiple_of(blk * MOE_TILE, MOE_TILE), MOE_TILE)],
                                           zsem)
                cp.wait() if wait else cp.start()
                return carry
            lax.fori_loop(last_ref[N_EXPERTS], xs_hbm.shape[0] // MOE_TILE, body, 0)

        slack(False)
        slack(True)

    pos = pos_ref[...]
    rid = lax.broadcasted_iota(jnp.int32, (rows, tm), 0)
    sel = jnp.zeros((rows, tm), F32)
    for kk in range(TOP_K):
        sel = sel + jnp.where(rid == pos[kk:kk + 1, :], 1.0, 0.0)
    buf_s[...] = jnp.dot(sel.astype(BF16), h2_ref[...], preferred_element_type=F32).astype(BF16)

    for wait in (False, True):
        for e in range(N_EXPERTS):
            idx = i * N_EXPERTS + e
            _run_copies(cnt_ref[idx], tm, buf_s, start_ref[idx], xs_hbm, off_ref[idx], sem, True, wait)


def _dispatch(run_off, run_cnt, run_start, last_block, h2, pos, n_rows):
    t = h2.shape[0]
    tm = min(ROW_TILE, t)
    grid_spec = pltpu.PrefetchScalarGridSpec(
        num_scalar_prefetch=4,
        grid=(t // tm,),
        in_specs=[pl.BlockSpec((tm, D_MODEL), lambda i, *_: (i, 0)),
                  pl.BlockSpec((TOP_K, tm), lambda i, *_: (0, i))],
        out_specs=pl.BlockSpec(memory_space=pl.ANY),
        scratch_shapes=[pltpu.VMEM((_sorted_rows(tm), D_MODEL), BF16), pltpu.VMEM((MOE_TILE, D_MODEL), BF16),
                        pltpu.SemaphoreType.DMA(()), pltpu.SemaphoreType.DMA(())],
    )
    return pl.pallas_call(
        _dispatch_kernel,
        grid_spec=grid_spec,
        out_shape=jax.ShapeDtypeStruct((n_rows, D_MODEL), BF16),
        compiler_params=_params(("arbitrary",)),
        name="moe_dispatch",
    )(run_off, run_cnt, run_start, last_block, h2, pos)


def _expert_kernel(be_ref, nb_ref, x_ref, w1_ref, b1_ref, w2_ref, b2_ref, y_ref, w1_s, w2_s):
    i = pl.program_id(0)

    @pl.when(jnp.logical_or(i == 0, be_ref[i] != be_ref[jnp.maximum(i - 1, 0)]))
    def _new_expert():
        w1_s[...] = w1_ref[...].astype(BF16)
        w2_s[...] = w2_ref[...].astype(BF16)

    @pl.when(i < nb_ref[0])
    def _run():
        gu = jnp.dot(x_ref[...], w1_s[...], preferred_element_type=F32) + b1_ref[...]
        gate = jnp.minimum(gu[:, :D_EXPERT], SWIGLU_LIMIT)
        up = jnp.clip(gu[:, D_EXPERT:], -SWIGLU_LIMIT, SWIGLU_LIMIT)
        act = (up + 1.0) * (gate * _sigmoid(SWIGLU_ALPHA * gate))
        y = jnp.dot(act.astype(BF16), w2_s[...], preferred_element_type=F32) + b2_ref[...]
        y_ref[...] = y.astype(BF16)

    @pl.when(i >= nb_ref[0])
    def _slack():
        y_ref[...] = jnp.zeros(y_ref.shape, BF16)


def _experts(block_e, n_used, xs, w1, b1, w2, b2):
    n_rows = xs.shape[0]
    nb = n_rows // MOE_TILE
    row_in = lambda i, be, nu: (jnp.minimum(i, nu[0] - 1), 0)
    row = lambda i, be, nu: (i, 0)
    wsel = lambda i, be, nu: (be[i], 0, 0)
    grid_spec = pltpu.PrefetchScalarGridSpec(
        num_scalar_prefetch=2,
        grid=(nb,),
        in_specs=[pl.BlockSpec((MOE_TILE, D_MODEL), row_in),
                  pl.BlockSpec((None, D_MODEL, 2 * D_EXPERT), wsel), pl.BlockSpec((None, 1, 2 * D_EXPERT), wsel),
                  pl.BlockSpec((None, D_EXPERT, D_MODEL), wsel), pl.BlockSpec((None, 1, D_MODEL), wsel)],
        out_specs=pl.BlockSpec((MOE_TILE, D_MODEL), row),
        scratch_shapes=[pltpu.VMEM((D_MODEL, 2 * D_EXPERT), BF16), pltpu.VMEM((D_EXPERT, D_MODEL), BF16)],
    )
    return pl.pallas_call(
        _expert_kernel,
        grid_spec=grid_spec,
        out_shape=jax.ShapeDtypeStruct((n_rows, D_MODEL), BF16),
        compiler_params=_params(("arbitrary",)),
        name="moe_experts",
    )(block_e, n_used, xs, w1, b1, w2, b2)


def _combine_kernel(off_ref, cnt_ref, start_ref, y_hbm, x1_ref, pos_ref, w_ref, o_ref, buf_s, sem):
    i = pl.program_id(0)
    tm = x1_ref.shape[0]
    rows = buf_s.shape[0]

    @pl.when(i == 0)
    def _clear():
        buf_s[...] = jnp.zeros(buf_s.shape, BF16)

    def copies(wait):
        for e in range(N_EXPERTS):
            idx = i * N_EXPERTS + e
            _run_copies(cnt_ref[idx], tm, buf_s, start_ref[idx], y_hbm, off_ref[idx], sem, False, wait)

    copies(False)
    pos = pos_ref[...]
    w = w_ref[...]
    cid = lax.broadcasted_iota(jnp.int32, (tm, rows), 1)
    wsel = jnp.zeros((tm, rows), F32)
    for kk in range(TOP_K):
        wsel = wsel + jnp.where(cid == pos[:, kk:kk + 1], w[:, kk:kk + 1], 0.0)
    copies(True)
    o_ref[...] = x1_ref[...] + jnp.dot(wsel.astype(BF16), buf_s[...], preferred_element_type=F32)


def _combine(run_off, run_cnt, run_start, y, x1, pos_tok, w_tok):
    t = x1.shape[0]
    tm = min(ROW_TILE, t)
    grid_spec = pltpu.PrefetchScalarGridSpec(
        num_scalar_prefetch=3,
        grid=(t // tm,),
        in_specs=[pl.BlockSpec(memory_space=pl.ANY),
                  pl.BlockSpec((tm, D_MODEL), lambda i, *_: (i, 0)),
                  pl.BlockSpec((tm, TOP_K), lambda i, *_: (i, 0)), pl.BlockSpec((tm, TOP_K), lambda i, *_: (i, 0))],
        out_specs=pl.BlockSpec((tm, D_MODEL), lambda i, *_: (i, 0)),
        scratch_shapes=[pltpu.VMEM((_sorted_rows(tm), D_MODEL), BF16), pltpu.SemaphoreType.DMA(())],
    )
    return pl.pallas_call(
        _combine_kernel,
        grid_spec=grid_spec,
        out_shape=jax.ShapeDtypeStruct((t, D_MODEL), F32),
        compiler_params=_params(("arbitrary",)),
        name="moe_combine",
    )(run_off, run_cnt, run_start, y, x1, pos_tok, w_tok)


def kernel(x, g_mix, w_in, q_norm_g, k_norm_g, lambda_q1, lambda_k1, lambda_q2, lambda_k2, attn_sub_g, rel_bias,
           w_attn_o, conv_w, conv_b, dt_bias, a_log, d_skip, ssm_norm_g, w_ssm_o, w_out, g_ffn, w_router, b_router,
           w1, b1, w2, b2):
    b, s, d = x.shape
    t = b * s
    l = 0
    x2 = x.reshape(t, d)

    w = w_in[l]
    c0 = Q_COLS + K_COLS + V_COLS
    c1 = c0 + D_INNER + CONV_DIM
    w_qkv = w[:, :c0].astype(BF16)
    w_dt = jnp.pad(w[:, c1:c1 + SSM_HEADS], ((0, 0), (0, DT_PAD - SSM_HEADS)))
    w_ssm = jnp.concatenate([w[:, c0:c1], w_dt], axis=1).astype(BF16)
    w_gate = w[:, c1 + SSM_HEADS:].astype(BF16)
    n_hd = Q_COLS // ATTN_HEAD_DIM
    gq = (jnp.tile(q_norm_g[l], n_hd) * (ATTN_HEAD_DIM ** -0.5 * LOG2E)).reshape(1, Q_COLS)
    gk = jnp.tile(k_norm_g[l], n_hd).reshape(1, K_COLS)
    gm = g_mix[l].reshape(1, d)

    qn, kn, v = _qkv_proj(x2, gm, w_qkv, gq, gk)
    pad_h = (0, DT_PAD - SSM_HEADS)
    z, xbc, dt, acum, acum_t = _ssm_proj(x2, gm, w_ssm, conv_w[l], conv_b[l].reshape(1, -1),
                                         jnp.pad(dt_bias[l], pad_h).reshape(1, -1),
                                         jnp.pad(a_log[l], pad_h).reshape(1, -1), s)

    lam_vecs = jnp.stack([lambda_q1[l], lambda_k1[l], lambda_q2[l], lambda_k2[l]]).astype(F32)
    attn = _diff_attention(qn.reshape(b, s, -1), kn.reshape(b, s, -1), v.reshape(b, s, -1), rel_bias, q_norm_g[l],
                           k_norm_g[l], lam_vecs, attn_sub_g[l].reshape(1, ATTN_V_DIM))

    y = _ssd(z.reshape(b, s, -1), xbc.reshape(b, s, -1), dt.reshape(b, s, -1), acum.reshape(b, s, -1),
             acum_t.reshape(b, s, -1), jnp.repeat(d_skip[l], SSM_HEAD_DIM).reshape(1, -1),
             ssm_norm_g[l].reshape(1, -1))

    x1, h2, pos, top_w, tile_cnt = _merge(
        x2, attn.reshape(t, -1), y.reshape(t, -1), gm, w_gate, w_attn_o[l].astype(BF16), w_ssm_o[l].astype(BF16),
        w_out[l].astype(BF16), g_ffn[l].reshape(1, d), w_router[l].T, b_router[l].reshape(-1, 1))

    n_tiles = t // min(ROW_TILE, t)
    cnt = tile_cnt.reshape(n_tiles, N_EXPERTS, LANES)[:, :, 0]
    cnt_al = _round_up(cnt, ROW_ALIGN)
    run_start = jnp.cumsum(cnt_al, axis=1) - cnt_al
    padded = _round_up(jnp.sum(cnt_al, axis=0), MOE_TILE)
    end_pad = jnp.cumsum(padded)
    run_off = (end_pad - padded)[None, :] + jnp.cumsum(cnt_al, axis=0) - cnt_al
    n_rows = _round_up(t * TOP_K + n_tiles * N_EXPERTS * (ROW_ALIGN - 1), MOE_TILE) + N_EXPERTS * MOE_TILE
    block_start = jnp.arange(n_rows // MOE_TILE, dtype=jnp.int32) * MOE_TILE
    block_e = jnp.minimum(jnp.sum(block_start[:, None] >= end_pad[None, :], axis=1), N_EXPERTS - 1).astype(jnp.int32)
    n_used = (end_pad[-1:] // MOE_TILE).astype(jnp.int32)
    last_block = jnp.concatenate([jnp.maximum(end_pad - MOE_TILE, 0), end_pad[-1:] // MOE_TILE]).astype(jnp.int32)
    tables = [a.reshape(-1).astype(jnp.int32) for a in (run_off, cnt_al, run_start)]

    xs = _dispatch(*tables, last_block, h2, pos, n_rows)
    ys = _experts(block_e, n_used, xs, w1[l], b1[l][:, None, :], w2[l], b2[l][:, None, :])
    out = _combine(*tables, ys, x1, pos.T, top_w.T)
    return out.reshape(b, s, d)
```

```python
import functools
import math

import jax
import jax.numpy as jnp
from jax import lax
from jax.experimental import pallas as pl
from jax.experimental.pallas import tpu as pltpu

F32 = jnp.float32
BF16 = jnp.bfloat16

D_MODEL = 1024
ATTN_HEADS = 8
ATTN_HEAD_DIM = 64
ATTN_V_DIM = 2 * ATTN_HEAD_DIM
LAMBDA_INIT = 0.8 - 0.6 * math.exp(-0.3 * 0)
NUM_BUCKETS = 32
MAX_DISTANCE = 128
D_INNER = 2 * D_MODEL
SSM_HEAD_DIM = 64
SSM_HEADS = D_INNER // SSM_HEAD_DIM
SSM_GROUPS = 8
SSM_HEADS_PER_GROUP = SSM_HEADS // SSM_GROUPS
D_STATE = 128
CONV_WIDTH = 4
SSM_CHUNK = 128
CONV_DIM = D_INNER + 2 * SSM_GROUPS * D_STATE
N_EXPERTS = 32
TOP_K = 4
D_EXPERT = D_MODEL
SWIGLU_LIMIT = 7.0
SWIGLU_ALPHA = 1.702
RMS_EPS = 1e-6
SSM_EPS = 1e-5
Q_COLS = ATTN_HEADS * 2 * ATTN_HEAD_DIM
K_COLS = Q_COLS
V_COLS = ATTN_HEADS * ATTN_V_DIM

LANES = 128
MXU_DIM = 256
DT_PAD = LANES
NEG_BIG = -1e30
LOG2E = math.log2(math.e)
EXP2_SAFE_BOUND = 80.0
VMEM_LIMIT = 56 * 1024 * 1024

ROW_TILE = 512
ATTN_TILE = 512
MOE_TILE = 512
ROW_ALIGN = 16
SSD_CHUNKS_PER_STEP = 4


def _rms(x, eps):
    return x * lax.rsqrt(jnp.mean(x * x, axis=-1, keepdims=True) + eps)


def _sigmoid(x):
    return 1.0 / (1.0 + jnp.exp(-x))


def _params(sem):
    return pltpu.CompilerParams(dimension_semantics=sem, vmem_limit_bytes=VMEM_LIMIT)


def _resident(shape):
    return pl.BlockSpec(shape, lambda *_: (0,) * len(shape), pipeline_mode=pl.Buffered(1))


def _qkv_kernel(x_ref, g_ref, w_ref, gq_ref, gk_ref, q_ref, k_ref, v_ref):
    h = (_rms(x_ref[...], RMS_EPS) * g_ref[...]).astype(BF16)
    qkv = jnp.dot(h, w_ref[...], preferred_element_type=F32)
    r = lax.broadcasted_iota(jnp.int32, (MXU_DIM, MXU_DIM), 0) // ATTN_HEAD_DIM
    c = lax.broadcasted_iota(jnp.int32, (MXU_DIM, MXU_DIM), 1) // ATTN_HEAD_DIM
    group_ones = jnp.where(r == c, 1.0, 0.0).astype(BF16)

    def head_norm(t, gain_ref, out_ref):
        for cc in range(Q_COLS // MXU_DIM):
            sl = slice(cc * MXU_DIM, (cc + 1) * MXU_DIM)
            tc = t[:, sl]
            ss = jnp.dot((tc * tc).astype(BF16), group_ones, preferred_element_type=F32)
            out_ref[:, sl] = (tc * lax.rsqrt(ss * (1.0 / ATTN_HEAD_DIM) + RMS_EPS) * gain_ref[:, sl]).astype(BF16)

    head_norm(qkv[:, :Q_COLS], gq_ref, q_ref)
    head_norm(qkv[:, Q_COLS:Q_COLS + K_COLS], gk_ref, k_ref)
    v_ref[...] = qkv[:, Q_COLS + K_COLS:].astype(BF16)


def _qkv_proj(x2, g_mix, w_qkv, gq, gk):
    t = x2.shape[0]
    tm = min(ROW_TILE, t)
    row = lambda i: (i, 0)
    fix = lambda i: (0, 0)
    out = jax.ShapeDtypeStruct((t, D_MODEL), BF16)
    return pl.pallas_call(
        _qkv_kernel,
        grid=(t // tm,),
        in_specs=[pl.BlockSpec((tm, D_MODEL), row), pl.BlockSpec((1, D_MODEL), fix),
                  _resident((D_MODEL, 3 * D_MODEL)), pl.BlockSpec((1, D_MODEL), fix),
                  pl.BlockSpec((1, D_MODEL), fix)],
        out_specs=[pl.BlockSpec((tm, D_MODEL), row)] * 3,
        out_shape=[out, out, out],
        compiler_params=_params(("arbitrary",)),
        name="qkv_proj",
    )(x2, g_mix, w_qkv, gq, gk)


CONV_COLS = 1024
SUBLANES = 8
CHUNK_VREGS = SSM_CHUNK // SUBLANES


def _chunk_time(row):
    return (row >> 3) + CHUNK_VREGS * (row & (SUBLANES - 1))


def _ssm_proj_kernel(x_ref, g_ref, w_ref, cw_ref, cb_ref, dtb_ref, alog_ref, z_ref, xbc_ref, dt_ref, acum_ref,
                     acumt_ref, tail_s, h_s, p_s, *, tiles_per_seq):
    tm = x_ref.shape[0]
    nb = tm // SSM_CHUNK
    taps = CONV_WIDTH - 1

    @pl.when(pl.program_id(0) % tiles_per_seq == 0)
    def _sequence_start():
        tail_s[...] = jnp.zeros(tail_s.shape, F32)

    h = (_rms(x_ref[...], RMS_EPS) * g_ref[...]).astype(BF16)
    rr = lax.broadcasted_iota(jnp.int32, (SSM_CHUNK, SSM_CHUNK), 0)
    cc = lax.broadcasted_iota(jnp.int32, (SSM_CHUNK, SSM_CHUNK), 1)
    pick = jnp.where(cc == _chunk_time(rr), 1.0, 0.0).astype(BF16)
    h = jnp.concatenate(
        [jnp.dot(pick, h[b * SSM_CHUNK:(b + 1) * SSM_CHUNK], preferred_element_type=F32).astype(BF16)
         for b in range(nb)], axis=0)

    h_s[...] = h

    def project(stage):
        return jnp.dot(h_s[...], w_ref[:, stage * CONV_COLS:(stage + 1) * CONV_COLS], preferred_element_type=F32)

    z_stages = D_INNER // CONV_COLS
    n_stages = z_stages + CONV_DIM // CONV_COLS
    sub = lax.broadcasted_iota(jnp.int32, (nb * taps, SUBLANES, CONV_COLS), 1)
    p_s[0] = project(0)
    for stage in range(n_stages):
        if stage + 1 < n_stages:
            p_s[(stage + 1) % 2] = project(stage + 1)
        p = p_s[stage % 2]
        if stage < z_stages:
            z_ref[:, stage * CONV_COLS:(stage + 1) * CONV_COLS] = (p * _sigmoid(p)).astype(BF16)
            continue
        c = stage - z_stages
        cols = slice(c * CONV_COLS, (c + 1) * CONV_COLS)
        p4 = p.reshape(nb, CHUNK_VREGS, SUBLANES, CONV_COLS)
        last = p4[:, CHUNK_VREGS - taps:]
        seq = jnp.concatenate([tail_s[c][None], last], axis=0).reshape((nb + 1) * taps, SUBLANES, CONV_COLS)
        tail_s[c] = last[nb - 1]
        rolled = pltpu.roll(seq, 1, axis=1)
        wrapped = jnp.where(sub == 0, rolled[:nb * taps], rolled[taps:]).reshape(nb, taps, SUBLANES, CONV_COLS)
        conv = cb_ref[:, cols] + cw_ref[taps:taps + 1, cols] * p4
        for back in range(1, CONV_WIDTH):
            shifted = jnp.concatenate([wrapped[:, taps - back:], p4[:, :CHUNK_VREGS - back]], axis=1)
            conv = conv + cw_ref[taps - back:taps - back + 1, cols] * shifted
        xbc_ref[:, cols] = (conv * _sigmoid(conv)).reshape(tm, CONV_COLS).astype(BF16)
    dtl = jnp.dot(h_s[...], w_ref[:, D_INNER + CONV_DIM:], preferred_element_type=F32) + dtb_ref[...]
    dt = jnp.maximum(dtl, 0.0) + jnp.log(1.0 + jnp.exp(-jnp.abs(dtl)))
    dt_ref[...] = dt
    a = dt * (-jnp.exp(alog_ref[...]))
    upto = jnp.where(_chunk_time(rr) >= _chunk_time(cc), 1.0, 0.0).astype(F32)
    for b in range(nb):
        rows = slice(b * SSM_CHUNK, (b + 1) * SSM_CHUNK)
        acum = jnp.dot(upto, a[rows], preferred_element_type=F32, precision=lax.Precision.HIGHEST)
        acum_ref[rows, :] = acum
        acumt_ref[rows, :] = acum.T


def _ssm_proj(x2, g_mix, w_ssm, conv_w, conv_b, dt_bias, a_log, seq_len):
    t = x2.shape[0]
    tm = min(ROW_TILE, seq_len)
    assert seq_len % tm == 0
    ncol = D_INNER + CONV_DIM + DT_PAD
    row = lambda i: (i, 0)
    fix = lambda i: (0, 0)
    heads = jax.ShapeDtypeStruct((t, DT_PAD), F32)
    return pl.pallas_call(
        functools.partial(_ssm_proj_kernel, tiles_per_seq=seq_len // tm),
        grid=(t // tm,),
        in_specs=[pl.BlockSpec((tm, D_MODEL), row), pl.BlockSpec((1, D_MODEL), fix),
                  _resident((D_MODEL, ncol)), pl.BlockSpec((CONV_WIDTH, CONV_DIM), fix),
                  pl.BlockSpec((1, CONV_DIM), fix), pl.BlockSpec((1, DT_PAD), fix), pl.BlockSpec((1, DT_PAD), fix)],
        out_specs=[pl.BlockSpec((tm, D_INNER), row), pl.BlockSpec((tm, CONV_DIM), row)]
        + [pl.BlockSpec((tm, DT_PAD), row)] * 3,
        out_shape=[jax.ShapeDtypeStruct((t, D_INNER), BF16), jax.ShapeDtypeStruct((t, CONV_DIM), BF16),
                   heads, heads, heads],
        scratch_shapes=[pltpu.VMEM((CONV_DIM // CONV_COLS, CONV_WIDTH - 1, SUBLANES, CONV_COLS), F32),
                        pltpu.VMEM((tm, D_MODEL), BF16), pltpu.VMEM((2, tm, CONV_COLS), F32)],
        compiler_params=_params(("arbitrary",)),
        name="ssm_proj",
    )(x2, g_mix, w_ssm, conv_w, conv_b, dt_bias, a_log)


def _split_maps(q):
    lane = lax.broadcasted_iota(jnp.int32, q.shape, 1)
    zero = jnp.zeros_like(q)
    return jnp.where(lane < ATTN_HEAD_DIM, q, zero), jnp.where(lane >= ATTN_HEAD_DIM, q, zero)


def _attn_finalize(acc1, l1, acc2, l2, lam_ref, subg_ref, o_ref):
    lam_v = lam_ref[...]
    lam = (jnp.exp(jnp.sum(lam_v[0:1] * lam_v[1:2], axis=-1, keepdims=True))
           - jnp.exp(jnp.sum(lam_v[2:3] * lam_v[3:4], axis=-1, keepdims=True)) + LAMBDA_INIT)
    o = acc1 / l1 - lam * (acc2 / l2)
    o_ref[...] = (_rms(o, RMS_EPS) * subg_ref[...] * (1.0 - LAMBDA_INIT)).astype(BF16)


def _attn_bounded_kernel(q_ref, k_ref, v_ref, bias_ref, lam_ref, subg_ref, o_ref, qq_s, vv_s, acc_s, s_s, *, nq):
    pair_id = pl.program_id(2)
    tq = qq_s.shape[1] // 2
    tk = tq
    blocks = (pair_id, nq - 1 - pair_id)

    @pl.when(pair_id == 0)
    def _extend_v():
        vv_s[:, :ATTN_V_DIM] = v_ref[...]
        vv_s[:, ATTN_V_DIM:] = jnp.ones((vv_s.shape[0], ATTN_V_DIM), BF16)

    for side in range(2):
        q1, q2 = _split_maps(q_ref[pl.ds(pl.multiple_of(blocks[side] * tq, tq), tq), :])
        qq_s[side, 0:tq, :] = q1
        qq_s[side, tq:2 * tq, :] = q2
    acc_s[...] = jnp.zeros(acc_s.shape, F32)
    contract_last = (((1,), (1,)), ((), ()))

    def tile(t):
        first = t <= pair_id
        side = jnp.where(first, 0, 1)
        j = jnp.where(first, t, t - pair_id - 1)
        diag = jnp.where(first, blocks[0], blocks[1])
        kind = jnp.where(j == diag, 0, jnp.where(j == diag - 1, 1, 2))
        return side, pl.ds(pl.multiple_of(j * tk, tk), tk), kind

    def logits(t):
        side, rows, kind = tile(t)
        s = lax.dot_general(qq_s[side], k_ref[rows, :], contract_last, preferred_element_type=F32)
        b = bias_ref[kind]
        return jnp.concatenate([s[0:tq] + b, s[tq:2 * tq] + b], axis=0)

    def accumulate(t, slot):
        side, rows, _ = tile(t)
        acc_s[side] += jnp.dot(jnp.exp2(s_s[slot]).astype(BF16), vv_s[rows, :], preferred_element_type=F32)

    s_s[0] = logits(0)
    for t in range(nq + 1):
        if t < nq:
            s_s[(t + 1) % 2] = logits(t + 1)
        accumulate(t, t % 2)

    for side in range(2):
        acc = acc_s[side]
        _attn_finalize(acc[0:tq, :ATTN_V_DIM], acc[0:tq, ATTN_V_DIM:], acc[tq:2 * tq, :ATTN_V_DIM],
                       acc[tq:2 * tq, ATTN_V_DIM:], lam_ref, subg_ref,
                       o_ref.at[pl.ds(pl.multiple_of(blocks[side] * tq, tq), tq), :])


def _attn_online_kernel(it_ref, jt_ref, q_ref, k_ref, v_ref, bias_ref, lam_ref, subg_ref, o_ref,
                        q1_s, q2_s, m1_s, m2_s, l1_s, l2_s, acc1_s, acc2_s):
    step = pl.program_id(2)
    i = it_ref[step]
    j = jt_ref[step]

    @pl.when(j == 0)
    def _init():
        q1_s[...], q2_s[...] = _split_maps(q_ref[...])
        for m_s, l_s, acc_s in ((m1_s, l1_s, acc1_s), (m2_s, l2_s, acc2_s)):
            m_s[...] = jnp.full(m_s.shape, NEG_BIG, F32)
            l_s[...] = jnp.zeros(l_s.shape, F32)
            acc_s[...] = jnp.zeros(acc_s.shape, F32)

    def update(bias):
        k = k_ref[...]
        v = v_ref[...]
        contract_last = (((1,), (1,)), ((), ()))
        for q_s, m_s, l_s, acc_s in ((q1_s, m1_s, l1_s, acc1_s), (q2_s, m2_s, l2_s, acc2_s)):
            s = lax.dot_general(q_s[...], k, contract_last, preferred_element_type=F32)
            if bias is not None:
                s = s + bias_ref[bias]
            m_old = m_s[...]
            m_new = jnp.maximum(m_old, jnp.max(s, axis=-1, keepdims=True))
            alpha = jnp.exp2(m_old - m_new)
            p = jnp.exp2(s - m_new)
            l_s[...] = alpha * l_s[...] + jnp.sum(p, axis=-1, keepdims=True)
            acc_s[...] = alpha * acc_s[...] + jnp.dot(p.astype(BF16), v, preferred_element_type=F32)
            m_s[...] = m_new

    @pl.when(j == i)
    def _diag():
        update(0)

    @pl.when(j == i - 1)
    def _prev():
        update(1)

    @pl.when(j < i - 1)
    def _far():
        update(None)

    @pl.when(j == i)
    def _finalize():
        _attn_finalize(acc1_s[...], l1_s[...], acc2_s[...], l2_s[...], lam_ref, subg_ref, o_ref)


def _t5_bucket(dist):
    n = jnp.maximum(dist, 0)
    max_exact = NUM_BUCKETS // 2
    scaled = jnp.log(jnp.maximum(n, 1).astype(F32) / max_exact) / math.log(MAX_DISTANCE / max_exact)
    large = max_exact + (scaled * (NUM_BUCKETS - max_exact)).astype(jnp.int32)
    large = jnp.minimum(large, NUM_BUCKETS - 1)
    return jnp.where(n < max_exact, n, large)


def _bias_tiles(rel_bias, tile):
    blk = MAX_DISTANCE
    assert tile % blk == 0
    nb = tile // blk
    table = (rel_bias - rel_bias[NUM_BUCKETS - 1]).astype(F32) * LOG2E
    r = jnp.arange(blk, dtype=jnp.int32)
    d0 = r[:, None] - r[None, :]

    def lookup(dist):
        onehot = (_t5_bucket(dist)[..., None] == jnp.arange(NUM_BUCKETS, dtype=jnp.int32)).astype(F32)
        return jnp.einsum('qkn,nh->hqk', onehot, table, precision=lax.Precision.HIGHEST)

    on_diag = jnp.where(d0[None] >= 0, lookup(d0), NEG_BIG)
    sub_diag = lookup(d0 + blk)
    zeros = jnp.zeros_like(sub_diag)
    masked = jnp.full_like(sub_diag, NEG_BIG)

    def assemble(pick):
        return jnp.concatenate(
            [jnp.concatenate([pick(bi, bj) for bj in range(nb)], axis=-1) for bi in range(nb)], axis=-2)

    diag_tile = assemble(lambda bi, bj: on_diag if bi == bj else sub_diag if bi == bj + 1 else zeros if bi > bj else masked)
    prev_tile = assemble(lambda bi, bj: sub_diag if (bi == 0 and bj == nb - 1) else zeros)
    return jnp.stack([diag_tile, prev_tile, jnp.zeros_like(prev_tile)], axis=1)


def _attn_bounded(qn, kn, v, bias, lam_vecs, sub_g):
    b, s, _ = qn.shape
    tile = min(ATTN_TILE, s)
    nq = s // tile
    assert nq % 2 == 0
    seq_map = lambda h, bb, i: (bb, 0, h)
    return pl.pallas_call(
        functools.partial(_attn_bounded_kernel, nq=nq),
        grid=(ATTN_HEADS, b, nq // 2),
        in_specs=[pl.BlockSpec((None, s, ATTN_V_DIM), seq_map),
                  pl.BlockSpec((None, s, ATTN_V_DIM), seq_map),
                  pl.BlockSpec((None, s, ATTN_V_DIM), seq_map),
                  pl.BlockSpec((None, 3, tile, tile), lambda h, bb, i: (h, 0, 0, 0)),
                  pl.BlockSpec((4, ATTN_HEAD_DIM), lambda h, bb, i: (0, 0)),
                  pl.BlockSpec((1, ATTN_V_DIM), lambda h, bb, i: (0, 0))],
        out_specs=pl.BlockSpec((None, s, ATTN_V_DIM), seq_map),
        out_shape=jax.ShapeDtypeStruct((b, s, V_COLS), BF16),
        scratch_shapes=[pltpu.VMEM((2, 2 * tile, ATTN_V_DIM), BF16), pltpu.VMEM((s, 2 * ATTN_V_DIM), BF16),
                        pltpu.VMEM((2, 2 * tile, 2 * ATTN_V_DIM), F32), pltpu.VMEM((2, 2 * tile, tile), F32)],
        compiler_params=_params(("arbitrary", "arbitrary", "arbitrary")),
        name="diff_attn_bounded",
    )(qn, kn, v, bias, lam_vecs, sub_g)


def _attn_online(qn, kn, v, bias, lam_vecs, sub_g):
    b, s, _ = qn.shape
    tile = min(ATTN_TILE, s)
    nq = s // tile
    it = jnp.asarray([i for i in range(nq) for _ in range(i + 1)], jnp.int32)
    jt = jnp.asarray([j for i in range(nq) for j in range(i + 1)], jnp.int32)
    q_map = lambda bb, h, st, it_r, jt_r: (bb, it_r[st], h)
    kv_map = lambda bb, h, st, it_r, jt_r: (bb, jt_r[st], h)
    grid_spec = pltpu.PrefetchScalarGridSpec(
        num_scalar_prefetch=2,
        grid=(b, ATTN_HEADS, int(it.shape[0])),
        in_specs=[pl.BlockSpec((None, tile, ATTN_V_DIM), q_map),
                  pl.BlockSpec((None, tile, ATTN_V_DIM), kv_map),
                  pl.BlockSpec((None, tile, ATTN_V_DIM), kv_map),
                  pl.BlockSpec((None, 3, tile, tile), lambda bb, h, st, it_r, jt_r: (h, 0, 0, 0)),
                  pl.BlockSpec((4, ATTN_HEAD_DIM), lambda bb, h, st, it_r, jt_r: (0, 0)),
                  pl.BlockSpec((1, ATTN_V_DIM), lambda bb, h, st, it_r, jt_r: (0, 0))],
        out_specs=pl.BlockSpec((None, tile, ATTN_V_DIM), q_map),
        scratch_shapes=[pltpu.VMEM((tile, ATTN_V_DIM), BF16), pltpu.VMEM((tile, ATTN_V_DIM), BF16),
                        pltpu.VMEM((tile, 1), F32), pltpu.VMEM((tile, 1), F32),
                        pltpu.VMEM((tile, 1), F32), pltpu.VMEM((tile, 1), F32),
                        pltpu.VMEM((tile, ATTN_V_DIM), F32), pltpu.VMEM((tile, ATTN_V_DIM), F32)],
    )
    return pl.pallas_call(
        _attn_online_kernel,
        grid_spec=grid_spec,
        out_shape=jax.ShapeDtypeStruct((b, s, V_COLS), BF16),
        compiler_params=_params(("arbitrary", "arbitrary", "arbitrary")),
        name="diff_attn_online",
    )(it, jt, qn, kn, v, bias, lam_vecs, sub_g)


def _diff_attention(qn, kn, v, rel_bias, q_gain, k_gain, lam_vecs, sub_g):
    tile = min(ATTN_TILE, qn.shape[1])
    bias = _bias_tiles(rel_bias, tile)
    spread = jnp.max(jnp.abs(rel_bias - rel_bias[NUM_BUCKETS - 1]))
    bound = LOG2E * (1.05 * math.sqrt(ATTN_HEAD_DIM) * jnp.max(jnp.abs(q_gain)) * jnp.max(jnp.abs(k_gain)) + spread)
    args = (qn, kn, v, bias, lam_vecs, sub_g)
    return lax.cond(bound < EXP2_SAFE_BOUND, lambda a: _attn_bounded(*a), lambda a: _attn_online(*a), args)


def _ssd_kernel(z_ref, xbc_ref, dt_ref, acum_ref, acumt_ref, dskip_ref, ng_ref, y_ref, state_s):
    L = SSM_CHUNK

    @pl.when(pl.program_id(1) == 0)
    def _reset():
        state_s[...] = jnp.zeros(state_s.shape, F32)

    row = lax.broadcasted_iota(jnp.int32, (L, L), 0)
    col = lax.broadcasted_iota(jnp.int32, (L, L), 1)
    causal = _chunk_time(row) >= _chunk_time(col)
    unpick = jnp.where(row == _chunk_time(col), 1.0, 0.0).astype(BF16)
    lane = lax.broadcasted_iota(jnp.int32, (L, LANES), 1)
    low = lane < SSM_HEAD_DIM
    for c in range(z_ref.shape[0] // L):
        _ssd_chunk(slice(c * L, (c + 1) * L), z_ref, xbc_ref, dt_ref, acum_ref, acumt_ref, dskip_ref, ng_ref, y_ref,
                   state_s, causal, unpick, low)


def _ssd_chunk(rows, z_ref, xbc_ref, dt_ref, acum_ref, acumt_ref, dskip_ref, ng_ref, y_ref, state_s, causal, unpick,
               low):
    L = SSM_CHUNK
    xs = xbc_ref[rows, :D_INNER].astype(F32)
    bm = xbc_ref[rows, D_INNER:D_INNER + SSM_GROUPS * D_STATE]
    cm = xbc_ref[rows, D_INNER + SSM_GROUPS * D_STATE:]
    dt = dt_ref[rows, :]
    acum = acum_ref[rows, :]
    acum_t = acumt_ref[rows, :]
    a_end = acum[L - 1:L, :]

    def pair(lo, hi):
        return jnp.where(low[:lo.shape[0]], lo, hi)

    def spread(arr, h):
        return jnp.broadcast_to(arr[:, h:h + 1], (arr.shape[0], LANES))

    contract_last = (((1,), (1,)), ((), ()))
    contract_first = (((0,), (0,)), ((), ()))
    y_parts = []
    for g in range(SSM_GROUPS):
        bg = bm[:, g * D_STATE:(g + 1) * D_STATE]
        cg = cm[:, g * D_STATE:(g + 1) * D_STATE]
        cb = lax.dot_general(cg, bg, contract_last, preferred_element_type=F32)
        st = state_s[g]
        y_off = jnp.dot(cg, st.astype(BF16), preferred_element_type=F32)
        xw_parts = []
        dec_parts = []
        for pr in range(SSM_HEADS_PER_GROUP // 2):
            h0 = g * SSM_HEADS_PER_GROUP + 2 * pr
            ch = slice(h0 * SSM_HEAD_DIM, (h0 + 2) * SSM_HEAD_DIM)
            x_pair = xs[:, ch]
            acols = (spread(acum, h0), spread(acum, h0 + 1))
            acol = pair(*acols)
            aend = pair(spread(a_end, h0), spread(a_end, h0 + 1))
            xdt32 = x_pair * pair(spread(dt, h0), spread(dt, h0 + 1))
            xdt = xdt32.astype(BF16)
            yd = []
            for hh, ac in zip((h0, h0 + 1), acols):
                seg = ac - acum_t[hh:hh + 1, :]
                decay = jnp.exp(jnp.where(causal, seg, NEG_BIG))
                yd.append(jnp.dot((cb * decay).astype(BF16), xdt, preferred_element_type=F32))
            y_diag = pair(yd[0], yd[1])
            off = y_off[:, 2 * pr * SSM_HEAD_DIM:(2 * pr + 2) * SSM_HEAD_DIM]
            y_parts.append(y_diag + off * jnp.exp(acol))
            xw_parts.append(xdt32 * jnp.exp(aend - acol))
            dec_parts.append(jnp.exp(aend))
        xw = jnp.concatenate(xw_parts, axis=-1).astype(BF16)
        dec = jnp.concatenate(dec_parts, axis=-1)
        state_s[g] = st * dec + lax.dot_general(bg, xw, contract_first, preferred_element_type=F32)
    y = (jnp.concatenate(y_parts, axis=-1) + dskip_ref[...] * xs) * z_ref[rows, :].astype(F32)
    gsz = D_INNER // SSM_GROUPS
    for g in range(SSM_GROUPS):
        sl = slice(g * gsz, (g + 1) * gsz)
        yn = (_rms(y[:, sl], SSM_EPS) * ng_ref[:, sl]).astype(BF16)
        y_ref[rows, sl] = jnp.dot(unpick, yn, preferred_element_type=F32).astype(BF16)


def _ssd(z, xbc, dt, acum, acum_t, d_skip_ch, norm_g):
    b, s, _ = z.shape
    rows = SSD_CHUNKS_PER_STEP * SSM_CHUNK
    assert s % rows == 0
    blk = lambda bb, c: (bb, c, 0)
    fix = lambda bb, c: (0, 0)
    return pl.pallas_call(
        _ssd_kernel,
        grid=(b, s // rows),
        in_specs=[pl.BlockSpec((None, rows, D_INNER), blk), pl.BlockSpec((None, rows, CONV_DIM), blk)]
        + [pl.BlockSpec((None, rows, DT_PAD), blk)] * 3
        + [pl.BlockSpec((1, D_INNER), fix), pl.BlockSpec((1, D_INNER), fix)],
        out_specs=pl.BlockSpec((None, rows, D_INNER), blk),
        out_shape=jax.ShapeDtypeStruct((b, s, D_INNER), BF16),
        scratch_shapes=[pltpu.VMEM((SSM_GROUPS, D_STATE, SSM_HEADS_PER_GROUP * SSM_HEAD_DIM), F32)],
        compiler_params=_params(("arbitrary", "arbitrary")),
        name="ssd",
    )(z, xbc, dt, acum, acum_t, d_skip_ch, norm_g)


def _round_up(n, m):
    return (n + m - 1) // m * m


def _merge_kernel(x_ref, attn_ref, y_ref, gmix_ref, wg_ref, wao_ref, wso_ref, wout_ref, gffn_ref, wr_ref, br_ref,
                  x1_ref, h2_ref, pos_ref, topw_ref, cnt_ref):
    x = x_ref[...]
    tm = x.shape[0]
    h = (_rms(x, RMS_EPS) * gmix_ref[...]).astype(BF16)
    gates = _sigmoid(jnp.dot(h, wg_ref[...], preferred_element_type=F32))
    attn_out = jnp.dot(attn_ref[...], wao_ref[...], preferred_element_type=F32)
    ssm_out = jnp.dot(y_ref[...], wso_ref[...], preferred_element_type=F32)
    merged = gates[:, :D_MODEL] * attn_out + gates[:, D_MODEL:] * ssm_out
    x1 = x + jnp.dot(merged.astype(BF16), wout_ref[...], preferred_element_type=F32)
    x1_ref[...] = x1
    h2 = _rms(x1, RMS_EPS) * gffn_ref[...]
    h2_ref[...] = h2.astype(BF16)

    logits = lax.dot_general(wr_ref[...], h2, (((1,), (1,)), ((), ())), preferred_element_type=F32,
                             precision=lax.Precision.HIGHEST) + br_ref[...]
    eid = lax.broadcasted_iota(jnp.int32, logits.shape, 0)
    vals, hits = [], []
    member = jnp.zeros(logits.shape, F32)
    work = logits
    for kk in range(TOP_K):
        m = jnp.max(work, axis=0, keepdims=True)
        idx = jnp.min(jnp.where(work == m, eid, N_EXPERTS), axis=0, keepdims=True)
        hit = eid == idx
        vals.append(m)
        hits.append(hit)
        member = jnp.where(hit, 1.0, member)
        work = jnp.where(hit, -jnp.inf, work)
    ex = [jnp.exp(v - vals[0]) for v in vals]
    denom = ex[0] + ex[1] + ex[2] + ex[3]
    for kk in range(TOP_K):
        topw_ref[kk:kk + 1, :] = ex[kk] / denom

    r = lax.broadcasted_iota(jnp.int32, (tm, tm), 0)
    c = lax.broadcasted_iota(jnp.int32, (tm, tm), 1)
    before = jnp.where(r < c, 1.0, 0.0).astype(BF16)
    prefix = jnp.dot(member.astype(BF16), before, preferred_element_type=F32)
    cnt = jnp.sum(member, axis=1, keepdims=True).astype(jnp.int32)
    cnt_al = jnp.bitwise_and(cnt + (ROW_ALIGN - 1), -ROW_ALIGN)
    cnt_al = jnp.broadcast_to(cnt_al, (N_EXPERTS, LANES)).astype(F32)
    er = lax.broadcasted_iota(jnp.int32, (N_EXPERTS, N_EXPERTS), 0)
    ec = lax.broadcasted_iota(jnp.int32, (N_EXPERTS, N_EXPERTS), 1)
    lower = jnp.where(ec < er, 1.0, 0.0).astype(F32)
    run_start = jnp.dot(lower, cnt_al, preferred_element_type=F32, precision=lax.Precision.HIGHEST)[:, 0:1]
    base = prefix + run_start
    for kk in range(TOP_K):
        pos_ref[kk:kk + 1, :] = jnp.sum(jnp.where(hits[kk], base, 0.0), axis=0, keepdims=True).astype(jnp.int32)
    cnt_ref[...] = jnp.broadcast_to(cnt, cnt_ref.shape)


def _merge(x2, attn, y, g_mix, w_gate, w_ao, w_so, w_out, g_ffn, w_r_t, b_r):
    t = x2.shape[0]
    tm = min(ROW_TILE, t)
    row = lambda i: (i, 0)
    colb = lambda i: (0, i)
    fix = lambda i: (0, 0)
    return pl.pallas_call(
        _merge_kernel,
        grid=(t // tm,),
        in_specs=[pl.BlockSpec((tm, D_MODEL), row), pl.BlockSpec((tm, V_COLS), row), pl.BlockSpec((tm, D_INNER), row),
                  pl.BlockSpec((1, D_MODEL), fix), _resident((D_MODEL, 2 * D_MODEL)),
                  _resident((V_COLS, D_MODEL)), _resident((D_INNER, D_MODEL)),
                  _resident((D_MODEL, D_MODEL)), pl.BlockSpec((1, D_MODEL), fix),
                  pl.BlockSpec((N_EXPERTS, D_MODEL), fix), pl.BlockSpec((N_EXPERTS, 1), fix)],
        out_specs=[pl.BlockSpec((tm, D_MODEL), row), pl.BlockSpec((tm, D_MODEL), row),
                   pl.BlockSpec((TOP_K, tm), colb), pl.BlockSpec((TOP_K, tm), colb),
                   pl.BlockSpec((N_EXPERTS, LANES), row)],
        out_shape=[jax.ShapeDtypeStruct((t, D_MODEL), F32), jax.ShapeDtypeStruct((t, D_MODEL), BF16),
                   jax.ShapeDtypeStruct((TOP_K, t), jnp.int32), jax.ShapeDtypeStruct((TOP_K, t), F32),
                   jax.ShapeDtypeStruct((t // tm * N_EXPERTS, LANES), jnp.int32)],
        compiler_params=_params(("arbitrary",)),
        name="merge_router",
    )(x2, attn, y, g_mix, w_gate, w_ao, w_so, w_out, g_ffn, w_r_t, b_r)


def _sorted_rows(tm):
    return _round_up(TOP_K * tm + N_EXPERTS * (ROW_ALIGN - 1), LANES)


def _run_copies(n, max_n, vmem_ref, vmem_off, hbm_ref, hbm_off, sem, to_hbm, wait):
    done = 0
    pieces = [ROW_ALIGN << p for p in range((max_n // ROW_ALIGN).bit_length())]
    for bit in reversed(pieces):
        take = (n & bit) != 0

        @pl.when(take)
        def _piece(bit=bit, done=done):
            v = vmem_ref.at[pl.ds(pl.multiple_of(vmem_off + done, ROW_ALIGN), bit)]
            h = hbm_ref.at[pl.ds(pl.multiple_of(hbm_off + done, ROW_ALIGN), bit)]
            cp = pltpu.make_async_copy(v, h, sem) if to_hbm else pltpu.make_async_copy(h, v, sem)
            cp.wait() if wait else cp.start()

        done = done + jnp.where(take, bit, 0)


def _dispatch_kernel(off_ref, cnt_ref, start_ref, last_ref, h2_ref, pos_ref, xs_hbm, buf_s, zero_s, sem, zsem):
    i = pl.program_id(0)
    tm = h2_ref.shape[0]
    rows = buf_s.shape[0]

    @pl.when(i == 0)
    def _zero_last_blocks():
        zero_s[...] = jnp.zeros(zero_s.shape, BF16)
        for e in range(N_EXPERTS):
            start = pl.multiple_of(last_ref[e], MOE_TILE)
            pltpu.make_async_copy(zero_s, xs_hbm.at[pl.ds(start, MOE_TILE)], zsem).start()
        for e in range(N_EXPERTS):
            pltpu.make_async_copy(zero_s, xs_hbm.at[pl.ds(0, MOE_TILE)], zsem).wait()

        def slack(wait):
            def body(blk, carry):
                cp = pltpu.make_async_copy(zero_s, xs_hbm.at[pl.ds(pl.multiple_of(blk * MOE_TILE, MOE_TILE), MOE_TILE)],
                                           zsem)
                cp.wait() if wait else cp.start()
                return carry
            lax.fori_loop(last_ref[N_EXPERTS], xs_hbm.shape[0] // MOE_TILE, body, 0)

        slack(False)
        slack(True)

    pos = pos_ref[...]
    rid = lax.broadcasted_iota(jnp.int32, (rows, tm), 0)
    sel = jnp.zeros((rows, tm), F32)
    for kk in range(TOP_K):
        sel = sel + jnp.where(rid == pos[kk:kk + 1, :], 1.0, 0.0)
    buf_s[...] = jnp.dot(sel.astype(BF16), h2_ref[...], preferred_element_type=F32).astype(BF16)

    for wait in (False, True):
        for e in range(N_EXPERTS):
            idx = i * N_EXPERTS + e
            _run_copies(cnt_ref[idx], tm, buf_s, start_ref[idx], xs_hbm, off_ref[idx], sem, True, wait)


def _dispatch(run_off, run_cnt, run_start, last_block, h2, pos, n_rows):
    t = h2.shape[0]
    tm = min(ROW_TILE, t)
    grid_spec = pltpu.PrefetchScalarGridSpec(
        num_scalar_prefetch=4,
        grid=(t // tm,),
        in_specs=[pl.BlockSpec((tm, D_MODEL), lambda i, *_: (i, 0)),
                  pl.BlockSpec((TOP_K, tm), lambda i, *_: (0, i))],
        out_specs=pl.BlockSpec(memory_space=pl.ANY),
        scratch_shapes=[pltpu.VMEM((_sorted_rows(tm), D_MODEL), BF16), pltpu.VMEM((MOE_TILE, D_MODEL), BF16),
                        pltpu.SemaphoreType.DMA(()), pltpu.SemaphoreType.DMA(())],
    )
    return pl.pallas_call(
        _dispatch_kernel,
        grid_spec=grid_spec,
        out_shape=jax.ShapeDtypeStruct((n_rows, D_MODEL), BF16),
        compiler_params=_params(("arbitrary",)),
        name="moe_dispatch",
    )(run_off, run_cnt, run_start, last_block, h2, pos)


def _expert_kernel(be_ref, nb_ref, x_ref, w1_ref, b1_ref, w2_ref, b2_ref, y_ref, w1_s, w2_s):
    i = pl.program_id(0)

    @pl.when(jnp.logical_or(i == 0, be_ref[i] != be_ref[jnp.maximum(i - 1, 0)]))
    def _new_expert():
        w1_s[...] = w1_ref[...].astype(BF16)
        w2_s[...] = w2_ref[...].astype(BF16)

    @pl.when(i < nb_ref[0])
    def _run():
        gu = jnp.dot(x_ref[...], w1_s[...], preferred_element_type=F32) + b1_ref[...]
        gate = jnp.minimum(gu[:, :D_EXPERT], SWIGLU_LIMIT)
        up = jnp.clip(gu[:, D_EXPERT:], -SWIGLU_LIMIT, SWIGLU_LIMIT)
        act = (up + 1.0) * (gate * _sigmoid(SWIGLU_ALPHA * gate))
        y = jnp.dot(act.astype(BF16), w2_s[...], preferred_element_type=F32) + b2_ref[...]
        y_ref[...] = y.astype(BF16)

    @pl.when(i >= nb_ref[0])
    def _slack():
        y_ref[...] = jnp.zeros(y_ref.shape, BF16)


def _experts(block_e, n_used, xs, w1, b1, w2, b2):
    n_rows = xs.shape[0]
    nb = n_rows // MOE_TILE
    row_in = lambda i, be, nu: (jnp.minimum(i, nu[0] - 1), 0)
    row = lambda i, be, nu: (i, 0)
    wsel = lambda i, be, nu: (be[i], 0, 0)
    grid_spec = pltpu.PrefetchScalarGridSpec(
        num_scalar_prefetch=2,
        grid=(nb,),
        in_specs=[pl.BlockSpec((MOE_TILE, D_MODEL), row_in),
                  pl.BlockSpec((None, D_MODEL, 2 * D_EXPERT), wsel), pl.BlockSpec((None, 1, 2 * D_EXPERT), wsel),
                  pl.BlockSpec((None, D_EXPERT, D_MODEL), wsel), pl.BlockSpec((None, 1, D_MODEL), wsel)],
        out_specs=pl.BlockSpec((MOE_TILE, D_MODEL), row),
        scratch_shapes=[pltpu.VMEM((D_MODEL, 2 * D_EXPERT), BF16), pltpu.VMEM((D_EXPERT, D_MODEL), BF16)],
    )
    return pl.pallas_call(
        _expert_kernel,
        grid_spec=grid_spec,
        out_shape=jax.ShapeDtypeStruct((n_rows, D_MODEL), BF16),
        compiler_params=_params(("arbitrary",)),
        name="moe_experts",
    )(block_e, n_used, xs, w1, b1, w2, b2)


def _combine_kernel(off_ref, cnt_ref, start_ref, y_hbm, x1_ref, pos_ref, w_ref, o_ref, buf_s, sem):
    i = pl.program_id(0)
    tm = x1_ref.shape[0]
    rows = buf_s.shape[0]

    @pl.when(i == 0)
    def _clear():
        buf_s[...] = jnp.zeros(buf_s.shape, BF16)

    def copies(wait):
        for e in range(N_EXPERTS):
            idx = i * N_EXPERTS + e
            _run_copies(cnt_ref[idx], tm, buf_s, start_ref[idx], y_hbm, off_ref[idx], sem, False, wait)

    copies(False)
    pos = pos_ref[...]
    w = w_ref[...]
    cid = lax.broadcasted_iota(jnp.int32, (tm, rows), 1)
    wsel = jnp.zeros((tm, rows), F32)
    for kk in range(TOP_K):
        wsel = wsel + jnp.where(cid == pos[:, kk:kk + 1], w[:, kk:kk + 1], 0.0)
    copies(True)
    o_ref[...] = x1_ref[...] + jnp.dot(wsel.astype(BF16), buf_s[...], preferred_element_type=F32)


def _combine(run_off, run_cnt, run_start, y, x1, pos_tok, w_tok):
    t = x1.shape[0]
    tm = min(ROW_TILE, t)
    grid_spec = pltpu.PrefetchScalarGridSpec(
        num_scalar_prefetch=3,
        grid=(t // tm,),
        in_specs=[pl.BlockSpec(memory_space=pl.ANY),
                  pl.BlockSpec((tm, D_MODEL), lambda i, *_: (i, 0)),
                  pl.BlockSpec((tm, TOP_K), lambda i, *_: (i, 0)), pl.BlockSpec((tm, TOP_K), lambda i, *_: (i, 0))],
        out_specs=pl.BlockSpec((tm, D_MODEL), lambda i, *_: (i, 0)),
        scratch_shapes=[pltpu.VMEM((_sorted_rows(tm), D_MODEL), BF16), pltpu.SemaphoreType.DMA(())],
    )
    return pl.pallas_call(
        _combine_kernel,
        grid_spec=grid_spec,
        out_shape=jax.ShapeDtypeStruct((t, D_MODEL), F32),
        compiler_params=_params(("arbitrary",)),
        name="moe_combine",
    )(run_off, run_cnt, run_start, y, x1, pos_tok, w_tok)


def kernel(x, g_mix, w_in, q_norm_g, k_norm_g, lambda_q1, lambda_k1, lambda_q2, lambda_k2, attn_sub_g, rel_bias,
           w_attn_o, conv_w, conv_b, dt_bias, a_log, d_skip, ssm_norm_g, w_ssm_o, w_out, g_ffn, w_router, b_router,
           w1, b1, w2, b2):
    b, s, d = x.shape
    t = b * s
    l = 0
    x2 = x.reshape(t, d)

    w = w_in[l]
    c0 = Q_COLS + K_COLS + V_COLS
    c1 = c0 + D_INNER + CONV_DIM
    w_qkv = w[:, :c0].astype(BF16)
    w_dt = jnp.pad(w[:, c1:c1 + SSM_HEADS], ((0, 0), (0, DT_PAD - SSM_HEADS)))
    w_ssm = jnp.concatenate([w[:, c0:c1], w_dt], axis=1).astype(BF16)
    w_gate = w[:, c1 + SSM_HEADS:].astype(BF16)
    n_hd = Q_COLS // ATTN_HEAD_DIM
    gq = (jnp.tile(q_norm_g[l], n_hd) * (ATTN_HEAD_DIM ** -0.5 * LOG2E)).reshape(1, Q_COLS)
    gk = jnp.tile(k_norm_g[l], n_hd).reshape(1, K_COLS)
    gm = g_mix[l].reshape(1, d)

    qn, kn, v = _qkv_proj(x2, gm, w_qkv, gq, gk)
    pad_h = (0, DT_PAD - SSM_HEADS)
    z, xbc, dt, acum, acum_t = _ssm_proj(x2, gm, w_ssm, conv_w[l], conv_b[l].reshape(1, -1),
                                         jnp.pad(dt_bias[l], pad_h).reshape(1, -1),
                                         jnp.pad(a_log[l], pad_h).reshape(1, -1), s)

    lam_vecs = jnp.stack([lambda_q1[l], lambda_k1[l], lambda_q2[l], lambda_k2[l]]).astype(F32)
    attn = _diff_attention(qn.reshape(b, s, -1), kn.reshape(b, s, -1), v.reshape(b, s, -1), rel_bias, q_norm_g[l],
                           k_norm_g[l], lam_vecs, attn_sub_g[l].reshape(1, ATTN_V_DIM))

    y = _ssd(z.reshape(b, s, -1), xbc.reshape(b, s, -1), dt.reshape(b, s, -1), acum.reshape(b, s, -1),
             acum_t.reshape(b, s, -1), jnp.repeat(d_skip[l], SSM_HEAD_DIM).reshape(1, -1),
             ssm_norm_g[l].reshape(1, -1))

    x1, h2, pos, top_w, tile_cnt = _merge(
        x2, attn.reshape(t, -1), y.reshape(t, -1), gm, w_gate, w_attn_o[l].astype(BF16), w_ssm_o[l].astype(BF16),
        w_out[l].astype(BF16), g_ffn[l].reshape(1, d), w_router[l].T, b_router[l].reshape(-1, 1))

    n_tiles = t // min(ROW_TILE, t)
    cnt = tile_cnt.reshape(n_tiles, N_EXPERTS, LANES)[:, :, 0]
    cnt_al = _round_up(cnt, ROW_ALIGN)
    run_start = jnp.cumsum(cnt_al, axis=1) - cnt_al
    padded = _round_up(jnp.sum(cnt_al, axis=0), MOE_TILE)
    end_pad = jnp.cumsum(padded)
    run_off = (end_pad - padded)[None, :] + jnp.cumsum(cnt_al, axis=0) - cnt_al
    n_rows = _round_up(t * TOP_K + n_tiles * N_EXPERTS * (ROW_ALIGN - 1), MOE_TILE) + N_EXPERTS * MOE_TILE
    block_start = jnp.arange(n_rows // MOE_TILE, dtype=jnp.int32) * MOE_TILE
    block_e = jnp.minimum(jnp.sum(block_start[:, None] >= end_pad[None, :], axis=1), N_EXPERTS - 1).astype(jnp.int32)
    n_used = (end_pad[-1:] // MOE_TILE).astype(jnp.int32)
    last_block = jnp.concatenate([jnp.maximum(end_pad - MOE_TILE, 0), end_pad[-1:] // MOE_TILE]).astype(jnp.int32)
    tables = [a.reshape(-1).astype(jnp.int32) for a in (run_off, cnt_al, run_start)]

    xs = _dispatch(*tables, last_block, h2, pos, n_rows)
    ys = _experts(block_e, n_used, xs, w1[l], b1[l][:, None, :], w2[l], b2[l][:, None, :])
    out = _combine(*tables, ys, x1, pos.T, top_w.T)
    return out.reshape(b, s, d)
```

```python
import functools
import math

import jax
import jax.numpy as jnp
from jax import lax
from jax.experimental import pallas as pl
from jax.experimental.pallas import tpu as pltpu

F32 = jnp.float32
BF16 = jnp.bfloat16

D_MODEL = 1024
ATTN_HEADS = 8
ATTN_HEAD_DIM = 64
ATTN_V_DIM = 2 * ATTN_HEAD_DIM
LAMBDA_INIT = 0.8 - 0.6 * math.exp(-0.3 * 0)
NUM_BUCKETS = 32
MAX_DISTANCE = 128
D_INNER = 2 * D_MODEL
SSM_HEAD_DIM = 64
SSM_HEADS = D_INNER // SSM_HEAD_DIM
SSM_GROUPS = 8
SSM_HEADS_PER_GROUP = SSM_HEADS // SSM_GROUPS
D_STATE = 128
CONV_WIDTH = 4
SSM_CHUNK = 128
CONV_DIM = D_INNER + 2 * SSM_GROUPS * D_STATE
N_EXPERTS = 32
TOP_K = 4
D_EXPERT = D_MODEL
SWIGLU_LIMIT = 7.0
SWIGLU_ALPHA = 1.702
RMS_EPS = 1e-6
SSM_EPS = 1e-5
Q_COLS = ATTN_HEADS * 2 * ATTN_HEAD_DIM
K_COLS = Q_COLS
V_COLS = ATTN_HEADS * ATTN_V_DIM

LANES = 128
MXU_DIM = 256
DT_PAD = LANES
NEG_BIG = -1e30
LOG2E = math.log2(math.e)
EXP2_SAFE_BOUND = 80.0
VMEM_LIMIT = 56 * 1024 * 1024

ROW_TILE = 512
ATTN_TILE = 512
MOE_TILE = 512
ROW_ALIGN = 16
SSD_CHUNKS_PER_STEP = 4


def _rms(x, eps):
    return x * lax.rsqrt(jnp.mean(x * x, axis=-1, keepdims=True) + eps)


def _sigmoid(x):
    return 1.0 / (1.0 + jnp.exp(-x))


def _params(sem):
    return pltpu.CompilerParams(dimension_semantics=sem, vmem_limit_bytes=VMEM_LIMIT)


def _resident(shape):
    return pl.BlockSpec(shape, lambda *_: (0,) * len(shape), pipeline_mode=pl.Buffered(1))


def _qkv_kernel(x_ref, g_ref, w_ref, gq_ref, gk_ref, q_ref, k_ref, v_ref):
    h = (_rms(x_ref[...], RMS_EPS) * g_ref[...]).astype(BF16)
    qkv = jnp.dot(h, w_ref[...], preferred_element_type=F32)
    r = lax.broadcasted_iota(jnp.int32, (MXU_DIM, MXU_DIM), 0) // ATTN_HEAD_DIM
    c = lax.broadcasted_iota(jnp.int32, (MXU_DIM, MXU_DIM), 1) // ATTN_HEAD_DIM
    group_ones = jnp.where(r == c, 1.0, 0.0).astype(BF16)

    def head_norm(t, gain_ref, out_ref):
        for cc in range(Q_COLS // MXU_DIM):
            sl = slice(cc * MXU_DIM, (cc + 1) * MXU_DIM)
            tc = t[:, sl]
            ss = jnp.dot((tc * tc).astype(BF16), group_ones, preferred_element_type=F32)
            out_ref[:, sl] = (tc * lax.rsqrt(ss * (1.0 / ATTN_HEAD_DIM) + RMS_EPS) * gain_ref[:, sl]).astype(BF16)

    head_norm(qkv[:, :Q_COLS], gq_ref, q_ref)
    head_norm(qkv[:, Q_COLS:Q_COLS + K_COLS], gk_ref, k_ref)
    v_ref[...] = qkv[:, Q_COLS + K_COLS:].astype(BF16)


def _qkv_proj(x2, g_mix, w_qkv, gq, gk):
    t = x2.shape[0]
    tm = min(ROW_TILE, t)
    row = lambda i: (i, 0)
    fix = lambda i: (0, 0)
    out = jax.ShapeDtypeStruct((t, D_MODEL), BF16)
    return pl.pallas_call(
        _qkv_kernel,
        grid=(t // tm,),
        in_specs=[pl.BlockSpec((tm, D_MODEL), row), pl.BlockSpec((1, D_MODEL), fix),
                  _resident((D_MODEL, 3 * D_MODEL)), pl.BlockSpec((1, D_MODEL), fix),
                  pl.BlockSpec((1, D_MODEL), fix)],
        out_specs=[pl.BlockSpec((tm, D_MODEL), row)] * 3,
        out_shape=[out, out, out],
        compiler_params=_params(("arbitrary",)),
        name="qkv_proj",
    )(x2, g_mix, w_qkv, gq, gk)


CONV_COLS = 1024
SUBLANES = 8
CHUNK_VREGS = SSM_CHUNK // SUBLANES


def _chunk_time(row):
    return (row >> 3) + CHUNK_VREGS * (row & (SUBLANES - 1))


def _ssm_proj_kernel(x_ref, g_ref, w_ref, cw_ref, cb_ref, dtb_ref, alog_ref, z_ref, xbc_ref, dt_ref, acum_ref,
                     acumt_ref, tail_s, h_s, p_s, *, tiles_per_seq):
    tm = x_ref.shape[0]
    nb = tm // SSM_CHUNK
    taps = CONV_WIDTH - 1

    @pl.when(pl.program_id(0) % tiles_per_seq == 0)
    def _sequence_start():
        tail_s[...] = jnp.zeros(tail_s.shape, F32)

    h = (_rms(x_ref[...], RMS_EPS) * g_ref[...]).astype(BF16)
    rr = lax.broadcasted_iota(jnp.int32, (SSM_CHUNK, SSM_CHUNK), 0)
    cc = lax.broadcasted_iota(jnp.int32, (SSM_CHUNK, SSM_CHUNK), 1)
    pick = jnp.where(cc == _chunk_time(rr), 1.0, 0.0).astype(BF16)
    h = jnp.concatenate(
        [jnp.dot(pick, h[b * SSM_CHUNK:(b + 1) * SSM_CHUNK], preferred_element_type=F32).astype(BF16)
         for b in range(nb)], axis=0)

    h_s[...] = h

    def project(stage):
        return jnp.dot(h_s[...], w_ref[:, stage * CONV_COLS:(stage + 1) * CONV_COLS], preferred_element_type=F32)

    z_stages = D_INNER // CONV_COLS
    n_stages = z_stages + CONV_DIM // CONV_COLS
    sub = lax.broadcasted_iota(jnp.int32, (nb * taps, SUBLANES, CONV_COLS), 1)
    p_s[0] = project(0)
    for stage in range(n_stages):
        if stage + 1 < n_stages:
            p_s[(stage + 1) % 2] = project(stage + 1)
        p = p_s[stage % 2]
        if stage < z_stages:
            z_ref[:, stage * CONV_COLS:(stage + 1) * CONV_COLS] = (p * _sigmoid(p)).astype(BF16)
            continue
        c = stage - z_stages
        cols = slice(c * CONV_COLS, (c + 1) * CONV_COLS)
        p4 = p.reshape(nb, CHUNK_VREGS, SUBLANES, CONV_COLS)
        last = p4[:, CHUNK_VREGS - taps:]
        seq = jnp.concatenate([tail_s[c][None], last], axis=0).reshape((nb + 1) * taps, SUBLANES, CONV_COLS)
        tail_s[c] = last[nb - 1]
        rolled = pltpu.roll(seq, 1, axis=1)
        wrapped = jnp.where(sub == 0, rolled[:nb * taps], rolled[taps:]).reshape(nb, taps, SUBLANES, CONV_COLS)
        conv = cb_ref[:, cols] + cw_ref[taps:taps + 1, cols] * p4
        for back in range(1, CONV_WIDTH):
            shifted = jnp.concatenate([wrapped[:, taps - back:], p4[:, :CHUNK_VREGS - back]], axis=1)
            conv = conv + cw_ref[taps - back:taps - back + 1, cols] * shifted
        xbc_ref[:, cols] = (conv * _sigmoid(conv)).reshape(tm, CONV_COLS).astype(BF16)
    dtl = jnp.dot(h_s[...], w_ref[:, D_INNER + CONV_DIM:], preferred_element_type=F32) + dtb_ref[...]
    dt = jnp.maximum(dtl, 0.0) + jnp.log(1.0 + jnp.exp(-jnp.abs(dtl)))
    dt_ref[...] = dt
    a = dt * (-jnp.exp(alog_ref[...]))
    upto = jnp.where(_chunk_time(rr) >= _chunk_time(cc), 1.0, 0.0).astype(F32)
    for b in range(nb):
        rows = slice(b * SSM_CHUNK, (b + 1) * SSM_CHUNK)
        acum = jnp.dot(upto, a[rows], preferred_element_type=F32, precision=lax.Precision.HIGHEST)
        acum_ref[rows, :] = acum
        acumt_ref[rows, :] = acum.T


def _ssm_proj(x2, g_mix, w_ssm, conv_w, conv_b, dt_bias, a_log, seq_len):
    t = x2.shape[0]
    tm = min(ROW_TILE, seq_len)
    assert seq_len % tm == 0
    ncol = D_INNER + CONV_DIM + DT_PAD
    row = lambda i: (i, 0)
    fix = lambda i: (0, 0)
    heads = jax.ShapeDtypeStruct((t, DT_PAD), F32)
    return pl.pallas_call(
        functools.partial(_ssm_proj_kernel, tiles_per_seq=seq_len // tm),
        grid=(t // tm,),
        in_specs=[pl.BlockSpec((tm, D_MODEL), row), pl.BlockSpec((1, D_MODEL), fix),
                  _resident((D_MODEL, ncol)), pl.BlockSpec((CONV_WIDTH, CONV_DIM), fix),
                  pl.BlockSpec((1, CONV_DIM), fix), pl.BlockSpec((1, DT_PAD), fix), pl.BlockSpec((1, DT_PAD), fix)],
        out_specs=[pl.BlockSpec((tm, D_INNER), row), pl.BlockSpec((tm, CONV_DIM), row)]
        + [pl.BlockSpec((tm, DT_PAD), row)] * 3,
        out_shape=[jax.ShapeDtypeStruct((t, D_INNER), BF16), jax.ShapeDtypeStruct((t, CONV_DIM), BF16),
                   heads, heads, heads],
        scratch_shapes=[pltpu.VMEM((CONV_DIM // CONV_COLS, CONV_WIDTH - 1, SUBLANES, CONV_COLS), F32),
                        pltpu.VMEM((tm, D_MODEL), BF16), pltpu.VMEM((2, tm, CONV_COLS), F32)],
        compiler_params=_params(("arbitrary",)),
        name="ssm_proj",
    )(x2, g_mix, w_ssm, conv_w, conv_b, dt_bias, a_log)


def _split_maps(q):
    lane = lax.broadcasted_iota(jnp.int32, q.shape, 1)
    zero = jnp.zeros_like(q)
    return jnp.where(lane < ATTN_HEAD_DIM, q, zero), jnp.where(lane >= ATTN_HEAD_DIM, q, zero)


def _attn_finalize(acc1, l1, acc2, l2, lam_ref, subg_ref, o_ref):
    lam_v = lam_ref[...]
    lam = (jnp.exp(jnp.sum(lam_v[0:1] * lam_v[1:2], axis=-1, keepdims=True))
           - jnp.exp(jnp.sum(lam_v[2:3] * lam_v[3:4], axis=-1, keepdims=True)) + LAMBDA_INIT)
    o = acc1 / l1 - lam * (acc2 / l2)
    o_ref[...] = (_rms(o, RMS_EPS) * subg_ref[...] * (1.0 - LAMBDA_INIT)).astype(BF16)


def _attn_bounded_kernel(q_ref, k_ref, v_ref, bias_ref, lam_ref, subg_ref, o_ref, qq_s, vv_s, acc_s, s_s, *, nq):
    pair_id = pl.program_id(2)
    tq = qq_s.shape[1] // 2
    tk = tq
    blocks = (pair_id, nq - 1 - pair_id)

    @pl.when(pair_id == 0)
    def _extend_v():
        vv_s[:, :ATTN_V_DIM] = v_ref[...]
        vv_s[:, ATTN_V_DIM:] = jnp.ones((vv_s.shape[0], ATTN_V_DIM), BF16)

    for side in range(2):
        q1, q2 = _split_maps(q_ref[pl.ds(pl.multiple_of(blocks[side] * tq, tq), tq), :])
        qq_s[side, 0:tq, :] = q1
        qq_s[side, tq:2 * tq, :] = q2
    acc_s[...] = jnp.zeros(acc_s.shape, F32)
    contract_last = (((1,), (1,)), ((), ()))

    def tile(t):
        first = t <= pair_id
        side = jnp.where(first, 0, 1)
        j = jnp.where(first, t, t - pair_id - 1)
        diag = jnp.where(first, blocks[0], blocks[1])
        kind = jnp.where(j == diag, 0, jnp.where(j == diag - 1, 1, 2))
        return side, pl.ds(pl.multiple_of(j * tk, tk), tk), kind

    def logits(t):
        side, rows, kind = tile(t)
        s = lax.dot_general(qq_s[side], k_ref[rows, :], contract_last, preferred_element_type=F32)
        b = bias_ref[kind]
        return jnp.concatenate([s[0:tq] + b, s[tq:2 * tq] + b], axis=0)

    def accumulate(t, slot):
        side, rows, _ = tile(t)
        acc_s[side] += jnp.dot(jnp.exp2(s_s[slot]).astype(BF16), vv_s[rows, :], preferred_element_type=F32)

    s_s[0] = logits(0)
    for t in range(nq + 1):
        if t < nq:
            s_s[(t + 1) % 2] = logits(t + 1)
        accumulate(t, t % 2)

    for side in range(2):
        acc = acc_s[side]
        _attn_finalize(acc[0:tq, :ATTN_V_DIM], acc[0:tq, ATTN_V_DIM:], acc[tq:2 * tq, :ATTN_V_DIM],
                       acc[tq:2 * tq, ATTN_V_DIM:], lam_ref, subg_ref,
                       o_ref.at[pl.ds(pl.multiple_of(blocks[side] * tq, tq), tq), :])


def _attn_online_kernel(it_ref, jt_ref, q_ref, k_ref, v_ref, bias_ref, lam_ref, subg_ref, o_ref,
                        q1_s, q2_s, m1_s, m2_s, l1_s, l2_s, acc1_s, acc2_s):
    step = pl.program_id(2)
    i = it_ref[step]
    j = jt_ref[step]

    @pl.when(j == 0)
    def _init():
        q1_s[...], q2_s[...] = _split_maps(q_ref[...])
        for m_s, l_s, acc_s in ((m1_s, l1_s, acc1_s), (m2_s, l2_s, acc2_s)):
            m_s[...] = jnp.full(m_s.shape, NEG_BIG, F32)
            l_s[...] = jnp.zeros(l_s.shape, F32)
            acc_s[...] = jnp.zeros(acc_s.shape, F32)

    def update(bias):
        k = k_ref[...]
        v = v_ref[...]
        contract_last = (((1,), (1,)), ((), ()))
        for q_s, m_s, l_s, acc_s in ((q1_s, m1_s, l1_s, acc1_s), (q2_s, m2_s, l2_s, acc2_s)):
            s = lax.dot_general(q_s[...], k, contract_last, preferred_element_type=F32)
            if bias is not None:
                s = s + bias_ref[bias]
            m_old = m_s[...]
            m_new = jnp.maximum(m_old, jnp.max(s, axis=-1, keepdims=True))
            alpha = jnp.exp2(m_old - m_new)
            p = jnp.exp2(s - m_new)
            l_s[...] = alpha * l_s[...] + jnp.sum(p, axis=-1, keepdims=True)
            acc_s[...] = alpha * acc_s[...] + jnp.dot(p.astype(BF16), v, preferred_element_type=F32)
            m_s[...] = m_new

    @pl.when(j == i)
    def _diag():
        update(0)

    @pl.when(j == i - 1)
    def _prev():
        update(1)

    @pl.when(j < i - 1)
    def _far():
        update(None)

    @pl.when(j == i)
    def _finalize():
        _attn_finalize(acc1_s[...], l1_s[...], acc2_s[...], l2_s[...], lam_ref, subg_ref, o_ref)


def _t5_bucket(dist):
    n = jnp.maximum(dist, 0)
    max_exact = NUM_BUCKETS // 2
    scaled = jnp.log(jnp.maximum(n, 1).astype(F32) / max_exact) / math.log(MAX_DISTANCE / max_exact)
    large = max_exact + (scaled * (NUM_BUCKETS - max_exact)).astype(jnp.int32)
    large = jnp.minimum(large, NUM_BUCKETS - 1)
    return jnp.where(n < max_exact, n, large)


def _bias_tiles(rel_bias, tile):
    blk = MAX_DISTANCE
    assert tile % blk == 0
    nb = tile // blk
    table = (rel_bias - rel_bias[NUM_BUCKETS - 1]).astype(F32) * LOG2E
    r = jnp.arange(blk, dtype=jnp.int32)
    d0 = r[:, None] - r[None, :]

    def lookup(dist):
        onehot = (_t5_bucket(dist)[..., None] == jnp.arange(NUM_BUCKETS, dtype=jnp.int32)).astype(F32)
        return jnp.einsum('qkn,nh->hqk', onehot, table, precision=lax.Precision.HIGHEST)

    on_diag = jnp.where(d0[None] >= 0, lookup(d0), NEG_BIG)
    sub_diag = lookup(d0 + blk)
    zeros = jnp.zeros_like(sub_diag)
    masked = jnp.full_like(sub_diag, NEG_BIG)

    def assemble(pick):
        return jnp.concatenate(
            [jnp.concatenate([pick(bi, bj) for bj in range(nb)], axis=-1) for bi in range(nb)], axis=-2)

    diag_tile = assemble(lambda bi, bj: on_diag if bi == bj else sub_diag if bi == bj + 1 else zeros if bi > bj else masked)
    prev_tile = assemble(lambda bi, bj: sub_diag if (bi == 0 and bj == nb - 1) else zeros)
    return jnp.stack([diag_tile, prev_tile, jnp.zeros_like(prev_tile)], axis=1)


def _attn_bounded(qn, kn, v, bias, lam_vecs, sub_g):
    b, s, _ = qn.shape
    tile = min(ATTN_TILE, s)
    nq = s // tile
    assert nq % 2 == 0
    seq_map = lambda h, bb, i: (bb, 0, h)
    return pl.pallas_call(
        functools.partial(_attn_bounded_kernel, nq=nq),
        grid=(ATTN_HEADS, b, nq // 2),
        in_specs=[pl.BlockSpec((None, s, ATTN_V_DIM), seq_map),
                  pl.BlockSpec((None, s, ATTN_V_DIM), seq_map),
                  pl.BlockSpec((None, s, ATTN_V_DIM), seq_map),
                  pl.BlockSpec((None, 3, tile, tile), lambda h, bb, i: (h, 0, 0, 0)),
                  pl.BlockSpec((4, ATTN_HEAD_DIM), lambda h, bb, i: (0, 0)),
                  pl.BlockSpec((1, ATTN_V_DIM), lambda h, bb, i: (0, 0))],
        out_specs=pl.BlockSpec((None, s, ATTN_V_DIM), seq_map),
        out_shape=jax.ShapeDtypeStruct((b, s, V_COLS), BF16),
        scratch_shapes=[pltpu.VMEM((2, 2 * tile, ATTN_V_DIM), BF16), pltpu.VMEM((s, 2 * ATTN_V_DIM), BF16),
                        pltpu.VMEM((2, 2 * tile, 2 * ATTN_V_DIM), F32), pltpu.VMEM((2, 2 * tile, tile), F32)],
        compiler_params=_params(("arbitrary", "arbitrary", "arbitrary")),
        name="diff_attn_bounded",
    )(qn, kn, v, bias, lam_vecs, sub_g)


def _attn_online(qn, kn, v, bias, lam_vecs, sub_g):
    b, s, _ = qn.shape
    tile = min(ATTN_TILE, s)
    nq = s // tile
    it = jnp.asarray([i for i in range(nq) for _ in range(i + 1)], jnp.int32)
    jt = jnp.asarray([j for i in range(nq) for j in range(i + 1)], jnp.int32)
    q_map = lambda bb, h, st, it_r, jt_r: (bb, it_r[st], h)
    kv_map = lambda bb, h, st, it_r, jt_r: (bb, jt_r[st], h)
    grid_spec = pltpu.PrefetchScalarGridSpec(
        num_scalar_prefetch=2,
        grid=(b, ATTN_HEADS, int(it.shape[0])),
        in_specs=[pl.BlockSpec((None, tile, ATTN_V_DIM), q_map),
                  pl.BlockSpec((None, tile, ATTN_V_DIM), kv_map),
                  pl.BlockSpec((None, tile, ATTN_V_DIM), kv_map),
                  pl.BlockSpec((None, 3, tile, tile), lambda bb, h, st, it_r, jt_r: (h, 0, 0, 0)),
                  pl.BlockSpec((4, ATTN_HEAD_DIM), lambda bb, h, st, it_r, jt_r: (0, 0)),
                  pl.BlockSpec((1, ATTN_V_DIM), lambda bb, h, st, it_r, jt_r: (0, 0))],
        out_specs=pl.BlockSpec((None, tile, ATTN_V_DIM), q_map),
        scratch_shapes=[pltpu.VMEM((tile, ATTN_V_DIM), BF16), pltpu.VMEM((tile, ATTN_V_DIM), BF16),
                        pltpu.VMEM((tile, 1), F32), pltpu.VMEM((tile, 1), F32),
                        pltpu.VMEM((tile, 1), F32), pltpu.VMEM((tile, 1), F32),
                        pltpu.VMEM((tile, ATTN_V_DIM), F32), pltpu.VMEM((tile, ATTN_V_DIM), F32)],
    )
    return pl.pallas_call(
        _attn_online_kernel,
        grid_spec=grid_spec,
        out_shape=jax.ShapeDtypeStruct((b, s, V_COLS), BF16),
        compiler_params=_params(("arbitrary", "arbitrary", "arbitrary")),
        name="diff_attn_online",
    )(it, jt, qn, kn, v, bias, lam_vecs, sub_g)


def _diff_attention(qn, kn, v, rel_bias, q_gain, k_gain, lam_vecs, sub_g):
    tile = min(ATTN_TILE, qn.shape[1])
    bias = _bias_tiles(rel_bias, tile)
    spread = jnp.max(jnp.abs(rel_bias - rel_bias[NUM_BUCKETS - 1]))
    bound = LOG2E * (1.05 * math.sqrt(ATTN_HEAD_DIM) * jnp.max(jnp.abs(q_gain)) * jnp.max(jnp.abs(k_gain)) + spread)
    args = (qn, kn, v, bias, lam_vecs, sub_g)
    return lax.cond(bound < EXP2_SAFE_BOUND, lambda a: _attn_bounded(*a), lambda a: _attn_online(*a), args)


def _ssd_kernel(z_ref, xbc_ref, dt_ref, acum_ref, acumt_ref, dskip_ref, ng_ref, y_ref, state_s):
    L = SSM_CHUNK

    @pl.when(pl.program_id(1) == 0)
    def _reset():
        state_s[...] = jnp.zeros(state_s.shape, F32)

    row = lax.broadcasted_iota(jnp.int32, (L, L), 0)
    col = lax.broadcasted_iota(jnp.int32, (L, L), 1)
    causal = _chunk_time(row) >= _chunk_time(col)
    unpick = jnp.where(row == _chunk_time(col), 1.0, 0.0).astype(BF16)
    lane = lax.broadcasted_iota(jnp.int32, (L, LANES), 1)
    low = lane < SSM_HEAD_DIM
    for c in range(z_ref.shape[0] // L):
        _ssd_chunk(slice(c * L, (c + 1) * L), z_ref, xbc_ref, dt_ref, acum_ref, acumt_ref, dskip_ref, ng_ref, y_ref,
                   state_s, causal, unpick, low)


def _ssd_chunk(rows, z_ref, xbc_ref, dt_ref, acum_ref, acumt_ref, dskip_ref, ng_ref, y_ref, state_s, causal, unpick,
               low):
    L = SSM_CHUNK
    xs = xbc_ref[rows, :D_INNER].astype(F32)
    bm = xbc_ref[rows, D_INNER:D_INNER + SSM_GROUPS * D_STATE]
    cm = xbc_ref[rows, D_INNER + SSM_GROUPS * D_STATE:]
    dt = dt_ref[rows, :]
    acum = acum_ref[rows, :]
    acum_t = acumt_ref[rows, :]
    a_end = acum[L - 1:L, :]

    def pair(lo, hi):
        return jnp.where(low[:lo.shape[0]], lo, hi)

    def spread(arr, h):
        return jnp.broadcast_to(arr[:, h:h + 1], (arr.shape[0], LANES))

    contract_last = (((1,), (1,)), ((), ()))
    contract_first = (((0,), (0,)), ((), ()))
    y_parts = []
    for g in range(SSM_GROUPS):
        bg = bm[:, g * D_STATE:(g + 1) * D_STATE]
        cg = cm[:, g * D_STATE:(g + 1) * D_STATE]
        cb = lax.dot_general(cg, bg, contract_last, preferred_element_type=F32)
        st = state_s[g]
        y_off = jnp.dot(cg, st.astype(BF16), preferred_element_type=F32)
        xw_parts = []
        dec_parts = []
        for pr in range(SSM_HEADS_PER_GROUP // 2):
            h0 = g * SSM_HEADS_PER_GROUP + 2 * pr
            ch = slice(h0 * SSM_HEAD_DIM, (h0 + 2) * SSM_HEAD_DIM)
            x_pair = xs[:, ch]
            acols = (spread(acum, h0), spread(acum, h0 + 1))
            acol = pair(*acols)
            aend = pair(spread(a_end, h0), spread(a_end, h0 + 1))
            xdt32 = x_pair * pair(spread(dt, h0), spread(dt, h0 + 1))
            xdt = xdt32.astype(BF16)
            yd = []
            for hh, ac in zip((h0, h0 + 1), acols):
                seg = ac - acum_t[hh:hh + 1, :]
                decay = jnp.exp(jnp.where(causal, seg, NEG_BIG))
                yd.append(jnp.dot((cb * decay).astype(BF16), xdt, preferred_element_type=F32))
            y_diag = pair(yd[0], yd[1])
            off = y_off[:, 2 * pr * SSM_HEAD_DIM:(2 * pr + 2) * SSM_HEAD_DIM]
            y_parts.append(y_diag + off * jnp.exp(acol))
            xw_parts.append(xdt32 * jnp.exp(aend - acol))
            dec_parts.append(jnp.exp(aend))
        xw = jnp.concatenate(xw_parts, axis=-1).astype(BF16)
        dec = jnp.concatenate(dec_parts, axis=-1)
        state_s[g] = st * dec + lax.dot_general(bg, xw, contract_first, preferred_element_type=F32)
    y = (jnp.concatenate(y_parts, axis=-1) + dskip_ref[...] * xs) * z_ref[rows, :].astype(F32)
    gsz = D_INNER // SSM_GROUPS
    for g in range(SSM_GROUPS):
        sl = slice(g * gsz, (g + 1) * gsz)
        yn = (_rms(y[:, sl], SSM_EPS) * ng_ref[:, sl]).astype(BF16)
        y_ref[rows, sl] = jnp.dot(unpick, yn, preferred_element_type=F32).astype(BF16)


def _ssd(z, xbc, dt, acum, acum_t, d_skip_ch, norm_g):
    b, s, _ = z.shape
    rows = SSD_CHUNKS_PER_STEP * SSM_CHUNK
    assert s % rows == 0
    blk = lambda bb, c: (bb, c, 0)
    fix = lambda bb, c: (0, 0)
    return pl.pallas_call(
        _ssd_kernel,
        grid=(b, s // rows),
        in_specs=[pl.BlockSpec((None, rows, D_INNER), blk), pl.BlockSpec((None, rows, CONV_DIM), blk)]
        + [pl.BlockSpec((None, rows, DT_PAD), blk)] * 3
        + [pl.BlockSpec((1, D_INNER), fix), pl.BlockSpec((1, D_INNER), fix)],
        out_specs=pl.BlockSpec((None, rows, D_INNER), blk),
        out_shape=jax.ShapeDtypeStruct((b, s, D_INNER), BF16),
        scratch_shapes=[pltpu.VMEM((SSM_GROUPS, D_STATE, SSM_HEADS_PER_GROUP * SSM_HEAD_DIM), F32)],
        compiler_params=_params(("arbitrary", "arbitrary")),
        name="ssd",
    )(z, xbc, dt, acum, acum_t, d_skip_ch, norm_g)


def _round_up(n, m):
    return (n + m - 1) // m * m


def _merge_kernel(x_ref, attn_ref, y_ref, gmix_ref, wg_ref, wao_ref, wso_ref, wout_ref, gffn_ref, wr_ref, br_ref,
                  x1_ref, h2_ref, pos_ref, topw_ref, cnt_ref):
    x = x_ref[...]
    tm = x.shape[0]
    h = (_rms(x, RMS_EPS) * gmix_ref[...]).astype(BF16)
    gates = _sigmoid(jnp.dot(h, wg_ref[...], preferred_element_type=F32))
    attn_out = jnp.dot(attn_ref[...], wao_ref[...], preferred_element_type=F32)
    ssm_out = jnp.dot(y_ref[...], wso_ref[...], preferred_element_type=F32)
    merged = gates[:, :D_MODEL] * attn_out + gates[:, D_MODEL:] * ssm_out
    x1 = x + jnp.dot(merged.astype(BF16), wout_ref[...], preferred_element_type=F32)
    x1_ref[...] = x1
    h2 = _rms(x1, RMS_EPS) * gffn_ref[...]
    h2_ref[...] = h2.astype(BF16)

    logits = lax.dot_general(wr_ref[...], h2, (((1,), (1,)), ((), ())), preferred_element_type=F32,
                             precision=lax.Precision.HIGHEST) + br_ref[...]
    eid = lax.broadcasted_iota(jnp.int32, logits.shape, 0)
    vals, hits = [], []
    member = jnp.zeros(logits.shape, F32)
    work = logits
    for kk in range(TOP_K):
        m = jnp.max(work, axis=0, keepdims=True)
        idx = jnp.min(jnp.where(work == m, eid, N_EXPERTS), axis=0, keepdims=True)
        hit = eid == idx
        vals.append(m)
        hits.append(hit)
        member = jnp.where(hit, 1.0, member)
        work = jnp.where(hit, -jnp.inf, work)
    ex = [jnp.exp(v - vals[0]) for v in vals]
    denom = ex[0] + ex[1] + ex[2] + ex[3]
    for kk in range(TOP_K):
        topw_ref[kk:kk + 1, :] = ex[kk] / denom

    r = lax.broadcasted_iota(jnp.int32, (tm, tm), 0)
    c = lax.broadcasted_iota(jnp.int32, (tm, tm), 1)
    before = jnp.where(r < c, 1.0, 0.0).astype(BF16)
    prefix = jnp.dot(member.astype(BF16), before, preferred_element_type=F32)
    cnt = jnp.sum(member, axis=1, keepdims=True).astype(jnp.int32)
    cnt_al = jnp.bitwise_and(cnt + (ROW_ALIGN - 1), -ROW_ALIGN)
    cnt_al = jnp.broadcast_to(cnt_al, (N_EXPERTS, LANES)).astype(F32)
    er = lax.broadcasted_iota(jnp.int32, (N_EXPERTS, N_EXPERTS), 0)
    ec = lax.broadcasted_iota(jnp.int32, (N_EXPERTS, N_EXPERTS), 1)
    lower = jnp.where(ec < er, 1.0, 0.0).astype(F32)
    run_start = jnp.dot(lower, cnt_al, preferred_element_type=F32, precision=lax.Precision.HIGHEST)[:, 0:1]
    base = prefix + run_start
    for kk in range(TOP_K):
        pos_ref[kk:kk + 1, :] = jnp.sum(jnp.where(hits[kk], base, 0.0), axis=0, keepdims=True).astype(jnp.int32)
    cnt_ref[...] = jnp.broadcast_to(cnt, cnt_ref.shape)


def _merge(x2, attn, y, g_mix, w_gate, w_ao, w_so, w_out, g_ffn, w_r_t, b_r):
    t = x2.shape[0]
    tm = min(ROW_TILE, t)
    row = lambda i: (i, 0)
    colb = lambda i: (0, i)
    fix = lambda i: (0, 0)
    return pl.pallas_call(
        _merge_kernel,
        grid=(t // tm,),
        in_specs=[pl.BlockSpec((tm, D_MODEL), row), pl.BlockSpec((tm, V_COLS), row), pl.BlockSpec((tm, D_INNER), row),
                  pl.BlockSpec((1, D_MODEL), fix), _resident((D_MODEL, 2 * D_MODEL)),
                  _resident((V_COLS, D_MODEL)), _resident((D_INNER, D_MODEL)),
                  _resident((D_MODEL, D_MODEL)), pl.BlockSpec((1, D_MODEL), fix),
                  pl.BlockSpec((N_EXPERTS, D_MODEL), fix), pl.BlockSpec((N_EXPERTS, 1), fix)],
        out_specs=[pl.BlockSpec((tm, D_MODEL), row), pl.BlockSpec((tm, D_MODEL), row),
                   pl.BlockSpec((TOP_K, tm), colb), pl.BlockSpec((TOP_K, tm), colb),
                   pl.BlockSpec((N_EXPERTS, LANES), row)],
        out_shape=[jax.ShapeDtypeStruct((t, D_MODEL), F32), jax.ShapeDtypeStruct((t, D_MODEL), BF16),
                   jax.ShapeDtypeStruct((TOP_K, t), jnp.int32), jax.ShapeDtypeStruct((TOP_K, t), F32),
                   jax.ShapeDtypeStruct((t // tm * N_EXPERTS, LANES), jnp.int32)],
        compiler_params=_params(("arbitrary",)),
        name="merge_router",
    )(x2, attn, y, g_mix, w_gate, w_ao, w_so, w_out, g_ffn, w_r_t, b_r)


def _sorted_rows(tm):
    return _round_up(TOP_K * tm + N_EXPERTS * (ROW_ALIGN - 1), LANES)


def _run_copies(n, max_n, vmem_ref, vmem_off, hbm_ref, hbm_off, sem, to_hbm, wait):
    done = 0
    pieces = [ROW_ALIGN << p for p in range((max_n // ROW_ALIGN).bit_length())]
    for bit in reversed(pieces):
        take = (n & bit) != 0

        @pl.when(take)
        def _piece(bit=bit, done=done):
            v = vmem_ref.at[pl.ds(pl.multiple_of(vmem_off + done, ROW_ALIGN), bit)]
            h = hbm_ref.at[pl.ds(pl.multiple_of(hbm_off + done, ROW_ALIGN), bit)]
            cp = pltpu.make_async_copy(v, h, sem) if to_hbm else pltpu.make_async_copy(h, v, sem)
            cp.wait() if wait else cp.start()

        done = done + jnp.where(take, bit, 0)


def _dispatch_kernel(off_ref, cnt_ref, start_ref, last_ref, h2_ref, pos_ref, xs_hbm, buf_s, zero_s, sems, zsem):
    i = pl.program_id(0)
    tm = h2_ref.shape[0]
    rows = buf_s.shape[1]
    slot = i & 1

    @pl.when(i == 0)
    def _zero_last_blocks():
        zero_s[...] = jnp.zeros(zero_s.shape, BF16)
        for e in range(N_EXPERTS):
            start = pl.multiple_of(last_ref[e], MOE_TILE)
            pltpu.make_async_copy(zero_s, xs_hbm.at[pl.ds(start, MOE_TILE)], zsem).start()
        for e in range(N_EXPERTS):
            pltpu.make_async_copy(zero_s, xs_hbm.at[pl.ds(0, MOE_TILE)], zsem).wait()

        def slack(wait):
            def body(blk, carry):
                cp = pltpu.make_async_copy(zero_s, xs_hbm.at[pl.ds(pl.multiple_of(blk * MOE_TILE, MOE_TILE), MOE_TILE)],
                                           zsem)
                cp.wait() if wait else cp.start()
                return carry
            lax.fori_loop(last_ref[N_EXPERTS], xs_hbm.shape[0] // MOE_TILE, body, 0)

        slack(False)
        slack(True)

    pos = pos_ref[...]
    rid = lax.broadcasted_iota(jnp.int32, (rows, tm), 0)
    sel = jnp.zeros((rows, tm), F32)
    for kk in range(TOP_K):
        sel = sel + jnp.where(rid == pos[kk:kk + 1, :], 1.0, 0.0)
    buf_s[slot] = jnp.dot(sel.astype(BF16), h2_ref[...], preferred_element_type=F32).astype(BF16)

    def copies(tile, tile_slot, wait):
        for e in range(N_EXPERTS):
            idx = tile * N_EXPERTS + e
            _run_copies(cnt_ref[idx], tm, buf_s.at[tile_slot], start_ref[idx], xs_hbm, off_ref[idx],
                        sems.at[tile_slot], True, wait)

    copies(i, slot, False)

    @pl.when(i >= 1)
    def _previous_tile_done():
        copies(i - 1, 1 - slot, True)

    @pl.when(i == pl.num_programs(0) - 1)
    def _last_tile_done():
        copies(i, slot, True)


def _dispatch(run_off, run_cnt, run_start, last_block, h2, pos, n_rows):
    t = h2.shape[0]
    tm = min(ROW_TILE, t)
    grid_spec = pltpu.PrefetchScalarGridSpec(
        num_scalar_prefetch=4,
        grid=(t // tm,),
        in_specs=[pl.BlockSpec((tm, D_MODEL), lambda i, *_: (i, 0)),
                  pl.BlockSpec((TOP_K, tm), lambda i, *_: (0, i))],
        out_specs=pl.BlockSpec(memory_space=pl.ANY),
        scratch_shapes=[pltpu.VMEM((2, _sorted_rows(tm), D_MODEL), BF16), pltpu.VMEM((MOE_TILE, D_MODEL), BF16),
                        pltpu.SemaphoreType.DMA((2,)), pltpu.SemaphoreType.DMA(())],
    )
    return pl.pallas_call(
        _dispatch_kernel,
        grid_spec=grid_spec,
        out_shape=jax.ShapeDtypeStruct((n_rows, D_MODEL), BF16),
        compiler_params=_params(("arbitrary",)),
        name="moe_dispatch",
    )(run_off, run_cnt, run_start, last_block, h2, pos)


def _expert_kernel(be_ref, nb_ref, x_ref, w1_ref, b1_ref, w2_ref, b2_ref, y_ref, w1_s, w2_s):
    i = pl.program_id(0)

    @pl.when(jnp.logical_or(i == 0, be_ref[i] != be_ref[jnp.maximum(i - 1, 0)]))
    def _new_expert():
        w1_s[...] = w1_ref[...].astype(BF16)
        w2_s[...] = w2_ref[...].astype(BF16)

    @pl.when(i < nb_ref[0])
    def _run():
        gu = jnp.dot(x_ref[...], w1_s[...], preferred_element_type=F32) + b1_ref[...]
        gate = jnp.minimum(gu[:, :D_EXPERT], SWIGLU_LIMIT)
        up = jnp.clip(gu[:, D_EXPERT:], -SWIGLU_LIMIT, SWIGLU_LIMIT)
        act = (up + 1.0) * (gate * _sigmoid(SWIGLU_ALPHA * gate))
        y = jnp.dot(act.astype(BF16), w2_s[...], preferred_element_type=F32) + b2_ref[...]
        y_ref[...] = y.astype(BF16)

    @pl.when(i >= nb_ref[0])
    def _slack():
        y_ref[...] = jnp.zeros(y_ref.shape, BF16)


def _experts(block_e, n_used, xs, w1, b1, w2, b2):
    n_rows = xs.shape[0]
    nb = n_rows // MOE_TILE
    row_in = lambda i, be, nu: (jnp.minimum(i, nu[0] - 1), 0)
    row = lambda i, be, nu: (i, 0)
    wsel = lambda i, be, nu: (be[i], 0, 0)
    grid_spec = pltpu.PrefetchScalarGridSpec(
        num_scalar_prefetch=2,
        grid=(nb,),
        in_specs=[pl.BlockSpec((MOE_TILE, D_MODEL), row_in),
                  pl.BlockSpec((None, D_MODEL, 2 * D_EXPERT), wsel), pl.BlockSpec((None, 1, 2 * D_EXPERT), wsel),
                  pl.BlockSpec((None, D_EXPERT, D_MODEL), wsel), pl.BlockSpec((None, 1, D_MODEL), wsel)],
        out_specs=pl.BlockSpec((MOE_TILE, D_MODEL), row),
        scratch_shapes=[pltpu.VMEM((D_MODEL, 2 * D_EXPERT), BF16), pltpu.VMEM((D_EXPERT, D_MODEL), BF16)],
    )
    return pl.pallas_call(
        _expert_kernel,
        grid_spec=grid_spec,
        out_shape=jax.ShapeDtypeStruct((n_rows, D_MODEL), BF16),
        compiler_params=_params(("arbitrary",)),
        name="moe_experts",
    )(block_e, n_used, xs, w1, b1, w2, b2)


def _combine_kernel(off_ref, cnt_ref, start_ref, y_hbm, x1_ref, pos_ref, w_ref, o_ref, buf_s, sems):
    i = pl.program_id(0)
    tm = x1_ref.shape[0]
    rows = buf_s.shape[1]
    slot = i & 1

    def copies(tile, tile_slot, wait):
        for e in range(N_EXPERTS):
            idx = tile * N_EXPERTS + e
            _run_copies(cnt_ref[idx], tm, buf_s.at[tile_slot], start_ref[idx], y_hbm, off_ref[idx],
                        sems.at[tile_slot], False, wait)

    @pl.when(i == 0)
    def _first_tile():
        buf_s[...] = jnp.zeros(buf_s.shape, BF16)
        copies(0, 0, False)

    @pl.when(i + 1 < pl.num_programs(0))
    def _prefetch_next_tile():
        copies(i + 1, 1 - slot, False)

    pos = pos_ref[...]
    w = w_ref[...]
    cid = lax.broadcasted_iota(jnp.int32, (tm, rows), 1)
    wsel = jnp.zeros((tm, rows), F32)
    for kk in range(TOP_K):
        wsel = wsel + jnp.where(cid == pos[:, kk:kk + 1], w[:, kk:kk + 1], 0.0)
    copies(i, slot, True)
    o_ref[...] = x1_ref[...] + jnp.dot(wsel.astype(BF16), buf_s[slot], preferred_element_type=F32)


def _combine(run_off, run_cnt, run_start, y, x1, pos_tok, w_tok):
    t = x1.shape[0]
    tm = min(ROW_TILE, t)
    grid_spec = pltpu.PrefetchScalarGridSpec(
        num_scalar_prefetch=3,
        grid=(t // tm,),
        in_specs=[pl.BlockSpec(memory_space=pl.ANY),
                  pl.BlockSpec((tm, D_MODEL), lambda i, *_: (i, 0)),
                  pl.BlockSpec((tm, TOP_K), lambda i, *_: (i, 0)), pl.BlockSpec((tm, TOP_K), lambda i, *_: (i, 0))],
        out_specs=pl.BlockSpec((tm, D_MODEL), lambda i, *_: (i, 0)),
        scratch_shapes=[pltpu.VMEM((2, _sorted_rows(tm), D_MODEL), BF16), pltpu.SemaphoreType.DMA((2,))],
    )
    return pl.pallas_call(
        _combine_kernel,
        grid_spec=grid_spec,
        out_shape=jax.ShapeDtypeStruct((t, D_MODEL), F32),
        compiler_params=_params(("arbitrary",)),
        name="moe_combine",
    )(run_off, run_cnt, run_start, y, x1, pos_tok, w_tok)


def kernel(x, g_mix, w_in, q_norm_g, k_norm_g, lambda_q1, lambda_k1, lambda_q2, lambda_k2, attn_sub_g, rel_bias,
           w_attn_o, conv_w, conv_b, dt_bias, a_log, d_skip, ssm_norm_g, w_ssm_o, w_out, g_ffn, w_router, b_router,
           w1, b1, w2, b2):
    b, s, d = x.shape
    t = b * s
    l = 0
    x2 = x.reshape(t, d)

    w = w_in[l]
    c0 = Q_COLS + K_COLS + V_COLS
    c1 = c0 + D_INNER + CONV_DIM
    w_qkv = w[:, :c0].astype(BF16)
    w_dt = jnp.pad(w[:, c1:c1 + SSM_HEADS], ((0, 0), (0, DT_PAD - SSM_HEADS)))
    w_ssm = jnp.concatenate([w[:, c0:c1], w_dt], axis=1).astype(BF16)
    w_gate = w[:, c1 + SSM_HEADS:].astype(BF16)
    n_hd = Q_COLS // ATTN_HEAD_DIM
    gq = (jnp.tile(q_norm_g[l], n_hd) * (ATTN_HEAD_DIM ** -0.5 * LOG2E)).reshape(1, Q_COLS)
    gk = jnp.tile(k_norm_g[l], n_hd).reshape(1, K_COLS)
    gm = g_mix[l].reshape(1, d)

    qn, kn, v = _qkv_proj(x2, gm, w_qkv, gq, gk)
    pad_h = (0, DT_PAD - SSM_HEADS)
    z, xbc, dt, acum, acum_t = _ssm_proj(x2, gm, w_ssm, conv_w[l], conv_b[l].reshape(1, -1),
                                         jnp.pad(dt_bias[l], pad_h).reshape(1, -1),
                                         jnp.pad(a_log[l], pad_h).reshape(1, -1), s)

    lam_vecs = jnp.stack([lambda_q1[l], lambda_k1[l], lambda_q2[l], lambda_k2[l]]).astype(F32)
    attn = _diff_attention(qn.reshape(b, s, -1), kn.reshape(b, s, -1), v.reshape(b, s, -1), rel_bias, q_norm_g[l],
                           k_norm_g[l], lam_vecs, attn_sub_g[l].reshape(1, ATTN_V_DIM))

    y = _ssd(z.reshape(b, s, -1), xbc.reshape(b, s, -1), dt.reshape(b, s, -1), acum.reshape(b, s, -1),
             acum_t.reshape(b, s, -1), jnp.repeat(d_skip[l], SSM_HEAD_DIM).reshape(1, -1),
             ssm_norm_g[l].reshape(1, -1))

    x1, h2, pos, top_w, tile_cnt = _merge(
        x2, attn.reshape(t, -1), y.reshape(t, -1), gm, w_gate, w_attn_o[l].astype(BF16), w_ssm_o[l].astype(BF16),
        w_out[l].astype(BF16), g_ffn[l].reshape(1, d), w_router[l].T, b_router[l].reshape(-1, 1))

    n_tiles = t // min(ROW_TILE, t)
    cnt = tile_cnt.reshape(n_tiles, N_EXPERTS, LANES)[:, :, 0]
    cnt_al = _round_up(cnt, ROW_ALIGN)
    run_start = jnp.cumsum(cnt_al, axis=1) - cnt_al
    padded = _round_up(jnp.sum(cnt_al, axis=0), MOE_TILE)
    end_pad = jnp.cumsum(padded)
    run_off = (end_pad - padded)[None, :] + jnp.cumsum(cnt_al, axis=0) - cnt_al
    n_rows = _round_up(t * TOP_K + n_tiles * N_EXPERTS * (ROW_ALIGN - 1), MOE_TILE) + N_EXPERTS * MOE_TILE
    block_start = jnp.arange(n_rows // MOE_TILE, dtype=jnp.int32) * MOE_TILE
    block_e = jnp.minimum(jnp.sum(block_start[:, None] >= end_pad[None, :], axis=1), N_EXPERTS - 1).astype(jnp.int32)
    n_used = (end_pad[-1:] // MOE_TILE).astype(jnp.int32)
    last_block = jnp.concatenate([jnp.maximum(end_pad - MOE_TILE, 0), end_pad[-1:] // MOE_TILE]).astype(jnp.int32)
    tables = [a.reshape(-1).astype(jnp.int32) for a in (run_off, cnt_al, run_start)]

    xs = _dispatch(*tables, last_block, h2, pos, n_rows)
    ys = _experts(block_e, n_used, xs, w1[l], b1[l][:, None, :], w2[l], b2[l][:, None, :])
    out = _combine(*tables, ys, x1, pos.T, top_w.T)
    return out.reshape(b, s, d)
```

```python
import functools
import math

import jax
import jax.numpy as jnp
from jax import lax
from jax.experimental import pallas as pl
from jax.experimental.pallas import tpu as pltpu

F32 = jnp.float32
BF16 = jnp.bfloat16

D_MODEL = 1024
ATTN_HEADS = 8
ATTN_HEAD_DIM = 64
ATTN_V_DIM = 2 * ATTN_HEAD_DIM
LAMBDA_INIT = 0.8 - 0.6 * math.exp(-0.3 * 0)
NUM_BUCKETS = 32
MAX_DISTANCE = 128
D_INNER = 2 * D_MODEL
SSM_HEAD_DIM = 64
SSM_HEADS = D_INNER // SSM_HEAD_DIM
SSM_GROUPS = 8
SSM_HEADS_PER_GROUP = SSM_HEADS // SSM_GROUPS
D_STATE = 128
CONV_WIDTH = 4
SSM_CHUNK = 128
CONV_DIM = D_INNER + 2 * SSM_GROUPS * D_STATE
N_EXPERTS = 32
TOP_K = 4
D_EXPERT = D_MODEL
SWIGLU_LIMIT = 7.0
SWIGLU_ALPHA = 1.702
RMS_EPS = 1e-6
SSM_EPS = 1e-5
Q_COLS = ATTN_HEADS * 2 * ATTN_HEAD_DIM
K_COLS = Q_COLS
V_COLS = ATTN_HEADS * ATTN_V_DIM

LANES = 128
MXU_DIM = 256
DT_PAD = LANES
NEG_BIG = -1e30
LOG2E = math.log2(math.e)
EXP2_SAFE_BOUND = 80.0
VMEM_LIMIT = 56 * 1024 * 1024

ROW_TILE = 512
ATTN_TILE = 512
MOE_TILE = 512
ROW_ALIGN = 16
SSD_CHUNKS_PER_STEP = 4


def _rms(x, eps):
    return x * lax.rsqrt(jnp.mean(x * x, axis=-1, keepdims=True) + eps)


def _sigmoid(x):
    return 1.0 / (1.0 + jnp.exp(-x))


def _params(sem):
    return pltpu.CompilerParams(dimension_semantics=sem, vmem_limit_bytes=VMEM_LIMIT)


def _resident(shape):
    return pl.BlockSpec(shape, lambda *_: (0,) * len(shape), pipeline_mode=pl.Buffered(1))


def _qkv_kernel(x_ref, g_ref, w_ref, gq_ref, gk_ref, q_ref, k_ref, v_ref):
    h = (_rms(x_ref[...], RMS_EPS) * g_ref[...]).astype(BF16)
    qkv = jnp.dot(h, w_ref[...], preferred_element_type=F32)
    r = lax.broadcasted_iota(jnp.int32, (MXU_DIM, MXU_DIM), 0) // ATTN_HEAD_DIM
    c = lax.broadcasted_iota(jnp.int32, (MXU_DIM, MXU_DIM), 1) // ATTN_HEAD_DIM
    group_ones = jnp.where(r == c, 1.0, 0.0).astype(BF16)

    def head_norm(t, gain_ref, out_ref):
        for cc in range(Q_COLS // MXU_DIM):
            sl = slice(cc * MXU_DIM, (cc + 1) * MXU_DIM)
            tc = t[:, sl]
            ss = jnp.dot((tc * tc).astype(BF16), group_ones, preferred_element_type=F32)
            out_ref[:, sl] = (tc * lax.rsqrt(ss * (1.0 / ATTN_HEAD_DIM) + RMS_EPS) * gain_ref[:, sl]).astype(BF16)

    head_norm(qkv[:, :Q_COLS], gq_ref, q_ref)
    head_norm(qkv[:, Q_COLS:Q_COLS + K_COLS], gk_ref, k_ref)
    v_ref[...] = qkv[:, Q_COLS + K_COLS:].astype(BF16)


def _qkv_proj(x2, g_mix, w_qkv, gq, gk):
    t = x2.shape[0]
    tm = min(ROW_TILE, t)
    row = lambda i: (i, 0)
    fix = lambda i: (0, 0)
    out = jax.ShapeDtypeStruct((t, D_MODEL), BF16)
    return pl.pallas_call(
        _qkv_kernel,
        grid=(t // tm,),
        in_specs=[pl.BlockSpec((tm, D_MODEL), row), pl.BlockSpec((1, D_MODEL), fix),
                  _resident((D_MODEL, 3 * D_MODEL)), pl.BlockSpec((1, D_MODEL), fix),
                  pl.BlockSpec((1, D_MODEL), fix)],
        out_specs=[pl.BlockSpec((tm, D_MODEL), row)] * 3,
        out_shape=[out, out, out],
        compiler_params=_params(("arbitrary",)),
        name="qkv_proj",
    )(x2, g_mix, w_qkv, gq, gk)


CONV_COLS = 1024
SUBLANES = 8
CHUNK_VREGS = SSM_CHUNK // SUBLANES


def _chunk_time(row):
    return (row >> 3) + CHUNK_VREGS * (row & (SUBLANES - 1))


def _ssm_proj_kernel(x_ref, g_ref, w_ref, cw_ref, cb_ref, dtb_ref, alog_ref, z_ref, xbc_ref, dt_ref, acum_ref,
                     acumt_ref, tail_s, h_s, p_s, *, tiles_per_seq):
    tm = x_ref.shape[0]
    nb = tm // SSM_CHUNK
    taps = CONV_WIDTH - 1

    @pl.when(pl.program_id(0) % tiles_per_seq == 0)
    def _sequence_start():
        tail_s[...] = jnp.zeros(tail_s.shape, F32)

    h = (_rms(x_ref[...], RMS_EPS) * g_ref[...]).astype(BF16)
    rr = lax.broadcasted_iota(jnp.int32, (SSM_CHUNK, SSM_CHUNK), 0)
    cc = lax.broadcasted_iota(jnp.int32, (SSM_CHUNK, SSM_CHUNK), 1)
    pick = jnp.where(cc == _chunk_time(rr), 1.0, 0.0).astype(BF16)
    h = jnp.concatenate(
        [jnp.dot(pick, h[b * SSM_CHUNK:(b + 1) * SSM_CHUNK], preferred_element_type=F32).astype(BF16)
         for b in range(nb)], axis=0)

    h_s[...] = h

    def project(stage):
        return jnp.dot(h_s[...], w_ref[:, stage * CONV_COLS:(stage + 1) * CONV_COLS], preferred_element_type=F32)

    z_stages = D_INNER // CONV_COLS
    n_stages = z_stages + CONV_DIM // CONV_COLS
    sub = lax.broadcasted_iota(jnp.int32, (nb * taps, SUBLANES, CONV_COLS), 1)
    p_s[0] = project(0)
    for stage in range(n_stages):
        if stage + 1 < n_stages:
            p_s[(stage + 1) % 2] = project(stage + 1)
        p = p_s[stage % 2]
        if stage < z_stages:
            z_ref[:, stage * CONV_COLS:(stage + 1) * CONV_COLS] = (p * _sigmoid(p)).astype(BF16)
            continue
        c = stage - z_stages
        cols = slice(c * CONV_COLS, (c + 1) * CONV_COLS)
        p4 = p.reshape(nb, CHUNK_VREGS, SUBLANES, CONV_COLS)
        last = p4[:, CHUNK_VREGS - taps:]
        seq = jnp.concatenate([tail_s[c][None], last], axis=0).reshape((nb + 1) * taps, SUBLANES, CONV_COLS)
        tail_s[c] = last[nb - 1]
        rolled = pltpu.roll(seq, 1, axis=1)
        wrapped = jnp.where(sub == 0, rolled[:nb * taps], rolled[taps:]).reshape(nb, taps, SUBLANES, CONV_COLS)
        conv = cb_ref[:, cols] + cw_ref[taps:taps + 1, cols] * p4
        for back in range(1, CONV_WIDTH):
            shifted = jnp.concatenate([wrapped[:, taps - back:], p4[:, :CHUNK_VREGS - back]], axis=1)
            conv = conv + cw_ref[taps - back:taps - back + 1, cols] * shifted
        xbc_ref[:, cols] = (conv * _sigmoid(conv)).reshape(tm, CONV_COLS).astype(BF16)
    dtl = jnp.dot(h_s[...], w_ref[:, D_INNER + CONV_DIM:], preferred_element_type=F32) + dtb_ref[...]
    dt = jnp.maximum(dtl, 0.0) + jnp.log(1.0 + jnp.exp(-jnp.abs(dtl)))
    dt_ref[...] = dt
    a = dt * (-jnp.exp(alog_ref[...]) * LOG2E)
    upto = jnp.where(_chunk_time(rr) >= _chunk_time(cc), 1.0, 0.0).astype(F32)
    for b in range(nb):
        rows = slice(b * SSM_CHUNK, (b + 1) * SSM_CHUNK)
        acum = jnp.dot(upto, a[rows], preferred_element_type=F32, precision=lax.Precision.HIGHEST)
        acum_ref[rows, :] = acum
        acumt_ref[rows, :] = acum.T


def _ssm_proj(x2, g_mix, w_ssm, conv_w, conv_b, dt_bias, a_log, seq_len):
    t = x2.shape[0]
    tm = min(ROW_TILE, seq_len)
    assert seq_len % tm == 0
    ncol = D_INNER + CONV_DIM + DT_PAD
    row = lambda i: (i, 0)
    fix = lambda i: (0, 0)
    heads = jax.ShapeDtypeStruct((t, DT_PAD), F32)
    return pl.pallas_call(
        functools.partial(_ssm_proj_kernel, tiles_per_seq=seq_len // tm),
        grid=(t // tm,),
        in_specs=[pl.BlockSpec((tm, D_MODEL), row), pl.BlockSpec((1, D_MODEL), fix),
                  _resident((D_MODEL, ncol)), pl.BlockSpec((CONV_WIDTH, CONV_DIM), fix),
                  pl.BlockSpec((1, CONV_DIM), fix), pl.BlockSpec((1, DT_PAD), fix), pl.BlockSpec((1, DT_PAD), fix)],
        out_specs=[pl.BlockSpec((tm, D_INNER), row), pl.BlockSpec((tm, CONV_DIM), row)]
        + [pl.BlockSpec((tm, DT_PAD), row)] * 3,
        out_shape=[jax.ShapeDtypeStruct((t, D_INNER), BF16), jax.ShapeDtypeStruct((t, CONV_DIM), BF16),
                   heads, heads, heads],
        scratch_shapes=[pltpu.VMEM((CONV_DIM // CONV_COLS, CONV_WIDTH - 1, SUBLANES, CONV_COLS), F32),
                        pltpu.VMEM((tm, D_MODEL), BF16), pltpu.VMEM((2, tm, CONV_COLS), F32)],
        compiler_params=_params(("arbitrary",)),
        name="ssm_proj",
    )(x2, g_mix, w_ssm, conv_w, conv_b, dt_bias, a_log)


def _split_maps(q):
    lane = lax.broadcasted_iota(jnp.int32, q.shape, 1)
    zero = jnp.zeros_like(q)
    return jnp.where(lane < ATTN_HEAD_DIM, q, zero), jnp.where(lane >= ATTN_HEAD_DIM, q, zero)


def _attn_finalize(acc1, l1, acc2, l2, lam_ref, subg_ref, o_ref):
    lam_v = lam_ref[...]
    lam = (jnp.exp(jnp.sum(lam_v[0:1] * lam_v[1:2], axis=-1, keepdims=True))
           - jnp.exp(jnp.sum(lam_v[2:3] * lam_v[3:4], axis=-1, keepdims=True)) + LAMBDA_INIT)
    o = acc1 / l1 - lam * (acc2 / l2)
    o_ref[...] = (_rms(o, RMS_EPS) * subg_ref[...] * (1.0 - LAMBDA_INIT)).astype(BF16)


def _attn_bounded_kernel(q_ref, k_ref, v_ref, bias_ref, lam_ref, subg_ref, o_ref, qq_s, vv_s, acc_s, s_s, *, nq):
    pair_id = pl.program_id(2)
    tq = qq_s.shape[1] // 2
    tk = tq
    blocks = (pair_id, nq - 1 - pair_id)

    @pl.when(pair_id == 0)
    def _extend_v():
        vv_s[:, :ATTN_V_DIM] = v_ref[...]
        vv_s[:, ATTN_V_DIM:] = jnp.ones((vv_s.shape[0], ATTN_V_DIM), BF16)

    for side in range(2):
        q1, q2 = _split_maps(q_ref[pl.ds(pl.multiple_of(blocks[side] * tq, tq), tq), :])
        qq_s[side, 0:tq, :] = q1
        qq_s[side, tq:2 * tq, :] = q2
    acc_s[...] = jnp.zeros(acc_s.shape, F32)
    contract_last = (((1,), (1,)), ((), ()))

    def tile(t):
        first = t <= pair_id
        side = jnp.where(first, 0, 1)
        j = jnp.where(first, t, t - pair_id - 1)
        diag = jnp.where(first, blocks[0], blocks[1])
        kind = jnp.where(j == diag, 0, jnp.where(j == diag - 1, 1, 2))
        return side, pl.ds(pl.multiple_of(j * tk, tk), tk), kind

    def logits(t):
        side, rows, kind = tile(t)
        s = lax.dot_general(qq_s[side], k_ref[rows, :], contract_last, preferred_element_type=F32)
        b = bias_ref[kind]
        return jnp.concatenate([s[0:tq] + b, s[tq:2 * tq] + b], axis=0)

    def accumulate(t, slot):
        side, rows, _ = tile(t)
        acc_s[side] += jnp.dot(jnp.exp2(s_s[slot]).astype(BF16), vv_s[rows, :], preferred_element_type=F32)

    s_s[0] = logits(0)
    for t in range(nq + 1):
        if t < nq:
            s_s[(t + 1) % 2] = logits(t + 1)
        accumulate(t, t % 2)

    for side in range(2):
        acc = acc_s[side]
        _attn_finalize(acc[0:tq, :ATTN_V_DIM], acc[0:tq, ATTN_V_DIM:], acc[tq:2 * tq, :ATTN_V_DIM],
                       acc[tq:2 * tq, ATTN_V_DIM:], lam_ref, subg_ref,
                       o_ref.at[pl.ds(pl.multiple_of(blocks[side] * tq, tq), tq), :])


def _attn_online_kernel(it_ref, jt_ref, q_ref, k_ref, v_ref, bias_ref, lam_ref, subg_ref, o_ref,
                        q1_s, q2_s, m1_s, m2_s, l1_s, l2_s, acc1_s, acc2_s):
    step = pl.program_id(2)
    i = it_ref[step]
    j = jt_ref[step]

    @pl.when(j == 0)
    def _init():
        q1_s[...], q2_s[...] = _split_maps(q_ref[...])
        for m_s, l_s, acc_s in ((m1_s, l1_s, acc1_s), (m2_s, l2_s, acc2_s)):
            m_s[...] = jnp.full(m_s.shape, NEG_BIG, F32)
            l_s[...] = jnp.zeros(l_s.shape, F32)
            acc_s[...] = jnp.zeros(acc_s.shape, F32)

    def update(bias):
        k = k_ref[...]
        v = v_ref[...]
        contract_last = (((1,), (1,)), ((), ()))
        for q_s, m_s, l_s, acc_s in ((q1_s, m1_s, l1_s, acc1_s), (q2_s, m2_s, l2_s, acc2_s)):
            s = lax.dot_general(q_s[...], k, contract_last, preferred_element_type=F32)
            if bias is not None:
                s = s + bias_ref[bias]
            m_old = m_s[...]
            m_new = jnp.maximum(m_old, jnp.max(s, axis=-1, keepdims=True))
            alpha = jnp.exp2(m_old - m_new)
            p = jnp.exp2(s - m_new)
            l_s[...] = alpha * l_s[...] + jnp.sum(p, axis=-1, keepdims=True)
            acc_s[...] = alpha * acc_s[...] + jnp.dot(p.astype(BF16), v, preferred_element_type=F32)
            m_s[...] = m_new

    @pl.when(j == i)
    def _diag():
        update(0)

    @pl.when(j == i - 1)
    def _prev():
        update(1)

    @pl.when(j < i - 1)
    def _far():
        update(None)

    @pl.when(j == i)
    def _finalize():
        _attn_finalize(acc1_s[...], l1_s[...], acc2_s[...], l2_s[...], lam_ref, subg_ref, o_ref)


def _t5_bucket(dist):
    n = jnp.maximum(dist, 0)
    max_exact = NUM_BUCKETS // 2
    scaled = jnp.log(jnp.maximum(n, 1).astype(F32) / max_exact) / math.log(MAX_DISTANCE / max_exact)
    large = max_exact + (scaled * (NUM_BUCKETS - max_exact)).astype(jnp.int32)
    large = jnp.minimum(large, NUM_BUCKETS - 1)
    return jnp.where(n < max_exact, n, large)


def _bias_tiles(rel_bias, tile):
    blk = MAX_DISTANCE
    assert tile % blk == 0
    nb = tile // blk
    table = (rel_bias - rel_bias[NUM_BUCKETS - 1]).astype(F32) * LOG2E
    r = jnp.arange(blk, dtype=jnp.int32)
    d0 = r[:, None] - r[None, :]

    def lookup(dist):
        onehot = (_t5_bucket(dist)[..., None] == jnp.arange(NUM_BUCKETS, dtype=jnp.int32)).astype(F32)
        return jnp.einsum('qkn,nh->hqk', onehot, table, precision=lax.Precision.HIGHEST)

    on_diag = jnp.where(d0[None] >= 0, lookup(d0), NEG_BIG)
    sub_diag = lookup(d0 + blk)
    zeros = jnp.zeros_like(sub_diag)
    masked = jnp.full_like(sub_diag, NEG_BIG)

    def assemble(pick):
        return jnp.concatenate(
            [jnp.concatenate([pick(bi, bj) for bj in range(nb)], axis=-1) for bi in range(nb)], axis=-2)

    diag_tile = assemble(lambda bi, bj: on_diag if bi == bj else sub_diag if bi == bj + 1 else zeros if bi > bj else masked)
    prev_tile = assemble(lambda bi, bj: sub_diag if (bi == 0 and bj == nb - 1) else zeros)
    return jnp.stack([diag_tile, prev_tile, jnp.zeros_like(prev_tile)], axis=1)


def _attn_bounded(qn, kn, v, bias, lam_vecs, sub_g):
    b, s, _ = qn.shape
    tile = min(ATTN_TILE, s)
    nq = s // tile
    assert nq % 2 == 0
    seq_map = lambda h, bb, i: (bb, 0, h)
    return pl.pallas_call(
        functools.partial(_attn_bounded_kernel, nq=nq),
        grid=(ATTN_HEADS, b, nq // 2),
        in_specs=[pl.BlockSpec((None, s, ATTN_V_DIM), seq_map),
                  pl.BlockSpec((None, s, ATTN_V_DIM), seq_map),
                  pl.BlockSpec((None, s, ATTN_V_DIM), seq_map),
                  pl.BlockSpec((None, 3, tile, tile), lambda h, bb, i: (h, 0, 0, 0)),
                  pl.BlockSpec((4, ATTN_HEAD_DIM), lambda h, bb, i: (0, 0)),
                  pl.BlockSpec((1, ATTN_V_DIM), lambda h, bb, i: (0, 0))],
        out_specs=pl.BlockSpec((None, s, ATTN_V_DIM), seq_map),
        out_shape=jax.ShapeDtypeStruct((b, s, V_COLS), BF16),
        scratch_shapes=[pltpu.VMEM((2, 2 * tile, ATTN_V_DIM), BF16), pltpu.VMEM((s, 2 * ATTN_V_DIM), BF16),
                        pltpu.VMEM((2, 2 * tile, 2 * ATTN_V_DIM), F32), pltpu.VMEM((2, 2 * tile, tile), F32)],
        compiler_params=_params(("arbitrary", "arbitrary", "arbitrary")),
        name="diff_attn_bounded",
    )(qn, kn, v, bias, lam_vecs, sub_g)


def _attn_online(qn, kn, v, bias, lam_vecs, sub_g):
    b, s, _ = qn.shape
    tile = min(ATTN_TILE, s)
    nq = s // tile
    it = jnp.asarray([i for i in range(nq) for _ in range(i + 1)], jnp.int32)
    jt = jnp.asarray([j for i in range(nq) for j in range(i + 1)], jnp.int32)
    q_map = lambda bb, h, st, it_r, jt_r: (bb, it_r[st], h)
    kv_map = lambda bb, h, st, it_r, jt_r: (bb, jt_r[st], h)
    grid_spec = pltpu.PrefetchScalarGridSpec(
        num_scalar_prefetch=2,
        grid=(b, ATTN_HEADS, int(it.shape[0])),
        in_specs=[pl.BlockSpec((None, tile, ATTN_V_DIM), q_map),
                  pl.BlockSpec((None, tile, ATTN_V_DIM), kv_map),
                  pl.BlockSpec((None, tile, ATTN_V_DIM), kv_map),
                  pl.BlockSpec((None, 3, tile, tile), lambda bb, h, st, it_r, jt_r: (h, 0, 0, 0)),
                  pl.BlockSpec((4, ATTN_HEAD_DIM), lambda bb, h, st, it_r, jt_r: (0, 0)),
                  pl.BlockSpec((1, ATTN_V_DIM), lambda bb, h, st, it_r, jt_r: (0, 0))],
        out_specs=pl.BlockSpec((None, tile, ATTN_V_DIM), q_map),
        scratch_shapes=[pltpu.VMEM((tile, ATTN_V_DIM), BF16), pltpu.VMEM((tile, ATTN_V_DIM), BF16),
                        pltpu.VMEM((tile, 1), F32), pltpu.VMEM((tile, 1), F32),
                        pltpu.VMEM((tile, 1), F32), pltpu.VMEM((tile, 1), F32),
                        pltpu.VMEM((tile, ATTN_V_DIM), F32), pltpu.VMEM((tile, ATTN_V_DIM), F32)],
    )
    return pl.pallas_call(
        _attn_online_kernel,
        grid_spec=grid_spec,
        out_shape=jax.ShapeDtypeStruct((b, s, V_COLS), BF16),
        compiler_params=_params(("arbitrary", "arbitrary", "arbitrary")),
        name="diff_attn_online",
    )(it, jt, qn, kn, v, bias, lam_vecs, sub_g)


def _diff_attention(qn, kn, v, rel_bias, q_gain, k_gain, lam_vecs, sub_g):
    tile = min(ATTN_TILE, qn.shape[1])
    bias = _bias_tiles(rel_bias, tile)
    spread = jnp.max(jnp.abs(rel_bias - rel_bias[NUM_BUCKETS - 1]))
    bound = LOG2E * (1.05 * math.sqrt(ATTN_HEAD_DIM) * jnp.max(jnp.abs(q_gain)) * jnp.max(jnp.abs(k_gain)) + spread)
    args = (qn, kn, v, bias, lam_vecs, sub_g)
    return lax.cond(bound < EXP2_SAFE_BOUND, lambda a: _attn_bounded(*a), lambda a: _attn_online(*a), args)


def _ssd_kernel(z_ref, xbc_ref, dt_ref, acum_ref, acumt_ref, dskip_ref, ng_ref, y_ref, state_s):
    L = SSM_CHUNK

    @pl.when(pl.program_id(1) == 0)
    def _reset():
        state_s[...] = jnp.zeros(state_s.shape, F32)

    row = lax.broadcasted_iota(jnp.int32, (L, L), 0)
    col = lax.broadcasted_iota(jnp.int32, (L, L), 1)
    causal = _chunk_time(row) >= _chunk_time(col)
    unpick = jnp.where(row == _chunk_time(col), 1.0, 0.0).astype(BF16)
    lane = lax.broadcasted_iota(jnp.int32, (L, LANES), 1)
    low = lane < SSM_HEAD_DIM
    for c in range(z_ref.shape[0] // L):
        _ssd_chunk(slice(c * L, (c + 1) * L), z_ref, xbc_ref, dt_ref, acum_ref, acumt_ref, dskip_ref, ng_ref, y_ref,
                   state_s, causal, unpick, low)


def _ssd_chunk(rows, z_ref, xbc_ref, dt_ref, acum_ref, acumt_ref, dskip_ref, ng_ref, y_ref, state_s, causal, unpick,
               low):
    L = SSM_CHUNK
    xs = xbc_ref[rows, :D_INNER].astype(F32)
    bm = xbc_ref[rows, D_INNER:D_INNER + SSM_GROUPS * D_STATE]
    cm = xbc_ref[rows, D_INNER + SSM_GROUPS * D_STATE:]
    dt = dt_ref[rows, :]
    acum = acum_ref[rows, :]
    acum_t = acumt_ref[rows, :]
    a_end = acum[L - 1:L, :]

    def pair(lo, hi):
        return jnp.where(low[:lo.shape[0]], lo, hi)

    def spread(arr, h):
        return jnp.broadcast_to(arr[:, h:h + 1], (arr.shape[0], LANES))

    contract_last = (((1,), (1,)), ((), ()))
    contract_first = (((0,), (0,)), ((), ()))
    y_parts = []
    for g in range(SSM_GROUPS):
        bg = bm[:, g * D_STATE:(g + 1) * D_STATE]
        cg = cm[:, g * D_STATE:(g + 1) * D_STATE]
        cb = lax.dot_general(cg, bg, contract_last, preferred_element_type=F32)
        st = state_s[g]
        y_off = jnp.dot(cg, st.astype(BF16), preferred_element_type=F32)
        xw_parts = []
        dec_parts = []
        for pr in range(SSM_HEADS_PER_GROUP // 2):
            h0 = g * SSM_HEADS_PER_GROUP + 2 * pr
            ch = slice(h0 * SSM_HEAD_DIM, (h0 + 2) * SSM_HEAD_DIM)
            x_pair = xs[:, ch]
            acols = (spread(acum, h0), spread(acum, h0 + 1))
            acol = pair(*acols)
            aend = pair(spread(a_end, h0), spread(a_end, h0 + 1))
            xdt32 = x_pair * pair(spread(dt, h0), spread(dt, h0 + 1))
            xdt = xdt32.astype(BF16)
            yd = []
            for hh, ac in zip((h0, h0 + 1), acols):
                seg = ac - acum_t[hh:hh + 1, :]
                decay = jnp.exp2(jnp.where(causal, seg, NEG_BIG))
                yd.append(jnp.dot((cb * decay).astype(BF16), xdt, preferred_element_type=F32))
            y_diag = pair(yd[0], yd[1])
            off = y_off[:, 2 * pr * SSM_HEAD_DIM:(2 * pr + 2) * SSM_HEAD_DIM]
            y_parts.append(y_diag + off * jnp.exp2(acol))
            xw_parts.append(xdt32 * jnp.exp2(aend - acol))
            dec_parts.append(jnp.exp2(aend))
        xw = jnp.concatenate(xw_parts, axis=-1).astype(BF16)
        dec = jnp.concatenate(dec_parts, axis=-1)
        state_s[g] = st * dec + lax.dot_general(bg, xw, contract_first, preferred_element_type=F32)
    y = (jnp.concatenate(y_parts, axis=-1) + dskip_ref[...] * xs) * z_ref[rows, :].astype(F32)
    gsz = D_INNER // SSM_GROUPS
    for g in range(SSM_GROUPS):
        sl = slice(g * gsz, (g + 1) * gsz)
        yn = (_rms(y[:, sl], SSM_EPS) * ng_ref[:, sl]).astype(BF16)
        y_ref[rows, sl] = jnp.dot(unpick, yn, preferred_element_type=F32).astype(BF16)


def _ssd(z, xbc, dt, acum, acum_t, d_skip_ch, norm_g):
    b, s, _ = z.shape
    rows = SSD_CHUNKS_PER_STEP * SSM_CHUNK
    assert s % rows == 0
    blk = lambda bb, c: (bb, c, 0)
    fix = lambda bb, c: (0, 0)
    return pl.pallas_call(
        _ssd_kernel,
        grid=(b, s // rows),
        in_specs=[pl.BlockSpec((None, rows, D_INNER), blk), pl.BlockSpec((None, rows, CONV_DIM), blk)]
        + [pl.BlockSpec((None, rows, DT_PAD), blk)] * 3
        + [pl.BlockSpec((1, D_INNER), fix), pl.BlockSpec((1, D_INNER), fix)],
        out_specs=pl.BlockSpec((None, rows, D_INNER), blk),
        out_shape=jax.ShapeDtypeStruct((b, s, D_INNER), BF16),
        scratch_shapes=[pltpu.VMEM((SSM_GROUPS, D_STATE, SSM_HEADS_PER_GROUP * SSM_HEAD_DIM), F32)],
        compiler_params=_params(("arbitrary", "arbitrary")),
        name="ssd",
    )(z, xbc, dt, acum, acum_t, d_skip_ch, norm_g)


def _round_up(n, m):
    return (n + m - 1) // m * m


def _merge_kernel(x_ref, attn_ref, y_ref, gmix_ref, wg_ref, wao_ref, wso_ref, wout_ref, gffn_ref, wr_ref, br_ref,
                  x1_ref, h2_ref, pos_ref, topw_ref, cnt_ref):
    x = x_ref[...]
    tm = x.shape[0]
    h = (_rms(x, RMS_EPS) * gmix_ref[...]).astype(BF16)
    gates = _sigmoid(jnp.dot(h, wg_ref[...], preferred_element_type=F32))
    attn_out = jnp.dot(attn_ref[...], wao_ref[...], preferred_element_type=F32)
    ssm_out = jnp.dot(y_ref[...], wso_ref[...], preferred_element_type=F32)
    merged = gates[:, :D_MODEL] * attn_out + gates[:, D_MODEL:] * ssm_out
    x1 = x + jnp.dot(merged.astype(BF16), wout_ref[...], preferred_element_type=F32)
    x1_ref[...] = x1
    h2 = _rms(x1, RMS_EPS) * gffn_ref[...]
    h2_ref[...] = h2.astype(BF16)

    logits = lax.dot_general(wr_ref[...], h2, (((1,), (1,)), ((), ())), preferred_element_type=F32,
                             precision=lax.Precision.HIGHEST) + br_ref[...]
    eid = lax.broadcasted_iota(jnp.int32, logits.shape, 0)
    vals, hits = [], []
    member = jnp.zeros(logits.shape, F32)
    work = logits
    for kk in range(TOP_K):
        m = jnp.max(work, axis=0, keepdims=True)
        idx = jnp.min(jnp.where(work == m, eid, N_EXPERTS), axis=0, keepdims=True)
        hit = eid == idx
        vals.append(m)
        hits.append(hit)
        member = jnp.where(hit, 1.0, member)
        work = jnp.where(hit, -jnp.inf, work)
    ex = [jnp.exp(v - vals[0]) for v in vals]
    denom = ex[0] + ex[1] + ex[2] + ex[3]
    for kk in range(TOP_K):
        topw_ref[kk:kk + 1, :] = ex[kk] / denom

    r = lax.broadcasted_iota(jnp.int32, (tm, tm), 0)
    c = lax.broadcasted_iota(jnp.int32, (tm, tm), 1)
    before = jnp.where(r < c, 1.0, 0.0).astype(BF16)
    prefix = jnp.dot(member.astype(BF16), before, preferred_element_type=F32)
    cnt = jnp.sum(member, axis=1, keepdims=True).astype(jnp.int32)
    cnt_al = jnp.bitwise_and(cnt + (ROW_ALIGN - 1), -ROW_ALIGN)
    cnt_al = jnp.broadcast_to(cnt_al, (N_EXPERTS, LANES)).astype(F32)
    er = lax.broadcasted_iota(jnp.int32, (N_EXPERTS, N_EXPERTS), 0)
    ec = lax.broadcasted_iota(jnp.int32, (N_EXPERTS, N_EXPERTS), 1)
    lower = jnp.where(ec < er, 1.0, 0.0).astype(F32)
    run_start = jnp.dot(lower, cnt_al, preferred_element_type=F32, precision=lax.Precision.HIGHEST)[:, 0:1]
    base = prefix + run_start
    for kk in range(TOP_K):
        pos_ref[kk:kk + 1, :] = jnp.sum(jnp.where(hits[kk], base, 0.0), axis=0, keepdims=True).astype(jnp.int32)
    cnt_ref[...] = jnp.broadcast_to(cnt, cnt_ref.shape)


def _merge(x2, attn, y, g_mix, w_gate, w_ao, w_so, w_out, g_ffn, w_r_t, b_r):
    t = x2.shape[0]
    tm = min(ROW_TILE, t)
    row = lambda i: (i, 0)
    colb = lambda i: (0, i)
    fix = lambda i: (0, 0)
    return pl.pallas_call(
        _merge_kernel,
        grid=(t // tm,),
        in_specs=[pl.BlockSpec((tm, D_MODEL), row), pl.BlockSpec((tm, V_COLS), row), pl.BlockSpec((tm, D_INNER), row),
                  pl.BlockSpec((1, D_MODEL), fix), _resident((D_MODEL, 2 * D_MODEL)),
                  _resident((V_COLS, D_MODEL)), _resident((D_INNER, D_MODEL)),
                  _resident((D_MODEL, D_MODEL)), pl.BlockSpec((1, D_MODEL), fix),
                  pl.BlockSpec((N_EXPERTS, D_MODEL), fix), pl.BlockSpec((N_EXPERTS, 1), fix)],
        out_specs=[pl.BlockSpec((tm, D_MODEL), row), pl.BlockSpec((tm, D_MODEL), row),
                   pl.BlockSpec((TOP_K, tm), colb), pl.BlockSpec((TOP_K, tm), colb),
                   pl.BlockSpec((N_EXPERTS, LANES), row)],
        out_shape=[jax.ShapeDtypeStruct((t, D_MODEL), F32), jax.ShapeDtypeStruct((t, D_MODEL), BF16),
                   jax.ShapeDtypeStruct((TOP_K, t), jnp.int32), jax.ShapeDtypeStruct((TOP_K, t), F32),
                   jax.ShapeDtypeStruct((t // tm * N_EXPERTS, LANES), jnp.int32)],
        compiler_params=_params(("arbitrary",)),
        name="merge_router",
    )(x2, attn, y, g_mix, w_gate, w_ao, w_so, w_out, g_ffn, w_r_t, b_r)


def _sorted_rows(tm):
    return _round_up(TOP_K * tm + N_EXPERTS * (ROW_ALIGN - 1), LANES)


def _run_copies(n, max_n, vmem_ref, vmem_off, hbm_ref, hbm_off, sem, to_hbm, wait):
    done = 0
    pieces = [ROW_ALIGN << p for p in range((max_n // ROW_ALIGN).bit_length())]
    for bit in reversed(pieces):
        take = (n & bit) != 0

        @pl.when(take)
        def _piece(bit=bit, done=done):
            v = vmem_ref.at[pl.ds(pl.multiple_of(vmem_off + done, ROW_ALIGN), bit)]
            h = hbm_ref.at[pl.ds(pl.multiple_of(hbm_off + done, ROW_ALIGN), bit)]
            cp = pltpu.make_async_copy(v, h, sem) if to_hbm else pltpu.make_async_copy(h, v, sem)
            cp.wait() if wait else cp.start()

        done = done + jnp.where(take, bit, 0)


def _dispatch_kernel(off_ref, cnt_ref, start_ref, last_ref, h2_ref, pos_ref, xs_hbm, buf_s, zero_s, sems, zsem):
    i = pl.program_id(0)
    tm = h2_ref.shape[0]
    rows = buf_s.shape[1]
    slot = i & 1

    @pl.when(i == 0)
    def _zero_last_blocks():
        zero_s[...] = jnp.zeros(zero_s.shape, BF16)
        for e in range(N_EXPERTS):
            start = pl.multiple_of(last_ref[e], MOE_TILE)
            pltpu.make_async_copy(zero_s, xs_hbm.at[pl.ds(start, MOE_TILE)], zsem).start()
        for e in range(N_EXPERTS):
            pltpu.make_async_copy(zero_s, xs_hbm.at[pl.ds(0, MOE_TILE)], zsem).wait()

        def slack(wait):
            def body(blk, carry):
                cp = pltpu.make_async_copy(zero_s, xs_hbm.at[pl.ds(pl.multiple_of(blk * MOE_TILE, MOE_TILE), MOE_TILE)],
                                           zsem)
                cp.wait() if wait else cp.start()
                return carry
            lax.fori_loop(last_ref[N_EXPERTS], xs_hbm.shape[0] // MOE_TILE, body, 0)

        slack(False)
        slack(True)

    pos = pos_ref[...]
    rid = lax.broadcasted_iota(jnp.int32, (rows, tm), 0)
    sel = jnp.zeros((rows, tm), F32)
    for kk in range(TOP_K):
        sel = sel + jnp.where(rid == pos[kk:kk + 1, :], 1.0, 0.0)
    buf_s[slot] = jnp.dot(sel.astype(BF16), h2_ref[...], preferred_element_type=F32).astype(BF16)

    def copies(tile, tile_slot, wait):
        for e in range(N_EXPERTS):
            idx = tile * N_EXPERTS + e
            _run_copies(cnt_ref[idx], tm, buf_s.at[tile_slot], start_ref[idx], xs_hbm, off_ref[idx],
                        sems.at[tile_slot], True, wait)

    copies(i, slot, False)

    @pl.when(i >= 1)
    def _previous_tile_done():
        copies(i - 1, 1 - slot, True)

    @pl.when(i == pl.num_programs(0) - 1)
    def _last_tile_done():
        copies(i, slot, True)


def _dispatch(run_off, run_cnt, run_start, last_block, h2, pos, n_rows):
    t = h2.shape[0]
    tm = min(ROW_TILE, t)
    grid_spec = pltpu.PrefetchScalarGridSpec(
        num_scalar_prefetch=4,
        grid=(t // tm,),
        in_specs=[pl.BlockSpec((tm, D_MODEL), lambda i, *_: (i, 0)),
                  pl.BlockSpec((TOP_K, tm), lambda i, *_: (0, i))],
        out_specs=pl.BlockSpec(memory_space=pl.ANY),
        scratch_shapes=[pltpu.VMEM((2, _sorted_rows(tm), D_MODEL), BF16), pltpu.VMEM((MOE_TILE, D_MODEL), BF16),
                        pltpu.SemaphoreType.DMA((2,)), pltpu.SemaphoreType.DMA(())],
    )
    return pl.pallas_call(
        _dispatch_kernel,
        grid_spec=grid_spec,
        out_shape=jax.ShapeDtypeStruct((n_rows, D_MODEL), BF16),
        compiler_params=_params(("arbitrary",)),
        name="moe_dispatch",
    )(run_off, run_cnt, run_start, last_block, h2, pos)


def _expert_kernel(be_ref, nb_ref, x_ref, w1_ref, b1_ref, w2_ref, b2_ref, y_ref, w1_s, w2_s):
    i = pl.program_id(0)

    @pl.when(jnp.logical_or(i == 0, be_ref[i] != be_ref[jnp.maximum(i - 1, 0)]))
    def _new_expert():
        w1_s[...] = w1_ref[...].astype(BF16)
        w2_s[...] = w2_ref[...].astype(BF16)

    @pl.when(i < nb_ref[0])
    def _run():
        gu = jnp.dot(x_ref[...], w1_s[...], preferred_element_type=F32) + b1_ref[...]
        gate = jnp.minimum(gu[:, :D_EXPERT], SWIGLU_LIMIT)
        up = jnp.clip(gu[:, D_EXPERT:], -SWIGLU_LIMIT, SWIGLU_LIMIT)
        act = (up + 1.0) * (gate * _sigmoid(SWIGLU_ALPHA * gate))
        y = jnp.dot(act.astype(BF16), w2_s[...], preferred_element_type=F32) + b2_ref[...]
        y_ref[...] = y.astype(BF16)

    @pl.when(i >= nb_ref[0])
    def _slack():
        y_ref[...] = jnp.zeros(y_ref.shape, BF16)


def _experts(block_e, n_used, xs, w1, b1, w2, b2):
    n_rows = xs.shape[0]
    nb = n_rows // MOE_TILE
    row_in = lambda i, be, nu: (jnp.minimum(i, nu[0] - 1), 0)
    row = lambda i, be, nu: (i, 0)
    wsel = lambda i, be, nu: (be[i], 0, 0)
    grid_spec = pltpu.PrefetchScalarGridSpec(
        num_scalar_prefetch=2,
        grid=(nb,),
        in_specs=[pl.BlockSpec((MOE_TILE, D_MODEL), row_in),
                  pl.BlockSpec((None, D_MODEL, 2 * D_EXPERT), wsel), pl.BlockSpec((None, 1, 2 * D_EXPERT), wsel),
                  pl.BlockSpec((None, D_EXPERT, D_MODEL), wsel), pl.BlockSpec((None, 1, D_MODEL), wsel)],
        out_specs=pl.BlockSpec((MOE_TILE, D_MODEL), row),
        scratch_shapes=[pltpu.VMEM((D_MODEL, 2 * D_EXPERT), BF16), pltpu.VMEM((D_EXPERT, D_MODEL), BF16)],
    )
    return pl.pallas_call(
        _expert_kernel,
        grid_spec=grid_spec,
        out_shape=jax.ShapeDtypeStruct((n_rows, D_MODEL), BF16),
        compiler_params=_params(("arbitrary",)),
        name="moe_experts",
    )(block_e, n_used, xs, w1, b1, w2, b2)


def _combine_kernel(off_ref, cnt_ref, start_ref, y_hbm, x1_ref, pos_ref, w_ref, o_ref, buf_s, sems):
    i = pl.program_id(0)
    tm = x1_ref.shape[0]
    rows = buf_s.shape[1]
    slot = i & 1

    def copies(tile, tile_slot, wait):
        for e in range(N_EXPERTS):
            idx = tile * N_EXPERTS + e
            _run_copies(cnt_ref[idx], tm, buf_s.at[tile_slot], start_ref[idx], y_hbm, off_ref[idx],
                        sems.at[tile_slot], False, wait)

    @pl.when(i == 0)
    def _first_tile():
        buf_s[...] = jnp.zeros(buf_s.shape, BF16)
        copies(0, 0, False)

    @pl.when(i + 1 < pl.num_programs(0))
    def _prefetch_next_tile():
        copies(i + 1, 1 - slot, False)

    pos = pos_ref[...]
    w = w_ref[...]
    cid = lax.broadcasted_iota(jnp.int32, (tm, rows), 1)
    wsel = jnp.zeros((tm, rows), F32)
    for kk in range(TOP_K):
        wsel = wsel + jnp.where(cid == pos[:, kk:kk + 1], w[:, kk:kk + 1], 0.0)
    copies(i, slot, True)
    o_ref[...] = x1_ref[...] + jnp.dot(wsel.astype(BF16), buf_s[slot], preferred_element_type=F32)


def _combine(run_off, run_cnt, run_start, y, x1, pos_tok, w_tok):
    t = x1.shape[0]
    tm = min(ROW_TILE, t)
    grid_spec = pltpu.PrefetchScalarGridSpec(
        num_scalar_prefetch=3,
        grid=(t // tm,),
        in_specs=[pl.BlockSpec(memory_space=pl.ANY),
                  pl.BlockSpec((tm, D_MODEL), lambda i, *_: (i, 0)),
                  pl.BlockSpec((tm, TOP_K), lambda i, *_: (i, 0)), pl.BlockSpec((tm, TOP_K), lambda i, *_: (i, 0))],
        out_specs=pl.BlockSpec((tm, D_MODEL), lambda i, *_: (i, 0)),
        scratch_shapes=[pltpu.VMEM((2, _sorted_rows(tm), D_MODEL), BF16), pltpu.SemaphoreType.DMA((2,))],
    )
    return pl.pallas_call(
        _combine_kernel,
        grid_spec=grid_spec,
        out_shape=jax.ShapeDtypeStruct((t, D_MODEL), F32),
        compiler_params=_params(("arbitrary",)),
        name="moe_combine",
    )(run_off, run_cnt, run_start, y, x1, pos_tok, w_tok)


def kernel(x, g_mix, w_in, q_norm_g, k_norm_g, lambda_q1, lambda_k1, lambda_q2, lambda_k2, attn_sub_g, rel_bias,
           w_attn_o, conv_w, conv_b, dt_bias, a_log, d_skip, ssm_norm_g, w_ssm_o, w_out, g_ffn, w_router, b_router,
           w1, b1, w2, b2):
    b, s, d = x.shape
    t = b * s
    l = 0
    x2 = x.reshape(t, d)

    w = w_in[l]
    c0 = Q_COLS + K_COLS + V_COLS
    c1 = c0 + D_INNER + CONV_DIM
    w_qkv = w[:, :c0].astype(BF16)
    w_dt = jnp.pad(w[:, c1:c1 + SSM_HEADS], ((0, 0), (0, DT_PAD - SSM_HEADS)))
    w_ssm = jnp.concatenate([w[:, c0:c1], w_dt], axis=1).astype(BF16)
    w_gate = w[:, c1 + SSM_HEADS:].astype(BF16)
    n_hd = Q_COLS // ATTN_HEAD_DIM
    gq = (jnp.tile(q_norm_g[l], n_hd) * (ATTN_HEAD_DIM ** -0.5 * LOG2E)).reshape(1, Q_COLS)
    gk = jnp.tile(k_norm_g[l], n_hd).reshape(1, K_COLS)
    gm = g_mix[l].reshape(1, d)

    qn, kn, v = _qkv_proj(x2, gm, w_qkv, gq, gk)
    pad_h = (0, DT_PAD - SSM_HEADS)
    z, xbc, dt, acum, acum_t = _ssm_proj(x2, gm, w_ssm, conv_w[l], conv_b[l].reshape(1, -1),
                                         jnp.pad(dt_bias[l], pad_h).reshape(1, -1),
                                         jnp.pad(a_log[l], pad_h).reshape(1, -1), s)

    lam_vecs = jnp.stack([lambda_q1[l], lambda_k1[l], lambda_q2[l], lambda_k2[l]]).astype(F32)
    attn = _diff_attention(qn.reshape(b, s, -1), kn.reshape(b, s, -1), v.reshape(b, s, -1), rel_bias, q_norm_g[l],
                           k_norm_g[l], lam_vecs, attn_sub_g[l].reshape(1, ATTN_V_DIM))

    y = _ssd(z.reshape(b, s, -1), xbc.reshape(b, s, -1), dt.reshape(b, s, -1), acum.reshape(b, s, -1),
             acum_t.reshape(b, s, -1), jnp.repeat(d_skip[l], SSM_HEAD_DIM).reshape(1, -1),
             ssm_norm_g[l].reshape(1, -1))

    x1, h2, pos, top_w, tile_cnt = _merge(
        x2, attn.reshape(t, -1), y.reshape(t, -1), gm, w_gate, w_attn_o[l].astype(BF16), w_ssm_o[l].astype(BF16),
        w_out[l].astype(BF16), g_ffn[l].reshape(1, d), w_router[l].T, b_router[l].reshape(-1, 1))

    n_tiles = t // min(ROW_TILE, t)
    cnt = tile_cnt.reshape(n_tiles, N_EXPERTS, LANES)[:, :, 0]
    cnt_al = _round_up(cnt, ROW_ALIGN)
    run_start = jnp.cumsum(cnt_al, axis=1) - cnt_al
    padded = _round_up(jnp.sum(cnt_al, axis=0), MOE_TILE)
    end_pad = jnp.cumsum(padded)
    run_off = (end_pad - padded)[None, :] + jnp.cumsum(cnt_al, axis=0) - cnt_al
    n_rows = _round_up(t * TOP_K + n_tiles * N_EXPERTS * (ROW_ALIGN - 1), MOE_TILE) + N_EXPERTS * MOE_TILE
    block_start = jnp.arange(n_rows // MOE_TILE, dtype=jnp.int32) * MOE_TILE
    block_e = jnp.minimum(jnp.sum(block_start[:, None] >= end_pad[None, :], axis=1), N_EXPERTS - 1).astype(jnp.int32)
    n_used = (end_pad[-1:] // MOE_TILE).astype(jnp.int32)
    last_block = jnp.concatenate([jnp.maximum(end_pad - MOE_TILE, 0), end_pad[-1:] // MOE_TILE]).astype(jnp.int32)
    tables = [a.reshape(-1).astype(jnp.int32) for a in (run_off, cnt_al, run_start)]

    xs = _dispatch(*tables, last_block, h2, pos, n_rows)
    ys = _experts(block_e, n_used, xs, w1[l], b1[l][:, None, :], w2[l], b2[l][:, None, :])
    out = _combine(*tables, ys, x1, pos.T, top_w.T)
    return out.reshape(b, s, d)
```

```python
import functools
import math

import jax
import jax.numpy as jnp
from jax import lax
from jax.experimental import pallas as pl
from jax.experimental.pallas import tpu as pltpu

F32 = jnp.float32
BF16 = jnp.bfloat16

D_MODEL = 1024
ATTN_HEADS = 8
ATTN_HEAD_DIM = 64
ATTN_V_DIM = 2 * ATTN_HEAD_DIM
LAMBDA_INIT = 0.8 - 0.6 * math.exp(-0.3 * 0)
NUM_BUCKETS = 32
MAX_DISTANCE = 128
D_INNER = 2 * D_MODEL
SSM_HEAD_DIM = 64
SSM_HEADS = D_INNER // SSM_HEAD_DIM
SSM_GROUPS = 8
SSM_HEADS_PER_GROUP = SSM_HEADS // SSM_GROUPS
D_STATE = 128
CONV_WIDTH = 4
SSM_CHUNK = 128
CONV_DIM = D_INNER + 2 * SSM_GROUPS * D_STATE
N_EXPERTS = 32
TOP_K = 4
D_EXPERT = D_MODEL
SWIGLU_LIMIT = 7.0
SWIGLU_ALPHA = 1.702
RMS_EPS = 1e-6
SSM_EPS = 1e-5
Q_COLS = ATTN_HEADS * 2 * ATTN_HEAD_DIM
K_COLS = Q_COLS
V_COLS = ATTN_HEADS * ATTN_V_DIM

LANES = 128
MXU_DIM = 256
DT_PAD = LANES
NEG_BIG = -1e30
LOG2E = math.log2(math.e)
EXP2_SAFE_BOUND = 80.0
VMEM_LIMIT = 56 * 1024 * 1024

ROW_TILE = 512
ATTN_TILE = 512
HEADS_PER_STEP = 2
MOE_TILE = 512
ROW_ALIGN = 16
SSD_CHUNKS_PER_STEP = 4


def _rms(x, eps):
    return x * lax.rsqrt(jnp.mean(x * x, axis=-1, keepdims=True) + eps)


def _sigmoid(x):
    return 1.0 / (1.0 + jnp.exp(-x))


def _params(sem):
    return pltpu.CompilerParams(dimension_semantics=sem, vmem_limit_bytes=VMEM_LIMIT)


def _resident(shape):
    return pl.BlockSpec(shape, lambda *_: (0,) * len(shape), pipeline_mode=pl.Buffered(1))


def _qkv_kernel(x_ref, g_ref, w_ref, gq_ref, gk_ref, q_ref, k_ref, v_ref):
    h = (_rms(x_ref[...], RMS_EPS) * g_ref[...]).astype(BF16)
    qkv = jnp.dot(h, w_ref[...], preferred_element_type=F32)
    r = lax.broadcasted_iota(jnp.int32, (MXU_DIM, MXU_DIM), 0) // ATTN_HEAD_DIM
    c = lax.broadcasted_iota(jnp.int32, (MXU_DIM, MXU_DIM), 1) // ATTN_HEAD_DIM
    group_ones = jnp.where(r == c, 1.0, 0.0).astype(BF16)

    def head_norm(t, gain_ref, out_ref):
        for cc in range(Q_COLS // MXU_DIM):
            sl = slice(cc * MXU_DIM, (cc + 1) * MXU_DIM)
            tc = t[:, sl]
            ss = jnp.dot((tc * tc).astype(BF16), group_ones, preferred_element_type=F32)
            out_ref[:, sl] = (tc * lax.rsqrt(ss * (1.0 / ATTN_HEAD_DIM) + RMS_EPS) * gain_ref[:, sl]).astype(BF16)

    head_norm(qkv[:, :Q_COLS], gq_ref, q_ref)
    head_norm(qkv[:, Q_COLS:Q_COLS + K_COLS], gk_ref, k_ref)
    v_ref[...] = qkv[:, Q_COLS + K_COLS:].astype(BF16)


def _qkv_proj(x2, g_mix, w_qkv, gq, gk):
    t = x2.shape[0]
    tm = min(ROW_TILE, t)
    row = lambda i: (i, 0)
    fix = lambda i: (0, 0)
    out = jax.ShapeDtypeStruct((t, D_MODEL), BF16)
    return pl.pallas_call(
        _qkv_kernel,
        grid=(t // tm,),
        in_specs=[pl.BlockSpec((tm, D_MODEL), row), pl.BlockSpec((1, D_MODEL), fix),
                  _resident((D_MODEL, 3 * D_MODEL)), pl.BlockSpec((1, D_MODEL), fix),
                  pl.BlockSpec((1, D_MODEL), fix)],
        out_specs=[pl.BlockSpec((tm, D_MODEL), row)] * 3,
        out_shape=[out, out, out],
        compiler_params=_params(("arbitrary",)),
        name="qkv_proj",
    )(x2, g_mix, w_qkv, gq, gk)


CONV_COLS = 1024
SUBLANES = 8
CHUNK_VREGS = SSM_CHUNK // SUBLANES


def _chunk_time(row):
    return (row >> 3) + CHUNK_VREGS * (row & (SUBLANES - 1))


def _ssm_proj_kernel(x_ref, g_ref, w_ref, cw_ref, cb_ref, dtb_ref, alog_ref, z_ref, xbc_ref, dt_ref, acum_ref,
                     acumt_ref, tail_s, h_s, p_s, *, tiles_per_seq):
    tm = x_ref.shape[0]
    nb = tm // SSM_CHUNK
    taps = CONV_WIDTH - 1

    @pl.when(pl.program_id(0) % tiles_per_seq == 0)
    def _sequence_start():
        tail_s[...] = jnp.zeros(tail_s.shape, F32)

    h = (_rms(x_ref[...], RMS_EPS) * g_ref[...]).astype(BF16)
    rr = lax.broadcasted_iota(jnp.int32, (SSM_CHUNK, SSM_CHUNK), 0)
    cc = lax.broadcasted_iota(jnp.int32, (SSM_CHUNK, SSM_CHUNK), 1)
    pick = jnp.where(cc == _chunk_time(rr), 1.0, 0.0).astype(BF16)
    h = jnp.concatenate(
        [jnp.dot(pick, h[b * SSM_CHUNK:(b + 1) * SSM_CHUNK], preferred_element_type=F32).astype(BF16)
         for b in range(nb)], axis=0)

    h_s[...] = h

    def project(stage):
        return jnp.dot(h_s[...], w_ref[:, stage * CONV_COLS:(stage + 1) * CONV_COLS], preferred_element_type=F32)

    z_stages = D_INNER // CONV_COLS
    n_stages = z_stages + CONV_DIM // CONV_COLS
    sub = lax.broadcasted_iota(jnp.int32, (nb * taps, SUBLANES, CONV_COLS), 1)
    p_s[0] = project(0)
    for stage in range(n_stages):
        if stage + 1 < n_stages:
            p_s[(stage + 1) % 2] = project(stage + 1)
        p = p_s[stage % 2]
        if stage < z_stages:
            z_ref[:, stage * CONV_COLS:(stage + 1) * CONV_COLS] = (p * _sigmoid(p)).astype(BF16)
            continue
        c = stage - z_stages
        cols = slice(c * CONV_COLS, (c + 1) * CONV_COLS)
        p4 = p.reshape(nb, CHUNK_VREGS, SUBLANES, CONV_COLS)
        last = p4[:, CHUNK_VREGS - taps:]
        seq = jnp.concatenate([tail_s[c][None], last], axis=0).reshape((nb + 1) * taps, SUBLANES, CONV_COLS)
        tail_s[c] = last[nb - 1]
        rolled = pltpu.roll(seq, 1, axis=1)
        wrapped = jnp.where(sub == 0, rolled[:nb * taps], rolled[taps:]).reshape(nb, taps, SUBLANES, CONV_COLS)
        conv = cb_ref[:, cols] + cw_ref[taps:taps + 1, cols] * p4
        for back in range(1, CONV_WIDTH):
            shifted = jnp.concatenate([wrapped[:, taps - back:], p4[:, :CHUNK_VREGS - back]], axis=1)
            conv = conv + cw_ref[taps - back:taps - back + 1, cols] * shifted
        xbc_ref[:, cols] = (conv * _sigmoid(conv)).reshape(tm, CONV_COLS).astype(BF16)
    dtl = jnp.dot(h_s[...], w_ref[:, D_INNER + CONV_DIM:], preferred_element_type=F32) + dtb_ref[...]
    dt = jnp.maximum(dtl, 0.0) + jnp.log(1.0 + jnp.exp(-jnp.abs(dtl)))
    dt_ref[...] = dt
    a = dt * (-jnp.exp(alog_ref[...]) * LOG2E)
    upto = jnp.where(_chunk_time(rr) >= _chunk_time(cc), 1.0, 0.0).astype(F32)
    for b in range(nb):
        rows = slice(b * SSM_CHUNK, (b + 1) * SSM_CHUNK)
        acum = jnp.dot(upto, a[rows], preferred_element_type=F32, precision=lax.Precision.HIGHEST)
        acum_ref[rows, :] = acum
        acumt_ref[rows, :] = acum.T


def _ssm_proj(x2, g_mix, w_ssm, conv_w, conv_b, dt_bias, a_log, seq_len):
    t = x2.shape[0]
    tm = min(ROW_TILE, seq_len)
    assert seq_len % tm == 0
    ncol = D_INNER + CONV_DIM + DT_PAD
    row = lambda i: (i, 0)
    fix = lambda i: (0, 0)
    heads = jax.ShapeDtypeStruct((t, DT_PAD), F32)
    return pl.pallas_call(
        functools.partial(_ssm_proj_kernel, tiles_per_seq=seq_len // tm),
        grid=(t // tm,),
        in_specs=[pl.BlockSpec((tm, D_MODEL), row), pl.BlockSpec((1, D_MODEL), fix),
                  _resident((D_MODEL, ncol)), pl.BlockSpec((CONV_WIDTH, CONV_DIM), fix),
                  pl.BlockSpec((1, CONV_DIM), fix), pl.BlockSpec((1, DT_PAD), fix), pl.BlockSpec((1, DT_PAD), fix)],
        out_specs=[pl.BlockSpec((tm, D_INNER), row), pl.BlockSpec((tm, CONV_DIM), row)]
        + [pl.BlockSpec((tm, DT_PAD), row)] * 3,
        out_shape=[jax.ShapeDtypeStruct((t, D_INNER), BF16), jax.ShapeDtypeStruct((t, CONV_DIM), BF16),
                   heads, heads, heads],
        scratch_shapes=[pltpu.VMEM((CONV_DIM // CONV_COLS, CONV_WIDTH - 1, SUBLANES, CONV_COLS), F32),
                        pltpu.VMEM((tm, D_MODEL), BF16), pltpu.VMEM((2, tm, CONV_COLS), F32)],
        compiler_params=_params(("arbitrary",)),
        name="ssm_proj",
    )(x2, g_mix, w_ssm, conv_w, conv_b, dt_bias, a_log)


def _split_maps(q):
    lane = lax.broadcasted_iota(jnp.int32, q.shape, 1)
    zero = jnp.zeros_like(q)
    return jnp.where(lane < ATTN_HEAD_DIM, q, zero), jnp.where(lane >= ATTN_HEAD_DIM, q, zero)


def _attn_finalize(acc1, l1, acc2, l2, lam_ref, subg_ref, o_ref):
    lam_v = lam_ref[...]
    lam = (jnp.exp(jnp.sum(lam_v[0:1] * lam_v[1:2], axis=-1, keepdims=True))
           - jnp.exp(jnp.sum(lam_v[2:3] * lam_v[3:4], axis=-1, keepdims=True)) + LAMBDA_INIT)
    o = acc1 / l1 - lam * (acc2 / l2)
    o_ref[...] = (_rms(o, RMS_EPS) * subg_ref[...] * (1.0 - LAMBDA_INIT)).astype(BF16)


def _attn_bounded_kernel(q_ref, k_ref, v_ref, bias_ref, lam_ref, subg_ref, o_ref, vv_s, s_s, *head_scratch, nq):
    pair_id = pl.program_id(2)
    tq = s_s.shape[2]
    tk = tq
    blocks = (pair_id, nq - 1 - pair_id)
    contract_last = (((1,), (1,)), ((), ()))

    def tile(t):
        first = t <= pair_id
        side = jnp.where(first, 0, 1)
        j = jnp.where(first, t, t - pair_id - 1)
        diag = jnp.where(first, blocks[0], blocks[1])
        kind = jnp.where(j == diag, 0, jnp.where(j == diag - 1, 1, 2))
        return side, pl.ds(pl.multiple_of(j * tk, tk), tk), kind

    def lanes(head):
        return slice(head * ATTN_V_DIM, (head + 1) * ATTN_V_DIM)

    @pl.when(pair_id == 0)
    def _extend_v():
        for head in range(HEADS_PER_STEP):
            vv_s[head, :, :ATTN_V_DIM] = v_ref[:, lanes(head)]
            vv_s[head, :, ATTN_V_DIM:] = jnp.ones((vv_s.shape[1], ATTN_V_DIM), BF16)

    for head in range(HEADS_PER_STEP):
        qq_s, acc_s = head_scratch[2 * head:2 * head + 2]
        for side in range(2):
            q1, q2 = _split_maps(q_ref[pl.ds(pl.multiple_of(blocks[side] * tq, tq), tq), lanes(head)])
            qq_s[side, 0:tq, :] = q1
            qq_s[side, tq:2 * tq, :] = q2
        acc_s[...] = jnp.zeros(acc_s.shape, F32)

    def logits(head, t):
        side, rows, kind = tile(t)
        s = lax.dot_general(head_scratch[2 * head][side], k_ref[rows, lanes(head)], contract_last,
                            preferred_element_type=F32)
        b = bias_ref[head, kind]
        return jnp.concatenate([s[0:tq] + b, s[tq:2 * tq] + b], axis=0)

    def accumulate(head, t, slot):
        side, rows, _ = tile(t)
        head_scratch[2 * head + 1][side] += jnp.dot(jnp.exp2(s_s[slot]).astype(BF16), vv_s[head, rows, :],
                                                    preferred_element_type=F32)

    def finalize(head):
        for side in range(2):
            acc = head_scratch[2 * head + 1][side]
            _attn_finalize(acc[0:tq, :ATTN_V_DIM], acc[0:tq, ATTN_V_DIM:], acc[tq:2 * tq, :ATTN_V_DIM],
                           acc[tq:2 * tq, ATTN_V_DIM:], lam_ref, subg_ref,
                           o_ref.at[pl.ds(pl.multiple_of(blocks[side] * tq, tq), tq), lanes(head)])

    items = [(head, t) for head in range(HEADS_PER_STEP) for t in range(nq + 1)]
    s_s[0] = logits(*items[0])
    for k, (head, t) in enumerate(items):
        if k + 1 < len(items):
            s_s[(k + 1) % 2] = logits(*items[k + 1])
        accumulate(head, t, k % 2)
        if t == nq:
            finalize(head)


def _attn_online_kernel(it_ref, jt_ref, q_ref, k_ref, v_ref, bias_ref, lam_ref, subg_ref, o_ref,
                        q1_s, q2_s, m1_s, m2_s, l1_s, l2_s, acc1_s, acc2_s):
    step = pl.program_id(2)
    i = it_ref[step]
    j = jt_ref[step]

    @pl.when(j == 0)
    def _init():
        q1_s[...], q2_s[...] = _split_maps(q_ref[...])
        for m_s, l_s, acc_s in ((m1_s, l1_s, acc1_s), (m2_s, l2_s, acc2_s)):
            m_s[...] = jnp.full(m_s.shape, NEG_BIG, F32)
            l_s[...] = jnp.zeros(l_s.shape, F32)
            acc_s[...] = jnp.zeros(acc_s.shape, F32)

    def update(bias):
        k = k_ref[...]
        v = v_ref[...]
        contract_last = (((1,), (1,)), ((), ()))
        for q_s, m_s, l_s, acc_s in ((q1_s, m1_s, l1_s, acc1_s), (q2_s, m2_s, l2_s, acc2_s)):
            s = lax.dot_general(q_s[...], k, contract_last, preferred_element_type=F32)
            if bias is not None:
                s = s + bias_ref[bias]
            m_old = m_s[...]
            m_new = jnp.maximum(m_old, jnp.max(s, axis=-1, keepdims=True))
            alpha = jnp.exp2(m_old - m_new)
            p = jnp.exp2(s - m_new)
            l_s[...] = alpha * l_s[...] + jnp.sum(p, axis=-1, keepdims=True)
            acc_s[...] = alpha * acc_s[...] + jnp.dot(p.astype(BF16), v, preferred_element_type=F32)
            m_s[...] = m_new

    @pl.when(j == i)
    def _diag():
        update(0)

    @pl.when(j == i - 1)
    def _prev():
        update(1)

    @pl.when(j < i - 1)
    def _far():
        update(None)

    @pl.when(j == i)
    def _finalize():
        _attn_finalize(acc1_s[...], l1_s[...], acc2_s[...], l2_s[...], lam_ref, subg_ref, o_ref)


def _t5_bucket(dist):
    n = jnp.maximum(dist, 0)
    max_exact = NUM_BUCKETS // 2
    scaled = jnp.log(jnp.maximum(n, 1).astype(F32) / max_exact) / math.log(MAX_DISTANCE / max_exact)
    large = max_exact + (scaled * (NUM_BUCKETS - max_exact)).astype(jnp.int32)
    large = jnp.minimum(large, NUM_BUCKETS - 1)
    return jnp.where(n < max_exact, n, large)


def _bias_tiles(rel_bias, tile):
    blk = MAX_DISTANCE
    assert tile % blk == 0
    nb = tile // blk
    table = (rel_bias - rel_bias[NUM_BUCKETS - 1]).astype(F32) * LOG2E
    r = jnp.arange(blk, dtype=jnp.int32)
    d0 = r[:, None] - r[None, :]

    def lookup(dist):
        onehot = (_t5_bucket(dist)[..., None] == jnp.arange(NUM_BUCKETS, dtype=jnp.int32)).astype(F32)
        return jnp.einsum('qkn,nh->hqk', onehot, table, precision=lax.Precision.HIGHEST)

    on_diag = jnp.where(d0[None] >= 0, lookup(d0), NEG_BIG)
    sub_diag = lookup(d0 + blk)
    zeros = jnp.zeros_like(sub_diag)
    masked = jnp.full_like(sub_diag, NEG_BIG)

    def assemble(pick):
        return jnp.concatenate(
            [jnp.concatenate([pick(bi, bj) for bj in range(nb)], axis=-1) for bi in range(nb)], axis=-2)

    diag_tile = assemble(lambda bi, bj: on_diag if bi == bj else sub_diag if bi == bj + 1 else zeros if bi > bj else masked)
    prev_tile = assemble(lambda bi, bj: sub_diag if (bi == 0 and bj == nb - 1) else zeros)
    return jnp.stack([diag_tile, prev_tile, jnp.zeros_like(prev_tile)], axis=1)


def _attn_bounded(qn, kn, v, bias, lam_vecs, sub_g):
    b, s, _ = qn.shape
    tile = min(ATTN_TILE, s)
    nq = s // tile
    assert nq % 2 == 0
    g = HEADS_PER_STEP
    seq_map = lambda hg, bb, i: (bb, 0, hg)
    seq_spec = pl.BlockSpec((None, s, g * ATTN_V_DIM), seq_map)
    return pl.pallas_call(
        functools.partial(_attn_bounded_kernel, nq=nq),
        grid=(ATTN_HEADS // g, b, nq // 2),
        in_specs=[seq_spec, seq_spec, seq_spec,
                  pl.BlockSpec((g, 3, tile, tile), lambda hg, bb, i: (hg, 0, 0, 0), pipeline_mode=pl.Buffered(1)),
                  pl.BlockSpec((4, ATTN_HEAD_DIM), lambda hg, bb, i: (0, 0)),
                  pl.BlockSpec((1, ATTN_V_DIM), lambda hg, bb, i: (0, 0))],
        out_specs=seq_spec,
        out_shape=jax.ShapeDtypeStruct((b, s, V_COLS), BF16),
        scratch_shapes=[pltpu.VMEM((g, s, 2 * ATTN_V_DIM), BF16), pltpu.VMEM((2, 2 * tile, tile), F32)]
        + [pltpu.VMEM((2, 2 * tile, ATTN_V_DIM), BF16), pltpu.VMEM((2, 2 * tile, 2 * ATTN_V_DIM), F32)] * g,
        compiler_params=_params(("arbitrary", "arbitrary", "arbitrary")),
        name="diff_attn_bounded",
    )(qn, kn, v, bias, lam_vecs, sub_g)


def _attn_online(qn, kn, v, bias, lam_vecs, sub_g):
    b, s, _ = qn.shape
    tile = min(ATTN_TILE, s)
    nq = s // tile
    it = jnp.asarray([i for i in range(nq) for _ in range(i + 1)], jnp.int32)
    jt = jnp.asarray([j for i in range(nq) for j in range(i + 1)], jnp.int32)
    q_map = lambda bb, h, st, it_r, jt_r: (bb, it_r[st], h)
    kv_map = lambda bb, h, st, it_r, jt_r: (bb, jt_r[st], h)
    grid_spec = pltpu.PrefetchScalarGridSpec(
        num_scalar_prefetch=2,
        grid=(b, ATTN_HEADS, int(it.shape[0])),
        in_specs=[pl.BlockSpec((None, tile, ATTN_V_DIM), q_map),
                  pl.BlockSpec((None, tile, ATTN_V_DIM), kv_map),
                  pl.BlockSpec((None, tile, ATTN_V_DIM), kv_map),
                  pl.BlockSpec((None, 3, tile, tile), lambda bb, h, st, it_r, jt_r: (h, 0, 0, 0)),
                  pl.BlockSpec((4, ATTN_HEAD_DIM), lambda bb, h, st, it_r, jt_r: (0, 0)),
                  pl.BlockSpec((1, ATTN_V_DIM), lambda bb, h, st, it_r, jt_r: (0, 0))],
        out_specs=pl.BlockSpec((None, tile, ATTN_V_DIM), q_map),
        scratch_shapes=[pltpu.VMEM((tile, ATTN_V_DIM), BF16), pltpu.VMEM((tile, ATTN_V_DIM), BF16),
                        pltpu.VMEM((tile, 1), F32), pltpu.VMEM((tile, 1), F32),
                        pltpu.VMEM((tile, 1), F32), pltpu.VMEM((tile, 1), F32),
                        pltpu.VMEM((tile, ATTN_V_DIM), F32), pltpu.VMEM((tile, ATTN_V_DIM), F32)],
    )
    return pl.pallas_call(
        _attn_online_kernel,
        grid_spec=grid_spec,
        out_shape=jax.ShapeDtypeStruct((b, s, V_COLS), BF16),
        compiler_params=_params(("arbitrary", "arbitrary", "arbitrary")),
        name="diff_attn_online",
    )(it, jt, qn, kn, v, bias, lam_vecs, sub_g)


def _diff_attention(qn, kn, v, rel_bias, q_gain, k_gain, lam_vecs, sub_g):
    tile = min(ATTN_TILE, qn.shape[1])
    bias = _bias_tiles(rel_bias, tile)
    spread = jnp.max(jnp.abs(rel_bias - rel_bias[NUM_BUCKETS - 1]))
    bound = LOG2E * (1.05 * math.sqrt(ATTN_HEAD_DIM) * jnp.max(jnp.abs(q_gain)) * jnp.max(jnp.abs(k_gain)) + spread)
    args = (qn, kn, v, bias, lam_vecs, sub_g)
    return lax.cond(bound < EXP2_SAFE_BOUND, lambda a: _attn_bounded(*a), lambda a: _attn_online(*a), args)


def _ssd_kernel(z_ref, xbc_ref, dt_ref, acum_ref, acumt_ref, dskip_ref, ng_ref, y_ref, state_s):
    L = SSM_CHUNK

    @pl.when(pl.program_id(1) == 0)
    def _reset():
        state_s[...] = jnp.zeros(state_s.shape, F32)

    row = lax.broadcasted_iota(jnp.int32, (L, L), 0)
    col = lax.broadcasted_iota(jnp.int32, (L, L), 1)
    causal = _chunk_time(row) >= _chunk_time(col)
    unpick = jnp.where(row == _chunk_time(col), 1.0, 0.0).astype(BF16)
    lane = lax.broadcasted_iota(jnp.int32, (L, LANES), 1)
    low = lane < SSM_HEAD_DIM
    for c in range(z_ref.shape[0] // L):
        _ssd_chunk(slice(c * L, (c + 1) * L), z_ref, xbc_ref, dt_ref, acum_ref, acumt_ref, dskip_ref, ng_ref, y_ref,
                   state_s, causal, unpick, low)


def _ssd_chunk(rows, z_ref, xbc_ref, dt_ref, acum_ref, acumt_ref, dskip_ref, ng_ref, y_ref, state_s, causal, unpick,
               low):
    L = SSM_CHUNK
    xs = xbc_ref[rows, :D_INNER].astype(F32)
    bm = xbc_ref[rows, D_INNER:D_INNER + SSM_GROUPS * D_STATE]
    cm = xbc_ref[rows, D_INNER + SSM_GROUPS * D_STATE:]
    dt = dt_ref[rows, :]
    acum = acum_ref[rows, :]
    acum_t = acumt_ref[rows, :]
    a_end = acum[L - 1:L, :]

    def pair(lo, hi):
        return jnp.where(low[:lo.shape[0]], lo, hi)

    def spread(arr, h):
        return jnp.broadcast_to(arr[:, h:h + 1], (arr.shape[0], LANES))

    contract_last = (((1,), (1,)), ((), ()))
    contract_first = (((0,), (0,)), ((), ()))
    y_parts = []
    for g in range(SSM_GROUPS):
        bg = bm[:, g * D_STATE:(g + 1) * D_STATE]
        cg = cm[:, g * D_STATE:(g + 1) * D_STATE]
        cb = lax.dot_general(cg, bg, contract_last, preferred_element_type=F32)
        st = state_s[g]
        y_off = jnp.dot(cg, st.astype(BF16), preferred_element_type=F32)
        xw_parts = []
        dec_parts = []
        for pr in range(SSM_HEADS_PER_GROUP // 2):
            h0 = g * SSM_HEADS_PER_GROUP + 2 * pr
            ch = slice(h0 * SSM_HEAD_DIM, (h0 + 2) * SSM_HEAD_DIM)
            x_pair = xs[:, ch]
            acols = (spread(acum, h0), spread(acum, h0 + 1))
            acol = pair(*acols)
            aend = pair(spread(a_end, h0), spread(a_end, h0 + 1))
            xdt32 = x_pair * pair(spread(dt, h0), spread(dt, h0 + 1))
            xdt = xdt32.astype(BF16)
            yd = []
            for hh, ac in zip((h0, h0 + 1), acols):
                seg = ac - acum_t[hh:hh + 1, :]
                decay = jnp.exp2(jnp.where(causal, seg, NEG_BIG))
                yd.append(jnp.dot((cb * decay).astype(BF16), xdt, preferred_element_type=F32))
            y_diag = pair(yd[0], yd[1])
            off = y_off[:, 2 * pr * SSM_HEAD_DIM:(2 * pr + 2) * SSM_HEAD_DIM]
            y_parts.append(y_diag + off * jnp.exp2(acol))
            xw_parts.append(xdt32 * jnp.exp2(aend - acol))
            dec_parts.append(jnp.exp2(aend))
        xw = jnp.concatenate(xw_parts, axis=-1).astype(BF16)
        dec = jnp.concatenate(dec_parts, axis=-1)
        state_s[g] = st * dec + lax.dot_general(bg, xw, contract_first, preferred_element_type=F32)
    y = (jnp.concatenate(y_parts, axis=-1) + dskip_ref[...] * xs) * z_ref[rows, :].astype(F32)
    gsz = D_INNER // SSM_GROUPS
    for g in range(SSM_GROUPS):
        sl = slice(g * gsz, (g + 1) * gsz)
        yn = (_rms(y[:, sl], SSM_EPS) * ng_ref[:, sl]).astype(BF16)
        y_ref[rows, sl] = jnp.dot(unpick, yn, preferred_element_type=F32).astype(BF16)


def _ssd(z, xbc, dt, acum, acum_t, d_skip_ch, norm_g):
    b, s, _ = z.shape
    rows = SSD_CHUNKS_PER_STEP * SSM_CHUNK
    assert s % rows == 0
    blk = lambda bb, c: (bb, c, 0)
    fix = lambda bb, c: (0, 0)
    return pl.pallas_call(
        _ssd_kernel,
        grid=(b, s // rows),
        in_specs=[pl.BlockSpec((None, rows, D_INNER), blk), pl.BlockSpec((None, rows, CONV_DIM), blk)]
        + [pl.BlockSpec((None, rows, DT_PAD), blk)] * 3
        + [pl.BlockSpec((1, D_INNER), fix), pl.BlockSpec((1, D_INNER), fix)],
        out_specs=pl.BlockSpec((None, rows, D_INNER), blk),
        out_shape=jax.ShapeDtypeStruct((b, s, D_INNER), BF16),
        scratch_shapes=[pltpu.VMEM((SSM_GROUPS, D_STATE, SSM_HEADS_PER_GROUP * SSM_HEAD_DIM), F32)],
        compiler_params=_params(("arbitrary", "arbitrary")),
        name="ssd",
    )(z, xbc, dt, acum, acum_t, d_skip_ch, norm_g)


def _round_up(n, m):
    return (n + m - 1) // m * m


def _merge_kernel(x_ref, attn_ref, y_ref, gmix_ref, wg_ref, wao_ref, wso_ref, wout_ref, gffn_ref, wr_ref, br_ref,
                  x1_ref, h2_ref, pos_ref, topw_ref, cnt_ref):
    x = x_ref[...]
    tm = x.shape[0]
    h = (_rms(x, RMS_EPS) * gmix_ref[...]).astype(BF16)
    gates = _sigmoid(jnp.dot(h, wg_ref[...], preferred_element_type=F32))
    attn_out = jnp.dot(attn_ref[...], wao_ref[...], preferred_element_type=F32)
    ssm_out = jnp.dot(y_ref[...], wso_ref[...], preferred_element_type=F32)
    merged = gates[:, :D_MODEL] * attn_out + gates[:, D_MODEL:] * ssm_out
    x1 = x + jnp.dot(merged.astype(BF16), wout_ref[...], preferred_element_type=F32)
    x1_ref[...] = x1
    h2 = _rms(x1, RMS_EPS) * gffn_ref[...]
    h2_ref[...] = h2.astype(BF16)

    logits = lax.dot_general(wr_ref[...], h2, (((1,), (1,)), ((), ())), preferred_element_type=F32,
                             precision=lax.Precision.HIGHEST) + br_ref[...]
    eid = lax.broadcasted_iota(jnp.int32, logits.shape, 0)
    vals, hits = [], []
    member = jnp.zeros(logits.shape, F32)
    work = logits
    for kk in range(TOP_K):
        m = jnp.max(work, axis=0, keepdims=True)
        idx = jnp.min(jnp.where(work == m, eid, N_EXPERTS), axis=0, keepdims=True)
        hit = eid == idx
        vals.append(m)
        hits.append(hit)
        member = jnp.where(hit, 1.0, member)
        work = jnp.where(hit, -jnp.inf, work)
    ex = [jnp.exp(v - vals[0]) for v in vals]
    denom = ex[0] + ex[1] + ex[2] + ex[3]
    for kk in range(TOP_K):
        topw_ref[kk:kk + 1, :] = ex[kk] / denom

    r = lax.broadcasted_iota(jnp.int32, (tm, tm), 0)
    c = lax.broadcasted_iota(jnp.int32, (tm, tm), 1)
    before = jnp.where(r < c, 1.0, 0.0).astype(BF16)
    prefix = jnp.dot(member.astype(BF16), before, preferred_element_type=F32)
    cnt = jnp.sum(member, axis=1, keepdims=True).astype(jnp.int32)
    cnt_al = jnp.bitwise_and(cnt + (ROW_ALIGN - 1), -ROW_ALIGN)
    cnt_al = jnp.broadcast_to(cnt_al, (N_EXPERTS, LANES)).astype(F32)
    er = lax.broadcasted_iota(jnp.int32, (N_EXPERTS, N_EXPERTS), 0)
    ec = lax.broadcasted_iota(jnp.int32, (N_EXPERTS, N_EXPERTS), 1)
    lower = jnp.where(ec < er, 1.0, 0.0).astype(F32)
    run_start = jnp.dot(lower, cnt_al, preferred_element_type=F32, precision=lax.Precision.HIGHEST)[:, 0:1]
    base = prefix + run_start
    for kk in range(TOP_K):
        pos_ref[kk:kk + 1, :] = jnp.sum(jnp.where(hits[kk], base, 0.0), axis=0, keepdims=True).astype(jnp.int32)
    cnt_ref[...] = jnp.broadcast_to(cnt, cnt_ref.shape)


def _merge(x2, attn, y, g_mix, w_gate, w_ao, w_so, w_out, g_ffn, w_r_t, b_r):
    t = x2.shape[0]
    tm = min(ROW_TILE, t)
    row = lambda i: (i, 0)
    colb = lambda i: (0, i)
    fix = lambda i: (0, 0)
    return pl.pallas_call(
        _merge_kernel,
        grid=(t // tm,),
        in_specs=[pl.BlockSpec((tm, D_MODEL), row), pl.BlockSpec((tm, V_COLS), row), pl.BlockSpec((tm, D_INNER), row),
                  pl.BlockSpec((1, D_MODEL), fix), _resident((D_MODEL, 2 * D_MODEL)),
                  _resident((V_COLS, D_MODEL)), _resident((D_INNER, D_MODEL)),
                  _resident((D_MODEL, D_MODEL)), pl.BlockSpec((1, D_MODEL), fix),
                  pl.BlockSpec((N_EXPERTS, D_MODEL), fix), pl.BlockSpec((N_EXPERTS, 1), fix)],
        out_specs=[pl.BlockSpec((tm, D_MODEL), row), pl.BlockSpec((tm, D_MODEL), row),
                   pl.BlockSpec((TOP_K, tm), colb), pl.BlockSpec((TOP_K, tm), colb),
                   pl.BlockSpec((N_EXPERTS, LANES), row)],
        out_shape=[jax.ShapeDtypeStruct((t, D_MODEL), F32), jax.ShapeDtypeStruct((t, D_MODEL), BF16),
                   jax.ShapeDtypeStruct((TOP_K, t), jnp.int32), jax.ShapeDtypeStruct((TOP_K, t), F32),
                   jax.ShapeDtypeStruct((t // tm * N_EXPERTS, LANES), jnp.int32)],
        compiler_params=_params(("arbitrary",)),
        name="merge_router",
    )(x2, attn, y, g_mix, w_gate, w_ao, w_so, w_out, g_ffn, w_r_t, b_r)


def _sorted_rows(tm):
    return _round_up(TOP_K * tm + N_EXPERTS * (ROW_ALIGN - 1), LANES)


def _run_copies(n, max_n, vmem_ref, vmem_off, hbm_ref, hbm_off, sem, to_hbm, wait):
    done = 0
    pieces = [ROW_ALIGN << p for p in range((max_n // ROW_ALIGN).bit_length())]
    for bit in reversed(pieces):
        take = (n & bit) != 0

        @pl.when(take)
        def _piece(bit=bit, done=done):
            v = vmem_ref.at[pl.ds(pl.multiple_of(vmem_off + done, ROW_ALIGN), bit)]
            h = hbm_ref.at[pl.ds(pl.multiple_of(hbm_off + done, ROW_ALIGN), bit)]
            cp = pltpu.make_async_copy(v, h, sem) if to_hbm else pltpu.make_async_copy(h, v, sem)
            cp.wait() if wait else cp.start()

        done = done + jnp.where(take, bit, 0)


def _dispatch_kernel(off_ref, cnt_ref, start_ref, last_ref, h2_ref, pos_ref, xs_hbm, buf_s, zero_s, sems, zsem):
    i = pl.program_id(0)
    tm = h2_ref.shape[0]
    rows = buf_s.shape[1]
    slot = i & 1

    @pl.when(i == 0)
    def _zero_last_blocks():
        zero_s[...] = jnp.zeros(zero_s.shape, BF16)
        for e in range(N_EXPERTS):
            start = pl.multiple_of(last_ref[e], MOE_TILE)
            pltpu.make_async_copy(zero_s, xs_hbm.at[pl.ds(start, MOE_TILE)], zsem).start()
        for e in range(N_EXPERTS):
            pltpu.make_async_copy(zero_s, xs_hbm.at[pl.ds(0, MOE_TILE)], zsem).wait()

        def slack(wait):
            def body(blk, carry):
                cp = pltpu.make_async_copy(zero_s, xs_hbm.at[pl.ds(pl.multiple_of(blk * MOE_TILE, MOE_TILE), MOE_TILE)],
                                           zsem)
                cp.wait() if wait else cp.start()
                return carry
            lax.fori_loop(last_ref[N_EXPERTS], xs_hbm.shape[0] // MOE_TILE, body, 0)

        slack(False)
        slack(True)

    pos = pos_ref[...]
    rid = lax.broadcasted_iota(jnp.int32, (rows, tm), 0)
    sel = jnp.zeros((rows, tm), F32)
    for kk in range(TOP_K):
        sel = sel + jnp.where(rid == pos[kk:kk + 1, :], 1.0, 0.0)
    buf_s[slot] = jnp.dot(sel.astype(BF16), h2_ref[...], preferred_element_type=F32).astype(BF16)

    def copies(tile, tile_slot, wait):
        for e in range(N_EXPERTS):
            idx = tile * N_EXPERTS + e
            _run_copies(cnt_ref[idx], tm, buf_s.at[tile_slot], start_ref[idx], xs_hbm, off_ref[idx],
                        sems.at[tile_slot], True, wait)

    copies(i, slot, False)

    @pl.when(i >= 1)
    def _previous_tile_done():
        copies(i - 1, 1 - slot, True)

    @pl.when(i == pl.num_programs(0) - 1)
    def _last_tile_done():
        copies(i, slot, True)


def _dispatch(run_off, run_cnt, run_start, last_block, h2, pos, n_rows):
    t = h2.shape[0]
    tm = min(ROW_TILE, t)
    grid_spec = pltpu.PrefetchScalarGridSpec(
        num_scalar_prefetch=4,
        grid=(t // tm,),
        in_specs=[pl.BlockSpec((tm, D_MODEL), lambda i, *_: (i, 0)),
                  pl.BlockSpec((TOP_K, tm), lambda i, *_: (0, i))],
        out_specs=pl.BlockSpec(memory_space=pl.ANY),
        scratch_shapes=[pltpu.VMEM((2, _sorted_rows(tm), D_MODEL), BF16), pltpu.VMEM((MOE_TILE, D_MODEL), BF16),
                        pltpu.SemaphoreType.DMA((2,)), pltpu.SemaphoreType.DMA(())],
    )
    return pl.pallas_call(
        _dispatch_kernel,
        grid_spec=grid_spec,
        out_shape=jax.ShapeDtypeStruct((n_rows, D_MODEL), BF16),
        compiler_params=_params(("arbitrary",)),
        name="moe_dispatch",
    )(run_off, run_cnt, run_start, last_block, h2, pos)


def _expert_kernel(be_ref, nb_ref, x_ref, w1_ref, b1_ref, w2_ref, b2_ref, y_ref, w1_s, w2_s):
    i = pl.program_id(0)

    @pl.when(jnp.logical_or(i == 0, be_ref[i] != be_ref[jnp.maximum(i - 1, 0)]))
    def _new_expert():
        w1_s[...] = w1_ref[...].astype(BF16)
        w2_s[...] = w2_ref[...].astype(BF16)

    @pl.when(i < nb_ref[0])
    def _run():
        gu = jnp.dot(x_ref[...], w1_s[...], preferred_element_type=F32) + b1_ref[...]
        gate = jnp.minimum(gu[:, :D_EXPERT], SWIGLU_LIMIT)
        up = jnp.clip(gu[:, D_EXPERT:], -SWIGLU_LIMIT, SWIGLU_LIMIT)
        act = (up + 1.0) * (gate * _sigmoid(SWIGLU_ALPHA * gate))
        y = jnp.dot(act.astype(BF16), w2_s[...], preferred_element_type=F32) + b2_ref[...]
        y_ref[...] = y.astype(BF16)

    @pl.when(i >= nb_ref[0])
    def _slack():
        y_ref[...] = jnp.zeros(y_ref.shape, BF16)


def _experts(block_e, n_used, xs, w1, b1, w2, b2):
    n_rows = xs.shape[0]
    nb = n_rows // MOE_TILE
    row_in = lambda i, be, nu: (jnp.minimum(i, nu[0] - 1), 0)
    row = lambda i, be, nu: (i, 0)
    wsel = lambda i, be, nu: (be[i], 0, 0)
    grid_spec = pltpu.PrefetchScalarGridSpec(
        num_scalar_prefetch=2,
        grid=(nb,),
        in_specs=[pl.BlockSpec((MOE_TILE, D_MODEL), row_in),
                  pl.BlockSpec((None, D_MODEL, 2 * D_EXPERT), wsel), pl.BlockSpec((None, 1, 2 * D_EXPERT), wsel),
                  pl.BlockSpec((None, D_EXPERT, D_MODEL), wsel), pl.BlockSpec((None, 1, D_MODEL), wsel)],
        out_specs=pl.BlockSpec((MOE_TILE, D_MODEL), row),
        scratch_shapes=[pltpu.VMEM((D_MODEL, 2 * D_EXPERT), BF16), pltpu.VMEM((D_EXPERT, D_MODEL), BF16)],
    )
    return pl.pallas_call(
        _expert_kernel,
        grid_spec=grid_spec,
        out_shape=jax.ShapeDtypeStruct((n_rows, D_MODEL), BF16),
        compiler_params=_params(("arbitrary",)),
        name="moe_experts",
    )(block_e, n_used, xs, w1, b1, w2, b2)


def _combine_kernel(off_ref, cnt_ref, start_ref, y_hbm, x1_ref, pos_ref, w_ref, o_ref, buf_s, sems):
    i = pl.program_id(0)
    tm = x1_ref.shape[0]
    rows = buf_s.shape[1]
    slot = i & 1

    def copies(tile, tile_slot, wait):
        for e in range(N_EXPERTS):
            idx = tile * N_EXPERTS + e
            _run_copies(cnt_ref[idx], tm, buf_s.at[tile_slot], start_ref[idx], y_hbm, off_ref[idx],
                        sems.at[tile_slot], False, wait)

    @pl.when(i == 0)
    def _first_tile():
        buf_s[...] = jnp.zeros(buf_s.shape, BF16)
        copies(0, 0, False)

    @pl.when(i + 1 < pl.num_programs(0))
    def _prefetch_next_tile():
        copies(i + 1, 1 - slot, False)

    pos = pos_ref[...]
    w = w_ref[...]
    cid = lax.broadcasted_iota(jnp.int32, (tm, rows), 1)
    wsel = jnp.zeros((tm, rows), F32)
    for kk in range(TOP_K):
        wsel = wsel + jnp.where(cid == pos[:, kk:kk + 1], w[:, kk:kk + 1], 0.0)
    copies(i, slot, True)
    o_ref[...] = x1_ref[...] + jnp.dot(wsel.astype(BF16), buf_s[slot], preferred_element_type=F32)


def _combine(run_off, run_cnt, run_start, y, x1, pos_tok, w_tok):
    t = x1.shape[0]
    tm = min(ROW_TILE, t)
    grid_spec = pltpu.PrefetchScalarGridSpec(
        num_scalar_prefetch=3,
        grid=(t // tm,),
        in_specs=[pl.BlockSpec(memory_space=pl.ANY),
                  pl.BlockSpec((tm, D_MODEL), lambda i, *_: (i, 0)),
                  pl.BlockSpec((tm, TOP_K), lambda i, *_: (i, 0)), pl.BlockSpec((tm, TOP_K), lambda i, *_: (i, 0))],
        out_specs=pl.BlockSpec((tm, D_MODEL), lambda i, *_: (i, 0)),
        scratch_shapes=[pltpu.VMEM((2, _sorted_rows(tm), D_MODEL), BF16), pltpu.SemaphoreType.DMA((2,))],
    )
    return pl.pallas_call(
        _combine_kernel,
        grid_spec=grid_spec,
        out_shape=jax.ShapeDtypeStruct((t, D_MODEL), F32),
        compiler_params=_params(("arbitrary",)),
        name="moe_combine",
    )(run_off, run_cnt, run_start, y, x1, pos_tok, w_tok)


def kernel(x, g_mix, w_in, q_norm_g, k_norm_g, lambda_q1, lambda_k1, lambda_q2, lambda_k2, attn_sub_g, rel_bias,
           w_attn_o, conv_w, conv_b, dt_bias, a_log, d_skip, ssm_norm_g, w_ssm_o, w_out, g_ffn, w_router, b_router,
           w1, b1, w2, b2):
    b, s, d = x.shape
    t = b * s
    l = 0
    x2 = x.reshape(t, d)

    w = w_in[l]
    c0 = Q_COLS + K_COLS + V_COLS
    c1 = c0 + D_INNER + CONV_DIM
    w_qkv = w[:, :c0].astype(BF16)
    w_dt = jnp.pad(w[:, c1:c1 + SSM_HEADS], ((0, 0), (0, DT_PAD - SSM_HEADS)))
    w_ssm = jnp.concatenate([w[:, c0:c1], w_dt], axis=1).astype(BF16)
    w_gate = w[:, c1 + SSM_HEADS:].astype(BF16)
    n_hd = Q_COLS // ATTN_HEAD_DIM
    gq = (jnp.tile(q_norm_g[l], n_hd) * (ATTN_HEAD_DIM ** -0.5 * LOG2E)).reshape(1, Q_COLS)
    gk = jnp.tile(k_norm_g[l], n_hd).reshape(1, K_COLS)
    gm = g_mix[l].reshape(1, d)

    qn, kn, v = _qkv_proj(x2, gm, w_qkv, gq, gk)
    pad_h = (0, DT_PAD - SSM_HEADS)
    z, xbc, dt, acum, acum_t = _ssm_proj(x2, gm, w_ssm, conv_w[l], conv_b[l].reshape(1, -1),
                                         jnp.pad(dt_bias[l], pad_h).reshape(1, -1),
                                         jnp.pad(a_log[l], pad_h).reshape(1, -1), s)

    lam_vecs = jnp.stack([lambda_q1[l], lambda_k1[l], lambda_q2[l], lambda_k2[l]]).astype(F32)
    attn = _diff_attention(qn.reshape(b, s, -1), kn.reshape(b, s, -1), v.reshape(b, s, -1), rel_bias, q_norm_g[l],
                           k_norm_g[l], lam_vecs, attn_sub_g[l].reshape(1, ATTN_V_DIM))

    y = _ssd(z.reshape(b, s, -1), xbc.reshape(b, s, -1), dt.reshape(b, s, -1), acum.reshape(b, s, -1),
             acum_t.reshape(b, s, -1), jnp.repeat(d_skip[l], SSM_HEAD_DIM).reshape(1, -1),
             ssm_norm_g[l].reshape(1, -1))

    x1, h2, pos, top_w, tile_cnt = _merge(
        x2, attn.reshape(t, -1), y.reshape(t, -1), gm, w_gate, w_attn_o[l].astype(BF16), w_ssm_o[l].astype(BF16),
        w_out[l].astype(BF16), g_ffn[l].reshape(1, d), w_router[l].T, b_router[l].reshape(-1, 1))

    n_tiles = t // min(ROW_TILE, t)
    cnt = tile_cnt.reshape(n_tiles, N_EXPERTS, LANES)[:, :, 0]
    cnt_al = _round_up(cnt, ROW_ALIGN)
    run_start = jnp.cumsum(cnt_al, axis=1) - cnt_al
    padded = _round_up(jnp.sum(cnt_al, axis=0), MOE_TILE)
    end_pad = jnp.cumsum(padded)
    run_off = (end_pad - padded)[None, :] + jnp.cumsum(cnt_al, axis=0) - cnt_al
    n_rows = _round_up(t * TOP_K + n_tiles * N_EXPERTS * (ROW_ALIGN - 1), MOE_TILE) + N_EXPERTS * MOE_TILE
    block_start = jnp.arange(n_rows // MOE_TILE, dtype=jnp.int32) * MOE_TILE
    block_e = jnp.minimum(jnp.sum(block_start[:, None] >= end_pad[None, :], axis=1), N_EXPERTS - 1).astype(jnp.int32)
    n_used = (end_pad[-1:] // MOE_TILE).astype(jnp.int32)
    last_block = jnp.concatenate([jnp.maximum(end_pad - MOE_TILE, 0), end_pad[-1:] // MOE_TILE]).astype(jnp.int32)
    tables = [a.reshape(-1).astype(jnp.int32) for a in (run_off, cnt_al, run_start)]

    xs = _dispatch(*tables, last_block, h2, pos, n_rows)
    ys = _experts(block_e, n_used, xs, w1[l], b1[l][:, None, :], w2[l], b2[l][:, None, :])
    out = _combine(*tables, ys, x1, pos.T, top_w.T)
    return out.reshape(b, s, d)
```

```python
import functools
import math

import jax
import jax.numpy as jnp
from jax import lax
from jax.experimental import pallas as pl
from jax.experimental.pallas import tpu as pltpu

F32 = jnp.float32
BF16 = jnp.bfloat16

D_MODEL = 1024
ATTN_HEADS = 8
ATTN_HEAD_DIM = 64
ATTN_V_DIM = 2 * ATTN_HEAD_DIM
LAMBDA_INIT = 0.8 - 0.6 * math.exp(-0.3 * 0)
NUM_BUCKETS = 32
MAX_DISTANCE = 128
D_INNER = 2 * D_MODEL
SSM_HEAD_DIM = 64
SSM_HEADS = D_INNER // SSM_HEAD_DIM
SSM_GROUPS = 8
SSM_HEADS_PER_GROUP = SSM_HEADS // SSM_GROUPS
D_STATE = 128
CONV_WIDTH = 4
SSM_CHUNK = 128
CONV_DIM = D_INNER + 2 * SSM_GROUPS * D_STATE
N_EXPERTS = 32
TOP_K = 4
D_EXPERT = D_MODEL
SWIGLU_LIMIT = 7.0
SWIGLU_ALPHA = 1.702
RMS_EPS = 1e-6
SSM_EPS = 1e-5
Q_COLS = ATTN_HEADS * 2 * ATTN_HEAD_DIM
K_COLS = Q_COLS
V_COLS = ATTN_HEADS * ATTN_V_DIM

LANES = 128
MXU_DIM = 256
DT_PAD = LANES
NEG_BIG = -1e30
LOG2E = math.log2(math.e)
EXP2_SAFE_BOUND = 80.0
VMEM_LIMIT = 56 * 1024 * 1024

ROW_TILE = 512
ATTN_TILE = 512
HEADS_PER_STEP = 2
MOE_TILE = 512
ROW_ALIGN = 16
SSD_CHUNKS_PER_STEP = 4


def _rms(x, eps):
    return x * lax.rsqrt(jnp.mean(x * x, axis=-1, keepdims=True) + eps)


def _sigmoid(x):
    return 1.0 / (1.0 + jnp.exp(-x))


def _params(sem):
    return pltpu.CompilerParams(dimension_semantics=sem, vmem_limit_bytes=VMEM_LIMIT)


def _resident(shape):
    return pl.BlockSpec(shape, lambda *_: (0,) * len(shape), pipeline_mode=pl.Buffered(1))


def _qkv_kernel(x_ref, g_ref, w_ref, gq_ref, gk_ref, q_ref, k_ref, v_ref):
    h = (_rms(x_ref[...], RMS_EPS) * g_ref[...]).astype(BF16)
    qkv = jnp.dot(h, w_ref[...], preferred_element_type=F32)
    r = lax.broadcasted_iota(jnp.int32, (MXU_DIM, MXU_DIM), 0) // ATTN_HEAD_DIM
    c = lax.broadcasted_iota(jnp.int32, (MXU_DIM, MXU_DIM), 1) // ATTN_HEAD_DIM
    group_ones = jnp.where(r == c, 1.0, 0.0).astype(BF16)

    def head_norm(t, gain_ref, out_ref):
        for cc in range(Q_COLS // MXU_DIM):
            sl = slice(cc * MXU_DIM, (cc + 1) * MXU_DIM)
            tc = t[:, sl]
            ss = jnp.dot((tc * tc).astype(BF16), group_ones, preferred_element_type=F32)
            out_ref[:, sl] = (tc * lax.rsqrt(ss * (1.0 / ATTN_HEAD_DIM) + RMS_EPS) * gain_ref[:, sl]).astype(BF16)

    head_norm(qkv[:, :Q_COLS], gq_ref, q_ref)
    head_norm(qkv[:, Q_COLS:Q_COLS + K_COLS], gk_ref, k_ref)
    v_ref[...] = qkv[:, Q_COLS + K_COLS:].astype(BF16)


def _qkv_proj(x2, g_mix, w_qkv, gq, gk):
    t = x2.shape[0]
    tm = min(ROW_TILE, t)
    row = lambda i: (i, 0)
    fix = lambda i: (0, 0)
    out = jax.ShapeDtypeStruct((t, D_MODEL), BF16)
    return pl.pallas_call(
        _qkv_kernel,
        grid=(t // tm,),
        in_specs=[pl.BlockSpec((tm, D_MODEL), row), pl.BlockSpec((1, D_MODEL), fix),
                  _resident((D_MODEL, 3 * D_MODEL)), pl.BlockSpec((1, D_MODEL), fix),
                  pl.BlockSpec((1, D_MODEL), fix)],
        out_specs=[pl.BlockSpec((tm, D_MODEL), row)] * 3,
        out_shape=[out, out, out],
        compiler_params=_params(("arbitrary",)),
        name="qkv_proj",
    )(x2, g_mix, w_qkv, gq, gk)


CONV_COLS = 1024
SUBLANES = 8
CHUNK_VREGS = SSM_CHUNK // SUBLANES


def _chunk_time(row):
    return (row >> 3) + CHUNK_VREGS * (row & (SUBLANES - 1))


def _ssm_proj_kernel(x_ref, g_ref, w_ref, cw_ref, cb_ref, dtb_ref, alog_ref, z_ref, xbc_ref, dt_ref, acum_ref,
                     acumt_ref, tail_s, h_s, p_s, *, tiles_per_seq):
    tm = x_ref.shape[0]
    nb = tm // SSM_CHUNK
    taps = CONV_WIDTH - 1

    @pl.when(pl.program_id(0) % tiles_per_seq == 0)
    def _sequence_start():
        tail_s[...] = jnp.zeros(tail_s.shape, F32)

    h = (_rms(x_ref[...], RMS_EPS) * g_ref[...]).astype(BF16)
    rr = lax.broadcasted_iota(jnp.int32, (SSM_CHUNK, SSM_CHUNK), 0)
    cc = lax.broadcasted_iota(jnp.int32, (SSM_CHUNK, SSM_CHUNK), 1)
    pick = jnp.where(cc == _chunk_time(rr), 1.0, 0.0).astype(BF16)
    h = jnp.concatenate(
        [jnp.dot(pick, h[b * SSM_CHUNK:(b + 1) * SSM_CHUNK], preferred_element_type=F32).astype(BF16)
         for b in range(nb)], axis=0)

    h_s[...] = h

    def project(stage):
        return jnp.dot(h_s[...], w_ref[:, stage * CONV_COLS:(stage + 1) * CONV_COLS], preferred_element_type=F32)

    z_stages = D_INNER // CONV_COLS
    n_stages = z_stages + CONV_DIM // CONV_COLS
    sub = lax.broadcasted_iota(jnp.int32, (nb * taps, SUBLANES, CONV_COLS), 1)
    p_s[0] = project(0)
    for stage in range(n_stages):
        if stage + 1 < n_stages:
            p_s[(stage + 1) % 2] = project(stage + 1)
        p = p_s[stage % 2]
        if stage < z_stages:
            z_ref[:, stage * CONV_COLS:(stage + 1) * CONV_COLS] = (p * _sigmoid(p)).astype(BF16)
            continue
        c = stage - z_stages
        cols = slice(c * CONV_COLS, (c + 1) * CONV_COLS)
        p4 = p.reshape(nb, CHUNK_VREGS, SUBLANES, CONV_COLS)
        last = p4[:, CHUNK_VREGS - taps:]
        seq = jnp.concatenate([tail_s[c][None], last], axis=0).reshape((nb + 1) * taps, SUBLANES, CONV_COLS)
        tail_s[c] = last[nb - 1]
        rolled = pltpu.roll(seq, 1, axis=1)
        wrapped = jnp.where(sub == 0, rolled[:nb * taps], rolled[taps:]).reshape(nb, taps, SUBLANES, CONV_COLS)
        conv = cb_ref[:, cols] + cw_ref[taps:taps + 1, cols] * p4
        for back in range(1, CONV_WIDTH):
            shifted = jnp.concatenate([wrapped[:, taps - back:], p4[:, :CHUNK_VREGS - back]], axis=1)
            conv = conv + cw_ref[taps - back:taps - back + 1, cols] * shifted
        xbc_ref[:, cols] = (conv * _sigmoid(conv)).reshape(tm, CONV_COLS).astype(BF16)
    dtl = jnp.dot(h_s[...], w_ref[:, D_INNER + CONV_DIM:], preferred_element_type=F32) + dtb_ref[...]
    dt = jnp.maximum(dtl, 0.0) + jnp.log(1.0 + jnp.exp(-jnp.abs(dtl)))
    dt_ref[...] = dt
    a = dt * (-jnp.exp(alog_ref[...]) * LOG2E)
    upto = jnp.where(_chunk_time(rr) >= _chunk_time(cc), 1.0, 0.0).astype(F32)
    for b in range(nb):
        rows = slice(b * SSM_CHUNK, (b + 1) * SSM_CHUNK)
        acum = jnp.dot(upto, a[rows], preferred_element_type=F32, precision=lax.Precision.HIGHEST)
        acum_ref[rows, :] = acum
        acumt_ref[rows, :] = acum.T


def _ssm_proj(x2, g_mix, w_ssm, conv_w, conv_b, dt_bias, a_log, seq_len):
    t = x2.shape[0]
    tm = min(ROW_TILE, seq_len)
    assert seq_len % tm == 0
    ncol = D_INNER + CONV_DIM + DT_PAD
    row = lambda i: (i, 0)
    fix = lambda i: (0, 0)
    heads = jax.ShapeDtypeStruct((t, DT_PAD), F32)
    return pl.pallas_call(
        functools.partial(_ssm_proj_kernel, tiles_per_seq=seq_len // tm),
        grid=(t // tm,),
        in_specs=[pl.BlockSpec((tm, D_MODEL), row), pl.BlockSpec((1, D_MODEL), fix),
                  _resident((D_MODEL, ncol)), pl.BlockSpec((CONV_WIDTH, CONV_DIM), fix),
                  pl.BlockSpec((1, CONV_DIM), fix), pl.BlockSpec((1, DT_PAD), fix), pl.BlockSpec((1, DT_PAD), fix)],
        out_specs=[pl.BlockSpec((tm, D_INNER), row), pl.BlockSpec((tm, CONV_DIM), row)]
        + [pl.BlockSpec((tm, DT_PAD), row)] * 3,
        out_shape=[jax.ShapeDtypeStruct((t, D_INNER), BF16), jax.ShapeDtypeStruct((t, CONV_DIM), BF16),
                   heads, heads, heads],
        scratch_shapes=[pltpu.VMEM((CONV_DIM // CONV_COLS, CONV_WIDTH - 1, SUBLANES, CONV_COLS), F32),
                        pltpu.VMEM((tm, D_MODEL), BF16), pltpu.VMEM((2, tm, CONV_COLS), F32)],
        compiler_params=_params(("arbitrary",)),
        name="ssm_proj",
    )(x2, g_mix, w_ssm, conv_w, conv_b, dt_bias, a_log)


def _split_maps(q):
    lane = lax.broadcasted_iota(jnp.int32, q.shape, 1)
    zero = jnp.zeros_like(q)
    return jnp.where(lane < ATTN_HEAD_DIM, q, zero), jnp.where(lane >= ATTN_HEAD_DIM, q, zero)


def _attn_finalize(acc1, l1, acc2, l2, lam_ref, subg_ref, o_ref):
    lam_v = lam_ref[...]
    lam = (jnp.exp(jnp.sum(lam_v[0:1] * lam_v[1:2], axis=-1, keepdims=True))
           - jnp.exp(jnp.sum(lam_v[2:3] * lam_v[3:4], axis=-1, keepdims=True)) + LAMBDA_INIT)
    o = acc1 / l1 - lam * (acc2 / l2)
    o_ref[...] = (_rms(o, RMS_EPS) * subg_ref[...] * (1.0 - LAMBDA_INIT)).astype(BF16)


def _attn_bounded_kernel(q_ref, k_ref, v_ref, bias_ref, lam_ref, subg_ref, o_ref, vv_s, s_s, *head_scratch, nq):
    pair_id = pl.program_id(2)
    tq = s_s.shape[2]
    tk = tq
    blocks = (pair_id, nq - 1 - pair_id)
    contract_last = (((1,), (1,)), ((), ()))

    def tile(t):
        first = t <= pair_id
        side = jnp.where(first, 0, 1)
        j = jnp.where(first, t, t - pair_id - 1)
        diag = jnp.where(first, blocks[0], blocks[1])
        kind = jnp.where(j == diag, 0, jnp.where(j == diag - 1, 1, 2))
        return side, pl.ds(pl.multiple_of(j * tk, tk), tk), kind

    def lanes(head):
        return slice(head * ATTN_V_DIM, (head + 1) * ATTN_V_DIM)

    @pl.when(pair_id == 0)
    def _extend_v():
        for head in range(HEADS_PER_STEP):
            vv_s[head, :, :ATTN_V_DIM] = v_ref[:, lanes(head)]
            vv_s[head, :, ATTN_V_DIM:] = jnp.ones((vv_s.shape[1], ATTN_V_DIM), BF16)

    for head in range(HEADS_PER_STEP):
        qq_s, acc_s = head_scratch[2 * head:2 * head + 2]
        for side in range(2):
            q1, q2 = _split_maps(q_ref[pl.ds(pl.multiple_of(blocks[side] * tq, tq), tq), lanes(head)])
            qq_s[side, 0:tq, :] = q1
            qq_s[side, tq:2 * tq, :] = q2
        acc_s[...] = jnp.zeros(acc_s.shape, F32)

    def logits(head, t):
        side, rows, kind = tile(t)
        s = lax.dot_general(head_scratch[2 * head][side], k_ref[rows, lanes(head)], contract_last,
                            preferred_element_type=F32)
        b = bias_ref[head, kind]
        return jnp.concatenate([s[0:tq] + b, s[tq:2 * tq] + b], axis=0)

    def accumulate(head, t, slot):
        side, rows, _ = tile(t)
        head_scratch[2 * head + 1][side] += jnp.dot(jnp.exp2(s_s[slot]).astype(BF16), vv_s[head, rows, :],
                                                    preferred_element_type=F32)

    def finalize(head):
        for side in range(2):
            acc = head_scratch[2 * head + 1][side]
            _attn_finalize(acc[0:tq, :ATTN_V_DIM], acc[0:tq, ATTN_V_DIM:], acc[tq:2 * tq, :ATTN_V_DIM],
                           acc[tq:2 * tq, ATTN_V_DIM:], lam_ref, subg_ref,
                           o_ref.at[pl.ds(pl.multiple_of(blocks[side] * tq, tq), tq), lanes(head)])

    items = [(head, t) for head in range(HEADS_PER_STEP) for t in range(nq + 1)]
    s_s[0] = logits(*items[0])
    for k, (head, t) in enumerate(items):
        if k + 1 < len(items):
            s_s[(k + 1) % 2] = logits(*items[k + 1])
        accumulate(head, t, k % 2)
        if t == nq:
            finalize(head)


def _attn_online_kernel(it_ref, jt_ref, q_ref, k_ref, v_ref, bias_ref, lam_ref, subg_ref, o_ref,
                        q1_s, q2_s, m1_s, m2_s, l1_s, l2_s, acc1_s, acc2_s):
    step = pl.program_id(2)
    i = it_ref[step]
    j = jt_ref[step]

    @pl.when(j == 0)
    def _init():
        q1_s[...], q2_s[...] = _split_maps(q_ref[...])
        for m_s, l_s, acc_s in ((m1_s, l1_s, acc1_s), (m2_s, l2_s, acc2_s)):
            m_s[...] = jnp.full(m_s.shape, NEG_BIG, F32)
            l_s[...] = jnp.zeros(l_s.shape, F32)
            acc_s[...] = jnp.zeros(acc_s.shape, F32)

    def update(bias):
        k = k_ref[...]
        v = v_ref[...]
        contract_last = (((1,), (1,)), ((), ()))
        for q_s, m_s, l_s, acc_s in ((q1_s, m1_s, l1_s, acc1_s), (q2_s, m2_s, l2_s, acc2_s)):
            s = lax.dot_general(q_s[...], k, contract_last, preferred_element_type=F32)
            if bias is not None:
                s = s + bias_ref[bias]
            m_old = m_s[...]
            m_new = jnp.maximum(m_old, jnp.max(s, axis=-1, keepdims=True))
            alpha = jnp.exp2(m_old - m_new)
            p = jnp.exp2(s - m_new)
            l_s[...] = alpha * l_s[...] + jnp.sum(p, axis=-1, keepdims=True)
            acc_s[...] = alpha * acc_s[...] + jnp.dot(p.astype(BF16), v, preferred_element_type=F32)
            m_s[...] = m_new

    @pl.when(j == i)
    def _diag():
        update(0)

    @pl.when(j == i - 1)
    def _prev():
        update(1)

    @pl.when(j < i - 1)
    def _far():
        update(None)

    @pl.when(j == i)
    def _finalize():
        _attn_finalize(acc1_s[...], l1_s[...], acc2_s[...], l2_s[...], lam_ref, subg_ref, o_ref)


def _t5_bucket(dist):
    n = jnp.maximum(dist, 0)
    max_exact = NUM_BUCKETS // 2
    scaled = jnp.log(jnp.maximum(n, 1).astype(F32) / max_exact) / math.log(MAX_DISTANCE / max_exact)
    large = max_exact + (scaled * (NUM_BUCKETS - max_exact)).astype(jnp.int32)
    large = jnp.minimum(large, NUM_BUCKETS - 1)
    return jnp.where(n < max_exact, n, large)


def _bias_tiles(rel_bias, tile):
    blk = MAX_DISTANCE
    assert tile % blk == 0
    nb = tile // blk
    table = (rel_bias - rel_bias[NUM_BUCKETS - 1]).astype(F32) * LOG2E
    r = jnp.arange(blk, dtype=jnp.int32)
    d0 = r[:, None] - r[None, :]

    def lookup(dist):
        onehot = (_t5_bucket(dist)[..., None] == jnp.arange(NUM_BUCKETS, dtype=jnp.int32)).astype(F32)
        return jnp.einsum('qkn,nh->hqk', onehot, table, precision=lax.Precision.HIGHEST)

    on_diag = jnp.where(d0[None] >= 0, lookup(d0), NEG_BIG)
    sub_diag = lookup(d0 + blk)
    zeros = jnp.zeros_like(sub_diag)
    masked = jnp.full_like(sub_diag, NEG_BIG)

    def assemble(pick):
        return jnp.concatenate(
            [jnp.concatenate([pick(bi, bj) for bj in range(nb)], axis=-1) for bi in range(nb)], axis=-2)

    diag_tile = assemble(lambda bi, bj: on_diag if bi == bj else sub_diag if bi == bj + 1 else zeros if bi > bj else masked)
    prev_tile = assemble(lambda bi, bj: sub_diag if (bi == 0 and bj == nb - 1) else zeros)
    return jnp.stack([diag_tile, prev_tile, jnp.zeros_like(prev_tile)], axis=1)


def _attn_bounded(qn, kn, v, bias, lam_vecs, sub_g):
    b, s, _ = qn.shape
    tile = min(ATTN_TILE, s)
    nq = s // tile
    assert nq % 2 == 0
    g = HEADS_PER_STEP
    seq_map = lambda hg, bb, i: (bb, 0, hg)
    seq_spec = pl.BlockSpec((None, s, g * ATTN_V_DIM), seq_map)
    return pl.pallas_call(
        functools.partial(_attn_bounded_kernel, nq=nq),
        grid=(ATTN_HEADS // g, b, nq // 2),
        in_specs=[seq_spec, seq_spec, seq_spec,
                  pl.BlockSpec((g, 3, tile, tile), lambda hg, bb, i: (hg, 0, 0, 0), pipeline_mode=pl.Buffered(1)),
                  pl.BlockSpec((4, ATTN_HEAD_DIM), lambda hg, bb, i: (0, 0)),
                  pl.BlockSpec((1, ATTN_V_DIM), lambda hg, bb, i: (0, 0))],
        out_specs=seq_spec,
        out_shape=jax.ShapeDtypeStruct((b, s, V_COLS), BF16),
        scratch_shapes=[pltpu.VMEM((g, s, 2 * ATTN_V_DIM), BF16), pltpu.VMEM((2, 2 * tile, tile), F32)]
        + [pltpu.VMEM((2, 2 * tile, ATTN_V_DIM), BF16), pltpu.VMEM((2, 2 * tile, 2 * ATTN_V_DIM), F32)] * g,
        compiler_params=_params(("arbitrary", "arbitrary", "arbitrary")),
        name="diff_attn_bounded",
    )(qn, kn, v, bias, lam_vecs, sub_g)


def _attn_online(qn, kn, v, bias, lam_vecs, sub_g):
    b, s, _ = qn.shape
    tile = min(ATTN_TILE, s)
    nq = s // tile
    it = jnp.asarray([i for i in range(nq) for _ in range(i + 1)], jnp.int32)
    jt = jnp.asarray([j for i in range(nq) for j in range(i + 1)], jnp.int32)
    q_map = lambda bb, h, st, it_r, jt_r: (bb, it_r[st], h)
    kv_map = lambda bb, h, st, it_r, jt_r: (bb, jt_r[st], h)
    grid_spec = pltpu.PrefetchScalarGridSpec(
        num_scalar_prefetch=2,
        grid=(b, ATTN_HEADS, int(it.shape[0])),
        in_specs=[pl.BlockSpec((None, tile, ATTN_V_DIM), q_map),
                  pl.BlockSpec((None, tile, ATTN_V_DIM), kv_map),
                  pl.BlockSpec((None, tile, ATTN_V_DIM), kv_map),
                  pl.BlockSpec((None, 3, tile, tile), lambda bb, h, st, it_r, jt_r: (h, 0, 0, 0)),
                  pl.BlockSpec((4, ATTN_HEAD_DIM), lambda bb, h, st, it_r, jt_r: (0, 0)),
                  pl.BlockSpec((1, ATTN_V_DIM), lambda bb, h, st, it_r, jt_r: (0, 0))],
        out_specs=pl.BlockSpec((None, tile, ATTN_V_DIM), q_map),
        scratch_shapes=[pltpu.VMEM((tile, ATTN_V_DIM), BF16), pltpu.VMEM((tile, ATTN_V_DIM), BF16),
                        pltpu.VMEM((tile, 1), F32), pltpu.VMEM((tile, 1), F32),
                        pltpu.VMEM((tile, 1), F32), pltpu.VMEM((tile, 1), F32),
                        pltpu.VMEM((tile, ATTN_V_DIM), F32), pltpu.VMEM((tile, ATTN_V_DIM), F32)],
    )
    return pl.pallas_call(
        _attn_online_kernel,
        grid_spec=grid_spec,
        out_shape=jax.ShapeDtypeStruct((b, s, V_COLS), BF16),
        compiler_params=_params(("arbitrary", "arbitrary", "arbitrary")),
        name="diff_attn_online",
    )(it, jt, qn, kn, v, bias, lam_vecs, sub_g)


def _diff_attention(qn, kn, v, rel_bias, q_gain, k_gain, lam_vecs, sub_g):
    tile = min(ATTN_TILE, qn.shape[1])
    bias = _bias_tiles(rel_bias, tile)
    spread = jnp.max(jnp.abs(rel_bias - rel_bias[NUM_BUCKETS - 1]))
    bound = LOG2E * (1.05 * math.sqrt(ATTN_HEAD_DIM) * jnp.max(jnp.abs(q_gain)) * jnp.max(jnp.abs(k_gain)) + spread)
    args = (qn, kn, v, bias, lam_vecs, sub_g)
    return lax.cond(bound < EXP2_SAFE_BOUND, lambda a: _attn_bounded(*a), lambda a: _attn_online(*a), args)


def _ssd_kernel(z_ref, xbc_ref, dt_ref, acum_ref, acumt_ref, dskip_ref, ng_ref, y_ref, state_s):
    L = SSM_CHUNK

    @pl.when(pl.program_id(1) == 0)
    def _reset():
        state_s[...] = jnp.zeros(state_s.shape, F32)

    row = lax.broadcasted_iota(jnp.int32, (L, L), 0)
    col = lax.broadcasted_iota(jnp.int32, (L, L), 1)
    causal = _chunk_time(row) >= _chunk_time(col)
    unpick = jnp.where(row == _chunk_time(col), 1.0, 0.0).astype(BF16)
    lane = lax.broadcasted_iota(jnp.int32, (L, LANES), 1)
    low = lane < SSM_HEAD_DIM
    for c in range(z_ref.shape[0] // L):
        _ssd_chunk(slice(c * L, (c + 1) * L), z_ref, xbc_ref, dt_ref, acum_ref, acumt_ref, dskip_ref, ng_ref, y_ref,
                   state_s, causal, unpick, low)


def _ssd_chunk(rows, z_ref, xbc_ref, dt_ref, acum_ref, acumt_ref, dskip_ref, ng_ref, y_ref, state_s, causal, unpick,
               low):
    L = SSM_CHUNK
    xs = xbc_ref[rows, :D_INNER].astype(F32)
    bm = xbc_ref[rows, D_INNER:D_INNER + SSM_GROUPS * D_STATE]
    cm = xbc_ref[rows, D_INNER + SSM_GROUPS * D_STATE:]
    dt = dt_ref[rows, :]
    acum = acum_ref[rows, :]
    acum_t = acumt_ref[rows, :]
    a_end = acum[L - 1:L, :]

    def pair(lo, hi):
        return jnp.where(low[:lo.shape[0]], lo, hi)

    def spread(arr, h):
        return jnp.broadcast_to(arr[:, h:h + 1], (arr.shape[0], LANES))

    contract_last = (((1,), (1,)), ((), ()))
    contract_first = (((0,), (0,)), ((), ()))
    y_parts = []
    for g in range(SSM_GROUPS):
        bg = bm[:, g * D_STATE:(g + 1) * D_STATE]
        cg = cm[:, g * D_STATE:(g + 1) * D_STATE]
        cb = lax.dot_general(cg, bg, contract_last, preferred_element_type=F32)
        st = state_s[g]
        y_off = jnp.dot(cg, st.astype(BF16), preferred_element_type=F32)
        xw_parts = []
        dec_parts = []
        for pr in range(SSM_HEADS_PER_GROUP // 2):
            h0 = g * SSM_HEADS_PER_GROUP + 2 * pr
            ch = slice(h0 * SSM_HEAD_DIM, (h0 + 2) * SSM_HEAD_DIM)
            x_pair = xs[:, ch]
            acols = (spread(acum, h0), spread(acum, h0 + 1))
            acol = pair(*acols)
            aend = pair(spread(a_end, h0), spread(a_end, h0 + 1))
            xdt32 = x_pair * pair(spread(dt, h0), spread(dt, h0 + 1))
            xdt = xdt32.astype(BF16)
            yd = []
            for hh, ac in zip((h0, h0 + 1), acols):
                seg = ac - acum_t[hh:hh + 1, :]
                decay = jnp.exp2(jnp.where(causal, seg, NEG_BIG))
                yd.append(jnp.dot((cb * decay).astype(BF16), xdt, preferred_element_type=F32))
            y_diag = pair(yd[0], yd[1])
            off = y_off[:, 2 * pr * SSM_HEAD_DIM:(2 * pr + 2) * SSM_HEAD_DIM]
            y_parts.append(y_diag + off * jnp.exp2(acol))
            xw_parts.append(xdt32 * jnp.exp2(aend - acol))
            dec_parts.append(jnp.exp2(aend))
        xw = jnp.concatenate(xw_parts, axis=-1).astype(BF16)
        dec = jnp.concatenate(dec_parts, axis=-1)
        state_s[g] = st * dec + lax.dot_general(bg, xw, contract_first, preferred_element_type=F32)
    y = (jnp.concatenate(y_parts, axis=-1) + dskip_ref[...] * xs) * z_ref[rows, :].astype(F32)
    gsz = D_INNER // SSM_GROUPS
    for g in range(SSM_GROUPS):
        sl = slice(g * gsz, (g + 1) * gsz)
        yn = (_rms(y[:, sl], SSM_EPS) * ng_ref[:, sl]).astype(BF16)
        y_ref[rows, sl] = jnp.dot(unpick, yn, preferred_element_type=F32).astype(BF16)


def _ssd(z, xbc, dt, acum, acum_t, d_skip_ch, norm_g):
    b, s, _ = z.shape
    rows = SSD_CHUNKS_PER_STEP * SSM_CHUNK
    assert s % rows == 0
    blk = lambda bb, c: (bb, c, 0)
    fix = lambda bb, c: (0, 0)
    return pl.pallas_call(
        _ssd_kernel,
        grid=(b, s // rows),
        in_specs=[pl.BlockSpec((None, rows, D_INNER), blk), pl.BlockSpec((None, rows, CONV_DIM), blk)]
        + [pl.BlockSpec((None, rows, DT_PAD), blk)] * 3
        + [pl.BlockSpec((1, D_INNER), fix), pl.BlockSpec((1, D_INNER), fix)],
        out_specs=pl.BlockSpec((None, rows, D_INNER), blk),
        out_shape=jax.ShapeDtypeStruct((b, s, D_INNER), BF16),
        scratch_shapes=[pltpu.VMEM((SSM_GROUPS, D_STATE, SSM_HEADS_PER_GROUP * SSM_HEAD_DIM), F32)],
        compiler_params=_params(("arbitrary", "arbitrary")),
        name="ssd",
    )(z, xbc, dt, acum, acum_t, d_skip_ch, norm_g)


def _round_up(n, m):
    return (n + m - 1) // m * m


def _merge_kernel(x_ref, attn_ref, y_ref, gmix_ref, wg_ref, wao_ref, wso_ref, wout_ref, gffn_ref, wr_ref, br_ref,
                  x1_ref, h2_ref, pos_ref, topw_ref, cnt_ref):
    x = x_ref[...]
    tm = x.shape[0]
    h = (_rms(x, RMS_EPS) * gmix_ref[...]).astype(BF16)
    gates = _sigmoid(jnp.dot(h, wg_ref[...], preferred_element_type=F32))
    attn_out = jnp.dot(attn_ref[...], wao_ref[...], preferred_element_type=F32)
    ssm_out = jnp.dot(y_ref[...], wso_ref[...], preferred_element_type=F32)
    merged = gates[:, :D_MODEL] * attn_out + gates[:, D_MODEL:] * ssm_out
    x1 = x + jnp.dot(merged.astype(BF16), wout_ref[...], preferred_element_type=F32)
    x1_ref[...] = x1
    h2 = _rms(x1, RMS_EPS) * gffn_ref[...]
    h2_ref[...] = h2.astype(BF16)

    logits = lax.dot_general(wr_ref[...], h2, (((1,), (1,)), ((), ())), preferred_element_type=F32,
                             precision=lax.Precision.HIGHEST) + br_ref[...]
    eid = lax.broadcasted_iota(jnp.int32, logits.shape, 0)
    vals, hits = [], []
    member = jnp.zeros(logits.shape, F32)
    work = logits
    for kk in range(TOP_K):
        m = jnp.max(work, axis=0, keepdims=True)
        idx = jnp.min(jnp.where(work == m, eid, N_EXPERTS), axis=0, keepdims=True)
        hit = eid == idx
        vals.append(m)
        hits.append(hit)
        member = jnp.where(hit, 1.0, member)
        work = jnp.where(hit, -jnp.inf, work)
    ex = [jnp.exp(v - vals[0]) for v in vals]
    denom = ex[0] + ex[1] + ex[2] + ex[3]
    for kk in range(TOP_K):
        topw_ref[kk:kk + 1, :] = ex[kk] / denom

    r = lax.broadcasted_iota(jnp.int32, (tm, tm), 0)
    c = lax.broadcasted_iota(jnp.int32, (tm, tm), 1)
    before = jnp.where(r < c, 1.0, 0.0).astype(BF16)
    prefix = jnp.dot(member.astype(BF16), before, preferred_element_type=F32)
    cnt = jnp.sum(member, axis=1, keepdims=True).astype(jnp.int32)
    cnt_al = jnp.bitwise_and(cnt + (ROW_ALIGN - 1), -ROW_ALIGN)
    cnt_al = jnp.broadcast_to(cnt_al, (N_EXPERTS, LANES)).astype(F32)
    er = lax.broadcasted_iota(jnp.int32, (N_EXPERTS, N_EXPERTS), 0)
    ec = lax.broadcasted_iota(jnp.int32, (N_EXPERTS, N_EXPERTS), 1)
    lower = jnp.where(ec < er, 1.0, 0.0).astype(F32)
    run_start = jnp.dot(lower, cnt_al, preferred_element_type=F32, precision=lax.Precision.HIGHEST)[:, 0:1]
    base = prefix + run_start
    for kk in range(TOP_K):
        pos_ref[kk:kk + 1, :] = jnp.sum(jnp.where(hits[kk], base, 0.0), axis=0, keepdims=True).astype(jnp.int32)
    cnt_ref[...] = jnp.broadcast_to(cnt, cnt_ref.shape)


def _merge(x2, attn, y, g_mix, w_gate, w_ao, w_so, w_out, g_ffn, w_r_t, b_r):
    t = x2.shape[0]
    tm = min(ROW_TILE, t)
    row = lambda i: (i, 0)
    colb = lambda i: (0, i)
    fix = lambda i: (0, 0)
    return pl.pallas_call(
        _merge_kernel,
        grid=(t // tm,),
        in_specs=[pl.BlockSpec((tm, D_MODEL), row), pl.BlockSpec((tm, V_COLS), row), pl.BlockSpec((tm, D_INNER), row),
                  pl.BlockSpec((1, D_MODEL), fix), _resident((D_MODEL, 2 * D_MODEL)),
                  _resident((V_COLS, D_MODEL)), _resident((D_INNER, D_MODEL)),
                  _resident((D_MODEL, D_MODEL)), pl.BlockSpec((1, D_MODEL), fix),
                  pl.BlockSpec((N_EXPERTS, D_MODEL), fix), pl.BlockSpec((N_EXPERTS, 1), fix)],
        out_specs=[pl.BlockSpec((tm, D_MODEL), row), pl.BlockSpec((tm, D_MODEL), row),
                   pl.BlockSpec((TOP_K, tm), colb), pl.BlockSpec((TOP_K, tm), colb),
                   pl.BlockSpec((N_EXPERTS, LANES), row)],
        out_shape=[jax.ShapeDtypeStruct((t, D_MODEL), F32), jax.ShapeDtypeStruct((t, D_MODEL), BF16),
                   jax.ShapeDtypeStruct((TOP_K, t), jnp.int32), jax.ShapeDtypeStruct((TOP_K, t), F32),
                   jax.ShapeDtypeStruct((t // tm * N_EXPERTS, LANES), jnp.int32)],
        compiler_params=_params(("arbitrary",)),
        name="merge_router",
    )(x2, attn, y, g_mix, w_gate, w_ao, w_so, w_out, g_ffn, w_r_t, b_r)


def _sorted_rows(tm):
    return _round_up(TOP_K * tm + N_EXPERTS * (ROW_ALIGN - 1), LANES)


def _run_copies(n, max_n, vmem_ref, vmem_off, hbm_ref, hbm_off, sem, to_hbm, wait, priority=0):
    done = 0
    pieces = [ROW_ALIGN << p for p in range((max_n // ROW_ALIGN).bit_length())]
    for bit in reversed(pieces):
        take = (n & bit) != 0

        @pl.when(take)
        def _piece(bit=bit, done=done):
            v = vmem_ref.at[pl.ds(pl.multiple_of(vmem_off + done, ROW_ALIGN), bit)]
            h = hbm_ref.at[pl.ds(pl.multiple_of(hbm_off + done, ROW_ALIGN), bit)]
            cp = pltpu.make_async_copy(v, h, sem) if to_hbm else pltpu.make_async_copy(h, v, sem)
            cp.wait() if wait else cp.start(priority=priority)

        done = done + jnp.where(take, bit, 0)


def _dispatch_kernel(off_ref, cnt_ref, start_ref, last_ref, h2_ref, pos_ref, xs_hbm, buf_s, zero_s, sems, zsem):
    i = pl.program_id(0)
    tm = h2_ref.shape[0]
    rows = buf_s.shape[1]
    slot = i & 1

    @pl.when(i == 0)
    def _zero_last_blocks():
        zero_s[...] = jnp.zeros(zero_s.shape, BF16)
        for e in range(N_EXPERTS):
            start = pl.multiple_of(last_ref[e], MOE_TILE)
            pltpu.make_async_copy(zero_s, xs_hbm.at[pl.ds(start, MOE_TILE)], zsem).start()
        for e in range(N_EXPERTS):
            pltpu.make_async_copy(zero_s, xs_hbm.at[pl.ds(0, MOE_TILE)], zsem).wait()

        def slack(wait):
            def body(blk, carry):
                cp = pltpu.make_async_copy(zero_s, xs_hbm.at[pl.ds(pl.multiple_of(blk * MOE_TILE, MOE_TILE), MOE_TILE)],
                                           zsem)
                cp.wait() if wait else cp.start()
                return carry
            lax.fori_loop(last_ref[N_EXPERTS], xs_hbm.shape[0] // MOE_TILE, body, 0)

        slack(False)
        slack(True)

    pos = pos_ref[...]
    rid = lax.broadcasted_iota(jnp.int32, (rows, tm), 0)
    sel = jnp.zeros((rows, tm), F32)
    for kk in range(TOP_K):
        sel = sel + jnp.where(rid == pos[kk:kk + 1, :], 1.0, 0.0)
    buf_s[slot] = jnp.dot(sel.astype(BF16), h2_ref[...], preferred_element_type=F32).astype(BF16)

    def copies(tile, tile_slot, wait):
        for e in range(N_EXPERTS):
            idx = tile * N_EXPERTS + e
            _run_copies(cnt_ref[idx], tm, buf_s.at[tile_slot], start_ref[idx], xs_hbm, off_ref[idx],
                        sems.at[tile_slot], True, wait, priority=e % 2)

    copies(i, slot, False)

    @pl.when(i >= 1)
    def _previous_tile_done():
        copies(i - 1, 1 - slot, True)

    @pl.when(i == pl.num_programs(0) - 1)
    def _last_tile_done():
        copies(i, slot, True)


def _dispatch(run_off, run_cnt, run_start, last_block, h2, pos, n_rows):
    t = h2.shape[0]
    tm = min(ROW_TILE, t)
    grid_spec = pltpu.PrefetchScalarGridSpec(
        num_scalar_prefetch=4,
        grid=(t // tm,),
        in_specs=[pl.BlockSpec((tm, D_MODEL), lambda i, *_: (i, 0)),
                  pl.BlockSpec((TOP_K, tm), lambda i, *_: (0, i))],
        out_specs=pl.BlockSpec(memory_space=pl.ANY),
        scratch_shapes=[pltpu.VMEM((2, _sorted_rows(tm), D_MODEL), BF16), pltpu.VMEM((MOE_TILE, D_MODEL), BF16),
                        pltpu.SemaphoreType.DMA((2,)), pltpu.SemaphoreType.DMA(())],
    )
    return pl.pallas_call(
        _dispatch_kernel,
        grid_spec=grid_spec,
        out_shape=jax.ShapeDtypeStruct((n_rows, D_MODEL), BF16),
        compiler_params=_params(("arbitrary",)),
        name="moe_dispatch",
    )(run_off, run_cnt, run_start, last_block, h2, pos)


def _expert_kernel(be_ref, nb_ref, x_ref, w1_ref, b1_ref, w2_ref, b2_ref, y_ref, w1_s, w2_s):
    i = pl.program_id(0)

    @pl.when(jnp.logical_or(i == 0, be_ref[i] != be_ref[jnp.maximum(i - 1, 0)]))
    def _new_expert():
        w1_s[...] = w1_ref[...].astype(BF16)
        w2_s[...] = w2_ref[...].astype(BF16)

    @pl.when(i < nb_ref[0])
    def _run():
        gu = jnp.dot(x_ref[...], w1_s[...], preferred_element_type=F32) + b1_ref[...]
        gate = jnp.minimum(gu[:, :D_EXPERT], SWIGLU_LIMIT)
        up = jnp.clip(gu[:, D_EXPERT:], -SWIGLU_LIMIT, SWIGLU_LIMIT)
        act = (up + 1.0) * (gate * _sigmoid(SWIGLU_ALPHA * gate))
        y = jnp.dot(act.astype(BF16), w2_s[...], preferred_element_type=F32) + b2_ref[...]
        y_ref[...] = y.astype(BF16)

    @pl.when(i >= nb_ref[0])
    def _slack():
        y_ref[...] = jnp.zeros(y_ref.shape, BF16)


def _experts(block_e, n_used, xs, w1, b1, w2, b2):
    n_rows = xs.shape[0]
    nb = n_rows // MOE_TILE
    row_in = lambda i, be, nu: (jnp.minimum(i, nu[0] - 1), 0)
    row = lambda i, be, nu: (i, 0)
    wsel = lambda i, be, nu: (be[i], 0, 0)
    grid_spec = pltpu.PrefetchScalarGridSpec(
        num_scalar_prefetch=2,
        grid=(nb,),
        in_specs=[pl.BlockSpec((MOE_TILE, D_MODEL), row_in),
                  pl.BlockSpec((None, D_MODEL, 2 * D_EXPERT), wsel), pl.BlockSpec((None, 1, 2 * D_EXPERT), wsel),
                  pl.BlockSpec((None, D_EXPERT, D_MODEL), wsel), pl.BlockSpec((None, 1, D_MODEL), wsel)],
        out_specs=pl.BlockSpec((MOE_TILE, D_MODEL), row),
        scratch_shapes=[pltpu.VMEM((D_MODEL, 2 * D_EXPERT), BF16), pltpu.VMEM((D_EXPERT, D_MODEL), BF16)],
    )
    return pl.pallas_call(
        _expert_kernel,
        grid_spec=grid_spec,
        out_shape=jax.ShapeDtypeStruct((n_rows, D_MODEL), BF16),
        compiler_params=_params(("arbitrary",)),
        name="moe_experts",
    )(block_e, n_used, xs, w1, b1, w2, b2)


def _combine_kernel(off_ref, cnt_ref, start_ref, y_hbm, x1_ref, pos_ref, w_ref, o_ref, buf_s, sems):
    i = pl.program_id(0)
    tm = x1_ref.shape[0]
    rows = buf_s.shape[1]
    slot = i & 1

    def copies(tile, tile_slot, wait):
        for e in range(N_EXPERTS):
            idx = tile * N_EXPERTS + e
            _run_copies(cnt_ref[idx], tm, buf_s.at[tile_slot], start_ref[idx], y_hbm, off_ref[idx],
                        sems.at[tile_slot], False, wait, priority=e % 2)

    @pl.when(i == 0)
    def _first_tile():
        buf_s[...] = jnp.zeros(buf_s.shape, BF16)
        copies(0, 0, False)

    @pl.when(i + 1 < pl.num_programs(0))
    def _prefetch_next_tile():
        copies(i + 1, 1 - slot, False)

    pos = pos_ref[...]
    w = w_ref[...]
    cid = lax.broadcasted_iota(jnp.int32, (tm, rows), 1)
    wsel = jnp.zeros((tm, rows), F32)
    for kk in range(TOP_K):
        wsel = wsel + jnp.where(cid == pos[:, kk:kk + 1], w[:, kk:kk + 1], 0.0)
    copies(i, slot, True)
    o_ref[...] = x1_ref[...] + jnp.dot(wsel.astype(BF16), buf_s[slot], preferred_element_type=F32)


def _combine(run_off, run_cnt, run_start, y, x1, pos_tok, w_tok):
    t = x1.shape[0]
    tm = min(ROW_TILE, t)
    grid_spec = pltpu.PrefetchScalarGridSpec(
        num_scalar_prefetch=3,
        grid=(t // tm,),
        in_specs=[pl.BlockSpec(memory_space=pl.ANY),
                  pl.BlockSpec((tm, D_MODEL), lambda i, *_: (i, 0)),
                  pl.BlockSpec((tm, TOP_K), lambda i, *_: (i, 0)), pl.BlockSpec((tm, TOP_K), lambda i, *_: (i, 0))],
        out_specs=pl.BlockSpec((tm, D_MODEL), lambda i, *_: (i, 0)),
        scratch_shapes=[pltpu.VMEM((2, _sorted_rows(tm), D_MODEL), BF16), pltpu.SemaphoreType.DMA((2,))],
    )
    return pl.pallas_call(
        _combine_kernel,
        grid_spec=grid_spec,
        out_shape=jax.ShapeDtypeStruct((t, D_MODEL), F32),
        compiler_params=_params(("arbitrary",)),
        name="moe_combine",
    )(run_off, run_cnt, run_start, y, x1, pos_tok, w_tok)


def kernel(x, g_mix, w_in, q_norm_g, k_norm_g, lambda_q1, lambda_k1, lambda_q2, lambda_k2, attn_sub_g, rel_bias,
           w_attn_o, conv_w, conv_b, dt_bias, a_log, d_skip, ssm_norm_g, w_ssm_o, w_out, g_ffn, w_router, b_router,
           w1, b1, w2, b2):
    b, s, d = x.shape
    t = b * s
    l = 0
    x2 = x.reshape(t, d)

    w = w_in[l]
    c0 = Q_COLS + K_COLS + V_COLS
    c1 = c0 + D_INNER + CONV_DIM
    w_qkv = w[:, :c0].astype(BF16)
    w_dt = jnp.pad(w[:, c1:c1 + SSM_HEADS], ((0, 0), (0, DT_PAD - SSM_HEADS)))
    w_ssm = jnp.concatenate([w[:, c0:c1], w_dt], axis=1).astype(BF16)
    w_gate = w[:, c1 + SSM_HEADS:].astype(BF16)
    n_hd = Q_COLS // ATTN_HEAD_DIM
    gq = (jnp.tile(q_norm_g[l], n_hd) * (ATTN_HEAD_DIM ** -0.5 * LOG2E)).reshape(1, Q_COLS)
    gk = jnp.tile(k_norm_g[l], n_hd).reshape(1, K_COLS)
    gm = g_mix[l].reshape(1, d)

    qn, kn, v = _qkv_proj(x2, gm, w_qkv, gq, gk)
    pad_h = (0, DT_PAD - SSM_HEADS)
    z, xbc, dt, acum, acum_t = _ssm_proj(x2, gm, w_ssm, conv_w[l], conv_b[l].reshape(1, -1),
                                         jnp.pad(dt_bias[l], pad_h).reshape(1, -1),
                                         jnp.pad(a_log[l], pad_h).reshape(1, -1), s)

    lam_vecs = jnp.stack([lambda_q1[l], lambda_k1[l], lambda_q2[l], lambda_k2[l]]).astype(F32)
    attn = _diff_attention(qn.reshape(b, s, -1), kn.reshape(b, s, -1), v.reshape(b, s, -1), rel_bias, q_norm_g[l],
                           k_norm_g[l], lam_vecs, attn_sub_g[l].reshape(1, ATTN_V_DIM))

    y = _ssd(z.reshape(b, s, -1), xbc.reshape(b, s, -1), dt.reshape(b, s, -1), acum.reshape(b, s, -1),
             acum_t.reshape(b, s, -1), jnp.repeat(d_skip[l], SSM_HEAD_DIM).reshape(1, -1),
             ssm_norm_g[l].reshape(1, -1))

    x1, h2, pos, top_w, tile_cnt = _merge(
        x2, attn.reshape(t, -1), y.reshape(t, -1), gm, w_gate, w_attn_o[l].astype(BF16), w_ssm_o[l].astype(BF16),
        w_out[l].astype(BF16), g_ffn[l].reshape(1, d), w_router[l].T, b_router[l].reshape(-1, 1))

    n_tiles = t // min(ROW_TILE, t)
    cnt = tile_cnt.reshape(n_tiles, N_EXPERTS, LANES)[:, :, 0]
    cnt_al = _round_up(cnt, ROW_ALIGN)
    run_start = jnp.cumsum(cnt_al, axis=1) - cnt_al
    padded = _round_up(jnp.sum(cnt_al, axis=0), MOE_TILE)
    end_pad = jnp.cumsum(padded)
    run_off = (end_pad - padded)[None, :] + jnp.cumsum(cnt_al, axis=0) - cnt_al
    n_rows = _round_up(t * TOP_K + n_tiles * N_EXPERTS * (ROW_ALIGN - 1), MOE_TILE) + N_EXPERTS * MOE_TILE
    block_start = jnp.arange(n_rows // MOE_TILE, dtype=jnp.int32) * MOE_TILE
    block_e = jnp.minimum(jnp.sum(block_start[:, None] >= end_pad[None, :], axis=1), N_EXPERTS - 1).astype(jnp.int32)
    n_used = (end_pad[-1:] // MOE_TILE).astype(jnp.int32)
    last_block = jnp.concatenate([jnp.maximum(end_pad - MOE_TILE, 0), end_pad[-1:] // MOE_TILE]).astype(jnp.int32)
    tables = [a.reshape(-1).astype(jnp.int32) for a in (run_off, cnt_al, run_start)]

    xs = _dispatch(*tables, last_block, h2, pos, n_rows)
    ys = _experts(block_e, n_used, xs, w1[l], b1[l][:, None, :], w2[l], b2[l][:, None, :])
    out = _combine(*tables, ys, x1, pos.T, top_w.T)
    return out.reshape(b, s, d)
```
